```python
import jax
import jax.numpy as jnp
from jax import lax
import numpy as np

D_MODEL = 1024
BATCH = 8
SEQ = 2048
DEPTH = 2

HEAD_DIM = 64
ROT_DIM = HEAD_DIM // 4
ROPE_THETA = 500000.0
DIL_GROUPS = ((128, 1), (512, 4), (2048, 16))
HEADS_PER_DIL_GROUP = 2
N_HEADS_A = len(DIL_GROUPS) * HEADS_PER_DIL_GROUP
N_HEADS_B = 6
N_HEADS_M = 4
MOBA_BLOCK = 256
MOBA_TOPK = 3
MOBA_Q_CHUNK = 32
N_MEM = 256
D_FF = 4 * D_MODEL
N_BRANCH = 3
WIDTH_A = N_HEADS_A * HEAD_DIM
WIDTH_A_OUT = HEADS_PER_DIL_GROUP * HEAD_DIM
WIDTH_B = N_HEADS_B * HEAD_DIM
WIDTH_M = N_HEADS_M * HEAD_DIM
D_IN = 3 * WIDTH_A + 3 * WIDTH_B + WIDTH_M + N_BRANCH * D_MODEL
RMS_EPS = 1e-6
NEG_INF = -1e30

kernel_name = 'hybrid_dilated_moba_memory_block'


def rms_norm(x, g):
    xf = x.astype(jnp.float32)
    y = xf * lax.rsqrt(jnp.mean(xf * xf, axis=-1, keepdims=True) + RMS_EPS)
    return (y * g.astype(jnp.float32)).astype(x.dtype)


def partial_rope(x, pos):
    inv_freq = 1.0 / (ROPE_THETA ** (jnp.arange(0, ROT_DIM, 2, dtype=jnp.float32) / ROT_DIM))
    ang = pos.astype(jnp.float32)[:, None] * inv_freq[None, :]
    cos = jnp.cos(ang)[None, :, None, :]
    sin = jnp.sin(ang)[None, :, None, :]
    xr = x[..., :ROT_DIM].astype(jnp.float32)
    x1, x2 = xr[..., :ROT_DIM // 2], xr[..., ROT_DIM // 2:]
    rot = jnp.concatenate([x1 * cos - x2 * sin, x2 * cos + x1 * sin], axis=-1).astype(x.dtype)
    return jnp.concatenate([rot, x[..., ROT_DIM:]], axis=-1)


def dilated_window_attn(q, k, v, window, dilation):
    B, S, H, Dh = q.shape
    n = window // dilation
    span = n * dilation
    Sp = -(-S // span) * span
    L = Sp // dilation
    nblk = L // n

    def to_blocks(t):
        t = jnp.pad(t, ((0, 0), (0, Sp - S), (0, 0), (0, 0)))
        t = t.reshape(B, L, dilation, H, Dh).transpose(0, 2, 3, 1, 4)
        return t.reshape(B, dilation, H, nblk, n, Dh)

    def with_prev(t):
        prev = jnp.pad(t, ((0, 0), (0, 0), (0, 0), (1, 0), (0, 0), (0, 0)))[:, :, :, :-1]
        return jnp.concatenate([prev, t], axis=-2)

    qb = to_blocks(q)
    kk = with_prev(to_blocks(k))
    vv = with_prev(to_blocks(v))
    s = jnp.einsum('brhnqd,brhnkd->brhnqk', qb, kk).astype(jnp.float32)
    i = jnp.arange(n)[:, None]
    j = jnp.arange(2 * n)[None, :]
    band = (j >= i) & (j <= i + n)
    first = (jnp.arange(nblk) == 0)[:, None, None] & (j < n)[None]
    mask = band[None] & ~first
    s = jnp.where(mask, s, NEG_INF)
    mx = jnp.max(s, axis=-1, keepdims=True)
    p = jnp.exp(s - mx)
    den = jnp.sum(p, axis=-1, keepdims=True)
    o = jnp.einsum('brhnqk,brhnkd->brhnqd', (p / den).astype(v.dtype), vv)
    lse = (mx + jnp.log(den))[..., 0]
    o = o.reshape(B, dilation, H, L, Dh).transpose(0, 3, 1, 2, 4).reshape(B, Sp, H, Dh)[:, :S]
    lse = lse.reshape(B, dilation, H, L).transpose(0, 3, 1, 2).reshape(B, Sp, H)[:, :S]
    return o, lse


def dilated_mixer(q, k, v):
    B, S = q.shape[0], q.shape[1]
    outs, lses = [], []
    for g, (win, dil) in enumerate(DIL_GROUPS):
        sl = slice(g * HEADS_PER_DIL_GROUP, (g + 1) * HEADS_PER_DIL_GROUP)
        o, l = dilated_window_attn(q[:, :, sl], k[:, :, sl], v[:, :, sl], win, dil)
        outs.append(o)
        lses.append(l)
    w = jax.nn.softmax(jnp.stack(lses, axis=0), axis=0)
    o = jnp.einsum('gbsh,gbshd->bshd', w, jnp.stack(outs, axis=0).astype(jnp.float32))
    return o.astype(q.dtype).reshape(B, S, WIDTH_A_OUT)


def moba_attn(q, k, v):
    B, S, H, Dh = q.shape
    nB = -(-S // MOBA_BLOCK)
    Sp = nB * MOBA_BLOCK
    padw = ((0, 0), (0, Sp - S), (0, 0), (0, 0))
    qt = jnp.pad(q, padw).transpose(0, 2, 1, 3)
    kt = jnp.pad(k, padw).transpose(0, 2, 1, 3)
    vt = jnp.pad(v, padw).transpose(0, 2, 1, 3)
    kb = kt.reshape(B, H, nB, MOBA_BLOCK, Dh)
    vb = vt.reshape(B, H, nB, MOBA_BLOCK, Dh)
    qpos = jnp.arange(Sp, dtype=jnp.int32)
    qblk = qpos // MOBA_BLOCK
    own = jnp.broadcast_to(qblk[None, None, :, None], (B, H, Sp, 1))
    n_sel = min(MOBA_TOPK, nB - 1)
    if n_sel > 0:
        kmean = jnp.mean(kb.astype(jnp.float32), axis=3)
        gate = jnp.einsum('bhsd,bhnd->bhsn', qt.astype(jnp.float32), kmean)
        past = jnp.arange(nB, dtype=jnp.int32)[None, :] < qblk[:, None]
        _, top_idx = lax.top_k(jnp.where(past, gate, NEG_INF), n_sel)
        top_ok = jnp.take_along_axis(jnp.broadcast_to(past, gate.shape), top_idx, axis=-1)
        blk_idx = jnp.concatenate([top_idx.astype(jnp.int32), own], axis=-1)
        blk_ok = jnp.concatenate([top_ok, jnp.ones_like(own, dtype=bool)], axis=-1)
    else:
        blk_idx = own
        blk_ok = jnp.ones_like(own, dtype=bool)
    n_tot = blk_idx.shape[-1]
    nc = Sp // MOBA_Q_CHUNK
    C = MOBA_Q_CHUNK
    q_c = qt.reshape(B, H, nc, C, Dh).transpose(2, 0, 1, 3, 4)
    idx_c = blk_idx.reshape(B, H, nc, C, n_tot).transpose(2, 0, 1, 3, 4)
    ok_c = blk_ok.reshape(B, H, nc, C, n_tot).transpose(2, 0, 1, 3, 4)
    pos_c = qpos.reshape(nc, C)
    bi = jnp.arange(B)[:, None, None, None]
    hi = jnp.arange(H)[None, :, None, None]
    offs = jnp.arange(MOBA_BLOCK, dtype=jnp.int32)

    def chunk(args):
        qc, ic, okc, pc = args
        ks = kb[bi, hi, ic]
        vs = vb[bi, hi, ic]
        s = jnp.einsum('bhcd,bhcnkd->bhcnk', qc, ks).astype(jnp.float32)
        kpos = ic[..., None] * MOBA_BLOCK + offs
        mask = okc[..., None] & (kpos <= pc[None, None, :, None, None])
        s = jnp.where(mask, s, NEG_INF).reshape(B, H, C, n_tot * MOBA_BLOCK)
        p = jax.nn.softmax(s, axis=-1).reshape(B, H, C, n_tot, MOBA_BLOCK)
        return jnp.einsum('bhcnk,bhcnkd->bhcd', p.astype(vs.dtype), vs)

    o = lax.map(chunk, (q_c, idx_c, ok_c, pos_c))
    o = o.transpose(1, 2, 0, 3, 4).reshape(B, H, Sp, Dh)
    return o.transpose(0, 2, 1, 3)[:, :S]


def memory_attn(q, mem_n, w_mkv):
    B, S = q.shape[0], q.shape[1]
    M = mem_n.shape[1]
    kv = (mem_n @ w_mkv).reshape(B, M, 2, N_HEADS_M, HEAD_DIM)
    s = jnp.einsum('bshd,bmhd->bhsm', q, kv[:, :, 0]).astype(jnp.float32)
    p = jax.nn.softmax(s, axis=-1)
    o = jnp.einsum('bhsm,bmhd->bshd', p.astype(q.dtype), kv[:, :, 1])
    return o.reshape(B, S, WIDTH_M)


def hybrid_layer(x, mem, pos, g_mix, w_in, w_pa, w_pb, w_pm, w_o, g_mem, w_mkv, g_mlp, w_up, w_down):
    B, S, _ = x.shape
    scale = HEAD_DIM ** -0.5
    h = rms_norm(x, g_mix)
    z = h @ w_in
    o1 = 3 * WIDTH_A
    o2 = o1 + 3 * WIDTH_B
    o3 = o2 + WIDTH_M
    qkv_a = z[..., :o1].reshape(B, S, 3, N_HEADS_A, HEAD_DIM)
    qkv_b = z[..., o1:o2].reshape(B, S, 3, N_HEADS_B, HEAD_DIM)
    q_m = z[..., o2:o3].reshape(B, S, N_HEADS_M, HEAD_DIM)
    gates = jax.nn.sigmoid(z[..., o3:].reshape(B, S, N_BRANCH, D_MODEL))
    o_a = dilated_mixer(partial_rope(qkv_a[:, :, 0], pos) * scale,
                        partial_rope(qkv_a[:, :, 1], pos), qkv_a[:, :, 2])
    o_b = moba_attn(partial_rope(qkv_b[:, :, 0], pos) * scale,
                    partial_rope(qkv_b[:, :, 1], pos), qkv_b[:, :, 2]).reshape(B, S, WIDTH_B)
    o_m = memory_attn(q_m * scale, rms_norm(mem, g_mem), w_mkv)
    y = (gates[:, :, 0] * (o_a @ w_pa) + gates[:, :, 1] * (o_b @ w_pb)
         + gates[:, :, 2] * (o_m @ w_pm))
    x = x + y @ w_o
    hm = rms_norm(x, g_mlp)
    x = x + jnp.square(jax.nn.relu(hm @ w_up)) @ w_down
    return x


def setup_inputs(seed: int = 0) -> dict:
    key = jax.random.key(seed)
    ks = jax.random.split(key, 16)

    def nrm(k, shape, fan_in):
        return jax.random.normal(k, shape, jnp.float32) * (fan_in ** -0.5)

    def gain(k, shape):
        return 1.0 + 0.05 * jax.random.normal(k, shape, jnp.float32)

    return {
        'x': jax.random.normal(ks[0], (BATCH, SEQ, D_MODEL), jnp.float32),
        'mem': jax.random.normal(ks[1], (BATCH, N_MEM, D_MODEL), jnp.float32),
        'norm_mix': gain(ks[2], (DEPTH, D_MODEL)),
        'w_in': nrm(ks[3], (DEPTH, D_MODEL, D_IN), D_MODEL),
        'w_proj_a': nrm(ks[4], (DEPTH, WIDTH_A_OUT, D_MODEL), WIDTH_A_OUT),
        'w_proj_b': nrm(ks[5], (DEPTH, WIDTH_B, D_MODEL), WIDTH_B),
        'w_proj_m': nrm(ks[6], (DEPTH, WIDTH_M, D_MODEL), WIDTH_M),
        'w_out': nrm(ks[7], (DEPTH, D_MODEL, D_MODEL), D_MODEL),
        'norm_mem': gain(ks[8], (DEPTH, D_MODEL)),
        'w_mem_kv': nrm(ks[9], (DEPTH, D_MODEL, 2 * WIDTH_M), D_MODEL),
        'norm_mlp': gain(ks[10], (DEPTH, D_MODEL)),
        'w_up': nrm(ks[11], (DEPTH, D_MODEL, D_FF), D_MODEL),
        'w_down': nrm(ks[12], (DEPTH, D_FF, D_MODEL), D_FF),
        'norm_final': gain(ks[13], (D_MODEL,)),
    }


def reference(x, mem, norm_mix, w_in, w_proj_a, w_proj_b, w_proj_m, w_out, norm_mem, w_mem_kv,
              norm_mlp, w_up, w_down, norm_final):
    pos = jnp.arange(x.shape[1], dtype=jnp.int32)
    for l in range(DEPTH):
        x = hybrid_layer(x, mem, pos, norm_mix[l], w_in[l], w_proj_a[l], w_proj_b[l], w_proj_m[l],
                         w_out[l], norm_mem[l], w_mem_kv[l], norm_mlp[l], w_up[l], w_down[l])
    return rms_norm(x, norm_final)
```

```python
import functools

import jax
import jax.numpy as jnp
import numpy as np
from jax import lax
from jax.experimental import pallas as pl
from jax.experimental.pallas import tpu as pltpu

D_MODEL = 1024
HEAD_DIM = 64
ROT_DIM = HEAD_DIM // 4
ROPE_THETA = 500000.0
DIL_GROUPS = ((128, 1), (512, 4), (2048, 16))
HEADS_PER_DIL_GROUP = 2
N_HEADS_A = len(DIL_GROUPS) * HEADS_PER_DIL_GROUP
N_HEADS_B = 6
N_HEADS_M = 4
MOBA_BLOCK = 256
MOBA_TOPK = 3
D_FF = 4 * D_MODEL
WIDTH_A = N_HEADS_A * HEAD_DIM
WIDTH_A_OUT = HEADS_PER_DIL_GROUP * HEAD_DIM
WIDTH_B = N_HEADS_B * HEAD_DIM
WIDTH_M = N_HEADS_M * HEAD_DIM
WIDTH_O = WIDTH_A_OUT + WIDTH_B + WIDTH_M
RMS_EPS = 1e-6
NEG_INF = -1e30
Q_SCALE = HEAD_DIM ** -0.5

LANES = 128
PAIR = 2 * HEAD_DIM
TQ = MOBA_BLOCK
TM_QKV = 512
TM_MIX = 512
TM_MLP = 512
FF_CHUNK = 1024
VMEM_LIMIT = 56 * 1024 * 1024

_B_ALLNEG = 0
_B_CAUSAL = 1
_B_DIL = 2


def _dil_tile_offsets():
    max_off = 2048 // TQ - 1
    return tuple(tuple(range(min((w + TQ - 1) // TQ, max_off) + 1)) for w, _ in DIL_GROUPS)


def _build_bias_tiles():
    c = np.arange(TQ)[:, None]
    r = np.arange(TQ)[None, :]
    tiles = [np.zeros((TQ, TQ), bool), (r - c) >= 0]
    ids = []
    for (w, d), offs in zip(DIL_GROUPS, _dil_tile_offsets()):
        per_off = []
        for o in offs:
            diff = o * TQ + r - c
            per_off.append((diff >= 0) & (diff <= w) & (diff % d == 0))
        uniq, gid = [], []
        for t in per_off:
            for k, u in enumerate(uniq):
                if np.array_equal(t, u):
                    gid.append(k)
                    break
            else:
                uniq.append(t)
                gid.append(len(uniq) - 1)
        ids.append(tuple(len(tiles) + k for k in gid))
        tiles.extend(uniq)
    bias = np.where(np.stack(tiles), 0.0, NEG_INF).astype(np.float32)
    return bias, tuple(ids)


_BIAS_NP, _DIL_BIAS_IDS = _build_bias_tiles()

_NT = (((1,), (1,)), ((), ()))


def _rms(x, g):
    return x * lax.rsqrt(jnp.mean(x * x, axis=-1, keepdims=True) + RMS_EPS) * g


def _qkv_kernel(x_ref, g_ref, wqka_ref, wvta_ref, wqkb_ref, wvtb_ref, wqm_ref, c_ref, s1_ref, s2_ref,
                qka_ref, vta_ref, qkb_ref, vtb_ref, qm_ref):
    h = _rms(x_ref[...], g_ref[...]).astype(jnp.bfloat16)
    cos = c_ref[...]
    sin_lo = s1_ref[...]
    sin_hi = s2_ref[...]
    for wqk_ref, wvt_ref, qk_ref, vt_ref, width in ((wqka_ref, wvta_ref, qka_ref, vta_ref, WIDTH_A),
                                                    (wqkb_ref, wvtb_ref, qkb_ref, vtb_ref, WIDTH_B)):
        z = jnp.dot(h, wqk_ref[...], preferred_element_type=jnp.float32)
        for blk in range(2 * width // LANES):
            zb = z[:, blk * LANES:(blk + 1) * LANES]
            rb = zb * cos + pltpu.roll(zb, LANES - ROT_DIM // 2, 1) * sin_lo + pltpu.roll(zb, ROT_DIM // 2, 1) * sin_hi
            if blk < width // LANES:
                rb = rb * Q_SCALE
            qk_ref[:, blk * LANES:(blk + 1) * LANES] = rb.astype(jnp.bfloat16)
        vt = lax.dot_general(wvt_ref[...], h, _NT, preferred_element_type=jnp.float32)
        vt_ref[...] = vt.astype(jnp.bfloat16)
    qm = jnp.dot(h, wqm_ref[...], preferred_element_type=jnp.float32) * Q_SCALE
    qm_ref[...] = qm.astype(jnp.bfloat16)


def _qkv_call(x2, g, wqka, wvta, wqkb, wvtb, wqm, rope_c, rope_s1, rope_s2, batch, seq):
    t = x2.shape[0]
    tm = TM_QKV
    nts = seq // tm
    const = lambda i: (0, 0)
    return pl.pallas_call(
        _qkv_kernel,
        grid=(t // tm,),
        in_specs=[
            pl.BlockSpec((tm, D_MODEL), lambda i: (i, 0)),
            pl.BlockSpec((1, D_MODEL), const),
            pl.BlockSpec((D_MODEL, 2 * WIDTH_A), const),
            pl.BlockSpec((WIDTH_A, D_MODEL), const),
            pl.BlockSpec((D_MODEL, 2 * WIDTH_B), const),
            pl.BlockSpec((WIDTH_B, D_MODEL), const),
            pl.BlockSpec((D_MODEL, WIDTH_M), const),
            pl.BlockSpec((tm, LANES), lambda i: (i % nts, 0)),
            pl.BlockSpec((tm, LANES), lambda i: (i % nts, 0)),
            pl.BlockSpec((tm, LANES), lambda i: (i % nts, 0)),
        ],
        out_specs=[
            pl.BlockSpec((tm, 2 * WIDTH_A), lambda i: (i, 0)),
            pl.BlockSpec((WIDTH_A, tm), lambda i: (i // nts, i % nts)),
            pl.BlockSpec((tm, 2 * WIDTH_B), lambda i: (i, 0)),
            pl.BlockSpec((WIDTH_B, tm), lambda i: (i // nts, i % nts)),
            pl.BlockSpec((tm, WIDTH_M), lambda i: (i, 0)),
        ],
        out_shape=[
            jax.ShapeDtypeStruct((t, 2 * WIDTH_A), jnp.bfloat16),
            jax.ShapeDtypeStruct((batch * WIDTH_A, seq), jnp.bfloat16),
            jax.ShapeDtypeStruct((t, 2 * WIDTH_B), jnp.bfloat16),
            jax.ShapeDtypeStruct((batch * WIDTH_B, seq), jnp.bfloat16),
            jax.ShapeDtypeStruct((t, WIDTH_M), jnp.bfloat16),
        ],
        compiler_params=pltpu.CompilerParams(dimension_semantics=("parallel",), vmem_limit_bytes=VMEM_LIMIT),
        name="qkv_proj",
    )(x2, g, wqka, wvta, wqkb, wvtb, wqm, rope_c, rope_s1, rope_s2)


def _flash_first(s_t, v_t):
    m = jnp.max(s_t, axis=0, keepdims=True)
    p = jnp.exp(s_t - m)
    l = jnp.sum(p, axis=0, keepdims=True)
    acc = jnp.dot(v_t, p.astype(jnp.bfloat16), preferred_element_type=jnp.float32)
    return m, l, acc


def _flash_next(carry, s_t, v_t):
    m, l, acc = carry
    m_new = jnp.maximum(m, jnp.max(s_t, axis=0, keepdims=True))
    alpha = jnp.exp(m - m_new)
    p = jnp.exp(s_t - m_new)
    l = alpha * l + jnp.sum(p, axis=0, keepdims=True)
    acc = alpha * acc + jnp.dot(v_t, p.astype(jnp.bfloat16), preferred_element_type=jnp.float32)
    return m_new, l, acc


def _scores_t(k_tile, q_head):
    return lax.dot_general(k_tile, q_head, _NT, preferred_element_type=jnp.float32)


def _attn_kernel(qa_ref, ka_ref, vta_ref, qb_ref, kb_ref, vtb_ref, qm_ref, mem_ref, gmem_ref, wmk_ref, wmvt_ref,
                 bias_ref, o_ref, kmean_ref, km_ref, vmt_ref, selb_ref):
    qi = pl.program_id(1)
    n_blocks = kb_ref.shape[0] // MOBA_BLOCK

    @pl.when(qi == 0)
    def _per_batch():
        kb = kb_ref[...].astype(jnp.float32)
        kmean_ref[...] = jnp.mean(kb.reshape(n_blocks, MOBA_BLOCK, WIDTH_B), axis=1)
        mem_n = _rms(mem_ref[...], gmem_ref[...]).astype(jnp.bfloat16)
        km_ref[...] = jnp.dot(mem_n, wmk_ref[...], preferred_element_type=jnp.float32).astype(jnp.bfloat16)
        vmt_ref[...] = lax.dot_general(wmvt_ref[...], mem_n, _NT,
                                       preferred_element_type=jnp.float32).astype(jnp.bfloat16)

    def tile_start(j):
        return pl.multiple_of(j * TQ, TQ)

    lane = lax.broadcasted_iota(jnp.int32, (TQ, PAIR), 1)
    head_lanes = [lane < HEAD_DIM, lane >= HEAD_DIM]
    out_rows = []

    for hh in range(HEADS_PER_DIL_GROUP):
        carry = None
        for g, offs in enumerate(_dil_tile_offsets()):
            cols = slice(g * PAIR, (g + 1) * PAIR)
            vrows = slice(g * PAIR + hh * HEAD_DIM, g * PAIR + (hh + 1) * HEAD_DIM)
            qh = jnp.where(head_lanes[hh], qa_ref[:, cols], jnp.zeros((), jnp.bfloat16))
            ids = _DIL_BIAS_IDS[g]
            static_offs = offs if len(offs) < n_blocks else offs[:1]
            for o in static_offs:
                kj = jnp.maximum(qi - o, 0)
                bid = ids[o] if o == 0 else jnp.where(qi >= o, ids[o], _B_ALLNEG)
                s_t = _scores_t(ka_ref[pl.ds(tile_start(kj), TQ), cols], qh) + bias_ref[bid]
                v_t = vta_ref[vrows, pl.ds(tile_start(kj), TQ)]
                carry = _flash_first(s_t, v_t) if carry is None else _flash_next(carry, s_t, v_t)
            if len(offs) >= n_blocks:
                assert len(set(ids[1:])) == 1

                def past_tile(j, c, cols=cols, vrows=vrows, qh=qh, bid=ids[1]):
                    s_t = _scores_t(ka_ref[pl.ds(tile_start(j), TQ), cols], qh) + bias_ref[bid]
                    return _flash_next(c, s_t, vta_ref[vrows, pl.ds(tile_start(j), TQ)])

                carry = lax.fori_loop(0, qi, past_tile, carry)
        out_rows.append(carry[2] / carry[1])

    blk = lax.broadcasted_iota(jnp.int32, (n_blocks, TQ), 0)
    for p in range(N_HEADS_B // 2):
        cols = slice(p * PAIR, (p + 1) * PAIR)
        kmean = kmean_ref[:, cols]
        kmean_hi = kmean.astype(jnp.bfloat16)
        kmean_lo = (kmean - kmean_hi.astype(jnp.float32)).astype(jnp.bfloat16)
        for hh in range(2):
            vrows = slice(p * PAIR + hh * HEAD_DIM, p * PAIR + (hh + 1) * HEAD_DIM)
            qh = jnp.where(head_lanes[hh], qb_ref[:, cols], jnp.zeros((), jnp.bfloat16))
            gate = _scores_t(kmean_hi, qh) + _scores_t(kmean_lo, qh)
            gate = jnp.where(blk < qi, gate, NEG_INF)
            beaten = jnp.zeros((n_blocks, TQ), jnp.float32)
            for j in range(n_blocks):
                gj = gate[j:j + 1, :]
                wins_tie = jnp.where(gj >= gate, 1.0, 0.0)
                wins_strict = jnp.where(gj > gate, 1.0, 0.0)
                beaten = beaten + jnp.where(blk > j, wins_tie, wins_strict)
            selb_ref[...] = jnp.where((beaten < MOBA_TOPK) & (blk < qi), 0.0, NEG_INF)
            s_t = _scores_t(kb_ref[pl.ds(tile_start(qi), TQ), cols], qh) + bias_ref[_B_CAUSAL]
            carry = _flash_first(s_t, vtb_ref[vrows, pl.ds(tile_start(qi), TQ)])

            def past_block(j, c, cols=cols, vrows=vrows, qh=qh):
                s_t = _scores_t(kb_ref[pl.ds(tile_start(j), TQ), cols], qh) + selb_ref[pl.ds(j, 1), :]
                return _flash_next(c, s_t, vtb_ref[vrows, pl.ds(tile_start(j), TQ)])

            carry = lax.fori_loop(0, qi, past_block, carry)
            out_rows.append(carry[2] / carry[1])

    lane_m = lax.broadcasted_iota(jnp.int32, (TQ, WIDTH_M), 1)
    for h in range(N_HEADS_M):
        in_head = (lane_m >= h * HEAD_DIM) & (lane_m < (h + 1) * HEAD_DIM)
        qh = jnp.where(in_head, qm_ref[...], jnp.zeros((), jnp.bfloat16))
        m, l, acc = _flash_first(_scores_t(km_ref[...], qh), vmt_ref[h * HEAD_DIM:(h + 1) * HEAD_DIM, :])
        out_rows.append(acc / l)

    for c in range(WIDTH_O // LANES):
        rows = jnp.concatenate(out_rows[2 * c:2 * c + 2], axis=0)
        o_ref[:, c * LANES:(c + 1) * LANES] = rows.T.astype(jnp.bfloat16)


def _attn_call(qka, vta, qkb, vtb, qm, mem2, gmem, wmk, wmvt, bias, batch, seq):
    nq = seq // TQ
    n_mem = mem2.shape[0] // batch
    const2 = lambda b, q: (0, 0)
    return pl.pallas_call(
        _attn_kernel,
        grid=(batch, nq),
        in_specs=[
            pl.BlockSpec((TQ, WIDTH_A), lambda b, q: (b * nq + q, 0)),
            pl.BlockSpec((seq, WIDTH_A), lambda b, q: (b, 1)),
            pl.BlockSpec((WIDTH_A, seq), lambda b, q: (b, 0)),
            pl.BlockSpec((TQ, WIDTH_B), lambda b, q: (b * nq + q, 0)),
            pl.BlockSpec((seq, WIDTH_B), lambda b, q: (b, 1)),
            pl.BlockSpec((WIDTH_B, seq), lambda b, q: (b, 0)),
            pl.BlockSpec((TQ, WIDTH_M), lambda b, q: (b * nq + q, 0)),
            pl.BlockSpec((n_mem, D_MODEL), lambda b, q: (b, 0)),
            pl.BlockSpec((1, D_MODEL), const2),
            pl.BlockSpec((D_MODEL, WIDTH_M), const2),
            pl.BlockSpec((WIDTH_M, D_MODEL), const2),
            pl.BlockSpec(bias.shape, lambda b, q: (0, 0, 0)),
        ],
        out_specs=pl.BlockSpec((TQ, WIDTH_O), lambda b, q: (b * nq + q, 0)),
        out_shape=jax.ShapeDtypeStruct((batch * seq, WIDTH_O), jnp.bfloat16),
        scratch_shapes=[
            pltpu.VMEM((seq // MOBA_BLOCK, WIDTH_B), jnp.float32),
            pltpu.VMEM((n_mem, WIDTH_M), jnp.bfloat16),
            pltpu.VMEM((WIDTH_M, n_mem), jnp.bfloat16),
            pltpu.VMEM((seq // MOBA_BLOCK, TQ), jnp.float32),
        ],
        compiler_params=pltpu.CompilerParams(dimension_semantics=("parallel", "arbitrary"),
                                             vmem_limit_bytes=VMEM_LIMIT),
        name="attn",
    )(qka, qka, vta, qkb, qkb, vtb, qm, mem2, gmem, wmk, wmvt, bias)


def _mix_kernel(x_ref, o_ref, g_ref, wg_ref, wpa_ref, wpb_ref, wpm_ref, wo_ref, out_ref):
    x = x_ref[...]
    h = _rms(x, g_ref[...]).astype(jnp.bfloat16)
    y = None
    col = 0
    for i, wp_ref in enumerate((wpa_ref, wpb_ref, wpm_ref)):
        width = wp_ref.shape[0]
        gate = jax.nn.sigmoid(jnp.dot(h, wg_ref[:, i * D_MODEL:(i + 1) * D_MODEL],
                                      preferred_element_type=jnp.float32))
        branch = gate * jnp.dot(o_ref[:, col:col + width], wp_ref[...], preferred_element_type=jnp.float32)
        y = branch if y is None else y + branch
        col += width
    out_ref[...] = x + jnp.dot(y.astype(jnp.bfloat16), wo_ref[...], preferred_element_type=jnp.float32)


def _mix_call(x2, o, g, wg, wpa, wpb, wpm, wo):
    t = x2.shape[0]
    tm = TM_MIX
    const = lambda i: (0, 0)
    return pl.pallas_call(
        _mix_kernel,
        grid=(t // tm,),
        in_specs=[
            pl.BlockSpec((tm, D_MODEL), lambda i: (i, 0)),
            pl.BlockSpec((tm, WIDTH_O), lambda i: (i, 0)),
            pl.BlockSpec((1, D_MODEL), const),
            pl.BlockSpec(wg.shape, const),
            pl.BlockSpec(wpa.shape, const),
            pl.BlockSpec(wpb.shape, const),
            pl.BlockSpec(wpm.shape, const),
            pl.BlockSpec(wo.shape, const),
        ],
        out_specs=pl.BlockSpec((tm, D_MODEL), lambda i: (i, 0)),
        out_shape=jax.ShapeDtypeStruct((t, D_MODEL), jnp.float32),
        compiler_params=pltpu.CompilerParams(dimension_semantics=("parallel",), vmem_limit_bytes=VMEM_LIMIT),
        name="gated_mix",
    )(x2, o, g, wg, wpa, wpb, wpm, wo)


def _mlp_kernel(x_ref, g_ref, wup_ref, wdown_ref, gfin_ref, out_ref, *, final_norm):
    x = x_ref[...]
    hm = _rms(x, g_ref[...]).astype(jnp.bfloat16)
    acc = x
    for c in range(D_FF // FF_CHUNK):
        u = jnp.dot(hm, wup_ref[:, c * FF_CHUNK:(c + 1) * FF_CHUNK], preferred_element_type=jnp.float32)
        u = jnp.square(jnp.maximum(u, 0.0)).astype(jnp.bfloat16)
        acc = acc + jnp.dot(u, wdown_ref[c * FF_CHUNK:(c + 1) * FF_CHUNK, :], preferred_element_type=jnp.float32)
    out_ref[...] = _rms(acc, gfin_ref[...]) if final_norm else acc


def _mlp_call(x2, g, wup, wdown, gfin, final_norm):
    t = x2.shape[0]
    tm = TM_MLP
    const = lambda i: (0, 0)
    return pl.pallas_call(
        functools.partial(_mlp_kernel, final_norm=final_norm),
        grid=(t // tm,),
        in_specs=[
            pl.BlockSpec((tm, D_MODEL), lambda i: (i, 0)),
            pl.BlockSpec((1, D_MODEL), const),
            pl.BlockSpec(wup.shape, const),
            pl.BlockSpec(wdown.shape, const),
            pl.BlockSpec((1, D_MODEL), const),
        ],
        out_specs=pl.BlockSpec((tm, D_MODEL), lambda i: (i, 0)),
        out_shape=jax.ShapeDtypeStruct((t, D_MODEL), jnp.float32),
        compiler_params=pltpu.CompilerParams(dimension_semantics=("parallel",), vmem_limit_bytes=VMEM_LIMIT),
        name="mlp",
    )(x2, g, wup, wdown, gfin)


def _rope_tables(seq):
    half = ROT_DIM // 2
    inv_freq = 1.0 / (ROPE_THETA ** (jnp.arange(0, ROT_DIM, 2, dtype=jnp.float32) / ROT_DIM))
    ang = jnp.arange(seq, dtype=jnp.int32).astype(jnp.float32)[:, None] * inv_freq[None, :]
    cos, sin = jnp.cos(ang), jnp.sin(ang)
    ones = jnp.ones((seq, HEAD_DIM - ROT_DIM), jnp.float32)
    zeros_half = jnp.zeros((seq, half), jnp.float32)
    zeros_rest = jnp.zeros((seq, HEAD_DIM - ROT_DIM), jnp.float32)
    c_head = jnp.concatenate([cos, cos, ones], axis=1)
    lo_head = jnp.concatenate([-sin, zeros_half, zeros_rest], axis=1)
    hi_head = jnp.concatenate([zeros_half, sin, zeros_rest], axis=1)
    rep = LANES // HEAD_DIM
    return jnp.tile(c_head, (1, rep)), jnp.tile(lo_head, (1, rep)), jnp.tile(hi_head, (1, rep))


def kernel(x, mem, norm_mix, w_in, w_proj_a, w_proj_b, w_proj_m, w_out, norm_mem, w_mem_kv, norm_mlp, w_up,
           w_down, norm_final):
    batch, seq, d = x.shape
    assert d == D_MODEL and seq % TQ == 0 and seq % TM_QKV == 0 and seq // MOBA_BLOCK == 8
    depth = w_in.shape[0]
    bf = jnp.bfloat16
    x2 = x.reshape(batch * seq, d)
    mem2 = mem.reshape(batch * mem.shape[1], d)
    rope_c, rope_s1, rope_s2 = _rope_tables(seq)
    bias = jnp.asarray(_BIAS_NP)
    o1 = 3 * WIDTH_A
    o2 = o1 + 3 * WIDTH_B
    o3 = o2 + WIDTH_M
    for l in range(depth):
        w = w_in[l]
        wqka = w[:, :2 * WIDTH_A].astype(bf)
        wvta = w[:, 2 * WIDTH_A:o1].T.astype(bf)
        wqkb = w[:, o1:o1 + 2 * WIDTH_B].astype(bf)
        wvtb = w[:, o1 + 2 * WIDTH_B:o2].T.astype(bf)
        wqm = w[:, o2:o3].astype(bf)
        wg = w[:, o3:].astype(bf)
        g_mix = norm_mix[l].reshape(1, d)
        qka, vta, qkb, vtb, qm = _qkv_call(x2, g_mix, wqka, wvta, wqkb, wvtb, wqm, rope_c, rope_s1, rope_s2,
                                           batch, seq)
        wmk = w_mem_kv[l][:, :WIDTH_M].astype(bf)
        wmvt = w_mem_kv[l][:, WIDTH_M:].T.astype(bf)
        o = _attn_call(qka, vta, qkb, vtb, qm, mem2, norm_mem[l].reshape(1, d), wmk, wmvt, bias, batch, seq)
        x2 = _mix_call(x2, o, g_mix, wg, w_proj_a[l].astype(bf), w_proj_b[l].astype(bf), w_proj_m[l].astype(bf),
                       w_out[l].astype(bf))
        x2 = _mlp_call(x2, norm_mlp[l].reshape(1, d), w_up[l].astype(bf), w_down[l].astype(bf),
                       norm_final.reshape(1, d), final_norm=(l == depth - 1))
    return x2.reshape(batch, seq, d)
```

```python
import functools

import jax
import jax.numpy as jnp
import numpy as np
from jax import lax
from jax.experimental import pallas as pl
from jax.experimental.pallas import tpu as pltpu

D_MODEL = 1024
SEQ = 2048
HEAD_DIM = 64
ROT_DIM = HEAD_DIM // 4
ROPE_THETA = 500000.0
DIL_GROUPS = ((128, 1), (512, 4), (2048, 16))
HEADS_PER_DIL_GROUP = 2
N_HEADS_A = len(DIL_GROUPS) * HEADS_PER_DIL_GROUP
N_HEADS_B = 6
N_HEADS_M = 4
MOBA_BLOCK = 256
MOBA_TOPK = 3
N_BLOCKS = SEQ // MOBA_BLOCK
D_FF = 4 * D_MODEL
WIDTH_A = N_HEADS_A * HEAD_DIM
WIDTH_A_OUT = HEADS_PER_DIL_GROUP * HEAD_DIM
WIDTH_B = N_HEADS_B * HEAD_DIM
WIDTH_M = N_HEADS_M * HEAD_DIM
WIDTH_O = WIDTH_A_OUT + WIDTH_B + WIDTH_M
RMS_EPS = 1e-6
NEG_INF = -1e30
Q_SCALE = HEAD_DIM ** -0.5

LANES = 128
PAIR = 2 * HEAD_DIM
TQ = MOBA_BLOCK
TM_QKV = 512
TM_MIX = 512
TM_MLP = 512
FF_CHUNK = 1024
VMEM_LIMIT = 56 * 1024 * 1024

_B_ALLNEG = 0
_B_CAUSAL = 1


def _dil_tile_offsets():
    return tuple(tuple(range(min((w + TQ - 1) // TQ, N_BLOCKS - 1) + 1)) for w, _ in DIL_GROUPS)


def _build_bias_tiles():
    c = np.arange(TQ)[:, None]
    r = np.arange(TQ)[None, :]
    tiles = [np.zeros((TQ, TQ), bool), (r - c) >= 0]
    ids = []
    for (w, d), offs in zip(DIL_GROUPS, _dil_tile_offsets()):
        per_off = []
        for o in offs:
            diff = o * TQ + r - c
            per_off.append((diff >= 0) & (diff <= w) & (diff % d == 0))
        uniq, gid = [], []
        for t in per_off:
            for k, u in enumerate(uniq):
                if np.array_equal(t, u):
                    gid.append(k)
                    break
            else:
                uniq.append(t)
                gid.append(len(uniq) - 1)
        ids.append(tuple(len(tiles) + k for k in gid))
        tiles.extend(uniq)
    bias = np.where(np.stack(tiles), 0.0, NEG_INF).astype(np.float32)
    return bias, tuple(ids)


_BIAS_NP, _DIL_BIAS_IDS = _build_bias_tiles()

_NT = (((1,), (1,)), ((), ()))


def _rms(x, g):
    return x * lax.rsqrt(jnp.mean(x * x, axis=-1, keepdims=True) + RMS_EPS) * g


def _qkv_kernel(x_ref, g_ref, wqka_ref, wvta_ref, wqkb_ref, wvtb_ref, wqm_ref, c_ref, s1_ref, s2_ref,
                qka_ref, vta_ref, qkb_ref, vtb_ref, qm_ref):
    h = _rms(x_ref[...], g_ref[...]).astype(jnp.bfloat16)
    cos = c_ref[...]
    sin_lo = s1_ref[...]
    sin_hi = s2_ref[...]
    for wqk_ref, wvt_ref, qk_ref, vt_ref, width in ((wqka_ref, wvta_ref, qka_ref, vta_ref, WIDTH_A),
                                                    (wqkb_ref, wvtb_ref, qkb_ref, vtb_ref, WIDTH_B)):
        z = jnp.dot(h, wqk_ref[...], preferred_element_type=jnp.float32)
        for blk in range(2 * width // LANES):
            zb = z[:, blk * LANES:(blk + 1) * LANES]
            rb = zb * cos + pltpu.roll(zb, LANES - ROT_DIM // 2, 1) * sin_lo + pltpu.roll(zb, ROT_DIM // 2, 1) * sin_hi
            if blk < width // LANES:
                rb = rb * Q_SCALE
            qk_ref[:, blk * LANES:(blk + 1) * LANES] = rb.astype(jnp.bfloat16)
        vt = lax.dot_general(wvt_ref[...], h, _NT, preferred_element_type=jnp.float32)
        vt_ref[...] = vt.astype(jnp.bfloat16)
    qm = jnp.dot(h, wqm_ref[...], preferred_element_type=jnp.float32) * Q_SCALE
    qm_ref[...] = qm.astype(jnp.bfloat16)


def _qkv_call(x2, g, wqka, wvta, wqkb, wvtb, wqm, rope_c, rope_s1, rope_s2, batch, seq):
    t = x2.shape[0]
    tm = TM_QKV
    nts = seq // tm
    const = lambda i: (0, 0)
    return pl.pallas_call(
        _qkv_kernel,
        grid=(t // tm,),
        in_specs=[
            pl.BlockSpec((tm, D_MODEL), lambda i: (i, 0)),
            pl.BlockSpec((1, D_MODEL), const),
            pl.BlockSpec((D_MODEL, 2 * WIDTH_A), const),
            pl.BlockSpec((WIDTH_A, D_MODEL), const),
            pl.BlockSpec((D_MODEL, 2 * WIDTH_B), const),
            pl.BlockSpec((WIDTH_B, D_MODEL), const),
            pl.BlockSpec((D_MODEL, WIDTH_M), const),
            pl.BlockSpec((tm, LANES), lambda i: (i % nts, 0)),
            pl.BlockSpec((tm, LANES), lambda i: (i % nts, 0)),
            pl.BlockSpec((tm, LANES), lambda i: (i % nts, 0)),
        ],
        out_specs=[
            pl.BlockSpec((tm, 2 * WIDTH_A), lambda i: (i, 0)),
            pl.BlockSpec((WIDTH_A, tm), lambda i: (i // nts, i % nts)),
            pl.BlockSpec((tm, 2 * WIDTH_B), lambda i: (i, 0)),
            pl.BlockSpec((WIDTH_B, tm), lambda i: (i // nts, i % nts)),
            pl.BlockSpec((tm, WIDTH_M), lambda i: (i, 0)),
        ],
        out_shape=[
            jax.ShapeDtypeStruct((t, 2 * WIDTH_A), jnp.bfloat16),
            jax.ShapeDtypeStruct((batch * WIDTH_A, seq), jnp.bfloat16),
            jax.ShapeDtypeStruct((t, 2 * WIDTH_B), jnp.bfloat16),
            jax.ShapeDtypeStruct((batch * WIDTH_B, seq), jnp.bfloat16),
            jax.ShapeDtypeStruct((t, WIDTH_M), jnp.bfloat16),
        ],
        compiler_params=pltpu.CompilerParams(dimension_semantics=("parallel",), vmem_limit_bytes=VMEM_LIMIT),
        name="qkv_proj",
    )(x2, g, wqka, wvta, wqkb, wvtb, wqm, rope_c, rope_s1, rope_s2)


def _pv(v_heads, p_bf):
    return jnp.concatenate([jnp.dot(v, p_bf[:, h * TQ:(h + 1) * TQ], preferred_element_type=jnp.float32)
                            for h, v in enumerate(v_heads)], axis=1)


def _softmax_first(parts):
    m = None
    for s_t, _ in parts:
        part_max = jnp.max(s_t, axis=0, keepdims=True)
        m = part_max if m is None else jnp.maximum(m, part_max)
    l = acc = None
    for s_t, v_heads in parts:
        p = jnp.exp(s_t - m)
        part_l = jnp.sum(p, axis=0, keepdims=True)
        part_acc = _pv(v_heads, p.astype(jnp.bfloat16))
        l = part_l if l is None else l + part_l
        acc = part_acc if acc is None else acc + part_acc
    return m, l, acc


def _softmax_next(carry, s_t, v_heads):
    m, l, acc = carry
    m_new = jnp.maximum(m, jnp.max(s_t, axis=0, keepdims=True))
    alpha = jnp.exp(m - m_new)
    p = jnp.exp(s_t - m_new)
    l = alpha * l + jnp.sum(p, axis=0, keepdims=True)
    acc = alpha * acc + _pv(v_heads, p.astype(jnp.bfloat16))
    return m_new, l, acc


def _scores_t(k_tile, q_heads):
    return lax.dot_general(k_tile, q_heads, _NT, preferred_element_type=jnp.float32)


def _stack_heads(q, n_heads):
    lane = lax.broadcasted_iota(jnp.int32, q.shape, 1)
    zero = jnp.zeros((), q.dtype)
    return jnp.concatenate([jnp.where((lane >= h * HEAD_DIM) & (lane < (h + 1) * HEAD_DIM), q, zero)
                            for h in range(n_heads)], axis=0)


def _head_rows(first_head, n_heads):
    return [slice((first_head + h) * HEAD_DIM, (first_head + h + 1) * HEAD_DIM) for h in range(n_heads)]


def _attn_kernel(qa_ref, ka_ref, vta_ref, qb_ref, kb_ref, vtb_ref, qm_ref, mem_ref, gmem_ref, wmk_ref, wmvt_ref,
                 bias_ref, o_ref, kmean_ref, km_ref, vmt_ref, selb_ref):
    qi = pl.program_id(1)

    @pl.when(qi == 0)
    def _per_batch():
        kb = kb_ref[...].astype(jnp.float32)
        kmean_ref[...] = jnp.mean(kb.reshape(N_BLOCKS, MOBA_BLOCK, WIDTH_B), axis=1)
        mem_n = _rms(mem_ref[...], gmem_ref[...]).astype(jnp.bfloat16)
        km_ref[...] = jnp.dot(mem_n, wmk_ref[...], preferred_element_type=jnp.float32).astype(jnp.bfloat16)
        vmt_ref[...] = lax.dot_general(wmvt_ref[...], mem_n, _NT,
                                       preferred_element_type=jnp.float32).astype(jnp.bfloat16)

    def keys_of(j):
        return pl.ds(pl.multiple_of(j * TQ, TQ), TQ)

    def both_heads(bias):
        return jnp.concatenate([bias, bias], axis=1)

    a_parts = []
    looped = None
    for g, offs in enumerate(_dil_tile_offsets()):
        cols = slice(g * PAIR, (g + 1) * PAIR)
        rows = _head_rows(g * HEADS_PER_DIL_GROUP, HEADS_PER_DIL_GROUP)
        q_heads = _stack_heads(qa_ref[:, cols], HEADS_PER_DIL_GROUP)
        ids = _DIL_BIAS_IDS[g]
        if len(offs) == N_BLOCKS:
            assert looped is None and len(set(ids[1:])) == 1
            looped = (cols, rows, q_heads, ids[1])
            offs = offs[:1]
        for o in offs:
            ks = keys_of(jnp.maximum(qi - o, 0))
            bid = ids[o] if o == 0 else jnp.where(qi >= o, ids[o], _B_ALLNEG)
            s_t = _scores_t(ka_ref[ks, cols], q_heads) + both_heads(bias_ref[bid])
            a_parts.append((s_t, [vta_ref[r, ks] for r in rows]))
    carry_a = _softmax_first(a_parts)

    blk = lax.broadcasted_iota(jnp.int32, (N_BLOCKS, 2 * TQ), 0)
    own = keys_of(qi)
    b_heads = []
    carry_b = []
    for p in range(N_HEADS_B // 2):
        cols = slice(p * PAIR, (p + 1) * PAIR)
        rows = _head_rows(2 * p, 2)
        q_heads = _stack_heads(qb_ref[:, cols], 2)
        kmean = kmean_ref[:, cols]
        kmean_hi = kmean.astype(jnp.bfloat16)
        kmean_lo = (kmean - kmean_hi.astype(jnp.float32)).astype(jnp.bfloat16)
        gate = _scores_t(kmean_hi, q_heads) + _scores_t(kmean_lo, q_heads)
        gate = jnp.where(blk < qi, gate, NEG_INF)
        beaten = jnp.zeros(gate.shape, jnp.float32)
        for j in range(N_BLOCKS):
            gj = gate[j:j + 1, :]
            wins_tie = jnp.where(gj >= gate, 1.0, 0.0)
            wins_strict = jnp.where(gj > gate, 1.0, 0.0)
            beaten = beaten + jnp.where(blk > j, wins_tie, wins_strict)
        selb_ref[p] = jnp.where((beaten < MOBA_TOPK) & (blk < qi), 0.0, NEG_INF)
        s_t = _scores_t(kb_ref[own, cols], q_heads) + both_heads(bias_ref[_B_CAUSAL])
        carry_b.append(_softmax_first([(s_t, [vtb_ref[r, own] for r in rows])]))
        b_heads.append((cols, rows, q_heads))

    def past_tile(j, carries):
        ks = keys_of(j)
        cols, rows, q_heads, bid = looped
        s_t = _scores_t(ka_ref[ks, cols], q_heads) + both_heads(bias_ref[bid])
        out = [_softmax_next(carries[0], s_t, [vta_ref[r, ks] for r in rows])]
        for p, (cols, rows, q_heads) in enumerate(b_heads):
            s_t = _scores_t(kb_ref[ks, cols], q_heads) + selb_ref[p, pl.ds(j, 1), :]
            out.append(_softmax_next(carries[1 + p], s_t, [vtb_ref[r, ks] for r in rows]))
        return tuple(out)

    carries = lax.fori_loop(0, qi, past_tile, (carry_a, *carry_b))

    q_heads = _stack_heads(qm_ref[...], N_HEADS_M)
    carry_m = _softmax_first([(_scores_t(km_ref[...], q_heads), [vmt_ref[r, :] for r in _head_rows(0, N_HEADS_M)])])

    col = 0
    for _, l, acc in (*carries, carry_m):
        o_t = acc / l
        for c in range(o_t.shape[1] // (2 * TQ)):
            pair = jnp.concatenate([o_t[:, (2 * c) * TQ:(2 * c + 1) * TQ],
                                    o_t[:, (2 * c + 1) * TQ:(2 * c + 2) * TQ]], axis=0)
            o_ref[:, col:col + LANES] = pair.T.astype(jnp.bfloat16)
            col += LANES


def _attn_call(qka, vta, qkb, vtb, qm, mem2, gmem, wmk, wmvt, bias, batch, seq):
    nq = seq // TQ
    n_mem = mem2.shape[0] // batch
    const2 = lambda b, q: (0, 0)
    return pl.pallas_call(
        _attn_kernel,
        grid=(batch, nq),
        in_specs=[
            pl.BlockSpec((TQ, WIDTH_A), lambda b, q: (b * nq + q, 0)),
            pl.BlockSpec((seq, WIDTH_A), lambda b, q: (b, 1)),
            pl.BlockSpec((WIDTH_A, seq), lambda b, q: (b, 0)),
            pl.BlockSpec((TQ, WIDTH_B), lambda b, q: (b * nq + q, 0)),
            pl.BlockSpec((seq, WIDTH_B), lambda b, q: (b, 1)),
            pl.BlockSpec((WIDTH_B, seq), lambda b, q: (b, 0)),
            pl.BlockSpec((TQ, WIDTH_M), lambda b, q: (b * nq + q, 0)),
            pl.BlockSpec((n_mem, D_MODEL), lambda b, q: (b, 0)),
            pl.BlockSpec((1, D_MODEL), const2),
            pl.BlockSpec((D_MODEL, WIDTH_M), const2),
            pl.BlockSpec((WIDTH_M, D_MODEL), const2),
            pl.BlockSpec(bias.shape, lambda b, q: (0, 0, 0)),
        ],
        out_specs=pl.BlockSpec((TQ, WIDTH_O), lambda b, q: (b * nq + q, 0)),
        out_shape=jax.ShapeDtypeStruct((batch * seq, WIDTH_O), jnp.bfloat16),
        scratch_shapes=[
            pltpu.VMEM((N_BLOCKS, WIDTH_B), jnp.float32),
            pltpu.VMEM((n_mem, WIDTH_M), jnp.bfloat16),
            pltpu.VMEM((WIDTH_M, n_mem), jnp.bfloat16),
            pltpu.VMEM((N_HEADS_B // 2, N_BLOCKS, 2 * TQ), jnp.float32),
        ],
        compiler_params=pltpu.CompilerParams(dimension_semantics=("parallel", "arbitrary"),
                                             vmem_limit_bytes=VMEM_LIMIT),
        name="attn",
    )(qka, qka, vta, qkb, qkb, vtb, qm, mem2, gmem, wmk, wmvt, bias)


def _mix_kernel(x_ref, o_ref, g_ref, wg_ref, wpa_ref, wpb_ref, wpm_ref, wo_ref, out_ref):
    x = x_ref[...]
    h = _rms(x, g_ref[...]).astype(jnp.bfloat16)
    y = None
    col = 0
    for i, wp_ref in enumerate((wpa_ref, wpb_ref, wpm_ref)):
        width = wp_ref.shape[0]
        gate = jax.nn.sigmoid(jnp.dot(h, wg_ref[:, i * D_MODEL:(i + 1) * D_MODEL],
                                      preferred_element_type=jnp.float32))
        branch = gate * jnp.dot(o_ref[:, col:col + width], wp_ref[...], preferred_element_type=jnp.float32)
        y = branch if y is None else y + branch
        col += width
    out_ref[...] = x + jnp.dot(y.astype(jnp.bfloat16), wo_ref[...], preferred_element_type=jnp.float32)


def _mix_call(x2, o, g, wg, wpa, wpb, wpm, wo):
    t = x2.shape[0]
    tm = TM_MIX
    const = lambda i: (0, 0)
    return pl.pallas_call(
        _mix_kernel,
        grid=(t // tm,),
        in_specs=[
            pl.BlockSpec((tm, D_MODEL), lambda i: (i, 0)),
            pl.BlockSpec((tm, WIDTH_O), lambda i: (i, 0)),
            pl.BlockSpec((1, D_MODEL), const),
            pl.BlockSpec(wg.shape, const),
            pl.BlockSpec(wpa.shape, const),
            pl.BlockSpec(wpb.shape, const),
            pl.BlockSpec(wpm.shape, const),
            pl.BlockSpec(wo.shape, const),
        ],
        out_specs=pl.BlockSpec((tm, D_MODEL), lambda i: (i, 0)),
        out_shape=jax.ShapeDtypeStruct((t, D_MODEL), jnp.float32),
        compiler_params=pltpu.CompilerParams(dimension_semantics=("parallel",), vmem_limit_bytes=VMEM_LIMIT),
        name="gated_mix",
    )(x2, o, g, wg, wpa, wpb, wpm, wo)


def _mlp_kernel(x_ref, g_ref, wup_ref, wdown_ref, gfin_ref, out_ref, *, final_norm):
    x = x_ref[...]
    hm = _rms(x, g_ref[...]).astype(jnp.bfloat16)
    acc = x
    for c in range(D_FF // FF_CHUNK):
        u = jnp.dot(hm, wup_ref[:, c * FF_CHUNK:(c + 1) * FF_CHUNK], preferred_element_type=jnp.float32)
        u = jnp.square(jnp.maximum(u, 0.0)).astype(jnp.bfloat16)
        acc = acc + jnp.dot(u, wdown_ref[c * FF_CHUNK:(c + 1) * FF_CHUNK, :], preferred_element_type=jnp.float32)
    out_ref[...] = _rms(acc, gfin_ref[...]) if final_norm else acc


def _mlp_call(x2, g, wup, wdown, gfin, final_norm):
    t = x2.shape[0]
    tm = TM_MLP
    const = lambda i: (0, 0)
    return pl.pallas_call(
        functools.partial(_mlp_kernel, final_norm=final_norm),
        grid=(t // tm,),
        in_specs=[
            pl.BlockSpec((tm, D_MODEL), lambda i: (i, 0)),
            pl.BlockSpec((1, D_MODEL), const),
            pl.BlockSpec(wup.shape, const),
            pl.BlockSpec(wdown.shape, const),
            pl.BlockSpec((1, D_MODEL), const),
        ],
        out_specs=pl.BlockSpec((tm, D_MODEL), lambda i: (i, 0)),
        out_shape=jax.ShapeDtypeStruct((t, D_MODEL), jnp.float32),
        compiler_params=pltpu.CompilerParams(dimension_semantics=("parallel",), vmem_limit_bytes=VMEM_LIMIT),
        name="mlp",
    )(x2, g, wup, wdown, gfin)


def _rope_tables(seq):
    half = ROT_DIM // 2
    inv_freq = 1.0 / (ROPE_THETA ** (jnp.arange(0, ROT_DIM, 2, dtype=jnp.float32) / ROT_DIM))
    ang = jnp.arange(seq, dtype=jnp.int32).astype(jnp.float32)[:, None] * inv_freq[None, :]
    cos, sin = jnp.cos(ang), jnp.sin(ang)
    ones = jnp.ones((seq, HEAD_DIM - ROT_DIM), jnp.float32)
    zeros_half = jnp.zeros((seq, half), jnp.float32)
    zeros_rest = jnp.zeros((seq, HEAD_DIM - ROT_DIM), jnp.float32)
    c_head = jnp.concatenate([cos, cos, ones], axis=1)
    lo_head = jnp.concatenate([-sin, zeros_half, zeros_rest], axis=1)
    hi_head = jnp.concatenate([zeros_half, sin, zeros_rest], axis=1)
    rep = LANES // HEAD_DIM
    return jnp.tile(c_head, (1, rep)), jnp.tile(lo_head, (1, rep)), jnp.tile(hi_head, (1, rep))


def kernel(x, mem, norm_mix, w_in, w_proj_a, w_proj_b, w_proj_m, w_out, norm_mem, w_mem_kv, norm_mlp, w_up,
           w_down, norm_final):
    batch, seq, d = x.shape
    assert d == D_MODEL and seq == SEQ and seq % TM_QKV == 0
    depth = w_in.shape[0]
    bf = jnp.bfloat16
    x2 = x.reshape(batch * seq, d)
    mem2 = mem.reshape(batch * mem.shape[1], d)
    rope_c, rope_s1, rope_s2 = _rope_tables(seq)
    bias = jnp.asarray(_BIAS_NP)
    o1 = 3 * WIDTH_A
    o2 = o1 + 3 * WIDTH_B
    o3 = o2 + WIDTH_M
    for l in range(depth):
        w = w_in[l]
        wqka = w[:, :2 * WIDTH_A].astype(bf)
        wvta = w[:, 2 * WIDTH_A:o1].T.astype(bf)
        wqkb = w[:, o1:o1 + 2 * WIDTH_B].astype(bf)
        wvtb = w[:, o1 + 2 * WIDTH_B:o2].T.astype(bf)
        wqm = w[:, o2:o3].astype(bf)
        wg = w[:, o3:].astype(bf)
        g_mix = norm_mix[l].reshape(1, d)
        qka, vta, qkb, vtb, qm = _qkv_call(x2, g_mix, wqka, wvta, wqkb, wvtb, wqm, rope_c, rope_s1, rope_s2,
                                           batch, seq)
        wmk = w_mem_kv[l][:, :WIDTH_M].astype(bf)
        wmvt = w_mem_kv[l][:, WIDTH_M:].T.astype(bf)
        o = _attn_call(qka, vta, qkb, vtb, qm, mem2, norm_mem[l].reshape(1, d), wmk, wmvt, bias, batch, seq)
        x2 = _mix_call(x2, o, g_mix, wg, w_proj_a[l].astype(bf), w_proj_b[l].astype(bf), w_proj_m[l].astype(bf),
                       w_out[l].astype(bf))
        x2 = _mlp_call(x2, norm_mlp[l].reshape(1, d), w_up[l].astype(bf), w_down[l].astype(bf),
                       norm_final.reshape(1, d), final_norm=(l == depth - 1))
    return x2.reshape(batch, seq, d)
```

```python
import functools

import jax
import jax.numpy as jnp
import numpy as np
from jax import lax
from jax.experimental import pallas as pl
from jax.experimental.pallas import tpu as pltpu

D_MODEL = 1024
SEQ = 2048
HEAD_DIM = 64
ROT_DIM = HEAD_DIM // 4
ROPE_THETA = 500000.0
DIL_GROUPS = ((128, 1), (512, 4), (2048, 16))
HEADS_PER_DIL_GROUP = 2
N_HEADS_A = len(DIL_GROUPS) * HEADS_PER_DIL_GROUP
N_HEADS_B = 6
N_HEADS_M = 4
MOBA_BLOCK = 256
MOBA_TOPK = 3
N_BLOCKS = SEQ // MOBA_BLOCK
D_FF = 4 * D_MODEL
WIDTH_A = N_HEADS_A * HEAD_DIM
WIDTH_A_OUT = HEADS_PER_DIL_GROUP * HEAD_DIM
WIDTH_B = N_HEADS_B * HEAD_DIM
WIDTH_M = N_HEADS_M * HEAD_DIM
WIDTH_O = WIDTH_A_OUT + WIDTH_B + WIDTH_M
RMS_EPS = 1e-6
NEG_INF = -1e30
Q_SCALE = HEAD_DIM ** -0.5

LANES = 128
PAIR = 2 * HEAD_DIM
TQ = MOBA_BLOCK
TM_QKV = 512
TM_MIX = 512
TM_MLP = 512
FF_CHUNK = 1024
VMEM_LIMIT = 56 * 1024 * 1024

_B_ALLNEG = 0
_B_CAUSAL = 1


def _dil_tile_offsets():
    return tuple(tuple(range(min((w + TQ - 1) // TQ, N_BLOCKS - 1) + 1)) for w, _ in DIL_GROUPS)


def _build_bias_tiles():
    c = np.arange(TQ)[:, None]
    r = np.arange(TQ)[None, :]
    tiles = [np.zeros((TQ, TQ), bool), (r - c) >= 0]
    ids = []
    for (w, d), offs in zip(DIL_GROUPS, _dil_tile_offsets()):
        per_off = []
        for o in offs:
            diff = o * TQ + r - c
            per_off.append((diff >= 0) & (diff <= w) & (diff % d == 0))
        uniq, gid = [], []
        for t in per_off:
            for k, u in enumerate(uniq):
                if np.array_equal(t, u):
                    gid.append(k)
                    break
            else:
                uniq.append(t)
                gid.append(len(uniq) - 1)
        ids.append(tuple(len(tiles) + k for k in gid))
        tiles.extend(uniq)
    bias = np.where(np.stack(tiles), 0.0, NEG_INF).astype(np.float32)
    return bias, tuple(ids)


_BIAS_NP, _DIL_BIAS_IDS = _build_bias_tiles()

_NT = (((1,), (1,)), ((), ()))


def _rms(x, g):
    return x * lax.rsqrt(jnp.mean(x * x, axis=-1, keepdims=True) + RMS_EPS) * g


def _qkv_kernel(x_ref, g_ref, wqka_ref, wvta_ref, wqkb_ref, wvtb_ref, wqm_ref, c_ref, s1_ref, s2_ref,
                qka_ref, vta_ref, qkb_ref, vtb_ref, qm_ref):
    h = _rms(x_ref[...], g_ref[...]).astype(jnp.bfloat16)
    cos = c_ref[...]
    sin_lo = s1_ref[...]
    sin_hi = s2_ref[...]
    for wqk_ref, wvt_ref, qk_ref, vt_ref, width in ((wqka_ref, wvta_ref, qka_ref, vta_ref, WIDTH_A),
                                                    (wqkb_ref, wvtb_ref, qkb_ref, vtb_ref, WIDTH_B)):
        z = jnp.dot(h, wqk_ref[...], preferred_element_type=jnp.float32)
        for blk in range(2 * width // LANES):
            zb = z[:, blk * LANES:(blk + 1) * LANES]
            rb = zb * cos + pltpu.roll(zb, LANES - ROT_DIM // 2, 1) * sin_lo + pltpu.roll(zb, ROT_DIM // 2, 1) * sin_hi
            if blk < width // LANES:
                rb = rb * Q_SCALE
            qk_ref[:, blk * LANES:(blk + 1) * LANES] = rb.astype(jnp.bfloat16)
        vt = lax.dot_general(wvt_ref[...], h, _NT, preferred_element_type=jnp.float32)
        vt_ref[...] = vt.astype(jnp.bfloat16)
    qm = jnp.dot(h, wqm_ref[...], preferred_element_type=jnp.float32) * Q_SCALE
    qm_ref[...] = qm.astype(jnp.bfloat16)


def _qkv_call(x2, g, wqka, wvta, wqkb, wvtb, wqm, rope_c, rope_s1, rope_s2, batch, seq):
    t = x2.shape[0]
    tm = TM_QKV
    nts = seq // tm
    const = lambda i: (0, 0)
    return pl.pallas_call(
        _qkv_kernel,
        grid=(t // tm,),
        in_specs=[
            pl.BlockSpec((tm, D_MODEL), lambda i: (i, 0)),
            pl.BlockSpec((1, D_MODEL), const),
            pl.BlockSpec((D_MODEL, 2 * WIDTH_A), const),
            pl.BlockSpec((WIDTH_A, D_MODEL), const),
            pl.BlockSpec((D_MODEL, 2 * WIDTH_B), const),
            pl.BlockSpec((WIDTH_B, D_MODEL), const),
            pl.BlockSpec((D_MODEL, WIDTH_M), const),
            pl.BlockSpec((tm, LANES), lambda i: (i % nts, 0)),
            pl.BlockSpec((tm, LANES), lambda i: (i % nts, 0)),
            pl.BlockSpec((tm, LANES), lambda i: (i % nts, 0)),
        ],
        out_specs=[
            pl.BlockSpec((tm, 2 * WIDTH_A), lambda i: (i, 0)),
            pl.BlockSpec((WIDTH_A, tm), lambda i: (i // nts, i % nts)),
            pl.BlockSpec((tm, 2 * WIDTH_B), lambda i: (i, 0)),
            pl.BlockSpec((WIDTH_B, tm), lambda i: (i // nts, i % nts)),
            pl.BlockSpec((tm, WIDTH_M), lambda i: (i, 0)),
        ],
        out_shape=[
            jax.ShapeDtypeStruct((t, 2 * WIDTH_A), jnp.bfloat16),
            jax.ShapeDtypeStruct((batch * WIDTH_A, seq), jnp.bfloat16),
            jax.ShapeDtypeStruct((t, 2 * WIDTH_B), jnp.bfloat16),
            jax.ShapeDtypeStruct((batch * WIDTH_B, seq), jnp.bfloat16),
            jax.ShapeDtypeStruct((t, WIDTH_M), jnp.bfloat16),
        ],
        compiler_params=pltpu.CompilerParams(dimension_semantics=("parallel",), vmem_limit_bytes=VMEM_LIMIT),
        name="qkv_proj",
    )(x2, g, wqka, wvta, wqkb, wvtb, wqm, rope_c, rope_s1, rope_s2)


def _pv(v_heads, p_bf):
    return jnp.concatenate([jnp.dot(v, p_bf[:, h * TQ:(h + 1) * TQ], preferred_element_type=jnp.float32)
                            for h, v in enumerate(v_heads)], axis=1)


def _softmax_first(parts):
    m = None
    for s_t, _ in parts:
        part_max = jnp.max(s_t, axis=0, keepdims=True)
        m = part_max if m is None else jnp.maximum(m, part_max)
    l = acc = None
    for s_t, v_heads in parts:
        p = jnp.exp(s_t - m)
        part_l = jnp.sum(p, axis=0, keepdims=True)
        part_acc = _pv(v_heads, p.astype(jnp.bfloat16))
        l = part_l if l is None else l + part_l
        acc = part_acc if acc is None else acc + part_acc
    return m, l, acc


def _softmax_next(carry, s_t, v_heads, issue_first=None):
    m, l, acc = carry
    m_new = jnp.maximum(m, jnp.max(s_t, axis=0, keepdims=True))
    alpha = jnp.exp(m - m_new)
    p = jnp.exp(s_t - m_new)
    l = alpha * l + jnp.sum(p, axis=0, keepdims=True)
    if issue_first is not None:
        l = l + 0.0 * issue_first[-1:, :]
    acc = alpha * acc + _pv(v_heads, p.astype(jnp.bfloat16))
    return m_new, l, acc


def _scores_t(k_tile, q_heads):
    return lax.dot_general(k_tile, q_heads, _NT, preferred_element_type=jnp.float32)


def _stack_heads(q, n_heads):
    lane = lax.broadcasted_iota(jnp.int32, q.shape, 1)
    zero = jnp.zeros((), q.dtype)
    return jnp.concatenate([jnp.where((lane >= h * HEAD_DIM) & (lane < (h + 1) * HEAD_DIM), q, zero)
                            for h in range(n_heads)], axis=0)


def _head_rows(first_head, n_heads):
    return [slice((first_head + h) * HEAD_DIM, (first_head + h + 1) * HEAD_DIM) for h in range(n_heads)]


def _attn_kernel(qa_ref, ka_ref, vta_ref, qb_ref, kb_ref, vtb_ref, qm_ref, mem_ref, gmem_ref, wmk_ref, wmvt_ref,
                 bias_ref, o_ref, kmean_ref, km_ref, vmt_ref, selb_ref):
    qi = pl.program_id(1)

    @pl.when(qi == 0)
    def _per_batch():
        kb = kb_ref[...].astype(jnp.float32)
        kmean_ref[...] = jnp.mean(kb.reshape(N_BLOCKS, MOBA_BLOCK, WIDTH_B), axis=1)
        mem_n = _rms(mem_ref[...], gmem_ref[...]).astype(jnp.bfloat16)
        km_ref[...] = jnp.dot(mem_n, wmk_ref[...], preferred_element_type=jnp.float32).astype(jnp.bfloat16)
        vmt_ref[...] = lax.dot_general(wmvt_ref[...], mem_n, _NT,
                                       preferred_element_type=jnp.float32).astype(jnp.bfloat16)

    def keys_of(j):
        return pl.ds(pl.multiple_of(j * TQ, TQ), TQ)

    def both_heads(bias):
        return jnp.concatenate([bias, bias], axis=1)

    a_static = []
    looped = None
    for g, offs in enumerate(_dil_tile_offsets()):
        cols = slice(g * PAIR, (g + 1) * PAIR)
        rows = _head_rows(g * HEADS_PER_DIL_GROUP, HEADS_PER_DIL_GROUP)
        q_heads = _stack_heads(qa_ref[:, cols], HEADS_PER_DIL_GROUP)
        ids = _DIL_BIAS_IDS[g]
        if len(offs) == N_BLOCKS:
            assert looped is None and len(set(ids[1:])) == 1
            looped = (cols, rows, q_heads, ids[1])
            offs = offs[:1]
        for o in offs:
            ks = keys_of(jnp.maximum(qi - o, 0))
            bid = ids[o] if o == 0 else jnp.where(qi >= o, ids[o], _B_ALLNEG)
            a_static.append((ks, bid, cols, rows, q_heads))

    blk = lax.broadcasted_iota(jnp.int32, (N_BLOCKS, 2 * TQ), 0)
    own = keys_of(qi)
    b_heads = []
    for p in range(N_HEADS_B // 2):
        cols = slice(p * PAIR, (p + 1) * PAIR)
        q_heads = _stack_heads(qb_ref[:, cols], 2)
        kmean = kmean_ref[:, cols]
        kmean_hi = kmean.astype(jnp.bfloat16)
        kmean_lo = (kmean - kmean_hi.astype(jnp.float32)).astype(jnp.bfloat16)
        gate = _scores_t(kmean_hi, q_heads) + _scores_t(kmean_lo, q_heads)
        gate = jnp.where(blk < qi, gate, NEG_INF)
        beaten = jnp.zeros(gate.shape, jnp.float32)
        for j in range(N_BLOCKS):
            gj = gate[j:j + 1, :]
            wins_tie = jnp.where(gj >= gate, 1.0, 0.0)
            wins_strict = jnp.where(gj > gate, 1.0, 0.0)
            beaten = beaten + jnp.where(blk > j, wins_tie, wins_strict)
        selb_ref[p] = jnp.where((beaten < MOBA_TOPK) & (blk < qi), 0.0, NEG_INF)
        b_heads.append((cols, _head_rows(2 * p, 2), q_heads))

    def a_scores(part):
        ks, _, cols, _, q_heads = part
        return _scores_t(ka_ref[ks, cols], q_heads)

    def b_scores(p, ks):
        cols, _, q_heads = b_heads[p]
        return _scores_t(kb_ref[ks, cols], q_heads)

    def looped_scores(j):
        cols, _, q_heads, _ = looped
        return _scores_t(ka_ref[keys_of(j), cols], q_heads)

    n_a = len(a_static)
    raw = a_scores(a_static[0])
    a_biased = []
    m_a = None
    for i in range(n_a):
        nxt = a_scores(a_static[i + 1]) if i + 1 < n_a else b_scores(0, own)
        s_t = raw + both_heads(bias_ref[a_static[i][1]])
        part_max = jnp.max(s_t, axis=0, keepdims=True) + 0.0 * nxt[-1:, :]
        m_a = part_max if m_a is None else jnp.maximum(m_a, part_max)
        a_biased.append(s_t)
        raw = nxt
    m_heads = _stack_heads(qm_ref[...], N_HEADS_M)
    ahead = [lambda: b_scores(1, own), lambda: b_scores(2, own), lambda: _scores_t(km_ref[...], m_heads),
             lambda: looped_scores(0)]
    later = [raw]
    l_a = acc_a = None
    for i in range(n_a):
        ks, _, _, rows, _ = a_static[i]
        p_t = jnp.exp(a_biased[i] - m_a)
        part_l = jnp.sum(p_t, axis=0, keepdims=True)
        if i < len(ahead):
            later.append(ahead[i]())
            part_l = part_l + 0.0 * later[-1][-1:, :part_l.shape[1]]
        part_acc = _pv([vta_ref[r, ks] for r in rows], p_t.astype(jnp.bfloat16))
        l_a = part_l if l_a is None else l_a + part_l
        acc_a = part_acc if acc_a is None else acc_a + part_acc
    assert n_a >= len(ahead)
    carry_a = (m_a, l_a, acc_a)

    carry_b = []
    for p, (_, rows, _) in enumerate(b_heads):
        s_t = later[p] + both_heads(bias_ref[_B_CAUSAL])
        carry_b.append(_softmax_first([(s_t, [vtb_ref[r, own] for r in rows])]))
    carry_m = _softmax_first([(later[3], [vmt_ref[r, :] for r in _head_rows(0, N_HEADS_M)])])

    def past_tile(j, state):
        carries, raw_a = state
        ks = keys_of(j)
        _, rows_a, _, bid = looped
        out = []
        raw_b = b_scores(0, ks)
        out.append(_softmax_next(carries[0], raw_a + both_heads(bias_ref[bid]), [vta_ref[r, ks] for r in rows_a],
                                 issue_first=raw_b))
        for p, (_, rows, _) in enumerate(b_heads):
            nxt = b_scores(p + 1, ks) if p + 1 < len(b_heads) else looped_scores(jnp.minimum(j + 1, qi - 1))
            s_t = raw_b + selb_ref[p, pl.ds(j, 1), :]
            out.append(_softmax_next(carries[1 + p], s_t, [vtb_ref[r, ks] for r in rows], issue_first=nxt))
            raw_b = nxt
        return tuple(out), raw_b

    carries, _ = lax.fori_loop(0, qi, past_tile, ((carry_a, *carry_b), later[4]))

    col = 0
    for _, l, acc in (*carries, carry_m):
        o_t = acc / l
        for c in range(o_t.shape[1] // (2 * TQ)):
            pair = jnp.concatenate([o_t[:, (2 * c) * TQ:(2 * c + 1) * TQ],
                                    o_t[:, (2 * c + 1) * TQ:(2 * c + 2) * TQ]], axis=0)
            o_ref[:, col:col + LANES] = pair.T.astype(jnp.bfloat16)
            col += LANES


def _attn_call(qka, vta, qkb, vtb, qm, mem2, gmem, wmk, wmvt, bias, batch, seq):
    nq = seq // TQ
    n_mem = mem2.shape[0] // batch
    const2 = lambda b, q: (0, 0)
    return pl.pallas_call(
        _attn_kernel,
        grid=(batch, nq),
        in_specs=[
            pl.BlockSpec((TQ, WIDTH_A), lambda b, q: (b * nq + q, 0)),
            pl.BlockSpec((seq, WIDTH_A), lambda b, q: (b, 1)),
            pl.BlockSpec((WIDTH_A, seq), lambda b, q: (b, 0)),
            pl.BlockSpec((TQ, WIDTH_B), lambda b, q: (b * nq + q, 0)),
            pl.BlockSpec((seq, WIDTH_B), lambda b, q: (b, 1)),
            pl.BlockSpec((WIDTH_B, seq), lambda b, q: (b, 0)),
            pl.BlockSpec((TQ, WIDTH_M), lambda b, q: (b * nq + q, 0)),
            pl.BlockSpec((n_mem, D_MODEL), lambda b, q: (b, 0)),
            pl.BlockSpec((1, D_MODEL), const2),
            pl.BlockSpec((D_MODEL, WIDTH_M), const2),
            pl.BlockSpec((WIDTH_M, D_MODEL), const2),
            pl.BlockSpec(bias.shape, lambda b, q: (0, 0, 0)),
        ],
        out_specs=pl.BlockSpec((TQ, WIDTH_O), lambda b, q: (b * nq + q, 0)),
        out_shape=jax.ShapeDtypeStruct((batch * seq, WIDTH_O), jnp.bfloat16),
        scratch_shapes=[
            pltpu.VMEM((N_BLOCKS, WIDTH_B), jnp.float32),
            pltpu.VMEM((n_mem, WIDTH_M), jnp.bfloat16),
            pltpu.VMEM((WIDTH_M, n_mem), jnp.bfloat16),
            pltpu.VMEM((N_HEADS_B // 2, N_BLOCKS, 2 * TQ), jnp.float32),
        ],
        compiler_params=pltpu.CompilerParams(dimension_semantics=("parallel", "arbitrary"),
                                             vmem_limit_bytes=VMEM_LIMIT),
        name="attn",
    )(qka, qka, vta, qkb, qkb, vtb, qm, mem2, gmem, wmk, wmvt, bias)


def _mix_kernel(x_ref, o_ref, g_ref, wg_ref, wpa_ref, wpb_ref, wpm_ref, wo_ref, out_ref):
    x = x_ref[...]
    h = _rms(x, g_ref[...]).astype(jnp.bfloat16)
    y = None
    col = 0
    for i, wp_ref in enumerate((wpa_ref, wpb_ref, wpm_ref)):
        width = wp_ref.shape[0]
        gate = jax.nn.sigmoid(jnp.dot(h, wg_ref[:, i * D_MODEL:(i + 1) * D_MODEL],
                                      preferred_element_type=jnp.float32))
        branch = gate * jnp.dot(o_ref[:, col:col + width], wp_ref[...], preferred_element_type=jnp.float32)
        y = branch if y is None else y + branch
        col += width
    out_ref[...] = x + jnp.dot(y.astype(jnp.bfloat16), wo_ref[...], preferred_element_type=jnp.float32)


def _mix_call(x2, o, g, wg, wpa, wpb, wpm, wo):
    t = x2.shape[0]
    tm = TM_MIX
    const = lambda i: (0, 0)
    return pl.pallas_call(
        _mix_kernel,
        grid=(t // tm,),
        in_specs=[
            pl.BlockSpec((tm, D_MODEL), lambda i: (i, 0)),
            pl.BlockSpec((tm, WIDTH_O), lambda i: (i, 0)),
            pl.BlockSpec((1, D_MODEL), const),
            pl.BlockSpec(wg.shape, const),
            pl.BlockSpec(wpa.shape, const),
            pl.BlockSpec(wpb.shape, const),
            pl.BlockSpec(wpm.shape, const),
            pl.BlockSpec(wo.shape, const),
        ],
        out_specs=pl.BlockSpec((tm, D_MODEL), lambda i: (i, 0)),
        out_shape=jax.ShapeDtypeStruct((t, D_MODEL), jnp.float32),
        compiler_params=pltpu.CompilerParams(dimension_semantics=("parallel",), vmem_limit_bytes=VMEM_LIMIT),
        name="gated_mix",
    )(x2, o, g, wg, wpa, wpb, wpm, wo)


def _mlp_kernel(x_ref, g_ref, wup_ref, wdown_ref, gfin_ref, out_ref, *, final_norm):
    x = x_ref[...]
    hm = _rms(x, g_ref[...]).astype(jnp.bfloat16)
    acc = x
    for c in range(D_FF // FF_CHUNK):
        u = jnp.dot(hm, wup_ref[:, c * FF_CHUNK:(c + 1) * FF_CHUNK], preferred_element_type=jnp.float32)
        u = jnp.square(jnp.maximum(u, 0.0)).astype(jnp.bfloat16)
        acc = acc + jnp.dot(u, wdown_ref[c * FF_CHUNK:(c + 1) * FF_CHUNK, :], preferred_element_type=jnp.float32)
    out_ref[...] = _rms(acc, gfin_ref[...]) if final_norm else acc


def _mlp_call(x2, g, wup, wdown, gfin, final_norm):
    t = x2.shape[0]
    tm = TM_MLP
    const = lambda i: (0, 0)
    return pl.pallas_call(
        functools.partial(_mlp_kernel, final_norm=final_norm),
        grid=(t // tm,),
        in_specs=[
            pl.BlockSpec((tm, D_MODEL), lambda i: (i, 0)),
            pl.BlockSpec((1, D_MODEL), const),
            pl.BlockSpec(wup.shape, const),
            pl.BlockSpec(wdown.shape, const),
            pl.BlockSpec((1, D_MODEL), const),
        ],
        out_specs=pl.BlockSpec((tm, D_MODEL), lambda i: (i, 0)),
        out_shape=jax.ShapeDtypeStruct((t, D_MODEL), jnp.float32),
        compiler_params=pltpu.CompilerParams(dimension_semantics=("parallel",), vmem_limit_bytes=VMEM_LIMIT),
        name="mlp",
    )(x2, g, wup, wdown, gfin)


def _rope_tables(seq):
    half = ROT_DIM // 2
    inv_freq = 1.0 / (ROPE_THETA ** (jnp.arange(0, ROT_DIM, 2, dtype=jnp.float32) / ROT_DIM))
    ang = jnp.arange(seq, dtype=jnp.int32).astype(jnp.float32)[:, None] * inv_freq[None, :]
    cos, sin = jnp.cos(ang), jnp.sin(ang)
    ones = jnp.ones((seq, HEAD_DIM - ROT_DIM), jnp.float32)
    zeros_half = jnp.zeros((seq, half), jnp.float32)
    zeros_rest = jnp.zeros((seq, HEAD_DIM - ROT_DIM), jnp.float32)
    c_head = jnp.concatenate([cos, cos, ones], axis=1)
    lo_head = jnp.concatenate([-sin, zeros_half, zeros_rest], axis=1)
    hi_head = jnp.concatenate([zeros_half, sin, zeros_rest], axis=1)
    rep = LANES // HEAD_DIM
    return jnp.tile(c_head, (1, rep)), jnp.tile(lo_head, (1, rep)), jnp.tile(hi_head, (1, rep))


def kernel(x, mem, norm_mix, w_in, w_proj_a, w_proj_b, w_proj_m, w_out, norm_mem, w_mem_kv, norm_mlp, w_up,
           w_down, norm_final):
    batch, seq, d = x.shape
    assert d == D_MODEL and seq == SEQ and seq % TM_QKV == 0
    depth = w_in.shape[0]
    bf = jnp.bfloat16
    x2 = x.reshape(batch * seq, d)
    mem2 = mem.reshape(batch * mem.shape[1], d)
    rope_c, rope_s1, rope_s2 = _rope_tables(seq)
    bias = jnp.asarray(_BIAS_NP)
    o1 = 3 * WIDTH_A
    o2 = o1 + 3 * WIDTH_B
    o3 = o2 + WIDTH_M
    for l in range(depth):
        w = w_in[l]
        wqka = w[:, :2 * WIDTH_A].astype(bf)
        wvta = w[:, 2 * WIDTH_A:o1].T.astype(bf)
        wqkb = w[:, o1:o1 + 2 * WIDTH_B].astype(bf)
        wvtb = w[:, o1 + 2 * WIDTH_B:o2].T.astype(bf)
        wqm = w[:, o2:o3].astype(bf)
        wg = w[:, o3:].astype(bf)
        g_mix = norm_mix[l].reshape(1, d)
        qka, vta, qkb, vtb, qm = _qkv_call(x2, g_mix, wqka, wvta, wqkb, wvtb, wqm, rope_c, rope_s1, rope_s2,
                                           batch, seq)
        wmk = w_mem_kv[l][:, :WIDTH_M].astype(bf)
        wmvt = w_mem_kv[l][:, WIDTH_M:].T.astype(bf)
        o = _attn_call(qka, vta, qkb, vtb, qm, mem2, norm_mem[l].reshape(1, d), wmk, wmvt, bias, batch, seq)
        x2 = _mix_call(x2, o, g_mix, wg, w_proj_a[l].astype(bf), w_proj_b[l].astype(bf), w_proj_m[l].astype(bf),
                       w_out[l].astype(bf))
        x2 = _mlp_call(x2, norm_mlp[l].reshape(1, d), w_up[l].astype(bf), w_down[l].astype(bf),
                       norm_final.reshape(1, d), final_norm=(l == depth - 1))
    return x2.reshape(batch, seq, d)
```

```python
import functools

import jax
import jax.numpy as jnp
import numpy as np
from jax import lax
from jax.experimental import pallas as pl
from jax.experimental.pallas import tpu as pltpu

D_MODEL = 1024
SEQ = 2048
HEAD_DIM = 64
ROT_DIM = HEAD_DIM // 4
ROPE_THETA = 500000.0
DIL_GROUPS = ((128, 1), (512, 4), (2048, 16))
HEADS_PER_DIL_GROUP = 2
N_HEADS_A = len(DIL_GROUPS) * HEADS_PER_DIL_GROUP
N_HEADS_B = 6
N_HEADS_M = 4
MOBA_BLOCK = 256
MOBA_TOPK = 3
N_BLOCKS = SEQ // MOBA_BLOCK
D_FF = 4 * D_MODEL
WIDTH_A = N_HEADS_A * HEAD_DIM
WIDTH_A_OUT = HEADS_PER_DIL_GROUP * HEAD_DIM
WIDTH_B = N_HEADS_B * HEAD_DIM
WIDTH_M = N_HEADS_M * HEAD_DIM
WIDTH_O = WIDTH_A_OUT + WIDTH_B + WIDTH_M
RMS_EPS = 1e-6
NEG_INF = -1e30
Q_SCALE = HEAD_DIM ** -0.5
LOG2_E = 1.4426950408889634

LANES = 128
PAIR = 2 * HEAD_DIM
SUM_ROWS = 16
TQ = MOBA_BLOCK
TM_QKV = 512
TM_MIX = 512
TM_MLP = 512
FF_CHUNK = 1024
VMEM_LIMIT = 56 * 1024 * 1024

_B_ALLNEG = 0
_B_CAUSAL = 1


def _dil_tile_offsets():
    return tuple(tuple(range(min((w + TQ - 1) // TQ, N_BLOCKS - 1) + 1)) for w, _ in DIL_GROUPS)


def _build_bias_tiles():
    c = np.arange(TQ)[:, None]
    r = np.arange(TQ)[None, :]
    tiles = [np.zeros((TQ, TQ), bool), (r - c) >= 0]
    ids = []
    for (w, d), offs in zip(DIL_GROUPS, _dil_tile_offsets()):
        per_off = []
        for o in offs:
            diff = o * TQ + r - c
            per_off.append((diff >= 0) & (diff <= w) & (diff % d == 0))
        uniq, gid = [], []
        for t in per_off:
            for k, u in enumerate(uniq):
                if np.array_equal(t, u):
                    gid.append(k)
                    break
            else:
                uniq.append(t)
                gid.append(len(uniq) - 1)
        ids.append(tuple(len(tiles) + k for k in gid))
        tiles.extend(uniq)
    bias = np.where(np.stack(tiles), 0.0, NEG_INF).astype(np.float32)
    return bias, tuple(ids)


_BIAS_NP, _DIL_BIAS_IDS = _build_bias_tiles()

_NT = (((1,), (1,)), ((), ()))


def _rms(x, g):
    return x * lax.rsqrt(jnp.mean(x * x, axis=-1, keepdims=True) + RMS_EPS) * g


def _qkv_kernel(x_ref, g_ref, wqka_ref, wvta_ref, wqkb_ref, wvtb_ref, wqm_ref, c_ref, s1_ref, s2_ref,
                qka_ref, vta_ref, qkb_ref, vtb_ref, qm_ref):
    h = _rms(x_ref[...], g_ref[...]).astype(jnp.bfloat16)
    cos = c_ref[...]
    sin_lo = s1_ref[...]
    sin_hi = s2_ref[...]
    for wqk_ref, wvt_ref, qk_ref, vt_ref, width in ((wqka_ref, wvta_ref, qka_ref, vta_ref, WIDTH_A),
                                                    (wqkb_ref, wvtb_ref, qkb_ref, vtb_ref, WIDTH_B)):
        z = jnp.dot(h, wqk_ref[...], preferred_element_type=jnp.float32)
        for blk in range(2 * width // LANES):
            zb = z[:, blk * LANES:(blk + 1) * LANES]
            rb = zb * cos + pltpu.roll(zb, LANES - ROT_DIM // 2, 1) * sin_lo + pltpu.roll(zb, ROT_DIM // 2, 1) * sin_hi
            if blk < width // LANES:
                rb = rb * (Q_SCALE * LOG2_E)
            qk_ref[:, blk * LANES:(blk + 1) * LANES] = rb.astype(jnp.bfloat16)
        vt = lax.dot_general(wvt_ref[...], h, _NT, preferred_element_type=jnp.float32)
        vt_ref[...] = vt.astype(jnp.bfloat16)
    qm = jnp.dot(h, wqm_ref[...], preferred_element_type=jnp.float32) * (Q_SCALE * LOG2_E)
    qm_ref[...] = qm.astype(jnp.bfloat16)


def _qkv_call(x2, g, wqka, wvta, wqkb, wvtb, wqm, rope_c, rope_s1, rope_s2, batch, seq):
    t = x2.shape[0]
    tm = TM_QKV
    nts = seq // tm
    const = lambda i: (0, 0)
    return pl.pallas_call(
        _qkv_kernel,
        grid=(t // tm,),
        in_specs=[
            pl.BlockSpec((tm, D_MODEL), lambda i: (i, 0)),
            pl.BlockSpec((1, D_MODEL), const),
            pl.BlockSpec((D_MODEL, 2 * WIDTH_A), const),
            pl.BlockSpec((WIDTH_A, D_MODEL), const),
            pl.BlockSpec((D_MODEL, 2 * WIDTH_B), const),
            pl.BlockSpec((WIDTH_B, D_MODEL), const),
            pl.BlockSpec((D_MODEL, WIDTH_M), const),
            pl.BlockSpec((tm, LANES), lambda i: (i % nts, 0)),
            pl.BlockSpec((tm, LANES), lambda i: (i % nts, 0)),
            pl.BlockSpec((tm, LANES), lambda i: (i % nts, 0)),
        ],
        out_specs=[
            pl.BlockSpec((tm, 2 * WIDTH_A), lambda i: (i, 0)),
            pl.BlockSpec((WIDTH_A, tm), lambda i: (i // nts, i % nts)),
            pl.BlockSpec((tm, 2 * WIDTH_B), lambda i: (i, 0)),
            pl.BlockSpec((WIDTH_B, tm), lambda i: (i // nts, i % nts)),
            pl.BlockSpec((tm, WIDTH_M), lambda i: (i, 0)),
        ],
        out_shape=[
            jax.ShapeDtypeStruct((t, 2 * WIDTH_A), jnp.bfloat16),
            jax.ShapeDtypeStruct((batch * WIDTH_A, seq), jnp.bfloat16),
            jax.ShapeDtypeStruct((t, 2 * WIDTH_B), jnp.bfloat16),
            jax.ShapeDtypeStruct((batch * WIDTH_B, seq), jnp.bfloat16),
            jax.ShapeDtypeStruct((t, WIDTH_M), jnp.bfloat16),
        ],
        compiler_params=pltpu.CompilerParams(dimension_semantics=("parallel",), vmem_limit_bytes=VMEM_LIMIT),
        name="qkv_proj",
    )(x2, g, wqka, wvta, wqkb, wvtb, wqm, rope_c, rope_s1, rope_s2)


def _pv(v_heads, p_bf):
    ones = jnp.ones((SUM_ROWS, p_bf.shape[0]), jnp.bfloat16)
    return jnp.concatenate([jnp.dot(jnp.concatenate([v, ones], axis=0), p_bf[:, h * TQ:(h + 1) * TQ],
                                    preferred_element_type=jnp.float32)
                            for h, v in enumerate(v_heads)], axis=1)


def _after(x, *matmul_results):
    for r in matmul_results:
        bits = lax.bitcast_convert_type(r[-1:, :x.shape[1]], jnp.uint32)
        bits = lax.shift_right_logical(lax.shift_right_logical(bits, jnp.uint32(16)), jnp.uint32(16))
        x = x + lax.bitcast_convert_type(bits, jnp.float32)
    return x


def _scores_t(k_tile, q_heads):
    return lax.dot_general(k_tile, q_heads, _NT, preferred_element_type=jnp.float32)


def _stack_heads(q, n_heads):
    lane = lax.broadcasted_iota(jnp.int32, q.shape, 1)
    zero = jnp.zeros((), q.dtype)
    return jnp.concatenate([jnp.where((lane >= h * HEAD_DIM) & (lane < (h + 1) * HEAD_DIM), q, zero)
                            for h in range(n_heads)], axis=0)


def _head_rows(first_head, n_heads):
    return [slice((first_head + h) * HEAD_DIM, (first_head + h + 1) * HEAD_DIM) for h in range(n_heads)]


def _attn_kernel(qa_ref, ka_ref, vta_ref, qb_ref, kb_ref, vtb_ref, qm_ref, mem_ref, gmem_ref, wmk_ref, wmvt_ref,
                 bias_ref, o_ref, kmean_ref, km_ref, vmt_ref, selb_ref, m_ref, acc_ref, raw_ref):
    qi = pl.program_id(1)

    @pl.when(qi == 0)
    def _per_batch():
        kb = kb_ref[...].astype(jnp.float32)
        kmean_ref[...] = jnp.mean(kb.reshape(N_BLOCKS, MOBA_BLOCK, WIDTH_B), axis=1)
        mem_n = _rms(mem_ref[...], gmem_ref[...]).astype(jnp.bfloat16)
        km_ref[...] = jnp.dot(mem_n, wmk_ref[...], preferred_element_type=jnp.float32).astype(jnp.bfloat16)
        vmt_ref[...] = lax.dot_general(wmvt_ref[...], mem_n, _NT,
                                       preferred_element_type=jnp.float32).astype(jnp.bfloat16)

    def keys_of(j):
        return pl.ds(pl.multiple_of(j * TQ, TQ), TQ)

    def both_heads(bias):
        return jnp.concatenate([bias, bias], axis=1)

    a_static = []
    looped = None
    for g, offs in enumerate(_dil_tile_offsets()):
        cols = slice(g * PAIR, (g + 1) * PAIR)
        rows = _head_rows(g * HEADS_PER_DIL_GROUP, HEADS_PER_DIL_GROUP)
        q_heads = _stack_heads(qa_ref[:, cols], HEADS_PER_DIL_GROUP)
        ids = _DIL_BIAS_IDS[g]
        if len(offs) == N_BLOCKS:
            assert looped is None and len(set(ids[1:])) == 1
            looped = (cols, rows, q_heads, ids[1])
            offs = offs[:1]
        for o in offs:
            ks = keys_of(jnp.maximum(qi - o, 0))
            bid = ids[o] if o == 0 else jnp.where(qi >= o, ids[o], _B_ALLNEG)
            a_static.append((ks, bid, cols, rows, q_heads))

    blk = lax.broadcasted_iota(jnp.int32, (N_BLOCKS, 2 * TQ), 0)
    own = keys_of(qi)
    b_heads = []
    for p in range(N_HEADS_B // 2):
        cols = slice(p * PAIR, (p + 1) * PAIR)
        q_heads = _stack_heads(qb_ref[:, cols], 2)
        kmean = kmean_ref[:, cols]
        kmean_hi = kmean.astype(jnp.bfloat16)
        kmean_lo = (kmean - kmean_hi.astype(jnp.float32)).astype(jnp.bfloat16)
        gate = _scores_t(kmean_hi, q_heads) + _scores_t(kmean_lo, q_heads)
        gate = jnp.where(blk < qi, gate, NEG_INF)
        beaten = jnp.zeros(gate.shape, jnp.float32)
        for j in range(N_BLOCKS):
            gj = gate[j:j + 1, :]
            wins_tie = jnp.where(gj >= gate, 1.0, 0.0)
            wins_strict = jnp.where(gj > gate, 1.0, 0.0)
            beaten = beaten + jnp.where(blk > j, wins_tie, wins_strict)
        selb_ref[p] = jnp.where((beaten < MOBA_TOPK) & (blk < qi), 0.0, NEG_INF)
        b_heads.append((cols, _head_rows(2 * p, 2), q_heads))

    def a_scores(part):
        ks, _, cols, _, q_heads = part
        return _scores_t(ka_ref[ks, cols], q_heads)

    def b_scores(p, ks):
        cols, _, q_heads = b_heads[p]
        return _scores_t(kb_ref[ks, cols], q_heads)

    def looped_scores(j):
        cols, _, q_heads, _ = looped
        return _scores_t(ka_ref[keys_of(j), cols], q_heads)

    n_a = len(a_static)
    raw = a_scores(a_static[0])
    a_biased = []
    m_a = None
    for i in range(n_a):
        nxt = a_scores(a_static[i + 1]) if i + 1 < n_a else b_scores(0, own)
        s_t = raw + both_heads(bias_ref[a_static[i][1]])
        part_max = _after(jnp.max(s_t, axis=0, keepdims=True), nxt)
        m_a = part_max if m_a is None else jnp.maximum(m_a, part_max)
        a_biased.append(s_t)
        raw = nxt
    m_heads = _stack_heads(qm_ref[...], N_HEADS_M)
    ahead = [lambda: b_scores(1, own), lambda: b_scores(2, own), lambda: _scores_t(km_ref[...], m_heads),
             lambda: looped_scores(0)]
    later = [raw]
    assert n_a > len(ahead)
    neg_m = -m_a
    acc_a = None
    for i in range(n_a):
        ks, _, _, rows, _ = a_static[i]
        p_t = jnp.exp2(a_biased[i] + neg_m).astype(jnp.bfloat16)
        if i < len(ahead):
            later.append(ahead[i]())
            neg_m = _after(neg_m, later[-1])
        part_acc = _pv([vta_ref[r, ks] for r in rows], p_t)
        acc_a = part_acc if acc_a is None else acc_a + part_acc
    m_ref[0] = m_a
    acc_ref[0] = acc_a

    def single_tile(s_t, v_heads):
        m = jnp.max(s_t, axis=0, keepdims=True)
        return m, _pv(v_heads, jnp.exp2(s_t - m).astype(jnp.bfloat16))

    for p, (_, rows, _) in enumerate(b_heads):
        m_ref[1 + p], acc_ref[1 + p] = single_tile(later[p] + both_heads(bias_ref[_B_CAUSAL]),
                                                   [vtb_ref[r, own] for r in rows])
    _, acc_m = single_tile(later[3], [vmt_ref[r, :] for r in _head_rows(0, N_HEADS_M)])
    raw_ref[...] = later[4]

    def online_update(c, s_t, shift_row, m_tile, v_heads, next_scores):
        m_old = m_ref[c]
        m_new = jnp.maximum(m_old, m_tile)
        alpha = _after(jnp.exp2(m_old - m_new), next_scores)
        shift = -m_new if shift_row is None else shift_row - m_new
        acc_ref[c] = alpha * acc_ref[c] + _pv(v_heads, jnp.exp2(s_t + shift).astype(jnp.bfloat16))
        m_ref[c] = m_new

    def past_tile(j, _):
        ks = keys_of(j)
        _, rows_a, _, bid = looped
        raw_b = b_scores(0, ks)
        s_t = raw_ref[...] + both_heads(bias_ref[bid])
        online_update(0, s_t, None, jnp.max(s_t, axis=0, keepdims=True), [vta_ref[r, ks] for r in rows_a], raw_b)
        for p, (_, rows, _) in enumerate(b_heads):
            nxt = b_scores(p + 1, ks) if p + 1 < len(b_heads) else looped_scores(jnp.minimum(j + 1, qi - 1))
            sel = selb_ref[p, pl.ds(j, 1), :]
            online_update(1 + p, raw_b, sel, jnp.max(raw_b, axis=0, keepdims=True) + sel,
                          [vtb_ref[r, ks] for r in rows], nxt)
            raw_b = nxt
        raw_ref[...] = raw_b
        return 0

    lax.fori_loop(0, qi, past_tile, 0)

    col = 0
    for acc in (*[acc_ref[c] for c in range(1 + len(b_heads))], acc_m):
        o_t = acc[:HEAD_DIM] / acc[HEAD_DIM:HEAD_DIM + 1]
        for c in range(o_t.shape[1] // (2 * TQ)):
            pair = jnp.concatenate([o_t[:, (2 * c) * TQ:(2 * c + 1) * TQ],
                                    o_t[:, (2 * c + 1) * TQ:(2 * c + 2) * TQ]], axis=0)
            o_ref[:, col:col + LANES] = pair.T.astype(jnp.bfloat16)
            col += LANES


def _attn_call(qka, vta, qkb, vtb, qm, mem2, gmem, wmk, wmvt, bias, batch, seq):
    nq = seq // TQ
    n_mem = mem2.shape[0] // batch
    const2 = lambda b, q: (0, 0)
    return pl.pallas_call(
        _attn_kernel,
        grid=(batch, nq),
        in_specs=[
            pl.BlockSpec((TQ, WIDTH_A), lambda b, q: (b * nq + q, 0)),
            pl.BlockSpec((seq, WIDTH_A), lambda b, q: (b, 1)),
            pl.BlockSpec((WIDTH_A, seq), lambda b, q: (b, 0)),
            pl.BlockSpec((TQ, WIDTH_B), lambda b, q: (b * nq + q, 0)),
            pl.BlockSpec((seq, WIDTH_B), lambda b, q: (b, 1)),
            pl.BlockSpec((WIDTH_B, seq), lambda b, q: (b, 0)),
            pl.BlockSpec((TQ, WIDTH_M), lambda b, q: (b * nq + q, 0)),
            pl.BlockSpec((n_mem, D_MODEL), lambda b, q: (b, 0)),
            pl.BlockSpec((1, D_MODEL), const2),
            pl.BlockSpec((D_MODEL, WIDTH_M), const2),
            pl.BlockSpec((WIDTH_M, D_MODEL), const2),
            pl.BlockSpec(bias.shape, lambda b, q: (0, 0, 0)),
        ],
        out_specs=pl.BlockSpec((TQ, WIDTH_O), lambda b, q: (b * nq + q, 0)),
        out_shape=jax.ShapeDtypeStruct((batch * seq, WIDTH_O), jnp.bfloat16),
        scratch_shapes=[
            pltpu.VMEM((N_BLOCKS, WIDTH_B), jnp.float32),
            pltpu.VMEM((n_mem, WIDTH_M), jnp.bfloat16),
            pltpu.VMEM((WIDTH_M, n_mem), jnp.bfloat16),
            pltpu.VMEM((N_HEADS_B // 2, N_BLOCKS, 2 * TQ), jnp.float32),
            pltpu.VMEM((1 + N_HEADS_B // 2, 1, 2 * TQ), jnp.float32),
            pltpu.VMEM((1 + N_HEADS_B // 2, HEAD_DIM + SUM_ROWS, 2 * TQ), jnp.float32),
            pltpu.VMEM((TQ, 2 * TQ), jnp.float32),
        ],
        compiler_params=pltpu.CompilerParams(dimension_semantics=("parallel", "arbitrary"),
                                             vmem_limit_bytes=VMEM_LIMIT),
        name="attn",
    )(qka, qka, vta, qkb, qkb, vtb, qm, mem2, gmem, wmk, wmvt, bias)


def _mix_kernel(x_ref, o_ref, g_ref, wg_ref, wpa_ref, wpb_ref, wpm_ref, wo_ref, out_ref):
    x = x_ref[...]
    h = _rms(x, g_ref[...]).astype(jnp.bfloat16)
    y = None
    col = 0
    for i, wp_ref in enumerate((wpa_ref, wpb_ref, wpm_ref)):
        width = wp_ref.shape[0]
        gate = jax.nn.sigmoid(jnp.dot(h, wg_ref[:, i * D_MODEL:(i + 1) * D_MODEL],
                                      preferred_element_type=jnp.float32))
        branch = gate * jnp.dot(o_ref[:, col:col + width], wp_ref[...], preferred_element_type=jnp.float32)
        y = branch if y is None else y + branch
        col += width
    out_ref[...] = x + jnp.dot(y.astype(jnp.bfloat16), wo_ref[...], preferred_element_type=jnp.float32)


def _mix_call(x2, o, g, wg, wpa, wpb, wpm, wo):
    t = x2.shape[0]
    tm = TM_MIX
    const = lambda i: (0, 0)
    return pl.pallas_call(
        _mix_kernel,
        grid=(t // tm,),
        in_specs=[
            pl.BlockSpec((tm, D_MODEL), lambda i: (i, 0)),
            pl.BlockSpec((tm, WIDTH_O), lambda i: (i, 0)),
            pl.BlockSpec((1, D_MODEL), const),
            pl.BlockSpec(wg.shape, const),
            pl.BlockSpec(wpa.shape, const),
            pl.BlockSpec(wpb.shape, const),
            pl.BlockSpec(wpm.shape, const),
            pl.BlockSpec(wo.shape, const),
        ],
        out_specs=pl.BlockSpec((tm, D_MODEL), lambda i: (i, 0)),
        out_shape=jax.ShapeDtypeStruct((t, D_MODEL), jnp.float32),
        compiler_params=pltpu.CompilerParams(dimension_semantics=("parallel",), vmem_limit_bytes=VMEM_LIMIT),
        name="gated_mix",
    )(x2, o, g, wg, wpa, wpb, wpm, wo)


def _mlp_kernel(x_ref, g_ref, wup_ref, wdown_ref, gfin_ref, out_ref, *, final_norm):
    x = x_ref[...]
    hm = _rms(x, g_ref[...]).astype(jnp.bfloat16)
    acc = x
    for c in range(D_FF // FF_CHUNK):
        u = jnp.dot(hm, wup_ref[:, c * FF_CHUNK:(c + 1) * FF_CHUNK], preferred_element_type=jnp.float32)
        u = jnp.square(jnp.maximum(u, 0.0)).astype(jnp.bfloat16)
        acc = acc + jnp.dot(u, wdown_ref[c * FF_CHUNK:(c + 1) * FF_CHUNK, :], preferred_element_type=jnp.float32)
    out_ref[...] = _rms(acc, gfin_ref[...]) if final_norm else acc


def _mlp_call(x2, g, wup, wdown, gfin, final_norm):
    t = x2.shape[0]
    tm = TM_MLP
    const = lambda i: (0, 0)
    return pl.pallas_call(
        functools.partial(_mlp_kernel, final_norm=final_norm),
        grid=(t // tm,),
        in_specs=[
            pl.BlockSpec((tm, D_MODEL), lambda i: (i, 0)),
            pl.BlockSpec((1, D_MODEL), const),
            pl.BlockSpec(wup.shape, const),
            pl.BlockSpec(wdown.shape, const),
            pl.BlockSpec((1, D_MODEL), const),
        ],
        out_specs=pl.BlockSpec((tm, D_MODEL), lambda i: (i, 0)),
        out_shape=jax.ShapeDtypeStruct((t, D_MODEL), jnp.float32),
        compiler_params=pltpu.CompilerParams(dimension_semantics=("parallel",), vmem_limit_bytes=VMEM_LIMIT),
        name="mlp",
    )(x2, g, wup, wdown, gfin)


def _rope_tables(seq):
    half = ROT_DIM // 2
    inv_freq = 1.0 / (ROPE_THETA ** (jnp.arange(0, ROT_DIM, 2, dtype=jnp.float32) / ROT_DIM))
    ang = jnp.arange(seq, dtype=jnp.int32).astype(jnp.float32)[:, None] * inv_freq[None, :]
    cos, sin = jnp.cos(ang), jnp.sin(ang)
    ones = jnp.ones((seq, HEAD_DIM - ROT_DIM), jnp.float32)
    zeros_half = jnp.zeros((seq, half), jnp.float32)
    zeros_rest = jnp.zeros((seq, HEAD_DIM - ROT_DIM), jnp.float32)
    c_head = jnp.concatenate([cos, cos, ones], axis=1)
    lo_head = jnp.concatenate([-sin, zeros_half, zeros_rest], axis=1)
    hi_head = jnp.concatenate([zeros_half, sin, zeros_rest], axis=1)
    rep = LANES // HEAD_DIM
    return jnp.tile(c_head, (1, rep)), jnp.tile(lo_head, (1, rep)), jnp.tile(hi_head, (1, rep))


def kernel(x, mem, norm_mix, w_in, w_proj_a, w_proj_b, w_proj_m, w_out, norm_mem, w_mem_kv, norm_mlp, w_up,
           w_down, norm_final):
    batch, seq, d = x.shape
    assert d == D_MODEL and seq == SEQ and seq % TM_QKV == 0
    depth = w_in.shape[0]
    bf = jnp.bfloat16
    x2 = x.reshape(batch * seq, d)
    mem2 = mem.reshape(batch * mem.shape[1], d)
    rope_c, rope_s1, rope_s2 = _rope_tables(seq)
    bias = jnp.asarray(_BIAS_NP)
    o1 = 3 * WIDTH_A
    o2 = o1 + 3 * WIDTH_B
    o3 = o2 + WIDTH_M
    for l in range(depth):
        w = w_in[l]
        wqka = w[:, :2 * WIDTH_A].astype(bf)
        wvta = w[:, 2 * WIDTH_A:o1].T.astype(bf)
        wqkb = w[:, o1:o1 + 2 * WIDTH_B].astype(bf)
        wvtb = w[:, o1 + 2 * WIDTH_B:o2].T.astype(bf)
        wqm = w[:, o2:o3].astype(bf)
        wg = w[:, o3:].astype(bf)
        g_mix = norm_mix[l].reshape(1, d)
        qka, vta, qkb, vtb, qm = _qkv_call(x2, g_mix, wqka, wvta, wqkb, wvtb, wqm, rope_c, rope_s1, rope_s2,
                                           batch, seq)
        wmk = w_mem_kv[l][:, :WIDTH_M].astype(bf)
        wmvt = w_mem_kv[l][:, WIDTH_M:].T.astype(bf)
        o = _attn_call(qka, vta, qkb, vtb, qm, mem2, norm_mem[l].reshape(1, d), wmk, wmvt, bias, batch, seq)
        x2 = _mix_call(x2, o, g_mix, wg, w_proj_a[l].astype(bf), w_proj_b[l].astype(bf), w_proj_m[l].astype(bf),
                       w_out[l].astype(bf))
        x2 = _mlp_call(x2, norm_mlp[l].reshape(1, d), w_up[l].astype(bf), w_down[l].astype(bf),
                       norm_final.reshape(1, d), final_norm=(l == depth - 1))
    return x2.reshape(batch, seq, d)
```

```python
import functools

import jax
import jax.numpy as jnp
import numpy as np
from jax import lax
from jax.experimental import pallas as pl
from jax.experimental.pallas import tpu as pltpu

D_MODEL = 1024
SEQ = 2048
HEAD_DIM = 64
ROT_DIM = HEAD_DIM // 4
ROPE_THETA = 500000.0
DIL_GROUPS = ((128, 1), (512, 4), (2048, 16))
HEADS_PER_DIL_GROUP = 2
N_HEADS_A = len(DIL_GROUPS) * HEADS_PER_DIL_GROUP
N_HEADS_B = 6
N_HEADS_M = 4
MOBA_BLOCK = 256
MOBA_TOPK = 3
N_BLOCKS = SEQ // MOBA_BLOCK
D_FF = 4 * D_MODEL
WIDTH_A = N_HEADS_A * HEAD_DIM
WIDTH_A_OUT = HEADS_PER_DIL_GROUP * HEAD_DIM
WIDTH_B = N_HEADS_B * HEAD_DIM
WIDTH_M = N_HEADS_M * HEAD_DIM
WIDTH_O = WIDTH_A_OUT + WIDTH_B + WIDTH_M
RMS_EPS = 1e-6
NEG_INF = -1e30
Q_SCALE = HEAD_DIM ** -0.5
LOG2_E = 1.4426950408889634

LANES = 128
PAIR = 2 * HEAD_DIM
SUM_ROWS = 16
TQ = MOBA_BLOCK
TM_QKV = 512
TM_MIX = 512
TM_MLP = 512
FF_CHUNK = 1024
VMEM_LIMIT = 56 * 1024 * 1024

_B_ALLNEG = 0
_B_CAUSAL = 1


def _dil_tile_offsets():
    return tuple(tuple(range(min((w + TQ - 1) // TQ, N_BLOCKS - 1) + 1)) for w, _ in DIL_GROUPS)


def _build_bias_tiles():
    c = np.arange(TQ)[:, None]
    r = np.arange(TQ)[None, :]
    tiles = [np.zeros((TQ, TQ), bool), (r - c) >= 0]
    ids = []
    for (w, d), offs in zip(DIL_GROUPS, _dil_tile_offsets()):
        per_off = []
        for o in offs:
            diff = o * TQ + r - c
            per_off.append((diff >= 0) & (diff <= w) & (diff % d == 0))
        uniq, gid = [], []
        for t in per_off:
            for k, u in enumerate(uniq):
                if np.array_equal(t, u):
                    gid.append(k)
                    break
            else:
                uniq.append(t)
                gid.append(len(uniq) - 1)
        ids.append(tuple(len(tiles) + k for k in gid))
        tiles.extend(uniq)
    bias = np.where(np.stack(tiles), 0.0, NEG_INF).astype(np.float32)
    return bias, tuple(ids)


_BIAS_NP, _DIL_BIAS_IDS = _build_bias_tiles()

_NT = (((1,), (1,)), ((), ()))


def _rms(x, g):
    return x * lax.rsqrt(jnp.mean(x * x, axis=-1, keepdims=True) + RMS_EPS) * g


def _qkv_kernel(x_ref, g_ref, wqka_ref, wvta_ref, wqkb_ref, wvtb_ref, wqm_ref, c_ref, s1_ref, s2_ref,
                qka_ref, vta_ref, qkb_ref, vtb_ref, qm_ref):
    h = _rms(x_ref[...], g_ref[...]).astype(jnp.bfloat16)
    cos = c_ref[...]
    sin_lo = s1_ref[...]
    sin_hi = s2_ref[...]
    for wqk_ref, wvt_ref, qk_ref, vt_ref, width in ((wqka_ref, wvta_ref, qka_ref, vta_ref, WIDTH_A),
                                                    (wqkb_ref, wvtb_ref, qkb_ref, vtb_ref, WIDTH_B)):
        z = jnp.dot(h, wqk_ref[...], preferred_element_type=jnp.float32)
        for blk in range(2 * width // LANES):
            zb = z[:, blk * LANES:(blk + 1) * LANES]
            rb = zb * cos + pltpu.roll(zb, LANES - ROT_DIM // 2, 1) * sin_lo + pltpu.roll(zb, ROT_DIM // 2, 1) * sin_hi
            if blk < width // LANES:
                rb = rb * (Q_SCALE * LOG2_E)
            qk_ref[:, blk * LANES:(blk + 1) * LANES] = rb.astype(jnp.bfloat16)
        vt = lax.dot_general(wvt_ref[...], h, _NT, preferred_element_type=jnp.float32)
        vt_ref[...] = vt.astype(jnp.bfloat16)
    qm = jnp.dot(h, wqm_ref[...], preferred_element_type=jnp.float32) * (Q_SCALE * LOG2_E)
    qm_ref[...] = qm.astype(jnp.bfloat16)


def _qkv_call(x2, g, wqka, wvta, wqkb, wvtb, wqm, rope_c, rope_s1, rope_s2, batch, seq):
    t = x2.shape[0]
    tm = TM_QKV
    nts = seq // tm
    const = lambda i: (0, 0)
    return pl.pallas_call(
        _qkv_kernel,
        grid=(t // tm,),
        in_specs=[
            pl.BlockSpec((tm, D_MODEL), lambda i: (i, 0)),
            pl.BlockSpec((1, D_MODEL), const),
            pl.BlockSpec((D_MODEL, 2 * WIDTH_A), const),
            pl.BlockSpec((WIDTH_A, D_MODEL), const),
            pl.BlockSpec((D_MODEL, 2 * WIDTH_B), const),
            pl.BlockSpec((WIDTH_B, D_MODEL), const),
            pl.BlockSpec((D_MODEL, WIDTH_M), const),
            pl.BlockSpec((tm, LANES), lambda i: (i % nts, 0)),
            pl.BlockSpec((tm, LANES), lambda i: (i % nts, 0)),
            pl.BlockSpec((tm, LANES), lambda i: (i % nts, 0)),
        ],
        out_specs=[
            pl.BlockSpec((tm, 2 * WIDTH_A), lambda i: (i, 0)),
            pl.BlockSpec((WIDTH_A, tm), lambda i: (i // nts, i % nts)),
            pl.BlockSpec((tm, 2 * WIDTH_B), lambda i: (i, 0)),
            pl.BlockSpec((WIDTH_B, tm), lambda i: (i // nts, i % nts)),
            pl.BlockSpec((tm, WIDTH_M), lambda i: (i, 0)),
        ],
        out_shape=[
            jax.ShapeDtypeStruct((t, 2 * WIDTH_A), jnp.bfloat16),
            jax.ShapeDtypeStruct((batch * WIDTH_A, seq), jnp.bfloat16),
            jax.ShapeDtypeStruct((t, 2 * WIDTH_B), jnp.bfloat16),
            jax.ShapeDtypeStruct((batch * WIDTH_B, seq), jnp.bfloat16),
            jax.ShapeDtypeStruct((t, WIDTH_M), jnp.bfloat16),
        ],
        compiler_params=pltpu.CompilerParams(dimension_semantics=("parallel",), vmem_limit_bytes=VMEM_LIMIT),
        name="qkv_proj",
    )(x2, g, wqka, wvta, wqkb, wvtb, wqm, rope_c, rope_s1, rope_s2)


def _pv(v_heads, p_bf):
    ones = jnp.ones((SUM_ROWS, p_bf.shape[0]), jnp.bfloat16)
    return jnp.concatenate([jnp.dot(jnp.concatenate([v, ones], axis=0), p_bf[:, h * TQ:(h + 1) * TQ],
                                    preferred_element_type=jnp.float32)
                            for h, v in enumerate(v_heads)], axis=1)


def _after(x, *matmul_results):
    for r in matmul_results:
        bits = lax.bitcast_convert_type(r[-1:, :x.shape[1]], jnp.uint32)
        bits = lax.shift_right_logical(lax.shift_right_logical(bits, jnp.uint32(16)), jnp.uint32(16))
        x = x + lax.bitcast_convert_type(bits, jnp.float32)
    return x


def _scores_t(k_tile, q_heads):
    return lax.dot_general(k_tile, q_heads, _NT, preferred_element_type=jnp.float32)


def _stack_heads(q, n_heads):
    lane = lax.broadcasted_iota(jnp.int32, q.shape, 1)
    zero = jnp.zeros((), q.dtype)
    return jnp.concatenate([jnp.where((lane >= h * HEAD_DIM) & (lane < (h + 1) * HEAD_DIM), q, zero)
                            for h in range(n_heads)], axis=0)


def _head_rows(first_head, n_heads):
    return [slice((first_head + h) * HEAD_DIM, (first_head + h + 1) * HEAD_DIM) for h in range(n_heads)]


def _attn_kernel(qa_ref, ka_ref, vta_ref, qb_ref, kb_ref, vtb_ref, qm_ref, mem_ref, gmem_ref, wmk_ref, wmvt_ref,
                 bias_ref, o_ref, kmean_ref, km_ref, vmt_ref, selb_ref, m_ref, acc_ref, raw_ref):
    qi = pl.program_id(1)

    @pl.when(qi == 0)
    def _per_batch():
        kb = kb_ref[...].astype(jnp.float32)
        kmean_ref[...] = jnp.mean(kb.reshape(N_BLOCKS, MOBA_BLOCK, WIDTH_B), axis=1)
        mem_n = _rms(mem_ref[...], gmem_ref[...]).astype(jnp.bfloat16)
        km_ref[...] = jnp.dot(mem_n, wmk_ref[...], preferred_element_type=jnp.float32).astype(jnp.bfloat16)
        vmt_ref[...] = lax.dot_general(wmvt_ref[...], mem_n, _NT,
                                       preferred_element_type=jnp.float32).astype(jnp.bfloat16)

    def keys_of(j):
        return pl.ds(pl.multiple_of(j * TQ, TQ), TQ)

    def both_heads(bias):
        return jnp.concatenate([bias, bias], axis=1)

    a_static = []
    looped = None
    for g, offs in enumerate(_dil_tile_offsets()):
        cols = slice(g * PAIR, (g + 1) * PAIR)
        rows = _head_rows(g * HEADS_PER_DIL_GROUP, HEADS_PER_DIL_GROUP)
        q_heads = _stack_heads(qa_ref[:, cols], HEADS_PER_DIL_GROUP)
        ids = _DIL_BIAS_IDS[g]
        if len(offs) == N_BLOCKS:
            assert looped is None and len(set(ids[1:])) == 1
            looped = (cols, rows, q_heads, ids[1])
            offs = offs[:1]
        for o in offs:
            ks = keys_of(jnp.maximum(qi - o, 0))
            bid = ids[o] if o == 0 else jnp.where(qi >= o, ids[o], _B_ALLNEG)
            a_static.append((ks, bid, cols, rows, q_heads))

    blk = lax.broadcasted_iota(jnp.int32, (N_BLOCKS, 2 * TQ), 0)
    own = keys_of(qi)
    b_heads = []
    for p in range(N_HEADS_B // 2):
        cols = slice(p * PAIR, (p + 1) * PAIR)
        q_heads = _stack_heads(qb_ref[:, cols], 2)
        kmean = kmean_ref[:, cols]
        kmean_hi = kmean.astype(jnp.bfloat16)
        kmean_lo = (kmean - kmean_hi.astype(jnp.float32)).astype(jnp.bfloat16)
        gate = _scores_t(kmean_hi, q_heads) + _scores_t(kmean_lo, q_heads)
        gate = jnp.where(blk < qi, gate, NEG_INF)
        beaten = jnp.zeros(gate.shape, jnp.float32)
        for j in range(N_BLOCKS):
            gj = gate[j:j + 1, :]
            wins_tie = jnp.where(gj >= gate, 1.0, 0.0)
            wins_strict = jnp.where(gj > gate, 1.0, 0.0)
            beaten = beaten + jnp.where(blk > j, wins_tie, wins_strict)
        selb_ref[p] = jnp.where((beaten < MOBA_TOPK) & (blk < qi), 0.0, NEG_INF)
        b_heads.append((cols, _head_rows(2 * p, 2), q_heads))

    def a_scores(part):
        ks, _, cols, _, q_heads = part
        return _scores_t(ka_ref[ks, cols], q_heads)

    def b_scores(p, ks):
        cols, _, q_heads = b_heads[p]
        return _scores_t(kb_ref[ks, cols], q_heads)

    def looped_scores(j):
        cols, _, q_heads, _ = looped
        return _scores_t(ka_ref[keys_of(j), cols], q_heads)

    n_a = len(a_static)
    raw = a_scores(a_static[0])
    a_biased = []
    m_a = None
    for i in range(n_a):
        nxt = a_scores(a_static[i + 1]) if i + 1 < n_a else b_scores(0, own)
        s_t = raw + both_heads(bias_ref[a_static[i][1]])
        part_max = _after(jnp.max(s_t, axis=0, keepdims=True), nxt)
        m_a = part_max if m_a is None else jnp.maximum(m_a, part_max)
        a_biased.append(s_t)
        raw = nxt
    m_heads = _stack_heads(qm_ref[...], N_HEADS_M)
    ahead = [lambda: b_scores(1, own), lambda: b_scores(2, own), lambda: _scores_t(km_ref[...], m_heads),
             lambda: looped_scores(0)]
    later = [raw]
    assert n_a > len(ahead)
    neg_m = -m_a
    acc_a = None
    for i in range(n_a):
        ks, _, _, rows, _ = a_static[i]
        p_t = jnp.exp2(a_biased[i] + neg_m).astype(jnp.bfloat16)
        if i < len(ahead):
            later.append(ahead[i]())
            neg_m = _after(neg_m, later[-1])
        part_acc = _pv([vta_ref[r, ks] for r in rows], p_t)
        acc_a = part_acc if acc_a is None else acc_a + part_acc
    m_ref[0] = m_a
    acc_ref[0] = acc_a

    def single_tile(s_t, v_heads):
        m = jnp.max(s_t, axis=0, keepdims=True)
        return m, _pv(v_heads, jnp.exp2(s_t - m).astype(jnp.bfloat16))

    for p, (_, rows, _) in enumerate(b_heads):
        m_ref[1 + p], acc_ref[1 + p] = single_tile(later[p] + both_heads(bias_ref[_B_CAUSAL]),
                                                   [vtb_ref[r, own] for r in rows])
    _, acc_m = single_tile(later[3], [vmt_ref[r, :] for r in _head_rows(0, N_HEADS_M)])
    raw_ref[0] = later[4]
    raw_ref[1] = b_scores(0, keys_of(0))

    def online_update(c, s_t, shift_row, m_tile, v_heads, next_scores):
        m_old = m_ref[c]
        m_new = jnp.maximum(m_old, m_tile)
        alpha = _after(jnp.exp2(m_old - m_new), next_scores)
        shift = -m_new if shift_row is None else shift_row - m_new
        acc_ref[c] = alpha * acc_ref[c] + _pv(v_heads, jnp.exp2(s_t + shift).astype(jnp.bfloat16))
        m_ref[c] = m_new

    def past_tile(j, _):
        ks = keys_of(j)
        ks_next = keys_of(jnp.minimum(j + 1, qi - 1))
        _, rows_a, _, bid = looped
        n_b = len(b_heads)
        scores = [raw_ref[0] + both_heads(bias_ref[bid]), raw_ref[1]]
        for c in range(1 + n_b):
            ahead = c + 2
            if ahead <= n_b:
                issued = b_scores(ahead - 1, ks)
            elif ahead == n_b + 1:
                issued = looped_scores(jnp.minimum(j + 1, qi - 1))
            else:
                issued = b_scores(0, ks_next)
            scores.append(issued)
            if c == 0:
                online_update(0, scores[0], None, jnp.max(scores[0], axis=0, keepdims=True),
                              [vta_ref[r, ks] for r in rows_a], issued)
            else:
                sel = selb_ref[c - 1, pl.ds(j, 1), :]
                online_update(c, scores[c], sel, jnp.max(scores[c], axis=0, keepdims=True) + sel,
                              [vtb_ref[r, ks] for r in b_heads[c - 1][1]], issued)
        raw_ref[0] = scores[1 + n_b]
        raw_ref[1] = scores[2 + n_b]
        return 0

    lax.fori_loop(0, qi, past_tile, 0)

    col = 0
    for acc in (*[acc_ref[c] for c in range(1 + len(b_heads))], acc_m):
        o_t = acc[:HEAD_DIM] / acc[HEAD_DIM:HEAD_DIM + 1]
        for c in range(o_t.shape[1] // (2 * TQ)):
            pair = jnp.concatenate([o_t[:, (2 * c) * TQ:(2 * c + 1) * TQ],
                                    o_t[:, (2 * c + 1) * TQ:(2 * c + 2) * TQ]], axis=0)
            o_ref[:, col:col + LANES] = pair.T.astype(jnp.bfloat16)
            col += LANES


def _attn_call(qka, vta, qkb, vtb, qm, mem2, gmem, wmk, wmvt, bias, batch, seq):
    nq = seq // TQ
    n_mem = mem2.shape[0] // batch
    const2 = lambda b, q: (0, 0)
    return pl.pallas_call(
        _attn_kernel,
        grid=(batch, nq),
        in_specs=[
            pl.BlockSpec((TQ, WIDTH_A), lambda b, q: (b * nq + q, 0)),
            pl.BlockSpec((seq, WIDTH_A), lambda b, q: (b, 1)),
            pl.BlockSpec((WIDTH_A, seq), lambda b, q: (b, 0)),
            pl.BlockSpec((TQ, WIDTH_B), lambda b, q: (b * nq + q, 0)),
            pl.BlockSpec((seq, WIDTH_B), lambda b, q: (b, 1)),
            pl.BlockSpec((WIDTH_B, seq), lambda b, q: (b, 0)),
            pl.BlockSpec((TQ, WIDTH_M), lambda b, q: (b * nq + q, 0)),
            pl.BlockSpec((n_mem, D_MODEL), lambda b, q: (b, 0)),
            pl.BlockSpec((1, D_MODEL), const2),
            pl.BlockSpec((D_MODEL, WIDTH_M), const2),
            pl.BlockSpec((WIDTH_M, D_MODEL), const2),
            pl.BlockSpec(bias.shape, lambda b, q: (0, 0, 0)),
        ],
        out_specs=pl.BlockSpec((TQ, WIDTH_O), lambda b, q: (b * nq + q, 0)),
        out_shape=jax.ShapeDtypeStruct((batch * seq, WIDTH_O), jnp.bfloat16),
        scratch_shapes=[
            pltpu.VMEM((N_BLOCKS, WIDTH_B), jnp.float32),
            pltpu.VMEM((n_mem, WIDTH_M), jnp.bfloat16),
            pltpu.VMEM((WIDTH_M, n_mem), jnp.bfloat16),
            pltpu.VMEM((N_HEADS_B // 2, N_BLOCKS, 2 * TQ), jnp.float32),
            pltpu.VMEM((1 + N_HEADS_B // 2, 1, 2 * TQ), jnp.float32),
            pltpu.VMEM((1 + N_HEADS_B // 2, HEAD_DIM + SUM_ROWS, 2 * TQ), jnp.float32),
            pltpu.VMEM((2, TQ, 2 * TQ), jnp.float32),
        ],
        compiler_params=pltpu.CompilerParams(dimension_semantics=("parallel", "arbitrary"),
                                             vmem_limit_bytes=VMEM_LIMIT),
        name="attn",
    )(qka, qka, vta, qkb, qkb, vtb, qm, mem2, gmem, wmk, wmvt, bias)


def _mix_kernel(x_ref, o_ref, g_ref, wg_ref, wpa_ref, wpb_ref, wpm_ref, wo_ref, out_ref):
    x = x_ref[...]
    h = _rms(x, g_ref[...]).astype(jnp.bfloat16)
    y = None
    col = 0
    for i, wp_ref in enumerate((wpa_ref, wpb_ref, wpm_ref)):
        width = wp_ref.shape[0]
        gate = jax.nn.sigmoid(jnp.dot(h, wg_ref[:, i * D_MODEL:(i + 1) * D_MODEL],
                                      preferred_element_type=jnp.float32))
        branch = gate * jnp.dot(o_ref[:, col:col + width], wp_ref[...], preferred_element_type=jnp.float32)
        y = branch if y is None else y + branch
        col += width
    out_ref[...] = x + jnp.dot(y.astype(jnp.bfloat16), wo_ref[...], preferred_element_type=jnp.float32)


def _mix_call(x2, o, g, wg, wpa, wpb, wpm, wo):
    t = x2.shape[0]
    tm = TM_MIX
    const = lambda i: (0, 0)
    return pl.pallas_call(
        _mix_kernel,
        grid=(t // tm,),
        in_specs=[
            pl.BlockSpec((tm, D_MODEL), lambda i: (i, 0)),
            pl.BlockSpec((tm, WIDTH_O), lambda i: (i, 0)),
            pl.BlockSpec((1, D_MODEL), const),
            pl.BlockSpec(wg.shape, const),
            pl.BlockSpec(wpa.shape, const),
            pl.BlockSpec(wpb.shape, const),
            pl.BlockSpec(wpm.shape, const),
            pl.BlockSpec(wo.shape, const),
        ],
        out_specs=pl.BlockSpec((tm, D_MODEL), lambda i: (i, 0)),
        out_shape=jax.ShapeDtypeStruct((t, D_MODEL), jnp.float32),
        compiler_params=pltpu.CompilerParams(dimension_semantics=("parallel",), vmem_limit_bytes=VMEM_LIMIT),
        name="gated_mix",
    )(x2, o, g, wg, wpa, wpb, wpm, wo)


def _mlp_kernel(x_ref, g_ref, wup_ref, wdown_ref, gfin_ref, out_ref, *, final_norm):
    x = x_ref[...]
    hm = _rms(x, g_ref[...]).astype(jnp.bfloat16)
    acc = x
    for c in range(D_FF // FF_CHUNK):
        u = jnp.dot(hm, wup_ref[:, c * FF_CHUNK:(c + 1) * FF_CHUNK], preferred_element_type=jnp.float32)
        u = jnp.square(jnp.maximum(u, 0.0)).astype(jnp.bfloat16)
        acc = acc + jnp.dot(u, wdown_ref[c * FF_CHUNK:(c + 1) * FF_CHUNK, :], preferred_element_type=jnp.float32)
    out_ref[...] = _rms(acc, gfin_ref[...]) if final_norm else acc


def _mlp_call(x2, g, wup, wdown, gfin, final_norm):
    t = x2.shape[0]
    tm = TM_MLP
    const = lambda i: (0, 0)
    return pl.pallas_call(
        functools.partial(_mlp_kernel, final_norm=final_norm),
        grid=(t // tm,),
        in_specs=[
            pl.BlockSpec((tm, D_MODEL), lambda i: (i, 0)),
            pl.BlockSpec((1, D_MODEL), const),
            pl.BlockSpec(wup.shape, const),
            pl.BlockSpec(wdown.shape, const),
            pl.BlockSpec((1, D_MODEL), const),
        ],
        out_specs=pl.BlockSpec((tm, D_MODEL), lambda i: (i, 0)),
        out_shape=jax.ShapeDtypeStruct((t, D_MODEL), jnp.float32),
        compiler_params=pltpu.CompilerParams(dimension_semantics=("parallel",), vmem_limit_bytes=VMEM_LIMIT),
        name="mlp",
    )(x2, g, wup, wdown, gfin)


def _rope_tables(seq):
    half = ROT_DIM // 2
    inv_freq = 1.0 / (ROPE_THETA ** (jnp.arange(0, ROT_DIM, 2, dtype=jnp.float32) / ROT_DIM))
    ang = jnp.arange(seq, dtype=jnp.int32).astype(jnp.float32)[:, None] * inv_freq[None, :]
    cos, sin = jnp.cos(ang), jnp.sin(ang)
    ones = jnp.ones((seq, HEAD_DIM - ROT_DIM), jnp.float32)
    zeros_half = jnp.zeros((seq, half), jnp.float32)
    zeros_rest = jnp.zeros((seq, HEAD_DIM - ROT_DIM), jnp.float32)
    c_head = jnp.concatenate([cos, cos, ones], axis=1)
    lo_head = jnp.concatenate([-sin, zeros_half, zeros_rest], axis=1)
    hi_head = jnp.concatenate([zeros_half, sin, zeros_rest], axis=1)
    rep = LANES // HEAD_DIM
    return jnp.tile(c_head, (1, rep)), jnp.tile(lo_head, (1, rep)), jnp.tile(hi_head, (1, rep))


def kernel(x, mem, norm_mix, w_in, w_proj_a, w_proj_b, w_proj_m, w_out, norm_mem, w_mem_kv, norm_mlp, w_up,
           w_down, norm_final):
    batch, seq, d = x.shape
    assert d == D_MODEL and seq == SEQ and seq % TM_QKV == 0
    depth = w_in.shape[0]
    bf = jnp.bfloat16
    x2 = x.reshape(batch * seq, d)
    mem2 = mem.reshape(batch * mem.shape[1], d)
    rope_c, rope_s1, rope_s2 = _rope_tables(seq)
    bias = jnp.asarray(_BIAS_NP)
    o1 = 3 * WIDTH_A
    o2 = o1 + 3 * WIDTH_B
    o3 = o2 + WIDTH_M
    for l in range(depth):
        w = w_in[l]
        wqka = w[:, :2 * WIDTH_A].astype(bf)
        wvta = w[:, 2 * WIDTH_A:o1].T.astype(bf)
        wqkb = w[:, o1:o1 + 2 * WIDTH_B].astype(bf)
        wvtb = w[:, o1 + 2 * WIDTH_B:o2].T.astype(bf)
        wqm = w[:, o2:o3].astype(bf)
        wg = w[:, o3:].astype(bf)
        g_mix = norm_mix[l].reshape(1, d)
        qka, vta, qkb, vtb, qm = _qkv_call(x2, g_mix, wqka, wvta, wqkb, wvtb, wqm, rope_c, rope_s1, rope_s2,
                                           batch, seq)
        wmk = w_mem_kv[l][:, :WIDTH_M].astype(bf)
        wmvt = w_mem_kv[l][:, WIDTH_M:].T.astype(bf)
        o = _attn_call(qka, vta, qkb, vtb, qm, mem2, norm_mem[l].reshape(1, d), wmk, wmvt, bias, batch, seq)
        x2 = _mix_call(x2, o, g_mix, wg, w_proj_a[l].astype(bf), w_proj_b[l].astype(bf), w_proj_m[l].astype(bf),
                       w_out[l].astype(bf))
        x2 = _mlp_call(x2, norm_mlp[l].reshape(1, d), w_up[l].astype(bf), w_down[l].astype(bf),
                       norm_final.reshape(1, d), final_norm=(l == depth - 1))
    return x2.reshape(batch, seq, d)
```

```python
import functools

import jax
import jax.numpy as jnp
import numpy as np
from jax import lax
from jax.experimental import pallas as pl
from jax.experimental.pallas import tpu as pltpu

D_MODEL = 1024
SEQ = 2048
HEAD_DIM = 64
ROT_DIM = HEAD_DIM // 4
ROPE_THETA = 500000.0
DIL_GROUPS = ((128, 1), (512, 4), (2048, 16))
HEADS_PER_DIL_GROUP = 2
N_HEADS_A = len(DIL_GROUPS) * HEADS_PER_DIL_GROUP
N_HEADS_B = 6
N_HEADS_M = 4
MOBA_BLOCK = 256
MOBA_TOPK = 3
N_BLOCKS = SEQ // MOBA_BLOCK
D_FF = 4 * D_MODEL
WIDTH_A = N_HEADS_A * HEAD_DIM
WIDTH_A_OUT = HEADS_PER_DIL_GROUP * HEAD_DIM
WIDTH_B = N_HEADS_B * HEAD_DIM
WIDTH_M = N_HEADS_M * HEAD_DIM
WIDTH_O = WIDTH_A_OUT + WIDTH_B + WIDTH_M
RMS_EPS = 1e-6
NEG_INF = -1e30
Q_SCALE = HEAD_DIM ** -0.5
LOG2_E = 1.4426950408889634

LANES = 128
PAIR = 2 * HEAD_DIM
SUM_ROWS = 16
TQ = MOBA_BLOCK
TM_QKV = 512
TM_MIX = 512
TM_MLP = 512
FF_CHUNK = 1024
VMEM_LIMIT = 56 * 1024 * 1024

_B_ALLNEG = 0
_B_CAUSAL = 1


def _dil_tile_offsets():
    return tuple(tuple(range(min((w + TQ - 1) // TQ, N_BLOCKS - 1) + 1)) for w, _ in DIL_GROUPS)


def _build_bias_tiles():
    c = np.arange(TQ)[:, None]
    r = np.arange(TQ)[None, :]
    tiles = [np.zeros((TQ, TQ), bool), (r - c) >= 0]
    ids = []
    for (w, d), offs in zip(DIL_GROUPS, _dil_tile_offsets()):
        per_off = []
        for o in offs:
            diff = o * TQ + r - c
            per_off.append((diff >= 0) & (diff <= w) & (diff % d == 0))
        uniq, gid = [], []
        for t in per_off:
            for k, u in enumerate(uniq):
                if np.array_equal(t, u):
                    gid.append(k)
                    break
            else:
                uniq.append(t)
                gid.append(len(uniq) - 1)
        ids.append(tuple(len(tiles) + k for k in gid))
        tiles.extend(uniq)
    bias = np.where(np.stack(tiles), 0.0, NEG_INF).astype(np.float32)
    return bias, tuple(ids)


_BIAS_NP, _DIL_BIAS_IDS = _build_bias_tiles()

_NT = (((1,), (1,)), ((), ()))
_TN = (((0,), (1,)), ((), ()))
COL_B = 3 * WIDTH_A
COL_M = COL_B + 3 * WIDTH_B
COL_GATES = COL_M + WIDTH_M


def _rms(x, g):
    return x * lax.rsqrt(jnp.mean(x * x, axis=-1, keepdims=True) + RMS_EPS) * g


def _qkv_kernel(x_ref, g_ref, win_ref, c_ref, s1_ref, s2_ref, qka_ref, vta_ref, qkb_ref, vtb_ref, qm_ref):
    h = _rms(x_ref[...], g_ref[...]).astype(jnp.bfloat16)
    cos = c_ref[...]
    sin_lo = s1_ref[...]
    sin_hi = s2_ref[...]
    for col0, qk_ref, vt_ref, width in ((0, qka_ref, vta_ref, WIDTH_A), (COL_B, qkb_ref, vtb_ref, WIDTH_B)):
        z = jnp.dot(h, win_ref[:, col0:col0 + 2 * width], preferred_element_type=jnp.float32)
        for blk in range(2 * width // LANES):
            zb = z[:, blk * LANES:(blk + 1) * LANES]
            rb = zb * cos + pltpu.roll(zb, LANES - ROT_DIM // 2, 1) * sin_lo + pltpu.roll(zb, ROT_DIM // 2, 1) * sin_hi
            if blk < width // LANES:
                rb = rb * (Q_SCALE * LOG2_E)
            qk_ref[:, blk * LANES:(blk + 1) * LANES] = rb.astype(jnp.bfloat16)
        vt = lax.dot_general(win_ref[:, col0 + 2 * width:col0 + 3 * width], h, _TN,
                             preferred_element_type=jnp.float32)
        vt_ref[...] = vt.astype(jnp.bfloat16)
    qm = jnp.dot(h, win_ref[:, COL_M:COL_GATES], preferred_element_type=jnp.float32) * (Q_SCALE * LOG2_E)
    qm_ref[...] = qm.astype(jnp.bfloat16)


def _qkv_call(x2, g, w_in, layer, rope_c, rope_s1, rope_s2, batch, seq):
    t = x2.shape[0]
    tm = TM_QKV
    nts = seq // tm
    per_layer = lambda i: (layer, 0, 0)
    return pl.pallas_call(
        _qkv_kernel,
        grid=(t // tm,),
        in_specs=[
            pl.BlockSpec((tm, D_MODEL), lambda i: (i, 0)),
            pl.BlockSpec((None, 1, D_MODEL), per_layer),
            pl.BlockSpec((None, D_MODEL, COL_GATES), per_layer),
            pl.BlockSpec((tm, LANES), lambda i: (i % nts, 0)),
            pl.BlockSpec((tm, LANES), lambda i: (i % nts, 0)),
            pl.BlockSpec((tm, LANES), lambda i: (i % nts, 0)),
        ],
        out_specs=[
            pl.BlockSpec((tm, 2 * WIDTH_A), lambda i: (i, 0)),
            pl.BlockSpec((WIDTH_A, tm), lambda i: (i // nts, i % nts)),
            pl.BlockSpec((tm, 2 * WIDTH_B), lambda i: (i, 0)),
            pl.BlockSpec((WIDTH_B, tm), lambda i: (i // nts, i % nts)),
            pl.BlockSpec((tm, WIDTH_M), lambda i: (i, 0)),
        ],
        out_shape=[
            jax.ShapeDtypeStruct((t, 2 * WIDTH_A), jnp.bfloat16),
            jax.ShapeDtypeStruct((batch * WIDTH_A, seq), jnp.bfloat16),
            jax.ShapeDtypeStruct((t, 2 * WIDTH_B), jnp.bfloat16),
            jax.ShapeDtypeStruct((batch * WIDTH_B, seq), jnp.bfloat16),
            jax.ShapeDtypeStruct((t, WIDTH_M), jnp.bfloat16),
        ],
        compiler_params=pltpu.CompilerParams(dimension_semantics=("parallel",), vmem_limit_bytes=VMEM_LIMIT),
        name="qkv_proj",
    )(x2, g, w_in, rope_c, rope_s1, rope_s2)


def _pv(v_heads, p_bf):
    ones = jnp.ones((SUM_ROWS, p_bf.shape[0]), jnp.bfloat16)
    return jnp.concatenate([jnp.dot(jnp.concatenate([v, ones], axis=0), p_bf[:, h * TQ:(h + 1) * TQ],
                                    preferred_element_type=jnp.float32)
                            for h, v in enumerate(v_heads)], axis=1)


def _after(x, *matmul_results):
    for r in matmul_results:
        bits = lax.bitcast_convert_type(r[-1:, :x.shape[1]], jnp.uint32)
        bits = lax.shift_right_logical(lax.shift_right_logical(bits, jnp.uint32(16)), jnp.uint32(16))
        x = x + lax.bitcast_convert_type(bits, jnp.float32)
    return x


def _scores_t(k_tile, q_heads):
    return lax.dot_general(k_tile, q_heads, _NT, preferred_element_type=jnp.float32)


def _stack_heads(q, n_heads):
    lane = lax.broadcasted_iota(jnp.int32, q.shape, 1)
    zero = jnp.zeros((), q.dtype)
    return jnp.concatenate([jnp.where((lane >= h * HEAD_DIM) & (lane < (h + 1) * HEAD_DIM), q, zero)
                            for h in range(n_heads)], axis=0)


def _head_rows(first_head, n_heads):
    return [slice((first_head + h) * HEAD_DIM, (first_head + h + 1) * HEAD_DIM) for h in range(n_heads)]


def _attn_kernel(qa_ref, ka_ref, vta_ref, qb_ref, kb_ref, vtb_ref, qm_ref, mem_ref, gmem_ref, wmkv_ref,
                 bias_ref, o_ref, kmean_ref, km_ref, vmt_ref, selb_ref, m_ref, acc_ref, raw_ref):
    qi = pl.program_id(1)

    @pl.when(qi == 0)
    def _per_batch():
        kb = kb_ref[...].astype(jnp.float32)
        kmean_ref[...] = jnp.mean(kb.reshape(N_BLOCKS, MOBA_BLOCK, WIDTH_B), axis=1)
        mem_n = _rms(mem_ref[...], gmem_ref[...]).astype(jnp.bfloat16)
        km_ref[...] = jnp.dot(mem_n, wmkv_ref[:, :WIDTH_M], preferred_element_type=jnp.float32).astype(jnp.bfloat16)
        vmt_ref[...] = lax.dot_general(wmkv_ref[:, WIDTH_M:], mem_n, _TN,
                                       preferred_element_type=jnp.float32).astype(jnp.bfloat16)

    def keys_of(j):
        return pl.ds(pl.multiple_of(j * TQ, TQ), TQ)

    def both_heads(bias):
        return jnp.concatenate([bias, bias], axis=1)

    a_static = []
    looped = None
    for g, offs in enumerate(_dil_tile_offsets()):
        cols = slice(g * PAIR, (g + 1) * PAIR)
        rows = _head_rows(g * HEADS_PER_DIL_GROUP, HEADS_PER_DIL_GROUP)
        q_heads = _stack_heads(qa_ref[:, cols], HEADS_PER_DIL_GROUP)
        ids = _DIL_BIAS_IDS[g]
        if len(offs) == N_BLOCKS:
            assert looped is None and len(set(ids[1:])) == 1
            looped = (cols, rows, q_heads, ids[1])
            offs = offs[:1]
        for o in offs:
            ks = keys_of(jnp.maximum(qi - o, 0))
            bid = ids[o] if o == 0 else jnp.where(qi >= o, ids[o], _B_ALLNEG)
            a_static.append((ks, bid, cols, rows, q_heads))

    blk = lax.broadcasted_iota(jnp.int32, (N_BLOCKS, 2 * TQ), 0)
    own = keys_of(qi)
    b_heads = []
    for p in range(N_HEADS_B // 2):
        cols = slice(p * PAIR, (p + 1) * PAIR)
        q_heads = _stack_heads(qb_ref[:, cols], 2)
        kmean = kmean_ref[:, cols]
        kmean_hi = kmean.astype(jnp.bfloat16)
        kmean_lo = (kmean - kmean_hi.astype(jnp.float32)).astype(jnp.bfloat16)
        gate = _scores_t(kmean_hi, q_heads) + _scores_t(kmean_lo, q_heads)
        gate = jnp.where(blk < qi, gate, NEG_INF)
        beaten = jnp.zeros(gate.shape, jnp.float32)
        for j in range(N_BLOCKS):
            gj = gate[j:j + 1, :]
            wins_tie = jnp.where(gj >= gate, 1.0, 0.0)
            wins_strict = jnp.where(gj > gate, 1.0, 0.0)
            beaten = beaten + jnp.where(blk > j, wins_tie, wins_strict)
        selb_ref[p] = jnp.where((beaten < MOBA_TOPK) & (blk < qi), 0.0, NEG_INF)
        b_heads.append((cols, _head_rows(2 * p, 2), q_heads))

    def a_scores(part):
        ks, _, cols, _, q_heads = part
        return _scores_t(ka_ref[ks, cols], q_heads)

    def b_scores(p, ks):
        cols, _, q_heads = b_heads[p]
        return _scores_t(kb_ref[ks, cols], q_heads)

    def looped_scores(j):
        cols, _, q_heads, _ = looped
        return _scores_t(ka_ref[keys_of(j), cols], q_heads)

    n_a = len(a_static)
    raw = a_scores(a_static[0])
    a_biased = []
    m_a = None
    for i in range(n_a):
        nxt = a_scores(a_static[i + 1]) if i + 1 < n_a else b_scores(0, own)
        s_t = raw + both_heads(bias_ref[a_static[i][1]])
        part_max = _after(jnp.max(s_t, axis=0, keepdims=True), nxt)
        m_a = part_max if m_a is None else jnp.maximum(m_a, part_max)
        a_biased.append(s_t)
        raw = nxt
    m_heads = _stack_heads(qm_ref[...], N_HEADS_M)
    ahead = [lambda: b_scores(1, own), lambda: b_scores(2, own), lambda: _scores_t(km_ref[...], m_heads),
             lambda: looped_scores(0)]
    later = [raw]
    assert n_a > len(ahead)
    neg_m = -m_a
    acc_a = None
    for i in range(n_a):
        ks, _, _, rows, _ = a_static[i]
        p_t = jnp.exp2(a_biased[i] + neg_m).astype(jnp.bfloat16)
        if i < len(ahead):
            later.append(ahead[i]())
            neg_m = _after(neg_m, later[-1])
        part_acc = _pv([vta_ref[r, ks] for r in rows], p_t)
        acc_a = part_acc if acc_a is None else acc_a + part_acc
    m_ref[0] = m_a
    acc_ref[0] = acc_a

    def single_tile(s_t, v_heads):
        m = jnp.max(s_t, axis=0, keepdims=True)
        return m, _pv(v_heads, jnp.exp2(s_t - m).astype(jnp.bfloat16))

    for p, (_, rows, _) in enumerate(b_heads):
        m_ref[1 + p], acc_ref[1 + p] = single_tile(later[p] + both_heads(bias_ref[_B_CAUSAL]),
                                                   [vtb_ref[r, own] for r in rows])
    _, acc_m = single_tile(later[3], [vmt_ref[r, :] for r in _head_rows(0, N_HEADS_M)])
    raw_ref[0] = later[4]
    raw_ref[1] = b_scores(0, keys_of(0))

    def online_update(c, s_t, shift_row, m_tile, v_heads, next_scores):
        m_old = m_ref[c]
        m_new = jnp.maximum(m_old, m_tile)
        alpha = _after(jnp.exp2(m_old - m_new), next_scores)
        shift = -m_new if shift_row is None else shift_row - m_new
        acc_ref[c] = alpha * acc_ref[c] + _pv(v_heads, jnp.exp2(s_t + shift).astype(jnp.bfloat16))
        m_ref[c] = m_new

    def past_tile(j, _):
        ks = keys_of(j)
        ks_next = keys_of(jnp.minimum(j + 1, qi - 1))
        _, rows_a, _, bid = looped
        n_b = len(b_heads)
        scores = [raw_ref[0] + both_heads(bias_ref[bid]), raw_ref[1]]
        for c in range(1 + n_b):
            ahead = c + 2
            if ahead <= n_b:
                issued = b_scores(ahead - 1, ks)
            elif ahead == n_b + 1:
                issued = looped_scores(jnp.minimum(j + 1, qi - 1))
            else:
                issued = b_scores(0, ks_next)
            scores.append(issued)
            if c == 0:
                online_update(0, scores[0], None, jnp.max(scores[0], axis=0, keepdims=True),
                              [vta_ref[r, ks] for r in rows_a], issued)
            else:
                sel = selb_ref[c - 1, pl.ds(j, 1), :]
                online_update(c, scores[c], sel, jnp.max(scores[c], axis=0, keepdims=True) + sel,
                              [vtb_ref[r, ks] for r in b_heads[c - 1][1]], issued)
        raw_ref[0] = scores[1 + n_b]
        raw_ref[1] = scores[2 + n_b]
        return 0

    lax.fori_loop(0, qi, past_tile, 0)

    col = 0
    for acc in (*[acc_ref[c] for c in range(1 + len(b_heads))], acc_m):
        o_t = acc[:HEAD_DIM] / acc[HEAD_DIM:HEAD_DIM + 1]
        for c in range(o_t.shape[1] // (2 * TQ)):
            pair = jnp.concatenate([o_t[:, (2 * c) * TQ:(2 * c + 1) * TQ],
                                    o_t[:, (2 * c + 1) * TQ:(2 * c + 2) * TQ]], axis=0)
            o_ref[:, col:col + LANES] = pair.T.astype(jnp.bfloat16)
            col += LANES


def _attn_call(qka, vta, qkb, vtb, qm, mem2, gmem, wmkv, layer, bias, batch, seq):
    nq = seq // TQ
    n_mem = mem2.shape[0] // batch
    per_layer = lambda b, q: (layer, 0, 0)
    return pl.pallas_call(
        _attn_kernel,
        grid=(batch, nq),
        in_specs=[
            pl.BlockSpec((TQ, WIDTH_A), lambda b, q: (b * nq + q, 0)),
            pl.BlockSpec((seq, WIDTH_A), lambda b, q: (b, 1)),
            pl.BlockSpec((WIDTH_A, seq), lambda b, q: (b, 0)),
            pl.BlockSpec((TQ, WIDTH_B), lambda b, q: (b * nq + q, 0)),
            pl.BlockSpec((seq, WIDTH_B), lambda b, q: (b, 1)),
            pl.BlockSpec((WIDTH_B, seq), lambda b, q: (b, 0)),
            pl.BlockSpec((TQ, WIDTH_M), lambda b, q: (b * nq + q, 0)),
            pl.BlockSpec((n_mem, D_MODEL), lambda b, q: (b, 0)),
            pl.BlockSpec((None, 1, D_MODEL), per_layer),
            pl.BlockSpec((None, D_MODEL, 2 * WIDTH_M), per_layer),
            pl.BlockSpec(bias.shape, lambda b, q: (0, 0, 0)),
        ],
        out_specs=pl.BlockSpec((TQ, WIDTH_O), lambda b, q: (b * nq + q, 0)),
        out_shape=jax.ShapeDtypeStruct((batch * seq, WIDTH_O), jnp.bfloat16),
        scratch_shapes=[
            pltpu.VMEM((N_BLOCKS, WIDTH_B), jnp.float32),
            pltpu.VMEM((n_mem, WIDTH_M), jnp.bfloat16),
            pltpu.VMEM((WIDTH_M, n_mem), jnp.bfloat16),
            pltpu.VMEM((N_HEADS_B // 2, N_BLOCKS, 2 * TQ), jnp.float32),
            pltpu.VMEM((1 + N_HEADS_B // 2, 1, 2 * TQ), jnp.float32),
            pltpu.VMEM((1 + N_HEADS_B // 2, HEAD_DIM + SUM_ROWS, 2 * TQ), jnp.float32),
            pltpu.VMEM((2, TQ, 2 * TQ), jnp.float32),
        ],
        compiler_params=pltpu.CompilerParams(dimension_semantics=("parallel", "arbitrary"),
                                             vmem_limit_bytes=VMEM_LIMIT),
        name="attn",
    )(qka, qka, vta, qkb, qkb, vtb, qm, mem2, gmem, wmkv, bias)


def _mix_kernel(x_ref, o_ref, g_ref, win_ref, wpa_ref, wpb_ref, wpm_ref, wo_ref, out_ref):
    x = x_ref[...]
    h = _rms(x, g_ref[...]).astype(jnp.bfloat16)
    y = None
    col = 0
    for i, wp_ref in enumerate((wpa_ref, wpb_ref, wpm_ref)):
        width = wp_ref.shape[0]
        gate = jax.nn.sigmoid(jnp.dot(h, win_ref[:, COL_GATES + i * D_MODEL:COL_GATES + (i + 1) * D_MODEL],
                                      preferred_element_type=jnp.float32))
        branch = gate * jnp.dot(o_ref[:, col:col + width], wp_ref[...], preferred_element_type=jnp.float32)
        y = branch if y is None else y + branch
        col += width
    out_ref[...] = x + jnp.dot(y.astype(jnp.bfloat16), wo_ref[...], preferred_element_type=jnp.float32)


def _mix_call(x2, o, g, w_in, wpa, wpb, wpm, wo, layer):
    t = x2.shape[0]
    tm = TM_MIX
    per_layer = lambda i: (layer, 0, 0)
    return pl.pallas_call(
        _mix_kernel,
        grid=(t // tm,),
        in_specs=[
            pl.BlockSpec((tm, D_MODEL), lambda i: (i, 0)),
            pl.BlockSpec((tm, WIDTH_O), lambda i: (i, 0)),
            pl.BlockSpec((None, 1, D_MODEL), per_layer),
            pl.BlockSpec((None,) + w_in.shape[1:], per_layer),
            pl.BlockSpec((None,) + wpa.shape[1:], per_layer),
            pl.BlockSpec((None,) + wpb.shape[1:], per_layer),
            pl.BlockSpec((None,) + wpm.shape[1:], per_layer),
            pl.BlockSpec((None,) + wo.shape[1:], per_layer),
        ],
        out_specs=pl.BlockSpec((tm, D_MODEL), lambda i: (i, 0)),
        out_shape=jax.ShapeDtypeStruct((t, D_MODEL), jnp.float32),
        compiler_params=pltpu.CompilerParams(dimension_semantics=("parallel",), vmem_limit_bytes=VMEM_LIMIT),
        name="gated_mix",
    )(x2, o, g, w_in, wpa, wpb, wpm, wo)


def _mlp_kernel(x_ref, g_ref, wup_ref, wdown_ref, gfin_ref, out_ref, *, final_norm):
    x = x_ref[...]
    hm = _rms(x, g_ref[...]).astype(jnp.bfloat16)
    acc = x
    for c in range(D_FF // FF_CHUNK):
        u = jnp.dot(hm, wup_ref[:, c * FF_CHUNK:(c + 1) * FF_CHUNK], preferred_element_type=jnp.float32)
        u = jnp.square(jnp.maximum(u, 0.0)).astype(jnp.bfloat16)
        acc = acc + jnp.dot(u, wdown_ref[c * FF_CHUNK:(c + 1) * FF_CHUNK, :], preferred_element_type=jnp.float32)
    out_ref[...] = _rms(acc, gfin_ref[...]) if final_norm else acc


def _mlp_call(x2, g, wup, wdown, gfin, layer, final_norm):
    t = x2.shape[0]
    tm = TM_MLP
    const = lambda i: (0, 0)
    per_layer = lambda i: (layer, 0, 0)
    return pl.pallas_call(
        functools.partial(_mlp_kernel, final_norm=final_norm),
        grid=(t // tm,),
        in_specs=[
            pl.BlockSpec((tm, D_MODEL), lambda i: (i, 0)),
            pl.BlockSpec((None, 1, D_MODEL), per_layer),
            pl.BlockSpec((None,) + wup.shape[1:], per_layer),
            pl.BlockSpec((None,) + wdown.shape[1:], per_layer),
            pl.BlockSpec((1, D_MODEL), const),
        ],
        out_specs=pl.BlockSpec((tm, D_MODEL), lambda i: (i, 0)),
        out_shape=jax.ShapeDtypeStruct((t, D_MODEL), jnp.float32),
        compiler_params=pltpu.CompilerParams(dimension_semantics=("parallel",), vmem_limit_bytes=VMEM_LIMIT),
        name="mlp",
    )(x2, g, wup, wdown, gfin)


def _rope_tables(seq):
    half = ROT_DIM // 2
    inv_freq = 1.0 / (ROPE_THETA ** (jnp.arange(0, ROT_DIM, 2, dtype=jnp.float32) / ROT_DIM))
    ang = jnp.arange(seq, dtype=jnp.int32).astype(jnp.float32)[:, None] * inv_freq[None, :]
    cos, sin = jnp.cos(ang), jnp.sin(ang)
    ones = jnp.ones((seq, HEAD_DIM - ROT_DIM), jnp.float32)
    zeros_half = jnp.zeros((seq, half), jnp.float32)
    zeros_rest = jnp.zeros((seq, HEAD_DIM - ROT_DIM), jnp.float32)
    c_head = jnp.concatenate([cos, cos, ones], axis=1)
    lo_head = jnp.concatenate([-sin, zeros_half, zeros_rest], axis=1)
    hi_head = jnp.concatenate([zeros_half, sin, zeros_rest], axis=1)
    rep = LANES // HEAD_DIM
    return jnp.tile(c_head, (1, rep)), jnp.tile(lo_head, (1, rep)), jnp.tile(hi_head, (1, rep))


def kernel(x, mem, norm_mix, w_in, w_proj_a, w_proj_b, w_proj_m, w_out, norm_mem, w_mem_kv, norm_mlp, w_up,
           w_down, norm_final):
    batch, seq, d = x.shape
    assert d == D_MODEL and seq == SEQ and seq % TM_QKV == 0
    depth = w_in.shape[0]
    bf = jnp.bfloat16
    x2 = x.reshape(batch * seq, d)
    mem2 = mem.reshape(batch * mem.shape[1], d)
    rope_c, rope_s1, rope_s2 = _rope_tables(seq)
    bias = jnp.asarray(_BIAS_NP)
    w_in_bf, w_pa, w_pb, w_pm = w_in.astype(bf), w_proj_a.astype(bf), w_proj_b.astype(bf), w_proj_m.astype(bf)
    w_o, w_mkv, w_up_bf, w_down_bf = w_out.astype(bf), w_mem_kv.astype(bf), w_up.astype(bf), w_down.astype(bf)
    g_mix, g_mem, g_mlp = (g.reshape(depth, 1, d) for g in (norm_mix, norm_mem, norm_mlp))
    for l in range(depth):
        qka, vta, qkb, vtb, qm = _qkv_call(x2, g_mix, w_in_bf, l, rope_c, rope_s1, rope_s2, batch, seq)
        o = _attn_call(qka, vta, qkb, vtb, qm, mem2, g_mem, w_mkv, l, bias, batch, seq)
        x2 = _mix_call(x2, o, g_mix, w_in_bf, w_pa, w_pb, w_pm, w_o, l)
        x2 = _mlp_call(x2, g_mlp, w_up_bf, w_down_bf, norm_final.reshape(1, d), l, final_norm=(l == depth - 1))
    return x2.reshape(batch, seq, d)
```

```python
import functools

import jax
import jax.numpy as jnp
import numpy as np
from jax import lax
from jax.experimental import pallas as pl
from jax.experimental.pallas import tpu as pltpu

D_MODEL = 1024
SEQ = 2048
HEAD_DIM = 64
ROT_DIM = HEAD_DIM // 4
ROPE_THETA = 500000.0
DIL_GROUPS = ((128, 1), (512, 4), (2048, 16))
HEADS_PER_DIL_GROUP = 2
N_HEADS_A = len(DIL_GROUPS) * HEADS_PER_DIL_GROUP
N_HEADS_B = 6
N_HEADS_M = 4
MOBA_BLOCK = 256
MOBA_TOPK = 3
N_BLOCKS = SEQ // MOBA_BLOCK
D_FF = 4 * D_MODEL
WIDTH_A = N_HEADS_A * HEAD_DIM
WIDTH_A_OUT = HEADS_PER_DIL_GROUP * HEAD_DIM
WIDTH_B = N_HEADS_B * HEAD_DIM
WIDTH_M = N_HEADS_M * HEAD_DIM
WIDTH_O = WIDTH_A_OUT + WIDTH_B + WIDTH_M
RMS_EPS = 1e-6
NEG_INF = -1e30
Q_SCALE = HEAD_DIM ** -0.5
LOG2_E = 1.4426950408889634

LANES = 128
PAIR = 2 * HEAD_DIM
SUM_ROWS = 16
TQ = MOBA_BLOCK
CLASS_GROUP = len(DIL_GROUPS) - 1
CLASS_STRIDE = DIL_GROUPS[CLASS_GROUP][1]
CLASS_LEN = SEQ // CLASS_STRIDE
CLASSES_PER_TILE = TQ // CLASS_LEN
TM_QKV = 512
TM_MIX = 512
TM_MLP = 512
FF_CHUNK = 1024
VMEM_LIMIT = 56 * 1024 * 1024

_B_ALLNEG = 0
_B_CAUSAL = 1


TILE_GROUPS = DIL_GROUPS[:CLASS_GROUP]
assert DIL_GROUPS[CLASS_GROUP][0] >= SEQ and TQ % CLASS_LEN == 0


def _dil_tile_offsets():
    return tuple(tuple(range(min((w + TQ - 1) // TQ, N_BLOCKS - 1) + 1)) for w, _ in TILE_GROUPS)


def _build_bias_tiles():
    c = np.arange(TQ)[:, None]
    r = np.arange(TQ)[None, :]
    tiles = [np.zeros((TQ, TQ), bool), (r - c) >= 0]
    ids = []
    for (w, d), offs in zip(TILE_GROUPS, _dil_tile_offsets()):
        per_off = []
        for o in offs:
            diff = o * TQ + r - c
            per_off.append((diff >= 0) & (diff <= w) & (diff % d == 0))
        uniq, gid = [], []
        for t in per_off:
            for k, u in enumerate(uniq):
                if np.array_equal(t, u):
                    gid.append(k)
                    break
            else:
                uniq.append(t)
                gid.append(len(uniq) - 1)
        ids.append(tuple(len(tiles) + k for k in gid))
        tiles.extend(uniq)
    bias = np.where(np.stack(tiles), 0.0, NEG_INF).astype(np.float32)
    return bias, tuple(ids)


_BIAS_NP, _DIL_BIAS_IDS = _build_bias_tiles()
_pos = np.arange(CLASS_LEN)
_CLASS_BIAS_NP = np.where(_pos[None, :] >= _pos[:, None], 0.0, NEG_INF).astype(np.float32)

_NT = (((1,), (1,)), ((), ()))
_TN = (((0,), (1,)), ((), ()))
COL_B = 3 * WIDTH_A
COL_M = COL_B + 3 * WIDTH_B
COL_GATES = COL_M + WIDTH_M


def _rms(x, g):
    return x * lax.rsqrt(jnp.mean(x * x, axis=-1, keepdims=True) + RMS_EPS) * g


def _qkv_kernel(x_ref, g_ref, win_ref, c_ref, s1_ref, s2_ref, qka_ref, vta_ref, qkb_ref, vtb_ref, qm_ref, cls_ref):
    h = _rms(x_ref[...], g_ref[...]).astype(jnp.bfloat16)
    cos = c_ref[...]
    sin_lo = s1_ref[...]
    sin_hi = s2_ref[...]
    for col0, qk_ref, vt_ref, width in ((0, qka_ref, vta_ref, WIDTH_A), (COL_B, qkb_ref, vtb_ref, WIDTH_B)):
        z = jnp.dot(h, win_ref[:, col0:col0 + 2 * width], preferred_element_type=jnp.float32)
        for blk in range(2 * width // LANES):
            zb = z[:, blk * LANES:(blk + 1) * LANES]
            rb = zb * cos + pltpu.roll(zb, LANES - ROT_DIM // 2, 1) * sin_lo + pltpu.roll(zb, ROT_DIM // 2, 1) * sin_hi
            if blk < width // LANES:
                rb = rb * (Q_SCALE * LOG2_E)
            qk_ref[:, blk * LANES:(blk + 1) * LANES] = rb.astype(jnp.bfloat16)
            if col0 == 0 and blk % (width // LANES) == CLASS_GROUP:
                part = blk // (width // LANES)
                cls_ref[part] = rb
        vt = lax.dot_general(win_ref[:, col0 + 2 * width:col0 + 3 * width], h, _TN,
                             preferred_element_type=jnp.float32)
        vt_ref[...] = vt.astype(jnp.bfloat16)
        if col0 == 0:
            cls_ref[2] = vt[CLASS_GROUP * PAIR:(CLASS_GROUP + 1) * PAIR, :].T
    qm = jnp.dot(h, win_ref[:, COL_M:COL_GATES], preferred_element_type=jnp.float32) * (Q_SCALE * LOG2_E)
    qm_ref[...] = qm.astype(jnp.bfloat16)


def _qkv_call(x2, g, w_in, layer, rope_c, rope_s1, rope_s2, batch, seq):
    t = x2.shape[0]
    tm = TM_QKV
    nts = seq // tm
    per_layer = lambda i: (layer, 0, 0)
    return pl.pallas_call(
        _qkv_kernel,
        grid=(t // tm,),
        in_specs=[
            pl.BlockSpec((tm, D_MODEL), lambda i: (i, 0)),
            pl.BlockSpec((None, 1, D_MODEL), per_layer),
            pl.BlockSpec((None, D_MODEL, COL_GATES), per_layer),
            pl.BlockSpec((tm, LANES), lambda i: (i % nts, 0)),
            pl.BlockSpec((tm, LANES), lambda i: (i % nts, 0)),
            pl.BlockSpec((tm, LANES), lambda i: (i % nts, 0)),
        ],
        out_specs=[
            pl.BlockSpec((tm, 2 * WIDTH_A), lambda i: (i, 0)),
            pl.BlockSpec((WIDTH_A, tm), lambda i: (i // nts, i % nts)),
            pl.BlockSpec((tm, 2 * WIDTH_B), lambda i: (i, 0)),
            pl.BlockSpec((WIDTH_B, tm), lambda i: (i // nts, i % nts)),
            pl.BlockSpec((tm, WIDTH_M), lambda i: (i, 0)),
            pl.BlockSpec((3, tm, PAIR), lambda i: (0, i, 0)),
        ],
        out_shape=[
            jax.ShapeDtypeStruct((t, 2 * WIDTH_A), jnp.bfloat16),
            jax.ShapeDtypeStruct((batch * WIDTH_A, seq), jnp.bfloat16),
            jax.ShapeDtypeStruct((t, 2 * WIDTH_B), jnp.bfloat16),
            jax.ShapeDtypeStruct((batch * WIDTH_B, seq), jnp.bfloat16),
            jax.ShapeDtypeStruct((t, WIDTH_M), jnp.bfloat16),
            jax.ShapeDtypeStruct((3, t, PAIR), jnp.float32),
        ],
        compiler_params=pltpu.CompilerParams(dimension_semantics=("parallel",), vmem_limit_bytes=VMEM_LIMIT),
        name="qkv_proj",
    )(x2, g, w_in, rope_c, rope_s1, rope_s2)


def _pv(v_heads, p_bf):
    ones = jnp.ones((SUM_ROWS, p_bf.shape[0]), jnp.bfloat16)
    nq = p_bf.shape[1] // len(v_heads)
    return jnp.concatenate([jnp.dot(jnp.concatenate([v, ones], axis=0), p_bf[:, h * nq:(h + 1) * nq],
                                    preferred_element_type=jnp.float32)
                            for h, v in enumerate(v_heads)], axis=1)


def _after(x, *matmul_results):
    for r in matmul_results:
        bits = lax.bitcast_convert_type(r[-1:, :x.shape[1]], jnp.uint32)
        bits = lax.shift_right_logical(lax.shift_right_logical(bits, jnp.uint32(16)), jnp.uint32(16))
        x = x + lax.bitcast_convert_type(bits, jnp.float32)
    return x


def _scores_t(k_tile, q_heads):
    return lax.dot_general(k_tile, q_heads, _NT, preferred_element_type=jnp.float32)


def _stack_heads(q, n_heads):
    lane = lax.broadcasted_iota(jnp.int32, q.shape, 1)
    zero = jnp.zeros((), q.dtype)
    return jnp.concatenate([jnp.where((lane >= h * HEAD_DIM) & (lane < (h + 1) * HEAD_DIM), q, zero)
                            for h in range(n_heads)], axis=0)


def _head_rows(first_head, n_heads):
    return [slice((first_head + h) * HEAD_DIM, (first_head + h + 1) * HEAD_DIM) for h in range(n_heads)]


def _lanes_to_rows(row, n_heads):
    nq = row.shape[1] // n_heads
    return jnp.concatenate([jnp.broadcast_to(row[:, h * nq:(h + 1) * nq], (HEAD_DIM, nq)) for h in range(n_heads)],
                           axis=0)


def _heads_to_rows(o_t, n_heads):
    nq = o_t.shape[1] // n_heads
    return jnp.concatenate([o_t[:, h * nq:(h + 1) * nq] for h in range(n_heads)], axis=0)


def _attn_kernel(qa_ref, ka_ref, vta_ref, qb_ref, kb_ref, vtb_ref, qm_ref, cls_ref, mem_ref, gmem_ref, wmkv_ref,
                 bias_ref, cbias_ref, o_ref, lse_ref, ocls_ref, lsecls_ref,
                 kmean_ref, km_ref, vmt_ref, selb_ref, m_ref, acc_ref, raw_ref):
    qi = pl.program_id(1)

    @pl.when(qi == 0)
    def _per_batch():
        kb = kb_ref[...].astype(jnp.float32)
        kmean_ref[...] = jnp.mean(kb.reshape(N_BLOCKS, MOBA_BLOCK, WIDTH_B), axis=1)
        mem_n = _rms(mem_ref[...], gmem_ref[...]).astype(jnp.bfloat16)
        km_ref[...] = jnp.dot(mem_n, wmkv_ref[:, :WIDTH_M], preferred_element_type=jnp.float32).astype(jnp.bfloat16)
        vmt_ref[...] = lax.dot_general(wmkv_ref[:, WIDTH_M:], mem_n, _TN,
                                       preferred_element_type=jnp.float32).astype(jnp.bfloat16)

    def keys_of(j):
        return pl.ds(pl.multiple_of(j * TQ, TQ), TQ)

    def both_heads(bias):
        return jnp.concatenate([bias, bias], axis=1)

    def single_tile(s_t, v_heads):
        m = jnp.max(s_t, axis=0, keepdims=True)
        return m, _pv(v_heads, jnp.exp2(s_t - m).astype(jnp.bfloat16))

    for c in range(CLASSES_PER_TILE):
        rows = pl.ds(qi * CLASSES_PER_TILE + c, CLASS_LEN, stride=CLASS_STRIDE)
        q_heads = _stack_heads(cls_ref[0, rows, :].astype(jnp.bfloat16), HEADS_PER_DIL_GROUP)
        k_cls = cls_ref[1, rows, :].astype(jnp.bfloat16)
        v_t = cls_ref[2, rows, :].T.astype(jnp.bfloat16)
        m, acc = single_tile(_scores_t(k_cls, q_heads) + both_heads(cbias_ref[...]),
                             [v_t[r] for r in _head_rows(0, HEADS_PER_DIL_GROUP)])
        l = acc[HEAD_DIM:HEAD_DIM + 1]
        out = slice(c * CLASS_LEN, (c + 1) * CLASS_LEN)
        ocls_ref[out, :] = _heads_to_rows(acc[:HEAD_DIM] / l, HEADS_PER_DIL_GROUP).T
        lsecls_ref[out, :] = _lanes_to_rows(m + jnp.log2(l), HEADS_PER_DIL_GROUP).T

    a_static = []
    for g, offs in enumerate(_dil_tile_offsets()):
        cols = slice(g * PAIR, (g + 1) * PAIR)
        rows = _head_rows(g * HEADS_PER_DIL_GROUP, HEADS_PER_DIL_GROUP)
        q_heads = _stack_heads(qa_ref[:, cols], HEADS_PER_DIL_GROUP)
        ids = _DIL_BIAS_IDS[g]
        for o in offs:
            ks = keys_of(jnp.maximum(qi - o, 0))
            bid = ids[o] if o == 0 else jnp.where(qi >= o, ids[o], _B_ALLNEG)
            a_static.append((ks, bid, cols, rows, q_heads))

    blk = lax.broadcasted_iota(jnp.int32, (N_BLOCKS, 2 * TQ), 0)
    own = keys_of(qi)
    b_heads = []
    for p in range(N_HEADS_B // 2):
        cols = slice(p * PAIR, (p + 1) * PAIR)
        q_heads = _stack_heads(qb_ref[:, cols], 2)
        kmean = kmean_ref[:, cols]
        kmean_hi = kmean.astype(jnp.bfloat16)
        kmean_lo = (kmean - kmean_hi.astype(jnp.float32)).astype(jnp.bfloat16)
        gate = _scores_t(kmean_hi, q_heads) + _scores_t(kmean_lo, q_heads)
        gate = jnp.where(blk < qi, gate, NEG_INF)
        beaten = jnp.zeros(gate.shape, jnp.float32)
        for j in range(N_BLOCKS):
            gj = gate[j:j + 1, :]
            wins_tie = jnp.where(gj >= gate, 1.0, 0.0)
            wins_strict = jnp.where(gj > gate, 1.0, 0.0)
            beaten = beaten + jnp.where(blk > j, wins_tie, wins_strict)
        selb_ref[p] = jnp.where((beaten < MOBA_TOPK) & (blk < qi), 0.0, NEG_INF)
        b_heads.append((cols, _head_rows(2 * p, 2), q_heads))
    n_b = len(b_heads)

    def a_scores(part):
        ks, _, cols, _, q_heads = part
        return _scores_t(ka_ref[ks, cols], q_heads)

    def b_scores(p, ks):
        cols, _, q_heads = b_heads[p]
        return _scores_t(kb_ref[ks, cols], q_heads)

    n_a = len(a_static)
    raw = a_scores(a_static[0])
    a_biased = []
    m_a = None
    for i in range(n_a):
        nxt = a_scores(a_static[i + 1]) if i + 1 < n_a else b_scores(0, own)
        s_t = raw + both_heads(bias_ref[a_static[i][1]])
        part_max = _after(jnp.max(s_t, axis=0, keepdims=True), nxt)
        m_a = part_max if m_a is None else jnp.maximum(m_a, part_max)
        a_biased.append(s_t)
        raw = nxt
    m_heads = _stack_heads(qm_ref[...], N_HEADS_M)
    ahead = [lambda: b_scores(1, own), lambda: b_scores(2, own), lambda: _scores_t(km_ref[...], m_heads),
             lambda: b_scores(0, keys_of(0))]
    later = [raw]
    assert n_a > len(ahead)
    neg_m = -m_a
    acc_a = None
    for i in range(n_a):
        ks, _, _, rows, _ = a_static[i]
        p_t = jnp.exp2(a_biased[i] + neg_m).astype(jnp.bfloat16)
        if i < len(ahead):
            later.append(ahead[i]())
            neg_m = _after(neg_m, later[-1])
        part_acc = _pv([vta_ref[r, ks] for r in rows], p_t)
        acc_a = part_acc if acc_a is None else acc_a + part_acc
    l_a = acc_a[HEAD_DIM:HEAD_DIM + 1]
    o_tiles = [acc_a[:HEAD_DIM] / l_a]
    lse_ref[...] = _lanes_to_rows(m_a + jnp.log2(l_a), HEADS_PER_DIL_GROUP).T

    for p, (_, rows, _) in enumerate(b_heads):
        m_ref[p], acc_ref[p] = single_tile(later[p] + both_heads(bias_ref[_B_CAUSAL]),
                                           [vtb_ref[r, own] for r in rows])
    _, acc_m = single_tile(later[3], [vmt_ref[r, :] for r in _head_rows(0, N_HEADS_M)])
    raw_ref[0] = later[4]
    raw_ref[1] = b_scores(1, keys_of(0))

    def past_tile(j, _):
        ks = keys_of(j)
        ks_next = keys_of(jnp.minimum(j + 1, qi - 1))
        scores = [raw_ref[0], raw_ref[1]]
        for c in range(n_b):
            issued = b_scores(c + 2, ks) if c + 2 < n_b else b_scores(c + 2 - n_b, ks_next)
            scores.append(issued)
            sel = selb_ref[c, pl.ds(j, 1), :]
            m_old = m_ref[c]
            m_new = jnp.maximum(m_old, jnp.max(scores[c], axis=0, keepdims=True) + sel)
            alpha = _after(jnp.exp2(m_old - m_new), issued)
            p_t = jnp.exp2(scores[c] + (sel - m_new)).astype(jnp.bfloat16)
            acc_ref[c] = alpha * acc_ref[c] + _pv([vtb_ref[r, ks] for r in b_heads[c][1]], p_t)
            m_ref[c] = m_new
        raw_ref[0] = scores[n_b]
        raw_ref[1] = scores[n_b + 1]
        return 0

    lax.fori_loop(0, qi, past_tile, 0)

    for acc in (*[acc_ref[c] for c in range(n_b)], acc_m):
        o_tiles.append(acc[:HEAD_DIM] / acc[HEAD_DIM:HEAD_DIM + 1])
    col = 0
    for o_t in o_tiles:
        for c in range(o_t.shape[1] // (2 * TQ)):
            o_ref[:, col:col + LANES] = _heads_to_rows(o_t[:, 2 * c * TQ:(2 * c + 2) * TQ], 2).T.astype(jnp.bfloat16)
            col += LANES


def _attn_call(qka, vta, qkb, vtb, qm, cls, mem2, gmem, wmkv, layer, bias, cbias, batch, seq):
    nq = seq // TQ
    n_mem = mem2.shape[0] // batch
    per_layer = lambda b, q: (layer, 0, 0)
    per_tile = lambda b, q: (b * nq + q, 0)
    return pl.pallas_call(
        _attn_kernel,
        grid=(batch, nq),
        in_specs=[
            pl.BlockSpec((TQ, WIDTH_A), lambda b, q: (b * nq + q, 0)),
            pl.BlockSpec((seq, WIDTH_A), lambda b, q: (b, 1)),
            pl.BlockSpec((WIDTH_A, seq), lambda b, q: (b, 0)),
            pl.BlockSpec((TQ, WIDTH_B), lambda b, q: (b * nq + q, 0)),
            pl.BlockSpec((seq, WIDTH_B), lambda b, q: (b, 1)),
            pl.BlockSpec((WIDTH_B, seq), lambda b, q: (b, 0)),
            pl.BlockSpec((TQ, WIDTH_M), lambda b, q: (b * nq + q, 0)),
            pl.BlockSpec((3, seq, PAIR), lambda b, q: (0, b, 0)),
            pl.BlockSpec((n_mem, D_MODEL), lambda b, q: (b, 0)),
            pl.BlockSpec((None, 1, D_MODEL), per_layer),
            pl.BlockSpec((None, D_MODEL, 2 * WIDTH_M), per_layer),
            pl.BlockSpec(bias.shape, lambda b, q: (0, 0, 0)),
            pl.BlockSpec(cbias.shape, lambda b, q: (0, 0)),
        ],
        out_specs=[pl.BlockSpec((TQ, WIDTH_O), per_tile)] + [pl.BlockSpec((TQ, PAIR), per_tile)] * 3,
        out_shape=[jax.ShapeDtypeStruct((batch * seq, WIDTH_O), jnp.bfloat16)]
        + [jax.ShapeDtypeStruct((batch * seq, PAIR), jnp.float32)] * 3,
        scratch_shapes=[
            pltpu.VMEM((N_BLOCKS, WIDTH_B), jnp.float32),
            pltpu.VMEM((n_mem, WIDTH_M), jnp.bfloat16),
            pltpu.VMEM((WIDTH_M, n_mem), jnp.bfloat16),
            pltpu.VMEM((N_HEADS_B // 2, N_BLOCKS, 2 * TQ), jnp.float32),
            pltpu.VMEM((N_HEADS_B // 2, 1, 2 * TQ), jnp.float32),
            pltpu.VMEM((N_HEADS_B // 2, HEAD_DIM + SUM_ROWS, 2 * TQ), jnp.float32),
            pltpu.VMEM((2, TQ, 2 * TQ), jnp.float32),
        ],
        compiler_params=pltpu.CompilerParams(dimension_semantics=("parallel", "arbitrary"),
                                             vmem_limit_bytes=VMEM_LIMIT),
        name="attn",
    )(qka, qka, vta, qkb, qkb, vtb, qm, cls, mem2, gmem, wmkv, bias, cbias)


def _mix_kernel(x_ref, o_ref, lse_ref, ocls_ref, lsecls_ref, g_ref, win_ref, wpa_ref, wpb_ref, wpm_ref, wo_ref,
                out_ref, ocn_ref, lcn_ref):
    tm = x_ref.shape[0]
    per_class = tm // CLASS_STRIDE
    first = (pl.program_id(0) % (SEQ // tm)) * per_class
    for r in range(CLASS_STRIDE):
        src = pl.ds(pl.multiple_of(r * CLASS_LEN + first, per_class), per_class)
        dst = pl.ds(r, per_class, stride=CLASS_STRIDE)
        ocn_ref[dst, :] = ocls_ref[src, :]
        lcn_ref[dst, :] = lsecls_ref[src, :]
    w_cls = 1.0 / (1.0 + jnp.exp2(lse_ref[...] - lcn_ref[...]))
    o_tiled = o_ref[:, :WIDTH_A_OUT].astype(jnp.float32)
    o_a = (o_tiled + w_cls * (ocn_ref[...] - o_tiled)).astype(jnp.bfloat16)
    x = x_ref[...]
    h = _rms(x, g_ref[...]).astype(jnp.bfloat16)
    y = None
    col = 0
    for i, wp_ref in enumerate((wpa_ref, wpb_ref, wpm_ref)):
        width = wp_ref.shape[0]
        gate = jax.nn.sigmoid(jnp.dot(h, win_ref[:, COL_GATES + i * D_MODEL:COL_GATES + (i + 1) * D_MODEL],
                                      preferred_element_type=jnp.float32))
        o_i = o_a if i == 0 else o_ref[:, col:col + width]
        branch = gate * jnp.dot(o_i, wp_ref[...], preferred_element_type=jnp.float32)
        y = branch if y is None else y + branch
        col += width
    out_ref[...] = x + jnp.dot(y.astype(jnp.bfloat16), wo_ref[...], preferred_element_type=jnp.float32)


def _mix_call(x2, o, lse, ocls, lsecls, g, w_in, wpa, wpb, wpm, wo, layer):
    t = x2.shape[0]
    tm = TM_MIX
    per_layer = lambda i: (layer, 0, 0)
    return pl.pallas_call(
        _mix_kernel,
        grid=(t // tm,),
        in_specs=[
            pl.BlockSpec((tm, D_MODEL), lambda i: (i, 0)),
            pl.BlockSpec((tm, WIDTH_O), lambda i: (i, 0)),
            pl.BlockSpec((tm, PAIR), lambda i: (i, 0)),
            pl.BlockSpec((SEQ, PAIR), lambda i: (i // (SEQ // tm), 0)),
            pl.BlockSpec((SEQ, PAIR), lambda i: (i // (SEQ // tm), 0)),
            pl.BlockSpec((None, 1, D_MODEL), per_layer),
            pl.BlockSpec((None,) + w_in.shape[1:], per_layer),
            pl.BlockSpec((None,) + wpa.shape[1:], per_layer),
            pl.BlockSpec((None,) + wpb.shape[1:], per_layer),
            pl.BlockSpec((None,) + wpm.shape[1:], per_layer),
            pl.BlockSpec((None,) + wo.shape[1:], per_layer),
        ],
        out_specs=pl.BlockSpec((tm, D_MODEL), lambda i: (i, 0)),
        out_shape=jax.ShapeDtypeStruct((t, D_MODEL), jnp.float32),
        scratch_shapes=[pltpu.VMEM((tm, PAIR), jnp.float32), pltpu.VMEM((tm, PAIR), jnp.float32)],
        compiler_params=pltpu.CompilerParams(dimension_semantics=("parallel",), vmem_limit_bytes=VMEM_LIMIT),
        name="gated_mix",
    )(x2, o, lse, ocls, lsecls, g, w_in, wpa, wpb, wpm, wo)


def _mlp_kernel(x_ref, g_ref, wup_ref, wdown_ref, gfin_ref, out_ref, *, final_norm):
    x = x_ref[...]
    hm = _rms(x, g_ref[...]).astype(jnp.bfloat16)
    acc = x
    for c in range(D_FF // FF_CHUNK):
        u = jnp.dot(hm, wup_ref[:, c * FF_CHUNK:(c + 1) * FF_CHUNK].astype(jnp.bfloat16),
                    preferred_element_type=jnp.float32)
        u = jnp.square(jnp.maximum(u, 0.0)).astype(jnp.bfloat16)
        acc = acc + jnp.dot(u, wdown_ref[c * FF_CHUNK:(c + 1) * FF_CHUNK, :].astype(jnp.bfloat16),
                            preferred_element_type=jnp.float32)
    out_ref[...] = _rms(acc, gfin_ref[...]) if final_norm else acc


def _mlp_call(x2, g, wup, wdown, gfin, layer, final_norm):
    t = x2.shape[0]
    tm = TM_MLP
    const = lambda i: (0, 0)
    per_layer = lambda i: (layer, 0, 0)
    return pl.pallas_call(
        functools.partial(_mlp_kernel, final_norm=final_norm),
        grid=(t // tm,),
        in_specs=[
            pl.BlockSpec((tm, D_MODEL), lambda i: (i, 0)),
            pl.BlockSpec((None, 1, D_MODEL), per_layer),
            pl.BlockSpec((None,) + wup.shape[1:], per_layer, pipeline_mode=pl.Buffered(1)),
            pl.BlockSpec((None,) + wdown.shape[1:], per_layer, pipeline_mode=pl.Buffered(1)),
            pl.BlockSpec((1, D_MODEL), const),
        ],
        out_specs=pl.BlockSpec((tm, D_MODEL), lambda i: (i, 0)),
        out_shape=jax.ShapeDtypeStruct((t, D_MODEL), jnp.float32),
        compiler_params=pltpu.CompilerParams(dimension_semantics=("parallel",), vmem_limit_bytes=VMEM_LIMIT),
        name="mlp",
    )(x2, g, wup, wdown, gfin)


def _rope_tables(seq):
    half = ROT_DIM // 2
    inv_freq = 1.0 / (ROPE_THETA ** (jnp.arange(0, ROT_DIM, 2, dtype=jnp.float32) / ROT_DIM))
    ang = jnp.arange(seq, dtype=jnp.int32).astype(jnp.float32)[:, None] * inv_freq[None, :]
    cos, sin = jnp.cos(ang), jnp.sin(ang)
    ones = jnp.ones((seq, HEAD_DIM - ROT_DIM), jnp.float32)
    zeros_half = jnp.zeros((seq, half), jnp.float32)
    zeros_rest = jnp.zeros((seq, HEAD_DIM - ROT_DIM), jnp.float32)
    c_head = jnp.concatenate([cos, cos, ones], axis=1)
    lo_head = jnp.concatenate([-sin, zeros_half, zeros_rest], axis=1)
    hi_head = jnp.concatenate([zeros_half, sin, zeros_rest], axis=1)
    rep = LANES // HEAD_DIM
    return jnp.tile(c_head, (1, rep)), jnp.tile(lo_head, (1, rep)), jnp.tile(hi_head, (1, rep))


def kernel(x, mem, norm_mix, w_in, w_proj_a, w_proj_b, w_proj_m, w_out, norm_mem, w_mem_kv, norm_mlp, w_up,
           w_down, norm_final):
    batch, seq, d = x.shape
    assert d == D_MODEL and seq == SEQ and seq % TM_QKV == 0
    depth = w_in.shape[0]
    bf = jnp.bfloat16
    x2 = x.reshape(batch * seq, d)
    mem2 = mem.reshape(batch * mem.shape[1], d)
    rope_c, rope_s1, rope_s2 = _rope_tables(seq)
    bias = jnp.asarray(_BIAS_NP)
    cbias = jnp.asarray(_CLASS_BIAS_NP)
    w_in_bf, w_pa, w_pb, w_pm = w_in.astype(bf), w_proj_a.astype(bf), w_proj_b.astype(bf), w_proj_m.astype(bf)
    w_o, w_mkv = w_out.astype(bf), w_mem_kv.astype(bf)
    g_mix, g_mem, g_mlp = (g.reshape(depth, 1, d) for g in (norm_mix, norm_mem, norm_mlp))
    for l in range(depth):
        qka, vta, qkb, vtb, qm, cls = _qkv_call(x2, g_mix, w_in_bf, l, rope_c, rope_s1, rope_s2, batch, seq)
        o, lse, ocls, lsecls = _attn_call(qka, vta, qkb, vtb, qm, cls, mem2, g_mem, w_mkv, l, bias, cbias, batch, seq)
        x2 = _mix_call(x2, o, lse, ocls, lsecls, g_mix, w_in_bf, w_pa, w_pb, w_pm, w_o, l)
        x2 = _mlp_call(x2, g_mlp, w_up, w_down, norm_final.reshape(1, d), l, final_norm=(l == depth - 1))
    return x2.reshape(batch, seq, d)
```

```python
import functools

import jax
import jax.numpy as jnp
import numpy as np
from jax import lax
from jax.experimental import pallas as pl
from jax.experimental.pallas import tpu as pltpu

D_MODEL = 1024
SEQ = 2048
HEAD_DIM = 64
ROT_DIM = HEAD_DIM // 4
ROPE_THETA = 500000.0
DIL_GROUPS = ((128, 1), (512, 4), (2048, 16))
HEADS_PER_DIL_GROUP = 2
N_HEADS_A = len(DIL_GROUPS) * HEADS_PER_DIL_GROUP
N_HEADS_B = 6
N_HEADS_M = 4
MOBA_BLOCK = 256
MOBA_TOPK = 3
N_BLOCKS = SEQ // MOBA_BLOCK
D_FF = 4 * D_MODEL
WIDTH_A = N_HEADS_A * HEAD_DIM
WIDTH_A_OUT = HEADS_PER_DIL_GROUP * HEAD_DIM
WIDTH_B = N_HEADS_B * HEAD_DIM
WIDTH_M = N_HEADS_M * HEAD_DIM
WIDTH_O = WIDTH_A_OUT + WIDTH_B + WIDTH_M
RMS_EPS = 1e-6
NEG_INF = -1e30
Q_SCALE = HEAD_DIM ** -0.5
LOG2_E = 1.4426950408889634

LANES = 128
PAIR = 2 * HEAD_DIM
SUM_ROWS = 16
TQ = MOBA_BLOCK
CLASS_GROUP = len(DIL_GROUPS) - 1
CLASS_STRIDE = DIL_GROUPS[CLASS_GROUP][1]
CLASS_LEN = SEQ // CLASS_STRIDE
CLASSES_PER_TILE = TQ // CLASS_LEN
TM_QKV = 512
TM_MIX = 512
TM_MLP = 512
FF_CHUNK = 1024
VMEM_LIMIT = 56 * 1024 * 1024

_B_ALLNEG = 0
_B_CAUSAL = 1


TILE_GROUPS = DIL_GROUPS[:CLASS_GROUP]
assert DIL_GROUPS[CLASS_GROUP][0] >= SEQ and TQ % CLASS_LEN == 0


def _dil_tile_offsets():
    return tuple(tuple(range(min((w + TQ - 1) // TQ, N_BLOCKS - 1) + 1)) for w, _ in TILE_GROUPS)


def _build_bias_tiles():
    c = np.arange(TQ)[:, None]
    r = np.arange(TQ)[None, :]
    tiles = [np.zeros((TQ, TQ), bool), (r - c) >= 0]
    ids = []
    for (w, d), offs in zip(TILE_GROUPS, _dil_tile_offsets()):
        per_off = []
        for o in offs:
            diff = o * TQ + r - c
            per_off.append((diff >= 0) & (diff <= w) & (diff % d == 0))
        uniq, gid = [], []
        for t in per_off:
            for k, u in enumerate(uniq):
                if np.array_equal(t, u):
                    gid.append(k)
                    break
            else:
                uniq.append(t)
                gid.append(len(uniq) - 1)
        ids.append(tuple(len(tiles) + k for k in gid))
        tiles.extend(uniq)
    bias = np.where(np.stack(tiles), 0.0, NEG_INF).astype(np.float32)
    return bias, tuple(ids)


_BIAS_NP, _DIL_BIAS_IDS = _build_bias_tiles()
_pos = np.arange(CLASS_LEN)
_CLASS_BIAS_NP = np.where(_pos[None, :] >= _pos[:, None], 0.0, NEG_INF).astype(np.float32)

_NT = (((1,), (1,)), ((), ()))
_TN = (((0,), (1,)), ((), ()))
COL_B = 3 * WIDTH_A
COL_M = COL_B + 3 * WIDTH_B
COL_GATES = COL_M + WIDTH_M
GATE_COLS = 3 * D_MODEL


def _rms(x, g):
    return x * lax.rsqrt(jnp.mean(x * x, axis=-1, keepdims=True) + RMS_EPS) * g


def _qkv_kernel(x_ref, g_ref, win_ref, c_ref, s1_ref, s2_ref, qka_ref, vta_ref, qkb_ref, vtb_ref, qm_ref, cls_ref):
    h = _rms(x_ref[...], g_ref[...]).astype(jnp.bfloat16)
    cos = c_ref[...]
    sin_lo = s1_ref[...]
    sin_hi = s2_ref[...]
    for col0, qk_ref, vt_ref, width in ((0, qka_ref, vta_ref, WIDTH_A), (COL_B, qkb_ref, vtb_ref, WIDTH_B)):
        z = jnp.dot(h, win_ref[:, col0:col0 + 2 * width].astype(jnp.bfloat16),
                    preferred_element_type=jnp.float32)
        for blk in range(2 * width // LANES):
            zb = z[:, blk * LANES:(blk + 1) * LANES]
            rb = zb * cos + pltpu.roll(zb, LANES - ROT_DIM // 2, 1) * sin_lo + pltpu.roll(zb, ROT_DIM // 2, 1) * sin_hi
            if blk < width // LANES:
                rb = rb * (Q_SCALE * LOG2_E)
            qk_ref[:, blk * LANES:(blk + 1) * LANES] = rb.astype(jnp.bfloat16)
            if col0 == 0 and blk % (width // LANES) == CLASS_GROUP:
                part = blk // (width // LANES)
                cls_ref[part] = rb
        vt = lax.dot_general(win_ref[:, col0 + 2 * width:col0 + 3 * width].astype(jnp.bfloat16), h, _TN,
                             preferred_element_type=jnp.float32)
        vt_ref[...] = vt.astype(jnp.bfloat16)
        if col0 == 0:
            cls_ref[2] = vt[CLASS_GROUP * PAIR:(CLASS_GROUP + 1) * PAIR, :].T
    qm = jnp.dot(h, win_ref[:, COL_M:COL_GATES].astype(jnp.bfloat16), preferred_element_type=jnp.float32) * (Q_SCALE * LOG2_E)
    qm_ref[...] = qm.astype(jnp.bfloat16)


def _qkv_call(x2, g, w_in, layer, rope_c, rope_s1, rope_s2, batch, seq):
    t = x2.shape[0]
    tm = TM_QKV
    nts = seq // tm
    per_layer = lambda i: (layer, 0, 0)
    return pl.pallas_call(
        _qkv_kernel,
        grid=(t // tm,),
        in_specs=[
            pl.BlockSpec((tm, D_MODEL), lambda i: (i, 0)),
            pl.BlockSpec((None, 1, D_MODEL), per_layer),
            pl.BlockSpec((None, D_MODEL, COL_GATES), per_layer),
            pl.BlockSpec((tm, LANES), lambda i: (i % nts, 0)),
            pl.BlockSpec((tm, LANES), lambda i: (i % nts, 0)),
            pl.BlockSpec((tm, LANES), lambda i: (i % nts, 0)),
        ],
        out_specs=[
            pl.BlockSpec((tm, 2 * WIDTH_A), lambda i: (i, 0)),
            pl.BlockSpec((WIDTH_A, tm), lambda i: (i // nts, i % nts)),
            pl.BlockSpec((tm, 2 * WIDTH_B), lambda i: (i, 0)),
            pl.BlockSpec((WIDTH_B, tm), lambda i: (i // nts, i % nts)),
            pl.BlockSpec((tm, WIDTH_M), lambda i: (i, 0)),
            pl.BlockSpec((3, tm, PAIR), lambda i: (0, i, 0)),
        ],
        out_shape=[
            jax.ShapeDtypeStruct((t, 2 * WIDTH_A), jnp.bfloat16),
            jax.ShapeDtypeStruct((batch * WIDTH_A, seq), jnp.bfloat16),
            jax.ShapeDtypeStruct((t, 2 * WIDTH_B), jnp.bfloat16),
            jax.ShapeDtypeStruct((batch * WIDTH_B, seq), jnp.bfloat16),
            jax.ShapeDtypeStruct((t, WIDTH_M), jnp.bfloat16),
            jax.ShapeDtypeStruct((3, t, PAIR), jnp.float32),
        ],
        compiler_params=pltpu.CompilerParams(dimension_semantics=("parallel",), vmem_limit_bytes=VMEM_LIMIT),
        name="qkv_proj",
    )(x2, g, w_in, rope_c, rope_s1, rope_s2)


def _pv(v_heads, p_bf):
    ones = jnp.ones((SUM_ROWS, p_bf.shape[0]), jnp.bfloat16)
    nq = p_bf.shape[1] // len(v_heads)
    return jnp.concatenate([jnp.dot(jnp.concatenate([v, ones], axis=0), p_bf[:, h * nq:(h + 1) * nq],
                                    preferred_element_type=jnp.float32)
                            for h, v in enumerate(v_heads)], axis=1)


def _after(x, *matmul_results):
    for r in matmul_results:
        bits = lax.bitcast_convert_type(r[-1:, :x.shape[1]], jnp.uint32)
        bits = lax.shift_right_logical(lax.shift_right_logical(bits, jnp.uint32(16)), jnp.uint32(16))
        x = x + lax.bitcast_convert_type(bits, jnp.float32)
    return x


def _scores_t(k_tile, q_heads):
    return lax.dot_general(k_tile, q_heads, _NT, preferred_element_type=jnp.float32)


def _stack_heads(q, n_heads):
    lane = lax.broadcasted_iota(jnp.int32, q.shape, 1)
    zero = jnp.zeros((), q.dtype)
    return jnp.concatenate([jnp.where((lane >= h * HEAD_DIM) & (lane < (h + 1) * HEAD_DIM), q, zero)
                            for h in range(n_heads)], axis=0)


def _head_rows(first_head, n_heads):
    return [slice((first_head + h) * HEAD_DIM, (first_head + h + 1) * HEAD_DIM) for h in range(n_heads)]


def _lanes_to_rows(row, n_heads):
    nq = row.shape[1] // n_heads
    return jnp.concatenate([jnp.broadcast_to(row[:, h * nq:(h + 1) * nq], (HEAD_DIM, nq)) for h in range(n_heads)],
                           axis=0)


def _heads_to_rows(o_t, n_heads):
    nq = o_t.shape[1] // n_heads
    return jnp.concatenate([o_t[:, h * nq:(h + 1) * nq] for h in range(n_heads)], axis=0)


def _attn_kernel(qa_ref, ka_ref, vta_ref, qb_ref, kb_ref, vtb_ref, qm_ref, cls_ref, mem_ref, gmem_ref, wmkv_ref,
                 bias_ref, cbias_ref, o_ref, lse_ref, ocls_ref, lsecls_ref,
                 kmean_ref, km_ref, vmt_ref, selb_ref, m_ref, acc_ref, raw_ref):
    qi = pl.program_id(1)

    @pl.when(qi == 0)
    def _per_batch():
        kb = kb_ref[...].astype(jnp.float32)
        kmean_ref[...] = jnp.mean(kb.reshape(N_BLOCKS, MOBA_BLOCK, WIDTH_B), axis=1)
        mem_n = _rms(mem_ref[...], gmem_ref[...]).astype(jnp.bfloat16)
        km_ref[...] = jnp.dot(mem_n, wmkv_ref[:, :WIDTH_M], preferred_element_type=jnp.float32).astype(jnp.bfloat16)
        vmt_ref[...] = lax.dot_general(wmkv_ref[:, WIDTH_M:], mem_n, _TN,
                                       preferred_element_type=jnp.float32).astype(jnp.bfloat16)

    def keys_of(j):
        return pl.ds(pl.multiple_of(j * TQ, TQ), TQ)

    def both_heads(bias):
        return jnp.concatenate([bias, bias], axis=1)

    def single_tile(s_t, v_heads):
        m = jnp.max(s_t, axis=0, keepdims=True)
        return m, _pv(v_heads, jnp.exp2(s_t - m).astype(jnp.bfloat16))

    for c in range(CLASSES_PER_TILE):
        rows = pl.ds(qi * CLASSES_PER_TILE + c, CLASS_LEN, stride=CLASS_STRIDE)
        q_heads = _stack_heads(cls_ref[0, rows, :].astype(jnp.bfloat16), HEADS_PER_DIL_GROUP)
        k_cls = cls_ref[1, rows, :].astype(jnp.bfloat16)
        v_t = cls_ref[2, rows, :].T.astype(jnp.bfloat16)
        m, acc = single_tile(_scores_t(k_cls, q_heads) + both_heads(cbias_ref[...]),
                             [v_t[r] for r in _head_rows(0, HEADS_PER_DIL_GROUP)])
        l = acc[HEAD_DIM:HEAD_DIM + 1]
        out = slice(c * CLASS_LEN, (c + 1) * CLASS_LEN)
        ocls_ref[out, :] = _heads_to_rows(acc[:HEAD_DIM] / l, HEADS_PER_DIL_GROUP).T
        lsecls_ref[out, :] = _lanes_to_rows(m + jnp.log2(l), HEADS_PER_DIL_GROUP).T

    a_static = []
    for g, offs in enumerate(_dil_tile_offsets()):
        cols = slice(g * PAIR, (g + 1) * PAIR)
        rows = _head_rows(g * HEADS_PER_DIL_GROUP, HEADS_PER_DIL_GROUP)
        q_heads = _stack_heads(qa_ref[:, cols], HEADS_PER_DIL_GROUP)
        ids = _DIL_BIAS_IDS[g]
        for o in offs:
            ks = keys_of(jnp.maximum(qi - o, 0))
            bid = ids[o] if o == 0 else jnp.where(qi >= o, ids[o], _B_ALLNEG)
            a_static.append((ks, bid, cols, rows, q_heads))

    blk = lax.broadcasted_iota(jnp.int32, (N_BLOCKS, 2 * TQ), 0)
    own = keys_of(qi)
    b_heads = []
    for p in range(N_HEADS_B // 2):
        cols = slice(p * PAIR, (p + 1) * PAIR)
        q_heads = _stack_heads(qb_ref[:, cols], 2)
        kmean = kmean_ref[:, cols]
        kmean_hi = kmean.astype(jnp.bfloat16)
        kmean_lo = (kmean - kmean_hi.astype(jnp.float32)).astype(jnp.bfloat16)
        gate = _scores_t(kmean_hi, q_heads) + _scores_t(kmean_lo, q_heads)
        gate = jnp.where(blk < qi, gate, NEG_INF)
        beaten = jnp.zeros(gate.shape, jnp.float32)
        for j in range(N_BLOCKS):
            gj = gate[j:j + 1, :]
            wins_tie = jnp.where(gj >= gate, 1.0, 0.0)
            wins_strict = jnp.where(gj > gate, 1.0, 0.0)
            beaten = beaten + jnp.where(blk > j, wins_tie, wins_strict)
        selb_ref[p] = jnp.where((beaten < MOBA_TOPK) & (blk < qi), 0.0, NEG_INF)
        b_heads.append((cols, _head_rows(2 * p, 2), q_heads))
    n_b = len(b_heads)

    def a_scores(part):
        ks, _, cols, _, q_heads = part
        return _scores_t(ka_ref[ks, cols], q_heads)

    def b_scores(p, ks):
        cols, _, q_heads = b_heads[p]
        return _scores_t(kb_ref[ks, cols], q_heads)

    n_a = len(a_static)
    raw = a_scores(a_static[0])
    a_biased = []
    m_a = None
    for i in range(n_a):
        nxt = a_scores(a_static[i + 1]) if i + 1 < n_a else b_scores(0, own)
        s_t = raw + both_heads(bias_ref[a_static[i][1]])
        part_max = _after(jnp.max(s_t, axis=0, keepdims=True), nxt)
        m_a = part_max if m_a is None else jnp.maximum(m_a, part_max)
        a_biased.append(s_t)
        raw = nxt
    m_heads = _stack_heads(qm_ref[...], N_HEADS_M)
    ahead = [lambda: b_scores(1, own), lambda: b_scores(2, own), lambda: _scores_t(km_ref[...], m_heads),
             lambda: b_scores(0, keys_of(0)), lambda: b_scores(1, keys_of(0))]
    later = [raw]
    assert n_a >= len(ahead)
    neg_m = -m_a
    acc_a = None
    for i in range(n_a):
        ks, _, _, rows, _ = a_static[i]
        p_t = jnp.exp2(a_biased[i] + neg_m).astype(jnp.bfloat16)
        if i < len(ahead):
            later.append(ahead[i]())
            neg_m = _after(neg_m, later[-1])
        part_acc = _pv([vta_ref[r, ks] for r in rows], p_t)
        acc_a = part_acc if acc_a is None else acc_a + part_acc
    l_a = acc_a[HEAD_DIM:HEAD_DIM + 1]
    o_tiles = [acc_a[:HEAD_DIM] / l_a]
    lse_ref[...] = _lanes_to_rows(m_a + jnp.log2(l_a), HEADS_PER_DIL_GROUP).T

    for p, (_, rows, _) in enumerate(b_heads):
        m_ref[p], acc_ref[p] = single_tile(later[p] + both_heads(bias_ref[_B_CAUSAL]),
                                           [vtb_ref[r, own] for r in rows])
    _, acc_m = single_tile(later[3], [vmt_ref[r, :] for r in _head_rows(0, N_HEADS_M)])
    raw_ref[0] = later[4]
    raw_ref[1] = later[5]

    def past_tile(j, _):
        ks = keys_of(j)
        ks_next = keys_of(jnp.minimum(j + 1, qi - 1))
        scores = [raw_ref[0], raw_ref[1]]
        for c in range(n_b):
            issued = b_scores(c + 2, ks) if c + 2 < n_b else b_scores(c + 2 - n_b, ks_next)
            scores.append(issued)
            sel = selb_ref[c, pl.ds(j, 1), :]
            m_old = m_ref[c]
            m_new = jnp.maximum(m_old, jnp.max(scores[c], axis=0, keepdims=True) + sel)
            alpha = _after(jnp.exp2(m_old - m_new), issued)
            p_t = jnp.exp2(scores[c] + (sel - m_new)).astype(jnp.bfloat16)
            acc_ref[c] = alpha * acc_ref[c] + _pv([vtb_ref[r, ks] for r in b_heads[c][1]], p_t)
            m_ref[c] = m_new
        raw_ref[0] = scores[n_b]
        raw_ref[1] = scores[n_b + 1]
        return 0

    lax.fori_loop(0, qi, past_tile, 0)

    for acc in (*[acc_ref[c] for c in range(n_b)], acc_m):
        o_tiles.append(acc[:HEAD_DIM] / acc[HEAD_DIM:HEAD_DIM + 1])
    col = 0
    for o_t in o_tiles:
        for c in range(o_t.shape[1] // (2 * TQ)):
            o_ref[:, col:col + LANES] = _heads_to_rows(o_t[:, 2 * c * TQ:(2 * c + 2) * TQ], 2).T.astype(jnp.bfloat16)
            col += LANES


def _attn_call(qka, vta, qkb, vtb, qm, cls, mem2, gmem, wmkv, layer, bias, cbias, batch, seq):
    nq = seq // TQ
    n_mem = mem2.shape[0] // batch
    per_layer = lambda b, q: (layer, 0, 0)
    per_tile = lambda b, q: (b * nq + q, 0)
    return pl.pallas_call(
        _attn_kernel,
        grid=(batch, nq),
        in_specs=[
            pl.BlockSpec((TQ, WIDTH_A), lambda b, q: (b * nq + q, 0)),
            pl.BlockSpec((seq, WIDTH_A), lambda b, q: (b, 1)),
            pl.BlockSpec((WIDTH_A, seq), lambda b, q: (b, 0)),
            pl.BlockSpec((TQ, WIDTH_B), lambda b, q: (b * nq + q, 0)),
            pl.BlockSpec((seq, WIDTH_B), lambda b, q: (b, 1)),
            pl.BlockSpec((WIDTH_B, seq), lambda b, q: (b, 0)),
            pl.BlockSpec((TQ, WIDTH_M), lambda b, q: (b * nq + q, 0)),
            pl.BlockSpec((3, seq, PAIR), lambda b, q: (0, b, 0)),
            pl.BlockSpec((n_mem, D_MODEL), lambda b, q: (b, 0)),
            pl.BlockSpec((None, 1, D_MODEL), per_layer),
            pl.BlockSpec((None, D_MODEL, 2 * WIDTH_M), per_layer),
            pl.BlockSpec(bias.shape, lambda b, q: (0, 0, 0)),
            pl.BlockSpec(cbias.shape, lambda b, q: (0, 0)),
        ],
        out_specs=[pl.BlockSpec((TQ, WIDTH_O), per_tile)] + [pl.BlockSpec((TQ, PAIR), per_tile)] * 3,
        out_shape=[jax.ShapeDtypeStruct((batch * seq, WIDTH_O), jnp.bfloat16)]
        + [jax.ShapeDtypeStruct((batch * seq, PAIR), jnp.float32)] * 3,
        scratch_shapes=[
            pltpu.VMEM((N_BLOCKS, WIDTH_B), jnp.float32),
            pltpu.VMEM((n_mem, WIDTH_M), jnp.bfloat16),
            pltpu.VMEM((WIDTH_M, n_mem), jnp.bfloat16),
            pltpu.VMEM((N_HEADS_B // 2, N_BLOCKS, 2 * TQ), jnp.float32),
            pltpu.VMEM((N_HEADS_B // 2, 1, 2 * TQ), jnp.float32),
            pltpu.VMEM((N_HEADS_B // 2, HEAD_DIM + SUM_ROWS, 2 * TQ), jnp.float32),
            pltpu.VMEM((2, TQ, 2 * TQ), jnp.float32),
        ],
        compiler_params=pltpu.CompilerParams(dimension_semantics=("parallel", "arbitrary"),
                                             vmem_limit_bytes=VMEM_LIMIT),
        name="attn",
    )(qka, qka, vta, qkb, qkb, vtb, qm, cls, mem2, gmem, wmkv, bias, cbias)


def _mix_kernel(x_ref, o_ref, lse_ref, ocls_ref, lsecls_ref, g_ref, wg_ref, wpa_ref, wpb_ref, wpm_ref, wo_ref,
                out_ref, ocn_ref, lcn_ref):
    tm = x_ref.shape[0]
    per_class = tm // CLASS_STRIDE
    first = (pl.program_id(0) % (SEQ // tm)) * per_class
    for r in range(CLASS_STRIDE):
        src = pl.ds(pl.multiple_of(r * CLASS_LEN + first, per_class), per_class)
        dst = pl.ds(r, per_class, stride=CLASS_STRIDE)
        ocn_ref[dst, :] = ocls_ref[src, :]
        lcn_ref[dst, :] = lsecls_ref[src, :]
    w_cls = 1.0 / (1.0 + jnp.exp2(lse_ref[...] - lcn_ref[...]))
    o_tiled = o_ref[:, :WIDTH_A_OUT].astype(jnp.float32)
    o_a = (o_tiled + w_cls * (ocn_ref[...] - o_tiled)).astype(jnp.bfloat16)
    x = x_ref[...]
    h = _rms(x, g_ref[...]).astype(jnp.bfloat16)
    y = None
    col = 0
    for i, wp_ref in enumerate((wpa_ref, wpb_ref, wpm_ref)):
        width = wp_ref.shape[0]
        gate = jax.nn.sigmoid(jnp.dot(h, wg_ref[0, :, i * D_MODEL:(i + 1) * D_MODEL].astype(jnp.bfloat16),
                                      preferred_element_type=jnp.float32))
        o_i = o_a if i == 0 else o_ref[:, col:col + width]
        branch = gate * jnp.dot(o_i, wp_ref[...], preferred_element_type=jnp.float32)
        y = branch if y is None else y + branch
        col += width
    out_ref[...] = x + jnp.dot(y.astype(jnp.bfloat16), wo_ref[...], preferred_element_type=jnp.float32)


def _mix_call(x2, o, lse, ocls, lsecls, g, w_in, wpa, wpb, wpm, wo, layer):
    t = x2.shape[0]
    tm = TM_MIX
    per_layer = lambda i: (layer, 0, 0)
    return pl.pallas_call(
        _mix_kernel,
        grid=(t // tm,),
        in_specs=[
            pl.BlockSpec((tm, D_MODEL), lambda i: (i, 0)),
            pl.BlockSpec((tm, WIDTH_O), lambda i: (i, 0)),
            pl.BlockSpec((tm, PAIR), lambda i: (i, 0)),
            pl.BlockSpec((SEQ, PAIR), lambda i: (i // (SEQ // tm), 0)),
            pl.BlockSpec((SEQ, PAIR), lambda i: (i // (SEQ // tm), 0)),
            pl.BlockSpec((None, 1, D_MODEL), per_layer),
            pl.BlockSpec((pl.Element(1), pl.Element(D_MODEL), pl.Element(GATE_COLS)), lambda i: (layer, 0, COL_GATES),
                         pipeline_mode=pl.Buffered(1)),
            pl.BlockSpec((None,) + wpa.shape[1:], per_layer),
            pl.BlockSpec((None,) + wpb.shape[1:], per_layer),
            pl.BlockSpec((None,) + wpm.shape[1:], per_layer),
            pl.BlockSpec((None,) + wo.shape[1:], per_layer),
        ],
        out_specs=pl.BlockSpec((tm, D_MODEL), lambda i: (i, 0)),
        out_shape=jax.ShapeDtypeStruct((t, D_MODEL), jnp.float32),
        scratch_shapes=[pltpu.VMEM((tm, PAIR), jnp.float32), pltpu.VMEM((tm, PAIR), jnp.float32)],
        compiler_params=pltpu.CompilerParams(dimension_semantics=("parallel",), vmem_limit_bytes=VMEM_LIMIT),
        name="gated_mix",
    )(x2, o, lse, ocls, lsecls, g, w_in, wpa, wpb, wpm, wo)


def _mlp_kernel(x_ref, g_ref, wup_ref, wdown_ref, gfin_ref, out_ref, *, final_norm):
    x = x_ref[...]
    hm = _rms(x, g_ref[...]).astype(jnp.bfloat16)
    acc = x
    for c in range(D_FF // FF_CHUNK):
        u = jnp.dot(hm, wup_ref[:, c * FF_CHUNK:(c + 1) * FF_CHUNK].astype(jnp.bfloat16),
                    preferred_element_type=jnp.float32)
        u = jnp.square(jnp.maximum(u, 0.0)).astype(jnp.bfloat16)
        acc = acc + jnp.dot(u, wdown_ref[c * FF_CHUNK:(c + 1) * FF_CHUNK, :].astype(jnp.bfloat16),
                            preferred_element_type=jnp.float32)
    out_ref[...] = _rms(acc, gfin_ref[...]) if final_norm else acc


def _mlp_call(x2, g, wup, wdown, gfin, layer, final_norm):
    t = x2.shape[0]
    tm = TM_MLP
    const = lambda i: (0, 0)
    per_layer = lambda i: (layer, 0, 0)
    return pl.pallas_call(
        functools.partial(_mlp_kernel, final_norm=final_norm),
        grid=(t // tm,),
        in_specs=[
            pl.BlockSpec((tm, D_MODEL), lambda i: (i, 0)),
            pl.BlockSpec((None, 1, D_MODEL), per_layer),
            pl.BlockSpec((None,) + wup.shape[1:], per_layer, pipeline_mode=pl.Buffered(1)),
            pl.BlockSpec((None,) + wdown.shape[1:], per_layer, pipeline_mode=pl.Buffered(1)),
            pl.BlockSpec((1, D_MODEL), const),
        ],
        out_specs=pl.BlockSpec((tm, D_MODEL), lambda i: (i, 0)),
        out_shape=jax.ShapeDtypeStruct((t, D_MODEL), jnp.float32),
        compiler_params=pltpu.CompilerParams(dimension_semantics=("parallel",), vmem_limit_bytes=VMEM_LIMIT),
        name="mlp",
    )(x2, g, wup, wdown, gfin)


def _rope_tables(seq):
    half = ROT_DIM // 2
    inv_freq = 1.0 / (ROPE_THETA ** (jnp.arange(0, ROT_DIM, 2, dtype=jnp.float32) / ROT_DIM))
    ang = jnp.arange(seq, dtype=jnp.int32).astype(jnp.float32)[:, None] * inv_freq[None, :]
    cos, sin = jnp.cos(ang), jnp.sin(ang)
    ones = jnp.ones((seq, HEAD_DIM - ROT_DIM), jnp.float32)
    zeros_half = jnp.zeros((seq, half), jnp.float32)
    zeros_rest = jnp.zeros((seq, HEAD_DIM - ROT_DIM), jnp.float32)
    c_head = jnp.concatenate([cos, cos, ones], axis=1)
    lo_head = jnp.concatenate([-sin, zeros_half, zeros_rest], axis=1)
    hi_head = jnp.concatenate([zeros_half, sin, zeros_rest], axis=1)
    rep = LANES // HEAD_DIM
    return jnp.tile(c_head, (1, rep)), jnp.tile(lo_head, (1, rep)), jnp.tile(hi_head, (1, rep))


def kernel(x, mem, norm_mix, w_in, w_proj_a, w_proj_b, w_proj_m, w_out, norm_mem, w_mem_kv, norm_mlp, w_up,
           w_down, norm_final):
    batch, seq, d = x.shape
    assert d == D_MODEL and seq == SEQ and seq % TM_QKV == 0
    depth = w_in.shape[0]
    bf = jnp.bfloat16
    x2 = x.reshape(batch * seq, d)
    mem2 = mem.reshape(batch * mem.shape[1], d)
    rope_c, rope_s1, rope_s2 = _rope_tables(seq)
    bias = jnp.asarray(_BIAS_NP)
    cbias = jnp.asarray(_CLASS_BIAS_NP)
    w_pa, w_pb, w_pm = w_proj_a.astype(bf), w_proj_b.astype(bf), w_proj_m.astype(bf)
    w_o, w_mkv = w_out.astype(bf), w_mem_kv.astype(bf)
    g_mix, g_mem, g_mlp = (g.reshape(depth, 1, d) for g in (norm_mix, norm_mem, norm_mlp))
    for l in range(depth):
        qka, vta, qkb, vtb, qm, cls = _qkv_call(x2, g_mix, w_in, l, rope_c, rope_s1, rope_s2, batch, seq)
        o, lse, ocls, lsecls = _attn_call(qka, vta, qkb, vtb, qm, cls, mem2, g_mem, w_mkv, l, bias, cbias, batch, seq)
        x2 = _mix_call(x2, o, lse, ocls, lsecls, g_mix, w_in, w_pa, w_pb, w_pm, w_o, l)
        x2 = _mlp_call(x2, g_mlp, w_up, w_down, norm_final.reshape(1, d), l, final_norm=(l == depth - 1))
    return x2.reshape(batch, seq, d)
```

```python
import functools

import jax
import jax.numpy as jnp
import numpy as np
from jax import lax
from jax.experimental import pallas as pl
from jax.experimental.pallas import tpu as pltpu

D_MODEL = 1024
SEQ = 2048
HEAD_DIM = 64
ROT_DIM = HEAD_DIM // 4
ROPE_THETA = 500000.0
DIL_GROUPS = ((128, 1), (512, 4), (2048, 16))
HEADS_PER_DIL_GROUP = 2
N_HEADS_A = len(DIL_GROUPS) * HEADS_PER_DIL_GROUP
N_HEADS_B = 6
N_HEADS_M = 4
MOBA_BLOCK = 256
MOBA_TOPK = 3
N_BLOCKS = SEQ // MOBA_BLOCK
D_FF = 4 * D_MODEL
WIDTH_A = N_HEADS_A * HEAD_DIM
WIDTH_A_OUT = HEADS_PER_DIL_GROUP * HEAD_DIM
WIDTH_B = N_HEADS_B * HEAD_DIM
WIDTH_M = N_HEADS_M * HEAD_DIM
WIDTH_O = WIDTH_A_OUT + WIDTH_B + WIDTH_M
RMS_EPS = 1e-6
NEG_INF = -1e30
Q_SCALE = HEAD_DIM ** -0.5
LOG2_E = 1.4426950408889634

LANES = 128
PAIR = 2 * HEAD_DIM
SUM_ROWS = 16
TQ = MOBA_BLOCK
CLASS_GROUP = len(DIL_GROUPS) - 1
CLASS_STRIDE = DIL_GROUPS[CLASS_GROUP][1]
CLASS_LEN = SEQ // CLASS_STRIDE
CLASSES_PER_TILE = TQ // CLASS_LEN
TM_QKV = 1024
TM_MIX = 1024
TM_MLP = 1024
FF_CHUNK = 1024
VMEM_LIMIT = 60 * 1024 * 1024

_B_ALLNEG = 0
_B_CAUSAL = 1


TILE_GROUPS = DIL_GROUPS[:CLASS_GROUP]
assert DIL_GROUPS[CLASS_GROUP][0] >= SEQ and TQ % CLASS_LEN == 0


def _dil_tile_offsets():
    return tuple(tuple(range(min((w + TQ - 1) // TQ, N_BLOCKS - 1) + 1)) for w, _ in TILE_GROUPS)


def _build_bias_tiles():
    c = np.arange(TQ)[:, None]
    r = np.arange(TQ)[None, :]
    tiles = [np.zeros((TQ, TQ), bool), (r - c) >= 0]
    ids = []
    for (w, d), offs in zip(TILE_GROUPS, _dil_tile_offsets()):
        per_off = []
        for o in offs:
            diff = o * TQ + r - c
            per_off.append((diff >= 0) & (diff <= w) & (diff % d == 0))
        uniq, gid = [], []
        for t in per_off:
            for k, u in enumerate(uniq):
                if np.array_equal(t, u):
                    gid.append(k)
                    break
            else:
                uniq.append(t)
                gid.append(len(uniq) - 1)
        ids.append(tuple(len(tiles) + k for k in gid))
        tiles.extend(uniq)
    bias = np.where(np.stack(tiles), 0.0, NEG_INF).astype(np.float32)
    return bias, tuple(ids)


_BIAS_NP, _DIL_BIAS_IDS = _build_bias_tiles()
_pos = np.arange(CLASS_LEN)
_CLASS_BIAS_NP = np.where(_pos[None, :] >= _pos[:, None], 0.0, NEG_INF).astype(np.float32)

_NT = (((1,), (1,)), ((), ()))
_TN = (((0,), (1,)), ((), ()))
COL_B = 3 * WIDTH_A
COL_M = COL_B + 3 * WIDTH_B
COL_GATES = COL_M + WIDTH_M
GATE_COLS = 3 * D_MODEL


def _rms(x, g):
    return x * lax.rsqrt(jnp.mean(x * x, axis=-1, keepdims=True) + RMS_EPS) * g


def _qkv_kernel(x_ref, g_ref, win_ref, c_ref, s1_ref, s2_ref, qka_ref, vta_ref, qkb_ref, vtb_ref, qm_ref, cls_ref):
    h = _rms(x_ref[...], g_ref[...]).astype(jnp.bfloat16)
    cos = c_ref[...]
    sin_lo = s1_ref[...]
    sin_hi = s2_ref[...]
    for col0, qk_ref, vt_ref, width in ((0, qka_ref, vta_ref, WIDTH_A), (COL_B, qkb_ref, vtb_ref, WIDTH_B)):
        z = jnp.dot(h, win_ref[:, col0:col0 + 2 * width].astype(jnp.bfloat16),
                    preferred_element_type=jnp.float32)
        for blk in range(2 * width // LANES):
            zb = z[:, blk * LANES:(blk + 1) * LANES]
            rb = zb * cos + pltpu.roll(zb, LANES - ROT_DIM // 2, 1) * sin_lo + pltpu.roll(zb, ROT_DIM // 2, 1) * sin_hi
            if blk < width // LANES:
                rb = rb * (Q_SCALE * LOG2_E)
            qk_ref[:, blk * LANES:(blk + 1) * LANES] = rb.astype(jnp.bfloat16)
            if col0 == 0 and blk % (width // LANES) == CLASS_GROUP:
                part = blk // (width // LANES)
                cls_ref[part] = rb
        vt = lax.dot_general(win_ref[:, col0 + 2 * width:col0 + 3 * width].astype(jnp.bfloat16), h, _TN,
                             preferred_element_type=jnp.float32)
        vt_ref[...] = vt.astype(jnp.bfloat16)
        if col0 == 0:
            cls_ref[2] = vt[CLASS_GROUP * PAIR:(CLASS_GROUP + 1) * PAIR, :].T
    qm = jnp.dot(h, win_ref[:, COL_M:COL_GATES].astype(jnp.bfloat16), preferred_element_type=jnp.float32) * (Q_SCALE * LOG2_E)
    qm_ref[...] = qm.astype(jnp.bfloat16)


def _qkv_call(x2, g, w_in, layer, rope_c, rope_s1, rope_s2, batch, seq):
    t = x2.shape[0]
    tm = TM_QKV
    nts = seq // tm
    per_layer = lambda i: (layer, 0, 0)
    return pl.pallas_call(
        _qkv_kernel,
        grid=(t // tm,),
        in_specs=[
            pl.BlockSpec((tm, D_MODEL), lambda i: (i, 0)),
            pl.BlockSpec((None, 1, D_MODEL), per_layer),
            pl.BlockSpec((None, D_MODEL, COL_GATES), per_layer),
            pl.BlockSpec((tm, LANES), lambda i: (i % nts, 0)),
            pl.BlockSpec((tm, LANES), lambda i: (i % nts, 0)),
            pl.BlockSpec((tm, LANES), lambda i: (i % nts, 0)),
        ],
        out_specs=[
            pl.BlockSpec((tm, 2 * WIDTH_A), lambda i: (i, 0)),
            pl.BlockSpec((WIDTH_A, tm), lambda i: (i // nts, i % nts)),
            pl.BlockSpec((tm, 2 * WIDTH_B), lambda i: (i, 0)),
            pl.BlockSpec((WIDTH_B, tm), lambda i: (i // nts, i % nts)),
            pl.BlockSpec((tm, WIDTH_M), lambda i: (i, 0)),
            pl.BlockSpec((3, tm, PAIR), lambda i: (0, i, 0)),
        ],
        out_shape=[
            jax.ShapeDtypeStruct((t, 2 * WIDTH_A), jnp.bfloat16),
            jax.ShapeDtypeStruct((batch * WIDTH_A, seq), jnp.bfloat16),
            jax.ShapeDtypeStruct((t, 2 * WIDTH_B), jnp.bfloat16),
            jax.ShapeDtypeStruct((batch * WIDTH_B, seq), jnp.bfloat16),
            jax.ShapeDtypeStruct((t, WIDTH_M), jnp.bfloat16),
            jax.ShapeDtypeStruct((3, t, PAIR), jnp.float32),
        ],
        compiler_params=pltpu.CompilerParams(dimension_semantics=("parallel",), vmem_limit_bytes=VMEM_LIMIT),
        name="qkv_proj",
    )(x2, g, w_in, rope_c, rope_s1, rope_s2)


def _pv(v_heads, p_bf):
    ones = jnp.ones((SUM_ROWS, p_bf.shape[0]), jnp.bfloat16)
    nq = p_bf.shape[1] // len(v_heads)
    return jnp.concatenate([jnp.dot(jnp.concatenate([v, ones], axis=0), p_bf[:, h * nq:(h + 1) * nq],
                                    preferred_element_type=jnp.float32)
                            for h, v in enumerate(v_heads)], axis=1)


def _after(x, *matmul_results):
    for r in matmul_results:
        bits = lax.bitcast_convert_type(r[-1:, :x.shape[1]], jnp.uint32)
        bits = lax.shift_right_logical(lax.shift_right_logical(bits, jnp.uint32(16)), jnp.uint32(16))
        x = x + lax.bitcast_convert_type(bits, jnp.float32)
    return x


def _scores_t(k_tile, q_heads):
    return lax.dot_general(k_tile, q_heads, _NT, preferred_element_type=jnp.float32)


def _stack_heads(q, n_heads):
    lane = lax.broadcasted_iota(jnp.int32, q.shape, 1)
    zero = jnp.zeros((), q.dtype)
    return jnp.concatenate([jnp.where((lane >= h * HEAD_DIM) & (lane < (h + 1) * HEAD_DIM), q, zero)
                            for h in range(n_heads)], axis=0)


def _head_rows(first_head, n_heads):
    return [slice((first_head + h) * HEAD_DIM, (first_head + h + 1) * HEAD_DIM) for h in range(n_heads)]


def _lanes_to_rows(row, n_heads):
    nq = row.shape[1] // n_heads
    return jnp.concatenate([jnp.broadcast_to(row[:, h * nq:(h + 1) * nq], (HEAD_DIM, nq)) for h in range(n_heads)],
                           axis=0)


def _heads_to_rows(o_t, n_heads):
    nq = o_t.shape[1] // n_heads
    return jnp.concatenate([o_t[:, h * nq:(h + 1) * nq] for h in range(n_heads)], axis=0)


def _attn_kernel(qa_ref, ka_ref, vta_ref, qb_ref, kb_ref, vtb_ref, qm_ref, cls_ref, mem_ref, gmem_ref, wmkv_ref,
                 bias_ref, cbias_ref, o_ref, lse_ref, ocls_ref, lsecls_ref,
                 kmean_ref, km_ref, vmt_ref, selb_ref, m_ref, acc_ref, raw_ref):
    qi = pl.program_id(1)

    @pl.when(qi == 0)
    def _per_batch():
        kb = kb_ref[...].astype(jnp.float32)
        kmean_ref[...] = jnp.mean(kb.reshape(N_BLOCKS, MOBA_BLOCK, WIDTH_B), axis=1)
        mem_n = _rms(mem_ref[...], gmem_ref[...]).astype(jnp.bfloat16)
        km_ref[...] = jnp.dot(mem_n, wmkv_ref[:, :WIDTH_M], preferred_element_type=jnp.float32).astype(jnp.bfloat16)
        vmt_ref[...] = lax.dot_general(wmkv_ref[:, WIDTH_M:], mem_n, _TN,
                                       preferred_element_type=jnp.float32).astype(jnp.bfloat16)

    def keys_of(j):
        return pl.ds(pl.multiple_of(j * TQ, TQ), TQ)

    def both_heads(bias):
        return jnp.concatenate([bias, bias], axis=1)

    def single_tile(s_t, v_heads):
        m = jnp.max(s_t, axis=0, keepdims=True)
        return m, _pv(v_heads, jnp.exp2(s_t - m).astype(jnp.bfloat16))

    for c in range(CLASSES_PER_TILE):
        rows = pl.ds(qi * CLASSES_PER_TILE + c, CLASS_LEN, stride=CLASS_STRIDE)
        q_heads = _stack_heads(cls_ref[0, rows, :].astype(jnp.bfloat16), HEADS_PER_DIL_GROUP)
        k_cls = cls_ref[1, rows, :].astype(jnp.bfloat16)
        v_t = cls_ref[2, rows, :].T.astype(jnp.bfloat16)
        m, acc = single_tile(_scores_t(k_cls, q_heads) + both_heads(cbias_ref[...]),
                             [v_t[r] for r in _head_rows(0, HEADS_PER_DIL_GROUP)])
        l = acc[HEAD_DIM:HEAD_DIM + 1]
        out = slice(c * CLASS_LEN, (c + 1) * CLASS_LEN)
        ocls_ref[out, :] = _heads_to_rows(acc[:HEAD_DIM] / l, HEADS_PER_DIL_GROUP).T
        lsecls_ref[out, :] = _lanes_to_rows(m + jnp.log2(l), HEADS_PER_DIL_GROUP).T

    a_static = []
    for g, offs in enumerate(_dil_tile_offsets()):
        cols = slice(g * PAIR, (g + 1) * PAIR)
        rows = _head_rows(g * HEADS_PER_DIL_GROUP, HEADS_PER_DIL_GROUP)
        q_heads = _stack_heads(qa_ref[:, cols], HEADS_PER_DIL_GROUP)
        ids = _DIL_BIAS_IDS[g]
        for o in offs:
            ks = keys_of(jnp.maximum(qi - o, 0))
            bid = ids[o] if o == 0 else jnp.where(qi >= o, ids[o], _B_ALLNEG)
            a_static.append((ks, bid, cols, rows, q_heads))

    blk = lax.broadcasted_iota(jnp.int32, (N_BLOCKS, 2 * TQ), 0)
    own = keys_of(qi)
    b_heads = []
    for p in range(N_HEADS_B // 2):
        cols = slice(p * PAIR, (p + 1) * PAIR)
        q_heads = _stack_heads(qb_ref[:, cols], 2)
        kmean = kmean_ref[:, cols]
        kmean_hi = kmean.astype(jnp.bfloat16)
        kmean_lo = (kmean - kmean_hi.astype(jnp.float32)).astype(jnp.bfloat16)
        gate = _scores_t(kmean_hi, q_heads) + _scores_t(kmean_lo, q_heads)
        gate = jnp.where(blk < qi, gate, NEG_INF)
        beaten = jnp.zeros(gate.shape, jnp.float32)
        for j in range(N_BLOCKS):
            gj = gate[j:j + 1, :]
            wins_tie = jnp.where(gj >= gate, 1.0, 0.0)
            wins_strict = jnp.where(gj > gate, 1.0, 0.0)
            beaten = beaten + jnp.where(blk > j, wins_tie, wins_strict)
        selb_ref[p] = jnp.where((beaten < MOBA_TOPK) & (blk < qi), 0.0, NEG_INF)
        b_heads.append((cols, _head_rows(2 * p, 2), q_heads))
    n_b = len(b_heads)

    def a_scores(part):
        ks, _, cols, _, q_heads = part
        return _scores_t(ka_ref[ks, cols], q_heads)

    def b_scores(p, ks):
        cols, _, q_heads = b_heads[p]
        return _scores_t(kb_ref[ks, cols], q_heads)

    n_a = len(a_static)
    raw = a_scores(a_static[0])
    a_biased = []
    m_a = None
    for i in range(n_a):
        nxt = a_scores(a_static[i + 1]) if i + 1 < n_a else b_scores(0, own)
        s_t = raw + both_heads(bias_ref[a_static[i][1]])
        part_max = _after(jnp.max(s_t, axis=0, keepdims=True), nxt)
        m_a = part_max if m_a is None else jnp.maximum(m_a, part_max)
        a_biased.append(s_t)
        raw = nxt
    m_heads = _stack_heads(qm_ref[...], N_HEADS_M)
    ahead = [lambda: b_scores(1, own), lambda: b_scores(2, own), lambda: _scores_t(km_ref[...], m_heads),
             lambda: b_scores(0, keys_of(0)), lambda: b_scores(1, keys_of(0))]
    later = [raw]
    assert n_a >= len(ahead)
    neg_m = -m_a
    acc_a = None
    for i in range(n_a):
        ks, _, _, rows, _ = a_static[i]
        p_t = jnp.exp2(a_biased[i] + neg_m).astype(jnp.bfloat16)
        if i < len(ahead):
            later.append(ahead[i]())
            neg_m = _after(neg_m, later[-1])
        part_acc = _pv([vta_ref[r, ks] for r in rows], p_t)
        acc_a = part_acc if acc_a is None else acc_a + part_acc
    l_a = acc_a[HEAD_DIM:HEAD_DIM + 1]
    o_tiles = [acc_a[:HEAD_DIM] / l_a]
    lse_ref[...] = _lanes_to_rows(m_a + jnp.log2(l_a), HEADS_PER_DIL_GROUP).T

    for p, (_, rows, _) in enumerate(b_heads):
        m_ref[p], acc_ref[p] = single_tile(later[p] + both_heads(bias_ref[_B_CAUSAL]),
                                           [vtb_ref[r, own] for r in rows])
    _, acc_m = single_tile(later[3], [vmt_ref[r, :] for r in _head_rows(0, N_HEADS_M)])
    raw_ref[0] = later[4]
    raw_ref[1] = later[5]

    def past_tile(j, _):
        ks = keys_of(j)
        ks_next = keys_of(jnp.minimum(j + 1, qi - 1))
        scores = [raw_ref[0], raw_ref[1]]
        for c in range(n_b):
            issued = b_scores(c + 2, ks) if c + 2 < n_b else b_scores(c + 2 - n_b, ks_next)
            scores.append(issued)
            sel = selb_ref[c, pl.ds(j, 1), :]
            m_old = m_ref[c]
            m_new = jnp.maximum(m_old, jnp.max(scores[c], axis=0, keepdims=True) + sel)
            alpha = _after(jnp.exp2(m_old - m_new), issued)
            p_t = jnp.exp2(scores[c] + (sel - m_new)).astype(jnp.bfloat16)
            acc_ref[c] = alpha * acc_ref[c] + _pv([vtb_ref[r, ks] for r in b_heads[c][1]], p_t)
            m_ref[c] = m_new
        raw_ref[0] = scores[n_b]
        raw_ref[1] = scores[n_b + 1]
        return 0

    lax.fori_loop(0, qi, past_tile, 0)

    for acc in (*[acc_ref[c] for c in range(n_b)], acc_m):
        o_tiles.append(acc[:HEAD_DIM] / acc[HEAD_DIM:HEAD_DIM + 1])
    col = 0
    for o_t in o_tiles:
        for c in range(o_t.shape[1] // (2 * TQ)):
            o_ref[:, col:col + LANES] = _heads_to_rows(o_t[:, 2 * c * TQ:(2 * c + 2) * TQ], 2).T.astype(jnp.bfloat16)
            col += LANES


def _attn_call(qka, vta, qkb, vtb, qm, cls, mem2, gmem, wmkv, layer, bias, cbias, batch, seq):
    nq = seq // TQ
    n_mem = mem2.shape[0] // batch
    per_layer = lambda b, q: (layer, 0, 0)
    per_tile = lambda b, q: (b * nq + q, 0)
    return pl.pallas_call(
        _attn_kernel,
        grid=(batch, nq),
        in_specs=[
            pl.BlockSpec((TQ, WIDTH_A), lambda b, q: (b * nq + q, 0)),
            pl.BlockSpec((seq, WIDTH_A), lambda b, q: (b, 1)),
            pl.BlockSpec((WIDTH_A, seq), lambda b, q: (b, 0)),
            pl.BlockSpec((TQ, WIDTH_B), lambda b, q: (b * nq + q, 0)),
            pl.BlockSpec((seq, WIDTH_B), lambda b, q: (b, 1)),
            pl.BlockSpec((WIDTH_B, seq), lambda b, q: (b, 0)),
            pl.BlockSpec((TQ, WIDTH_M), lambda b, q: (b * nq + q, 0)),
            pl.BlockSpec((3, seq, PAIR), lambda b, q: (0, b, 0)),
            pl.BlockSpec((n_mem, D_MODEL), lambda b, q: (b, 0)),
            pl.BlockSpec((None, 1, D_MODEL), per_layer),
            pl.BlockSpec((None, D_MODEL, 2 * WIDTH_M), per_layer),
            pl.BlockSpec(bias.shape, lambda b, q: (0, 0, 0)),
            pl.BlockSpec(cbias.shape, lambda b, q: (0, 0)),
        ],
        out_specs=[pl.BlockSpec((TQ, WIDTH_O), per_tile)] + [pl.BlockSpec((TQ, PAIR), per_tile)] * 3,
        out_shape=[jax.ShapeDtypeStruct((batch * seq, WIDTH_O), jnp.bfloat16)]
        + [jax.ShapeDtypeStruct((batch * seq, PAIR), jnp.float32)] * 3,
        scratch_shapes=[
            pltpu.VMEM((N_BLOCKS, WIDTH_B), jnp.float32),
            pltpu.VMEM((n_mem, WIDTH_M), jnp.bfloat16),
            pltpu.VMEM((WIDTH_M, n_mem), jnp.bfloat16),
            pltpu.VMEM((N_HEADS_B // 2, N_BLOCKS, 2 * TQ), jnp.float32),
            pltpu.VMEM((N_HEADS_B // 2, 1, 2 * TQ), jnp.float32),
            pltpu.VMEM((N_HEADS_B // 2, HEAD_DIM + SUM_ROWS, 2 * TQ), jnp.float32),
            pltpu.VMEM((2, TQ, 2 * TQ), jnp.float32),
        ],
        compiler_params=pltpu.CompilerParams(dimension_semantics=("parallel", "arbitrary"),
                                             vmem_limit_bytes=VMEM_LIMIT),
        name="attn",
    )(qka, qka, vta, qkb, qkb, vtb, qm, cls, mem2, gmem, wmkv, bias, cbias)


def _mix_kernel(x_ref, o_ref, lse_ref, ocls_ref, lsecls_ref, g_ref, wg_ref, wpa_ref, wpb_ref, wpm_ref, wo_ref,
                out_ref, ocn_ref, lcn_ref):
    tm = x_ref.shape[0]
    per_class = tm // CLASS_STRIDE
    first = (pl.program_id(0) % (SEQ // tm)) * per_class
    for r in range(CLASS_STRIDE):
        src = pl.ds(pl.multiple_of(r * CLASS_LEN + first, per_class), per_class)
        dst = pl.ds(r, per_class, stride=CLASS_STRIDE)
        ocn_ref[dst, :] = ocls_ref[src, :]
        lcn_ref[dst, :] = lsecls_ref[src, :]
    w_cls = 1.0 / (1.0 + jnp.exp2(lse_ref[...] - lcn_ref[...]))
    o_tiled = o_ref[:, :WIDTH_A_OUT].astype(jnp.float32)
    o_a = (o_tiled + w_cls * (ocn_ref[...] - o_tiled)).astype(jnp.bfloat16)
    x = x_ref[...]
    h = _rms(x, g_ref[...]).astype(jnp.bfloat16)
    y = None
    col = 0
    for i, wp_ref in enumerate((wpa_ref, wpb_ref, wpm_ref)):
        width = wp_ref.shape[0]
        gate = jax.nn.sigmoid(jnp.dot(h, wg_ref[0, :, i * D_MODEL:(i + 1) * D_MODEL].astype(jnp.bfloat16),
                                      preferred_element_type=jnp.float32))
        o_i = o_a if i == 0 else o_ref[:, col:col + width]
        branch = gate * jnp.dot(o_i, wp_ref[...], preferred_element_type=jnp.float32)
        y = branch if y is None else y + branch
        col += width
    out_ref[...] = x + jnp.dot(y.astype(jnp.bfloat16), wo_ref[...], preferred_element_type=jnp.float32)


def _mix_call(x2, o, lse, ocls, lsecls, g, w_in, wpa, wpb, wpm, wo, layer):
    t = x2.shape[0]
    tm = TM_MIX
    per_layer = lambda i: (layer, 0, 0)
    return pl.pallas_call(
        _mix_kernel,
        grid=(t // tm,),
        in_specs=[
            pl.BlockSpec((tm, D_MODEL), lambda i: (i, 0)),
            pl.BlockSpec((tm, WIDTH_O), lambda i: (i, 0)),
            pl.BlockSpec((tm, PAIR), lambda i: (i, 0)),
            pl.BlockSpec((SEQ, PAIR), lambda i: (i // (SEQ // tm), 0)),
            pl.BlockSpec((SEQ, PAIR), lambda i: (i // (SEQ // tm), 0)),
            pl.BlockSpec((None, 1, D_MODEL), per_layer),
            pl.BlockSpec((pl.Element(1), pl.Element(D_MODEL), pl.Element(GATE_COLS)), lambda i: (layer, 0, COL_GATES),
                         pipeline_mode=pl.Buffered(1)),
            pl.BlockSpec((None,) + wpa.shape[1:], per_layer),
            pl.BlockSpec((None,) + wpb.shape[1:], per_layer),
            pl.BlockSpec((None,) + wpm.shape[1:], per_layer),
            pl.BlockSpec((None,) + wo.shape[1:], per_layer),
        ],
        out_specs=pl.BlockSpec((tm, D_MODEL), lambda i: (i, 0)),
        out_shape=jax.ShapeDtypeStruct((t, D_MODEL), jnp.float32),
        scratch_shapes=[pltpu.VMEM((tm, PAIR), jnp.float32), pltpu.VMEM((tm, PAIR), jnp.float32)],
        compiler_params=pltpu.CompilerParams(dimension_semantics=("parallel",), vmem_limit_bytes=VMEM_LIMIT),
        name="gated_mix",
    )(x2, o, lse, ocls, lsecls, g, w_in, wpa, wpb, wpm, wo)


def _mlp_kernel(x_ref, g_ref, wup_ref, wdown_ref, gfin_ref, out_ref, *, final_norm):
    x = x_ref[...]
    hm = _rms(x, g_ref[...]).astype(jnp.bfloat16)
    acc = x
    for c in range(D_FF // FF_CHUNK):
        u = jnp.dot(hm, wup_ref[:, c * FF_CHUNK:(c + 1) * FF_CHUNK].astype(jnp.bfloat16),
                    preferred_element_type=jnp.float32)
        u = jnp.square(jnp.maximum(u, 0.0)).astype(jnp.bfloat16)
        acc = acc + jnp.dot(u, wdown_ref[c * FF_CHUNK:(c + 1) * FF_CHUNK, :].astype(jnp.bfloat16),
                            preferred_element_type=jnp.float32)
    out_ref[...] = _rms(acc, gfin_ref[...]) if final_norm else acc


def _mlp_call(x2, g, wup, wdown, gfin, layer, final_norm):
    t = x2.shape[0]
    tm = TM_MLP
    const = lambda i: (0, 0)
    per_layer = lambda i: (layer, 0, 0)
    return pl.pallas_call(
        functools.partial(_mlp_kernel, final_norm=final_norm),
        grid=(t // tm,),
        in_specs=[
            pl.BlockSpec((tm, D_MODEL), lambda i: (i, 0)),
            pl.BlockSpec((None, 1, D_MODEL), per_layer),
            pl.BlockSpec((None,) + wup.shape[1:], per_layer, pipeline_mode=pl.Buffered(1)),
            pl.BlockSpec((None,) + wdown.shape[1:], per_layer, pipeline_mode=pl.Buffered(1)),
            pl.BlockSpec((1, D_MODEL), const),
        ],
        out_specs=pl.BlockSpec((tm, D_MODEL), lambda i: (i, 0)),
        out_shape=jax.ShapeDtypeStruct((t, D_MODEL), jnp.float32),
        compiler_params=pltpu.CompilerParams(dimension_semantics=("parallel",), vmem_limit_bytes=VMEM_LIMIT),
        name="mlp",
    )(x2, g, wup, wdown, gfin)


def _rope_tables(seq):
    half = ROT_DIM // 2
    inv_freq = 1.0 / (ROPE_THETA ** (jnp.arange(0, ROT_DIM, 2, dtype=jnp.float32) / ROT_DIM))
    ang = jnp.arange(seq, dtype=jnp.int32).astype(jnp.float32)[:, None] * inv_freq[None, :]
    cos, sin = jnp.cos(ang), jnp.sin(ang)
    ones = jnp.ones((seq, HEAD_DIM - ROT_DIM), jnp.float32)
    zeros_half = jnp.zeros((seq, half), jnp.float32)
    zeros_rest = jnp.zeros((seq, HEAD_DIM - ROT_DIM), jnp.float32)
    c_head = jnp.concatenate([cos, cos, ones], axis=1)
    lo_head = jnp.concatenate([-sin, zeros_half, zeros_rest], axis=1)
    hi_head = jnp.concatenate([zeros_half, sin, zeros_rest], axis=1)
    rep = LANES // HEAD_DIM
    return jnp.tile(c_head, (1, rep)), jnp.tile(lo_head, (1, rep)), jnp.tile(hi_head, (1, rep))


def kernel(x, mem, norm_mix, w_in, w_proj_a, w_proj_b, w_proj_m, w_out, norm_mem, w_mem_kv, norm_mlp, w_up,
           w_down, norm_final):
    batch, seq, d = x.shape
    assert d == D_MODEL and seq == SEQ and seq % TM_QKV == 0
    depth = w_in.shape[0]
    bf = jnp.bfloat16
    x2 = x.reshape(batch * seq, d)
    mem2 = mem.reshape(batch * mem.shape[1], d)
    rope_c, rope_s1, rope_s2 = _rope_tables(seq)
    bias = jnp.asarray(_BIAS_NP)
    cbias = jnp.asarray(_CLASS_BIAS_NP)
    w_pa, w_pb, w_pm = w_proj_a.astype(bf), w_proj_b.astype(bf), w_proj_m.astype(bf)
    w_o, w_mkv = w_out.astype(bf), w_mem_kv.astype(bf)
    g_mix, g_mem, g_mlp = (g.reshape(depth, 1, d) for g in (norm_mix, norm_mem, norm_mlp))
    for l in range(depth):
        qka, vta, qkb, vtb, qm, cls = _qkv_call(x2, g_mix, w_in, l, rope_c, rope_s1, rope_s2, batch, seq)
        o, lse, ocls, lsecls = _attn_call(qka, vta, qkb, vtb, qm, cls, mem2, g_mem, w_mkv, l, bias, cbias, batch, seq)
        x2 = _mix_call(x2, o, lse, ocls, lsecls, g_mix, w_in, w_pa, w_pb, w_pm, w_o, l)
        x2 = _mlp_call(x2, g_mlp, w_up, w_down, norm_final.reshape(1, d), l, final_norm=(l == depth - 1))
    return x2.reshape(batch, seq, d)
```

```python
import functools

import jax
import jax.numpy as jnp
import numpy as np
from jax import lax
from jax.experimental import pallas as pl
from jax.experimental.pallas import tpu as pltpu

D_MODEL = 1024
SEQ = 2048
HEAD_DIM = 64
ROT_DIM = HEAD_DIM // 4
ROPE_THETA = 500000.0
DIL_GROUPS = ((128, 1), (512, 4), (2048, 16))
HEADS_PER_DIL_GROUP = 2
N_HEADS_A = len(DIL_GROUPS) * HEADS_PER_DIL_GROUP
N_HEADS_B = 6
N_HEADS_M = 4
MOBA_BLOCK = 256
MOBA_TOPK = 3
N_BLOCKS = SEQ // MOBA_BLOCK
D_FF = 4 * D_MODEL
WIDTH_A = N_HEADS_A * HEAD_DIM
WIDTH_A_OUT = HEADS_PER_DIL_GROUP * HEAD_DIM
WIDTH_B = N_HEADS_B * HEAD_DIM
WIDTH_M = N_HEADS_M * HEAD_DIM
WIDTH_O = WIDTH_A_OUT + WIDTH_B + WIDTH_M
RMS_EPS = 1e-6
NEG_INF = -1e30
Q_SCALE = HEAD_DIM ** -0.5
LOG2_E = 1.4426950408889634

LANES = 128
LSE_ROWS = 8
PAIR = 2 * HEAD_DIM
SUM_ROWS = 16
TQ = MOBA_BLOCK
CLASS_GROUP = len(DIL_GROUPS) - 1
CLASS_STRIDE = DIL_GROUPS[CLASS_GROUP][1]
CLASS_LEN = SEQ // CLASS_STRIDE
CLASSES_PER_TILE = TQ // CLASS_LEN
TM_QKV = 1024
TM_MIX = 1024
TM_MLP = 1024
FF_CHUNK = 1024
VMEM_LIMIT = 60 * 1024 * 1024

_B_ALLNEG = 0
_B_CAUSAL = 1


TILE_GROUPS = DIL_GROUPS[:CLASS_GROUP]
assert DIL_GROUPS[CLASS_GROUP][0] >= SEQ and TQ % CLASS_LEN == 0


def _dil_tile_offsets():
    return tuple(tuple(range(min((w + TQ - 1) // TQ, N_BLOCKS - 1) + 1)) for w, _ in TILE_GROUPS)


def _build_bias_tiles():
    c = np.arange(TQ)[:, None]
    r = np.arange(TQ)[None, :]
    tiles = [np.zeros((TQ, TQ), bool), (r - c) >= 0]
    ids = []
    key_rows = []
    for (w, d), offs in zip(TILE_GROUPS, _dil_tile_offsets()):
        per_off = []
        for o in offs:
            diff = o * TQ + r - c
            per_off.append((diff >= 0) & (diff <= w) & (diff % d == 0))
        uniq, gid = [], []
        for t in per_off:
            for k, u in enumerate(uniq):
                if np.array_equal(t, u):
                    gid.append(k)
                    break
            else:
                uniq.append(t)
                gid.append(len(uniq) - 1)
        ids.append(tuple(len(tiles) + k for k in gid))
        tiles.extend(uniq)
        used = [np.flatnonzero(t.any(axis=1)) for t in per_off]
        key_rows.append(tuple((int(u.min()) // LANES * LANES, -(-(int(u.max()) + 1) // LANES) * LANES) for u in used))
    bias = np.where(np.stack(tiles), 0.0, NEG_INF).astype(np.float32)
    return bias, tuple(ids), tuple(key_rows)


_BIAS_NP, _DIL_BIAS_IDS, _DIL_KEY_ROWS = _build_bias_tiles()
_pos = np.arange(CLASS_LEN)
_CLASS_BIAS_NP = np.where(_pos[None, :] >= _pos[:, None], 0.0, NEG_INF).astype(np.float32)

_NT = (((1,), (1,)), ((), ()))
_TN = (((0,), (1,)), ((), ()))
_TN_PLAIN = (((0,), (0,)), ((), ()))
COL_B = 3 * WIDTH_A
COL_M = COL_B + 3 * WIDTH_B
COL_GATES = COL_M + WIDTH_M
GATE_COLS = 3 * D_MODEL


def _rms(x, g):
    return x * lax.rsqrt(jnp.mean(x * x, axis=-1, keepdims=True) + RMS_EPS) * g


def _qkv_kernel(x_ref, g_ref, win_ref, c_ref, s1_ref, s2_ref, qka_ref, vta_ref, qkb_ref, vtb_ref, qm_ref, cls_ref):
    h = _rms(x_ref[...], g_ref[...]).astype(jnp.bfloat16)
    cos = c_ref[...]
    sin_lo = s1_ref[...]
    sin_hi = s2_ref[...]
    for col0, qk_ref, vt_ref, width in ((0, qka_ref, vta_ref, WIDTH_A), (COL_B, qkb_ref, vtb_ref, WIDTH_B)):
        z = jnp.dot(h, win_ref[:, col0:col0 + 2 * width].astype(jnp.bfloat16),
                    preferred_element_type=jnp.float32)
        for blk in range(2 * width // LANES):
            zb = z[:, blk * LANES:(blk + 1) * LANES]
            rb = zb * cos + pltpu.roll(zb, LANES - ROT_DIM // 2, 1) * sin_lo + pltpu.roll(zb, ROT_DIM // 2, 1) * sin_hi
            if blk < width // LANES:
                rb = rb * (Q_SCALE * LOG2_E)
            qk_ref[:, blk * LANES:(blk + 1) * LANES] = rb.astype(jnp.bfloat16)
            if col0 == 0 and blk % (width // LANES) == CLASS_GROUP:
                part = blk // (width // LANES)
                cls_ref[part] = rb
        vt = lax.dot_general(win_ref[:, col0 + 2 * width:col0 + 3 * width].astype(jnp.bfloat16), h, _TN,
                             preferred_element_type=jnp.float32)
        vt_ref[...] = vt.astype(jnp.bfloat16)
        if col0 == 0:
            cls_ref[2] = vt[CLASS_GROUP * PAIR:(CLASS_GROUP + 1) * PAIR, :].T
    qm = jnp.dot(h, win_ref[:, COL_M:COL_GATES].astype(jnp.bfloat16), preferred_element_type=jnp.float32) * (Q_SCALE * LOG2_E)
    qm_ref[...] = qm.astype(jnp.bfloat16)


def _qkv_call(x2, g, w_in, layer, rope_c, rope_s1, rope_s2, batch, seq):
    t = x2.shape[0]
    tm = TM_QKV
    nts = seq // tm
    per_layer = lambda i: (layer, 0, 0)
    return pl.pallas_call(
        _qkv_kernel,
        grid=(t // tm,),
        in_specs=[
            pl.BlockSpec((tm, D_MODEL), lambda i: (i, 0)),
            pl.BlockSpec((None, 1, D_MODEL), per_layer),
            pl.BlockSpec((None, D_MODEL, COL_GATES), per_layer),
            pl.BlockSpec((tm, LANES), lambda i: (i % nts, 0)),
            pl.BlockSpec((tm, LANES), lambda i: (i % nts, 0)),
            pl.BlockSpec((tm, LANES), lambda i: (i % nts, 0)),
        ],
        out_specs=[
            pl.BlockSpec((tm, 2 * WIDTH_A), lambda i: (i, 0)),
            pl.BlockSpec((WIDTH_A, tm), lambda i: (i // nts, i % nts)),
            pl.BlockSpec((tm, 2 * WIDTH_B), lambda i: (i, 0)),
            pl.BlockSpec((WIDTH_B, tm), lambda i: (i // nts, i % nts)),
            pl.BlockSpec((tm, WIDTH_M), lambda i: (i, 0)),
            pl.BlockSpec((3, tm, PAIR), lambda i: (0, i, 0)),
        ],
        out_shape=[
            jax.ShapeDtypeStruct((t, 2 * WIDTH_A), jnp.bfloat16),
            jax.ShapeDtypeStruct((batch * WIDTH_A, seq), jnp.bfloat16),
            jax.ShapeDtypeStruct((t, 2 * WIDTH_B), jnp.bfloat16),
            jax.ShapeDtypeStruct((batch * WIDTH_B, seq), jnp.bfloat16),
            jax.ShapeDtypeStruct((t, WIDTH_M), jnp.bfloat16),
            jax.ShapeDtypeStruct((3, t, PAIR), jnp.float32),
        ],
        compiler_params=pltpu.CompilerParams(dimension_semantics=("parallel",), vmem_limit_bytes=VMEM_LIMIT),
        name="qkv_proj",
    )(x2, g, w_in, rope_c, rope_s1, rope_s2)


def _pv(v_heads, p_bf):
    ones = jnp.ones((SUM_ROWS, p_bf.shape[0]), jnp.bfloat16)
    nq = p_bf.shape[1] // len(v_heads)
    return jnp.concatenate([jnp.dot(jnp.concatenate([v, ones], axis=0), p_bf[:, h * nq:(h + 1) * nq],
                                    preferred_element_type=jnp.float32)
                            for h, v in enumerate(v_heads)], axis=1)


def _after(x, *matmul_results):
    for r in matmul_results:
        bits = lax.bitcast_convert_type(r[-1:, :x.shape[1]], jnp.uint32)
        bits = lax.shift_right_logical(lax.shift_right_logical(bits, jnp.uint32(16)), jnp.uint32(16))
        x = x + lax.bitcast_convert_type(bits, jnp.float32)
    return x


def _scores_t(k_tile, q_heads):
    return lax.dot_general(k_tile, q_heads, _NT, preferred_element_type=jnp.float32)


def _stack_heads(q, n_heads):
    lane = lax.broadcasted_iota(jnp.int32, q.shape, 1)
    zero = jnp.zeros((), q.dtype)
    return jnp.concatenate([jnp.where((lane >= h * HEAD_DIM) & (lane < (h + 1) * HEAD_DIM), q, zero)
                            for h in range(n_heads)], axis=0)


def _head_rows(first_head, n_heads):
    return [slice((first_head + h) * HEAD_DIM, (first_head + h + 1) * HEAD_DIM) for h in range(n_heads)]


def _lanes_to_rows(row, n_heads):
    nq = row.shape[1] // n_heads
    return jnp.concatenate([jnp.broadcast_to(row[:, h * nq:(h + 1) * nq], (HEAD_DIM, nq)) for h in range(n_heads)],
                           axis=0)


def _heads_to_rows(o_t, n_heads):
    nq = o_t.shape[1] // n_heads
    return jnp.concatenate([o_t[:, h * nq:(h + 1) * nq] for h in range(n_heads)], axis=0)


def _attn_kernel(qa_ref, ka_ref, vta_ref, qb_ref, kb_ref, vtb_ref, qm_ref, cls_ref, mem_ref, gmem_ref, wmkv_ref,
                 bias_ref, cbias_ref, o_ref, lse_ref, ocls_ref, lsecls_ref,
                 kmean_ref, km_ref, vmt_ref, selb_ref, m_ref, acc_ref, raw_ref):
    qi = pl.program_id(1)

    @pl.when(qi == 0)
    def _per_batch():
        kb = kb_ref[...].astype(jnp.float32)
        kmean_ref[...] = jnp.mean(kb.reshape(N_BLOCKS, MOBA_BLOCK, WIDTH_B), axis=1)
        mem_n = _rms(mem_ref[...], gmem_ref[...]).astype(jnp.bfloat16)
        km_ref[...] = jnp.dot(mem_n, wmkv_ref[:, :WIDTH_M], preferred_element_type=jnp.float32).astype(jnp.bfloat16)
        vmt_ref[...] = lax.dot_general(wmkv_ref[:, WIDTH_M:], mem_n, _TN,
                                       preferred_element_type=jnp.float32).astype(jnp.bfloat16)

    def keys_of(j):
        return pl.ds(pl.multiple_of(j * TQ, TQ), TQ)

    def both_heads(bias):
        return jnp.concatenate([bias, bias], axis=1)

    def single_tile(s_t, v_heads):
        m = jnp.max(s_t, axis=0, keepdims=True)
        return m, _pv(v_heads, jnp.exp2(s_t - m).astype(jnp.bfloat16))

    for c in range(CLASSES_PER_TILE):
        rows = pl.ds(qi * CLASSES_PER_TILE + c, CLASS_LEN, stride=CLASS_STRIDE)
        q_heads = _stack_heads(cls_ref[0, rows, :].astype(jnp.bfloat16), HEADS_PER_DIL_GROUP)
        k_cls = cls_ref[1, rows, :].astype(jnp.bfloat16)
        v_t = cls_ref[2, rows, :].T.astype(jnp.bfloat16)
        m, acc = single_tile(_scores_t(k_cls, q_heads) + both_heads(cbias_ref[...]),
                             [v_t[r] for r in _head_rows(0, HEADS_PER_DIL_GROUP)])
        l = acc[HEAD_DIM:HEAD_DIM + 1]
        out = slice(c * CLASS_LEN, (c + 1) * CLASS_LEN)
        ocls_ref[out, :] = _heads_to_rows(acc[:HEAD_DIM] / l, HEADS_PER_DIL_GROUP).T
        lsecls_ref[out, :] = _lanes_to_rows(m + jnp.log2(l), HEADS_PER_DIL_GROUP).T

    a_static = []
    for g, offs in enumerate(_dil_tile_offsets()):
        cols = slice(g * PAIR, (g + 1) * PAIR)
        rows = _head_rows(g * HEADS_PER_DIL_GROUP, HEADS_PER_DIL_GROUP)
        q_heads = _stack_heads(qa_ref[:, cols], HEADS_PER_DIL_GROUP)
        ids = _DIL_BIAS_IDS[g]
        for o in offs:
            lo, hi = _DIL_KEY_ROWS[g][o]
            ks = pl.ds(pl.multiple_of(jnp.maximum(qi - o, 0) * TQ + lo, LANES), hi - lo)
            bid = ids[o] if o == 0 else jnp.where(qi >= o, ids[o], _B_ALLNEG)
            a_static.append((ks, (bid, slice(lo, hi)), cols, rows, q_heads))

    blk = lax.broadcasted_iota(jnp.int32, (N_BLOCKS, 2 * TQ), 0)
    own = keys_of(qi)
    b_heads = []
    for p in range(N_HEADS_B // 2):
        cols = slice(p * PAIR, (p + 1) * PAIR)
        q_heads = _stack_heads(qb_ref[:, cols], 2)
        kmean = kmean_ref[:, cols]
        kmean_hi = kmean.astype(jnp.bfloat16)
        kmean_lo = (kmean - kmean_hi.astype(jnp.float32)).astype(jnp.bfloat16)
        gate = _scores_t(kmean_hi, q_heads) + _scores_t(kmean_lo, q_heads)
        gate = jnp.where(blk < qi, gate, NEG_INF)
        beaten = jnp.zeros(gate.shape, jnp.float32)
        for j in range(N_BLOCKS):
            gj = gate[j:j + 1, :]
            wins_tie = jnp.where(gj >= gate, 1.0, 0.0)
            wins_strict = jnp.where(gj > gate, 1.0, 0.0)
            beaten = beaten + jnp.where(blk > j, wins_tie, wins_strict)
        selb_ref[p] = jnp.where((beaten < MOBA_TOPK) & (blk < qi), 0.0, NEG_INF)
        b_heads.append((cols, _head_rows(2 * p, 2), q_heads))
    n_b = len(b_heads)

    def a_scores(part):
        ks, _, cols, _, q_heads = part
        return _scores_t(ka_ref[ks, cols], q_heads)

    def b_scores(p, ks):
        cols, _, q_heads = b_heads[p]
        return _scores_t(kb_ref[ks, cols], q_heads)

    n_a = len(a_static)
    raw = a_scores(a_static[0])
    a_biased = []
    m_a = None
    for i in range(n_a):
        nxt = a_scores(a_static[i + 1]) if i + 1 < n_a else b_scores(0, own)
        s_t = raw + both_heads(bias_ref[a_static[i][1]])
        part_max = _after(jnp.max(s_t, axis=0, keepdims=True), nxt)
        m_a = part_max if m_a is None else jnp.maximum(m_a, part_max)
        a_biased.append(s_t)
        raw = nxt
    m_heads = _stack_heads(qm_ref[...], N_HEADS_M)
    ahead = [lambda: b_scores(1, own), lambda: b_scores(2, own), lambda: _scores_t(km_ref[...], m_heads),
             lambda: b_scores(0, keys_of(0)), lambda: b_scores(1, keys_of(0))]
    later = [raw]
    assert n_a >= len(ahead)
    neg_m = -m_a
    acc_a = None
    for i in range(n_a):
        ks, _, _, rows, _ = a_static[i]
        p_t = jnp.exp2(a_biased[i] + neg_m).astype(jnp.bfloat16)
        if i < len(ahead):
            later.append(ahead[i]())
            neg_m = _after(neg_m, later[-1])
        part_acc = _pv([vta_ref[r, ks] for r in rows], p_t)
        acc_a = part_acc if acc_a is None else acc_a + part_acc
    l_a = acc_a[HEAD_DIM:HEAD_DIM + 1]
    o_tiles = [acc_a[:HEAD_DIM] / l_a]
    lse_a = m_a + jnp.log2(l_a)
    lse_ref[...] = jnp.concatenate([jnp.broadcast_to(lse_a[:, h * TQ:(h + 1) * TQ], (LSE_ROWS // HEADS_PER_DIL_GROUP, TQ))
                                    for h in range(HEADS_PER_DIL_GROUP)], axis=0)

    for p, (_, rows, _) in enumerate(b_heads):
        m_ref[p], acc_ref[p] = single_tile(later[p] + both_heads(bias_ref[_B_CAUSAL]),
                                           [vtb_ref[r, own] for r in rows])
    _, acc_m = single_tile(later[3], [vmt_ref[r, :] for r in _head_rows(0, N_HEADS_M)])
    raw_ref[0] = later[4]
    raw_ref[1] = later[5]

    def past_tile(j, _):
        ks = keys_of(j)
        ks_next = keys_of(jnp.minimum(j + 1, qi - 1))
        scores = [raw_ref[0], raw_ref[1]]
        for c in range(n_b):
            issued = b_scores(c + 2, ks) if c + 2 < n_b else b_scores(c + 2 - n_b, ks_next)
            scores.append(issued)
            sel = selb_ref[c, pl.ds(j, 1), :]
            m_old = m_ref[c]
            m_new = jnp.maximum(m_old, jnp.max(scores[c], axis=0, keepdims=True) + sel)
            alpha = _after(jnp.exp2(m_old - m_new), issued)
            p_t = jnp.exp2(scores[c] + (sel - m_new)).astype(jnp.bfloat16)
            acc_ref[c] = alpha * acc_ref[c] + _pv([vtb_ref[r, ks] for r in b_heads[c][1]], p_t)
            m_ref[c] = m_new
        raw_ref[0] = scores[n_b]
        raw_ref[1] = scores[n_b + 1]
        return 0

    lax.fori_loop(0, qi, past_tile, 0)

    for acc in (*[acc_ref[c] for c in range(n_b)], acc_m):
        o_tiles.append(acc[:HEAD_DIM] / acc[HEAD_DIM:HEAD_DIM + 1])
    row = 0
    for o_t in o_tiles:
        for h in range(o_t.shape[1] // TQ):
            o_ref[row:row + HEAD_DIM, :] = o_t[:, h * TQ:(h + 1) * TQ].astype(jnp.bfloat16)
            row += HEAD_DIM


def _attn_call(qka, vta, qkb, vtb, qm, cls, mem2, gmem, wmkv, layer, bias, cbias, batch, seq):
    nq = seq // TQ
    n_mem = mem2.shape[0] // batch
    per_layer = lambda b, q: (layer, 0, 0)
    per_tile = lambda b, q: (b * nq + q, 0)
    return pl.pallas_call(
        _attn_kernel,
        grid=(batch, nq),
        in_specs=[
            pl.BlockSpec((TQ, WIDTH_A), lambda b, q: (b * nq + q, 0)),
            pl.BlockSpec((seq, WIDTH_A), lambda b, q: (b, 1)),
            pl.BlockSpec((WIDTH_A, seq), lambda b, q: (b, 0)),
            pl.BlockSpec((TQ, WIDTH_B), lambda b, q: (b * nq + q, 0)),
            pl.BlockSpec((seq, WIDTH_B), lambda b, q: (b, 1)),
            pl.BlockSpec((WIDTH_B, seq), lambda b, q: (b, 0)),
            pl.BlockSpec((TQ, WIDTH_M), lambda b, q: (b * nq + q, 0)),
            pl.BlockSpec((3, seq, PAIR), lambda b, q: (0, b, 0)),
            pl.BlockSpec((n_mem, D_MODEL), lambda b, q: (b, 0)),
            pl.BlockSpec((None, 1, D_MODEL), per_layer),
            pl.BlockSpec((None, D_MODEL, 2 * WIDTH_M), per_layer),
            pl.BlockSpec(bias.shape, lambda b, q: (0, 0, 0)),
            pl.BlockSpec(cbias.shape, lambda b, q: (0, 0)),
        ],
        out_specs=[pl.BlockSpec((WIDTH_O, TQ), lambda b, q: (0, b * nq + q)),
                   pl.BlockSpec((LSE_ROWS, TQ), lambda b, q: (0, b * nq + q)),
                   pl.BlockSpec((TQ, PAIR), per_tile), pl.BlockSpec((TQ, PAIR), per_tile)],
        out_shape=[jax.ShapeDtypeStruct((WIDTH_O, batch * seq), jnp.bfloat16),
                   jax.ShapeDtypeStruct((LSE_ROWS, batch * seq), jnp.float32),
                   jax.ShapeDtypeStruct((batch * seq, PAIR), jnp.float32),
                   jax.ShapeDtypeStruct((batch * seq, PAIR), jnp.float32)],
        scratch_shapes=[
            pltpu.VMEM((N_BLOCKS, WIDTH_B), jnp.float32),
            pltpu.VMEM((n_mem, WIDTH_M), jnp.bfloat16),
            pltpu.VMEM((WIDTH_M, n_mem), jnp.bfloat16),
            pltpu.VMEM((N_HEADS_B // 2, N_BLOCKS, 2 * TQ), jnp.float32),
            pltpu.VMEM((N_HEADS_B // 2, 1, 2 * TQ), jnp.float32),
            pltpu.VMEM((N_HEADS_B // 2, HEAD_DIM + SUM_ROWS, 2 * TQ), jnp.float32),
            pltpu.VMEM((2, TQ, 2 * TQ), jnp.float32),
        ],
        compiler_params=pltpu.CompilerParams(dimension_semantics=("parallel", "arbitrary"),
                                             vmem_limit_bytes=VMEM_LIMIT),
        name="attn",
    )(qka, qka, vta, qkb, qkb, vtb, qm, cls, mem2, gmem, wmkv, bias, cbias)


def _mix_kernel(x_ref, o_ref, lse_ref, ocls_ref, lsecls_ref, g_ref, wg_ref, wpa_ref, wpb_ref, wpm_ref, wo_ref,
                out_ref, ocn_ref, lcn_ref):
    tm = x_ref.shape[0]
    per_class = tm // CLASS_STRIDE
    first = (pl.program_id(0) % (SEQ // tm)) * per_class
    for r in range(CLASS_STRIDE):
        src = pl.ds(pl.multiple_of(r * CLASS_LEN + first, per_class), per_class)
        dst = pl.ds(r, per_class, stride=CLASS_STRIDE)
        ocn_ref[dst, :] = ocls_ref[src, :]
        lcn_ref[dst, :] = lsecls_ref[src, :]
    lse_t = lse_ref[...]
    per_head = LSE_ROWS // HEADS_PER_DIL_GROUP
    lse_tiled = jnp.concatenate([jnp.broadcast_to(lse_t[h * per_head:h * per_head + 1], (HEAD_DIM, tm))
                                 for h in range(HEADS_PER_DIL_GROUP)], axis=0)
    w_cls = 1.0 / (1.0 + jnp.exp2(lse_tiled - lcn_ref[...].T))
    o_tiled = o_ref[:WIDTH_A_OUT, :].astype(jnp.float32)
    o_a = (o_tiled + w_cls * (ocn_ref[...].T - o_tiled)).astype(jnp.bfloat16)
    x = x_ref[...]
    h = _rms(x, g_ref[...]).astype(jnp.bfloat16)
    y = None
    col = 0
    for i, wp_ref in enumerate((wpa_ref, wpb_ref, wpm_ref)):
        width = wp_ref.shape[0]
        gate = jax.nn.sigmoid(jnp.dot(h, wg_ref[0, :, i * D_MODEL:(i + 1) * D_MODEL].astype(jnp.bfloat16),
                                      preferred_element_type=jnp.float32))
        o_i = o_a if i == 0 else o_ref[col:col + width, :]
        branch = gate * lax.dot_general(o_i, wp_ref[...], _TN_PLAIN, preferred_element_type=jnp.float32)
        y = branch if y is None else y + branch
        col += width
    out_ref[...] = x + jnp.dot(y.astype(jnp.bfloat16), wo_ref[...], preferred_element_type=jnp.float32)


def _mix_call(x2, o, lse, ocls, lsecls, g, w_in, wpa, wpb, wpm, wo, layer):
    t = x2.shape[0]
    tm = TM_MIX
    per_layer = lambda i: (layer, 0, 0)
    return pl.pallas_call(
        _mix_kernel,
        grid=(t // tm,),
        in_specs=[
            pl.BlockSpec((tm, D_MODEL), lambda i: (i, 0)),
            pl.BlockSpec((WIDTH_O, tm), lambda i: (0, i)),
            pl.BlockSpec((LSE_ROWS, tm), lambda i: (0, i)),
            pl.BlockSpec((SEQ, PAIR), lambda i: (i // (SEQ // tm), 0)),
            pl.BlockSpec((SEQ, PAIR), lambda i: (i // (SEQ // tm), 0)),
            pl.BlockSpec((None, 1, D_MODEL), per_layer),
            pl.BlockSpec((pl.Element(1), pl.Element(D_MODEL), pl.Element(GATE_COLS)), lambda i: (layer, 0, COL_GATES),
                         pipeline_mode=pl.Buffered(1)),
            pl.BlockSpec((None,) + wpa.shape[1:], per_layer),
            pl.BlockSpec((None,) + wpb.shape[1:], per_layer),
            pl.BlockSpec((None,) + wpm.shape[1:], per_layer),
            pl.BlockSpec((None,) + wo.shape[1:], per_layer),
        ],
        out_specs=pl.BlockSpec((tm, D_MODEL), lambda i: (i, 0)),
        out_shape=jax.ShapeDtypeStruct((t, D_MODEL), jnp.float32),
        scratch_shapes=[pltpu.VMEM((tm, PAIR), jnp.float32), pltpu.VMEM((tm, PAIR), jnp.float32)],
        compiler_params=pltpu.CompilerParams(dimension_semantics=("parallel",), vmem_limit_bytes=VMEM_LIMIT),
        name="gated_mix",
    )(x2, o, lse, ocls, lsecls, g, w_in, wpa, wpb, wpm, wo)


def _mlp_kernel(x_ref, g_ref, wup_ref, wdown_ref, gfin_ref, out_ref, *, final_norm):
    x = x_ref[...]
    hm = _rms(x, g_ref[...]).astype(jnp.bfloat16)
    acc = x
    for c in range(D_FF // FF_CHUNK):
        u = jnp.dot(hm, wup_ref[:, c * FF_CHUNK:(c + 1) * FF_CHUNK].astype(jnp.bfloat16),
                    preferred_element_type=jnp.float32)
        u = jnp.square(jnp.maximum(u, 0.0)).astype(jnp.bfloat16)
        acc = acc + jnp.dot(u, wdown_ref[c * FF_CHUNK:(c + 1) * FF_CHUNK, :].astype(jnp.bfloat16),
                            preferred_element_type=jnp.float32)
    out_ref[...] = _rms(acc, gfin_ref[...]) if final_norm else acc


def _mlp_call(x2, g, wup, wdown, gfin, layer, final_norm):
    t = x2.shape[0]
    tm = TM_MLP
    const = lambda i: (0, 0)
    per_layer = lambda i: (layer, 0, 0)
    return pl.pallas_call(
        functools.partial(_mlp_kernel, final_norm=final_norm),
        grid=(t // tm,),
        in_specs=[
            pl.BlockSpec((tm, D_MODEL), lambda i: (i, 0)),
            pl.BlockSpec((None, 1, D_MODEL), per_layer),
            pl.BlockSpec((None,) + wup.shape[1:], per_layer, pipeline_mode=pl.Buffered(1)),
            pl.BlockSpec((None,) + wdown.shape[1:], per_layer, pipeline_mode=pl.Buffered(1)),
            pl.BlockSpec((1, D_MODEL), const),
        ],
        out_specs=pl.BlockSpec((tm, D_MODEL), lambda i: (i, 0)),
        out_shape=jax.ShapeDtypeStruct((t, D_MODEL), jnp.float32),
        compiler_params=pltpu.CompilerParams(dimension_semantics=("parallel",), vmem_limit_bytes=VMEM_LIMIT),
        name="mlp",
    )(x2, g, wup, wdown, gfin)


def _rope_tables(seq):
    half = ROT_DIM // 2
    inv_freq = 1.0 / (ROPE_THETA ** (jnp.arange(0, ROT_DIM, 2, dtype=jnp.float32) / ROT_DIM))
    ang = jnp.arange(seq, dtype=jnp.int32).astype(jnp.float32)[:, None] * inv_freq[None, :]
    cos, sin = jnp.cos(ang), jnp.sin(ang)
    ones = jnp.ones((seq, HEAD_DIM - ROT_DIM), jnp.float32)
    zeros_half = jnp.zeros((seq, half), jnp.float32)
    zeros_rest = jnp.zeros((seq, HEAD_DIM - ROT_DIM), jnp.float32)
    c_head = jnp.concatenate([cos, cos, ones], axis=1)
    lo_head = jnp.concatenate([-sin, zeros_half, zeros_rest], axis=1)
    hi_head = jnp.concatenate([zeros_half, sin, zeros_rest], axis=1)
    rep = LANES // HEAD_DIM
    return jnp.tile(c_head, (1, rep)), jnp.tile(lo_head, (1, rep)), jnp.tile(hi_head, (1, rep))


def kernel(x, mem, norm_mix, w_in, w_proj_a, w_proj_b, w_proj_m, w_out, norm_mem, w_mem_kv, norm_mlp, w_up,
           w_down, norm_final):
    batch, seq, d = x.shape
    assert d == D_MODEL and seq == SEQ and seq % TM_QKV == 0
    depth = w_in.shape[0]
    bf = jnp.bfloat16
    x2 = x.reshape(batch * seq, d)
    mem2 = mem.reshape(batch * mem.shape[1], d)
    rope_c, rope_s1, rope_s2 = _rope_tables(seq)
    bias = jnp.asarray(_BIAS_NP)
    cbias = jnp.asarray(_CLASS_BIAS_NP)
    w_pa, w_pb, w_pm = w_proj_a.astype(bf), w_proj_b.astype(bf), w_proj_m.astype(bf)
    w_o, w_mkv = w_out.astype(bf), w_mem_kv.astype(bf)
    g_mix, g_mem, g_mlp = (g.reshape(depth, 1, d) for g in (norm_mix, norm_mem, norm_mlp))
    for l in range(depth):
        qka, vta, qkb, vtb, qm, cls = _qkv_call(x2, g_mix, w_in, l, rope_c, rope_s1, rope_s2, batch, seq)
        o, lse, ocls, lsecls = _attn_call(qka, vta, qkb, vtb, qm, cls, mem2, g_mem, w_mkv, l, bias, cbias, batch, seq)
        x2 = _mix_call(x2, o, lse, ocls, lsecls, g_mix, w_in, w_pa, w_pb, w_pm, w_o, l)
        x2 = _mlp_call(x2, g_mlp, w_up, w_down, norm_final.reshape(1, d), l, final_norm=(l == depth - 1))
    return x2.reshape(batch, seq, d)
```

```python
import functools

import jax
import jax.numpy as jnp
import numpy as np
from jax import lax
from jax.experimental import pallas as pl
from jax.experimental.pallas import tpu as pltpu

D_MODEL = 1024
SEQ = 2048
HEAD_DIM = 64
ROT_DIM = HEAD_DIM // 4
ROPE_THETA = 500000.0
DIL_GROUPS = ((128, 1), (512, 4), (2048, 16))
HEADS_PER_DIL_GROUP = 2
N_HEADS_A = len(DIL_GROUPS) * HEADS_PER_DIL_GROUP
N_HEADS_B = 6
N_HEADS_M = 4
MOBA_BLOCK = 256
MOBA_TOPK = 3
N_BLOCKS = SEQ // MOBA_BLOCK
D_FF = 4 * D_MODEL
WIDTH_A = N_HEADS_A * HEAD_DIM
WIDTH_A_OUT = HEADS_PER_DIL_GROUP * HEAD_DIM
WIDTH_B = N_HEADS_B * HEAD_DIM
WIDTH_M = N_HEADS_M * HEAD_DIM
WIDTH_O = WIDTH_A_OUT + WIDTH_B + WIDTH_M
RMS_EPS = 1e-6
NEG_INF = -1e30
Q_SCALE = HEAD_DIM ** -0.5
LOG2_E = 1.4426950408889634

LANES = 128
LSE_ROWS = 8
PAIR = 2 * HEAD_DIM
SUM_ROWS = 16
TQ = MOBA_BLOCK
CLASS_GROUP = len(DIL_GROUPS) - 1
CLASS_STRIDE = DIL_GROUPS[CLASS_GROUP][1]
CLASS_LEN = SEQ // CLASS_STRIDE
CLASSES_PER_TILE = TQ // CLASS_LEN
TM_QKV = 1024
TM_MIX = 1024
TM_MLP = 1024
FF_CHUNK = 1024
VMEM_LIMIT = 60 * 1024 * 1024

_B_ALLNEG = 0
_B_CAUSAL = 1


TILE_GROUPS = DIL_GROUPS[:CLASS_GROUP]
assert DIL_GROUPS[CLASS_GROUP][0] >= SEQ and TQ % CLASS_LEN == 0


def _dil_tile_offsets():
    return tuple(tuple(range(min((w + TQ - 1) // TQ, N_BLOCKS - 1) + 1)) for w, _ in TILE_GROUPS)


def _build_bias_tiles():
    c = np.arange(TQ)[:, None]
    r = np.arange(TQ)[None, :]
    tiles = [np.zeros((TQ, TQ), bool), (r - c) >= 0]
    ids = []
    key_rows = []
    for (w, d), offs in zip(TILE_GROUPS, _dil_tile_offsets()):
        per_off = []
        for o in offs:
            diff = o * TQ + r - c
            per_off.append((diff >= 0) & (diff <= w) & (diff % d == 0))
        uniq, gid = [], []
        for t in per_off:
            for k, u in enumerate(uniq):
                if np.array_equal(t, u):
                    gid.append(k)
                    break
            else:
                uniq.append(t)
                gid.append(len(uniq) - 1)
        ids.append(tuple(len(tiles) + k for k in gid))
        tiles.extend(uniq)
        used = [np.flatnonzero(t.any(axis=1)) for t in per_off]
        key_rows.append(tuple((int(u.min()) // LANES * LANES, -(-(int(u.max()) + 1) // LANES) * LANES) for u in used))
    bias = np.where(np.stack(tiles), 0.0, NEG_INF).astype(np.float32)
    return bias, tuple(ids), tuple(key_rows)


_BIAS_NP, _DIL_BIAS_IDS, _DIL_KEY_ROWS = _build_bias_tiles()
_pos = np.arange(CLASS_LEN)
_CLASS_BIAS_NP = np.where(_pos[None, :] >= _pos[:, None], 0.0, NEG_INF).astype(np.float32)

_NT = (((1,), (1,)), ((), ()))
_TN = (((0,), (1,)), ((), ()))
_TN_PLAIN = (((0,), (0,)), ((), ()))
COL_B = 3 * WIDTH_A
COL_M = COL_B + 3 * WIDTH_B
COL_GATES = COL_M + WIDTH_M
GATE_COLS = 3 * D_MODEL


def _rms(x, g):
    return x * lax.rsqrt(jnp.mean(x * x, axis=-1, keepdims=True) + RMS_EPS) * g


def _qkv_kernel(x_ref, g_ref, win_ref, c_ref, s1_ref, s2_ref, qka_ref, vta_ref, qkb_ref, vtb_ref, qm_ref, cls_ref,
                *, layer):
    h = _rms(x_ref[...], g_ref[layer:layer + 1, :]).astype(jnp.bfloat16)
    cos = c_ref[...]
    sin_lo = s1_ref[...]
    sin_hi = s2_ref[...]
    for col0, qk_ref, vt_ref, width in ((0, qka_ref, vta_ref, WIDTH_A), (COL_B, qkb_ref, vtb_ref, WIDTH_B)):
        z = jnp.dot(h, win_ref[:, col0:col0 + 2 * width].astype(jnp.bfloat16),
                    preferred_element_type=jnp.float32)
        for blk in range(2 * width // LANES):
            zb = z[:, blk * LANES:(blk + 1) * LANES]
            rb = zb * cos + pltpu.roll(zb, LANES - ROT_DIM // 2, 1) * sin_lo + pltpu.roll(zb, ROT_DIM // 2, 1) * sin_hi
            if blk < width // LANES:
                rb = rb * (Q_SCALE * LOG2_E)
            qk_ref[:, blk * LANES:(blk + 1) * LANES] = rb.astype(jnp.bfloat16)
            if col0 == 0 and blk % (width // LANES) == CLASS_GROUP:
                part = blk // (width // LANES)
                cls_ref[part] = rb
        vt = lax.dot_general(win_ref[:, col0 + 2 * width:col0 + 3 * width].astype(jnp.bfloat16), h, _TN,
                             preferred_element_type=jnp.float32)
        vt_ref[...] = vt.astype(jnp.bfloat16)
        if col0 == 0:
            cls_ref[2] = vt[CLASS_GROUP * PAIR:(CLASS_GROUP + 1) * PAIR, :].T
    qm = jnp.dot(h, win_ref[:, COL_M:COL_GATES].astype(jnp.bfloat16), preferred_element_type=jnp.float32) * (Q_SCALE * LOG2_E)
    qm_ref[...] = qm.astype(jnp.bfloat16)


def _qkv_call(x2, g, w_in, layer, rope_c, rope_s1, rope_s2, batch, seq):
    t = x2.shape[0]
    tm = TM_QKV
    nts = seq // tm
    per_layer = lambda i: (layer, 0, 0)
    return pl.pallas_call(
        functools.partial(_qkv_kernel, layer=layer),
        grid=(t // tm,),
        in_specs=[
            pl.BlockSpec((tm, D_MODEL), lambda i: (i, 0)),
            pl.BlockSpec(g.shape, lambda i: (0, 0)),
            pl.BlockSpec((None, D_MODEL, COL_GATES), per_layer),
            pl.BlockSpec((tm, LANES), lambda i: (i % nts, 0)),
            pl.BlockSpec((tm, LANES), lambda i: (i % nts, 0)),
            pl.BlockSpec((tm, LANES), lambda i: (i % nts, 0)),
        ],
        out_specs=[
            pl.BlockSpec((tm, 2 * WIDTH_A), lambda i: (i, 0)),
            pl.BlockSpec((WIDTH_A, tm), lambda i: (i // nts, i % nts)),
            pl.BlockSpec((tm, 2 * WIDTH_B), lambda i: (i, 0)),
            pl.BlockSpec((WIDTH_B, tm), lambda i: (i // nts, i % nts)),
            pl.BlockSpec((tm, WIDTH_M), lambda i: (i, 0)),
            pl.BlockSpec((3, tm, PAIR), lambda i: (0, i, 0)),
        ],
        out_shape=[
            jax.ShapeDtypeStruct((t, 2 * WIDTH_A), jnp.bfloat16),
            jax.ShapeDtypeStruct((batch * WIDTH_A, seq), jnp.bfloat16),
            jax.ShapeDtypeStruct((t, 2 * WIDTH_B), jnp.bfloat16),
            jax.ShapeDtypeStruct((batch * WIDTH_B, seq), jnp.bfloat16),
            jax.ShapeDtypeStruct((t, WIDTH_M), jnp.bfloat16),
            jax.ShapeDtypeStruct((3, t, PAIR), jnp.float32),
        ],
        compiler_params=pltpu.CompilerParams(dimension_semantics=("parallel",), vmem_limit_bytes=VMEM_LIMIT),
        name="qkv_proj",
    )(x2, g, w_in, rope_c, rope_s1, rope_s2)


def _pv(v_heads, p_bf):
    ones = jnp.ones((SUM_ROWS, p_bf.shape[0]), jnp.bfloat16)
    nq = p_bf.shape[1] // len(v_heads)
    return jnp.concatenate([jnp.dot(jnp.concatenate([v, ones], axis=0), p_bf[:, h * nq:(h + 1) * nq],
                                    preferred_element_type=jnp.float32)
                            for h, v in enumerate(v_heads)], axis=1)


def _after(x, *matmul_results):
    for r in matmul_results:
        bits = lax.bitcast_convert_type(r[-1:, :x.shape[1]], jnp.uint32)
        bits = lax.shift_right_logical(lax.shift_right_logical(bits, jnp.uint32(16)), jnp.uint32(16))
        x = x + lax.bitcast_convert_type(bits, jnp.float32)
    return x


def _scores_t(k_tile, q_heads):
    return lax.dot_general(k_tile, q_heads, _NT, preferred_element_type=jnp.float32)


def _stack_heads(q, n_heads):
    lane = lax.broadcasted_iota(jnp.int32, q.shape, 1)
    zero = jnp.zeros((), q.dtype)
    return jnp.concatenate([jnp.where((lane >= h * HEAD_DIM) & (lane < (h + 1) * HEAD_DIM), q, zero)
                            for h in range(n_heads)], axis=0)


def _head_rows(first_head, n_heads):
    return [slice((first_head + h) * HEAD_DIM, (first_head + h + 1) * HEAD_DIM) for h in range(n_heads)]


def _lanes_to_rows(row, n_heads):
    nq = row.shape[1] // n_heads
    return jnp.concatenate([jnp.broadcast_to(row[:, h * nq:(h + 1) * nq], (HEAD_DIM, nq)) for h in range(n_heads)],
                           axis=0)


def _heads_to_rows(o_t, n_heads):
    nq = o_t.shape[1] // n_heads
    return jnp.concatenate([o_t[:, h * nq:(h + 1) * nq] for h in range(n_heads)], axis=0)


def _attn_kernel(qa_ref, ka_ref, vta_ref, qb_ref, kb_ref, vtb_ref, qm_ref, cls_ref, mem_ref, gmem_ref, wmkv_ref,
                 bias_ref, cbias_ref, o_ref, lse_ref, ocls_ref, lsecls_ref,
                 kmean_ref, km_ref, vmt_ref, selb_ref, m_ref, acc_ref, raw_ref, *, layer):
    qi = pl.program_id(1)

    @pl.when(qi == 0)
    def _per_batch():
        kb = kb_ref[...].astype(jnp.float32)
        kmean_ref[...] = jnp.mean(kb.reshape(N_BLOCKS, MOBA_BLOCK, WIDTH_B), axis=1)
        mem_n = _rms(mem_ref[...], gmem_ref[layer:layer + 1, :]).astype(jnp.bfloat16)
        w_mkv = wmkv_ref[...].astype(jnp.bfloat16)
        km_ref[...] = jnp.dot(mem_n, w_mkv[:, :WIDTH_M], preferred_element_type=jnp.float32).astype(jnp.bfloat16)
        vmt_ref[...] = lax.dot_general(w_mkv[:, WIDTH_M:], mem_n, _TN,
                                       preferred_element_type=jnp.float32).astype(jnp.bfloat16)

    def keys_of(j):
        return pl.ds(pl.multiple_of(j * TQ, TQ), TQ)

    def both_heads(bias):
        return jnp.concatenate([bias, bias], axis=1)

    def single_tile(s_t, v_heads):
        m = jnp.max(s_t, axis=0, keepdims=True)
        return m, _pv(v_heads, jnp.exp2(s_t - m).astype(jnp.bfloat16))

    for c in range(CLASSES_PER_TILE):
        rows = pl.ds(qi * CLASSES_PER_TILE + c, CLASS_LEN, stride=CLASS_STRIDE)
        q_heads = _stack_heads(cls_ref[0, rows, :].astype(jnp.bfloat16), HEADS_PER_DIL_GROUP)
        k_cls = cls_ref[1, rows, :].astype(jnp.bfloat16)
        v_t = cls_ref[2, rows, :].T.astype(jnp.bfloat16)
        m, acc = single_tile(_scores_t(k_cls, q_heads) + both_heads(cbias_ref[...]),
                             [v_t[r] for r in _head_rows(0, HEADS_PER_DIL_GROUP)])
        l = acc[HEAD_DIM:HEAD_DIM + 1]
        out = slice(c * CLASS_LEN, (c + 1) * CLASS_LEN)
        ocls_ref[out, :] = _heads_to_rows(acc[:HEAD_DIM] / l, HEADS_PER_DIL_GROUP).T
        lsecls_ref[out, :] = _lanes_to_rows(m + jnp.log2(l), HEADS_PER_DIL_GROUP).T

    a_static = []
    for g, offs in enumerate(_dil_tile_offsets()):
        cols = slice(g * PAIR, (g + 1) * PAIR)
        rows = _head_rows(g * HEADS_PER_DIL_GROUP, HEADS_PER_DIL_GROUP)
        q_heads = _stack_heads(qa_ref[:, cols], HEADS_PER_DIL_GROUP)
        ids = _DIL_BIAS_IDS[g]
        for o in offs:
            lo, hi = _DIL_KEY_ROWS[g][o]
            ks = pl.ds(pl.multiple_of(jnp.maximum(qi - o, 0) * TQ + lo, LANES), hi - lo)
            bid = ids[o] if o == 0 else jnp.where(qi >= o, ids[o], _B_ALLNEG)
            a_static.append((ks, (bid, slice(lo, hi)), cols, rows, q_heads))

    blk = lax.broadcasted_iota(jnp.int32, (N_BLOCKS, 2 * TQ), 0)
    own = keys_of(qi)
    b_heads = []
    for p in range(N_HEADS_B // 2):
        cols = slice(p * PAIR, (p + 1) * PAIR)
        q_heads = _stack_heads(qb_ref[:, cols], 2)
        kmean = kmean_ref[:, cols]
        kmean_hi = kmean.astype(jnp.bfloat16)
        kmean_lo = (kmean - kmean_hi.astype(jnp.float32)).astype(jnp.bfloat16)
        gate = _scores_t(kmean_hi, q_heads) + _scores_t(kmean_lo, q_heads)
        gate = jnp.where(blk < qi, gate, NEG_INF)
        beaten = jnp.zeros(gate.shape, jnp.float32)
        for j in range(N_BLOCKS):
            gj = gate[j:j + 1, :]
            wins_tie = jnp.where(gj >= gate, 1.0, 0.0)
            wins_strict = jnp.where(gj > gate, 1.0, 0.0)
            beaten = beaten + jnp.where(blk > j, wins_tie, wins_strict)
        selb_ref[p] = jnp.where((beaten < MOBA_TOPK) & (blk < qi), 0.0, NEG_INF)
        b_heads.append((cols, _head_rows(2 * p, 2), q_heads))
    n_b = len(b_heads)

    def a_scores(part):
        ks, _, cols, _, q_heads = part
        return _scores_t(ka_ref[ks, cols], q_heads)

    def b_scores(p, ks):
        cols, _, q_heads = b_heads[p]
        return _scores_t(kb_ref[ks, cols], q_heads)

    n_a = len(a_static)
    raw = a_scores(a_static[0])
    a_biased = []
    m_a = None
    for i in range(n_a):
        nxt = a_scores(a_static[i + 1]) if i + 1 < n_a else b_scores(0, own)
        s_t = raw + both_heads(bias_ref[a_static[i][1]])
        part_max = _after(jnp.max(s_t, axis=0, keepdims=True), nxt)
        m_a = part_max if m_a is None else jnp.maximum(m_a, part_max)
        a_biased.append(s_t)
        raw = nxt
    m_heads = _stack_heads(qm_ref[...], N_HEADS_M)
    ahead = [lambda: b_scores(1, own), lambda: b_scores(2, own), lambda: _scores_t(km_ref[...], m_heads),
             lambda: b_scores(0, keys_of(0)), lambda: b_scores(1, keys_of(0))]
    later = [raw]
    assert n_a >= len(ahead)
    neg_m = -m_a
    acc_a = None
    for i in range(n_a):
        ks, _, _, rows, _ = a_static[i]
        p_t = jnp.exp2(a_biased[i] + neg_m).astype(jnp.bfloat16)
        if i < len(ahead):
            later.append(ahead[i]())
            neg_m = _after(neg_m, later[-1])
        part_acc = _pv([vta_ref[r, ks] for r in rows], p_t)
        acc_a = part_acc if acc_a is None else acc_a + part_acc
    l_a = acc_a[HEAD_DIM:HEAD_DIM + 1]
    o_tiles = [acc_a[:HEAD_DIM] / l_a]
    lse_a = m_a + jnp.log2(l_a)
    lse_ref[...] = jnp.concatenate([jnp.broadcast_to(lse_a[:, h * TQ:(h + 1) * TQ], (LSE_ROWS // HEADS_PER_DIL_GROUP, TQ))
                                    for h in range(HEADS_PER_DIL_GROUP)], axis=0)

    for p, (_, rows, _) in enumerate(b_heads):
        m_ref[p], acc_ref[p] = single_tile(later[p] + both_heads(bias_ref[_B_CAUSAL]),
                                           [vtb_ref[r, own] for r in rows])
    _, acc_m = single_tile(later[3], [vmt_ref[r, :] for r in _head_rows(0, N_HEADS_M)])
    raw_ref[0] = later[4]
    raw_ref[1] = later[5]

    def past_tile(j, _):
        ks = keys_of(j)
        ks_next = keys_of(jnp.minimum(j + 1, qi - 1))
        scores = [raw_ref[0], raw_ref[1]]
        for c in range(n_b):
            issued = b_scores(c + 2, ks) if c + 2 < n_b else b_scores(c + 2 - n_b, ks_next)
            scores.append(issued)
            sel = selb_ref[c, pl.ds(j, 1), :]
            m_old = m_ref[c]
            m_new = jnp.maximum(m_old, jnp.max(scores[c], axis=0, keepdims=True) + sel)
            alpha = _after(jnp.exp2(m_old - m_new), issued)
            p_t = jnp.exp2(scores[c] + (sel - m_new)).astype(jnp.bfloat16)
            acc_ref[c] = alpha * acc_ref[c] + _pv([vtb_ref[r, ks] for r in b_heads[c][1]], p_t)
            m_ref[c] = m_new
        raw_ref[0] = scores[n_b]
        raw_ref[1] = scores[n_b + 1]
        return 0

    lax.fori_loop(0, qi, past_tile, 0)

    for acc in (*[acc_ref[c] for c in range(n_b)], acc_m):
        o_tiles.append(acc[:HEAD_DIM] / acc[HEAD_DIM:HEAD_DIM + 1])
    row = 0
    for o_t in o_tiles:
        for h in range(o_t.shape[1] // TQ):
            o_ref[row:row + HEAD_DIM, :] = o_t[:, h * TQ:(h + 1) * TQ].astype(jnp.bfloat16)
            row += HEAD_DIM


def _attn_call(qka, vta, qkb, vtb, qm, cls, mem2, gmem, wmkv, layer, bias, cbias, batch, seq):
    nq = seq // TQ
    n_mem = mem2.shape[0] // batch
    per_layer = lambda b, q: (layer, 0, 0)
    per_tile = lambda b, q: (b * nq + q, 0)
    return pl.pallas_call(
        functools.partial(_attn_kernel, layer=layer),
        grid=(batch, nq),
        in_specs=[
            pl.BlockSpec((TQ, WIDTH_A), lambda b, q: (b * nq + q, 0)),
            pl.BlockSpec((seq, WIDTH_A), lambda b, q: (b, 1)),
            pl.BlockSpec((WIDTH_A, seq), lambda b, q: (b, 0)),
            pl.BlockSpec((TQ, WIDTH_B), lambda b, q: (b * nq + q, 0)),
            pl.BlockSpec((seq, WIDTH_B), lambda b, q: (b, 1)),
            pl.BlockSpec((WIDTH_B, seq), lambda b, q: (b, 0)),
            pl.BlockSpec((TQ, WIDTH_M), lambda b, q: (b * nq + q, 0)),
            pl.BlockSpec((3, seq, PAIR), lambda b, q: (0, b, 0)),
            pl.BlockSpec((n_mem, D_MODEL), lambda b, q: (b, 0)),
            pl.BlockSpec(gmem.shape, lambda b, q: (0, 0)),
            pl.BlockSpec((None, D_MODEL, 2 * WIDTH_M), per_layer),
            pl.BlockSpec(bias.shape, lambda b, q: (0, 0, 0)),
            pl.BlockSpec(cbias.shape, lambda b, q: (0, 0)),
        ],
        out_specs=[pl.BlockSpec((WIDTH_O, TQ), lambda b, q: (0, b * nq + q)),
                   pl.BlockSpec((LSE_ROWS, TQ), lambda b, q: (0, b * nq + q)),
                   pl.BlockSpec((TQ, PAIR), per_tile), pl.BlockSpec((TQ, PAIR), per_tile)],
        out_shape=[jax.ShapeDtypeStruct((WIDTH_O, batch * seq), jnp.bfloat16),
                   jax.ShapeDtypeStruct((LSE_ROWS, batch * seq), jnp.float32),
                   jax.ShapeDtypeStruct((batch * seq, PAIR), jnp.float32),
                   jax.ShapeDtypeStruct((batch * seq, PAIR), jnp.float32)],
        scratch_shapes=[
            pltpu.VMEM((N_BLOCKS, WIDTH_B), jnp.float32),
            pltpu.VMEM((n_mem, WIDTH_M), jnp.bfloat16),
            pltpu.VMEM((WIDTH_M, n_mem), jnp.bfloat16),
            pltpu.VMEM((N_HEADS_B // 2, N_BLOCKS, 2 * TQ), jnp.float32),
            pltpu.VMEM((N_HEADS_B // 2, 1, 2 * TQ), jnp.float32),
            pltpu.VMEM((N_HEADS_B // 2, HEAD_DIM + SUM_ROWS, 2 * TQ), jnp.float32),
            pltpu.VMEM((2, TQ, 2 * TQ), jnp.float32),
        ],
        compiler_params=pltpu.CompilerParams(dimension_semantics=("parallel", "arbitrary"),
                                             vmem_limit_bytes=VMEM_LIMIT),
        name="attn",
    )(qka, qka, vta, qkb, qkb, vtb, qm, cls, mem2, gmem, wmkv, bias, cbias)


def _mix_kernel(x_ref, o_ref, lse_ref, ocls_ref, lsecls_ref, g_ref, wg_ref, wpa_ref, wpb_ref, wpm_ref, wo_ref,
                out_ref, ocn_ref, lcn_ref, *, layer):
    tm = x_ref.shape[0]
    per_class = tm // CLASS_STRIDE
    first = (pl.program_id(0) % (SEQ // tm)) * per_class
    for r in range(CLASS_STRIDE):
        src = pl.ds(pl.multiple_of(r * CLASS_LEN + first, per_class), per_class)
        dst = pl.ds(r, per_class, stride=CLASS_STRIDE)
        ocn_ref[dst, :] = ocls_ref[src, :]
        lcn_ref[dst, :] = lsecls_ref[src, :]
    lse_t = lse_ref[...]
    per_head = LSE_ROWS // HEADS_PER_DIL_GROUP
    lse_tiled = jnp.concatenate([jnp.broadcast_to(lse_t[h * per_head:h * per_head + 1], (HEAD_DIM, tm))
                                 for h in range(HEADS_PER_DIL_GROUP)], axis=0)
    w_cls = 1.0 / (1.0 + jnp.exp2(lse_tiled - lcn_ref[...].T))
    o_tiled = o_ref[:WIDTH_A_OUT, :].astype(jnp.float32)
    o_a = (o_tiled + w_cls * (ocn_ref[...].T - o_tiled)).astype(jnp.bfloat16)
    x = x_ref[...]
    h = _rms(x, g_ref[layer:layer + 1, :]).astype(jnp.bfloat16)
    y = None
    col = 0
    for i, wp_ref in enumerate((wpa_ref, wpb_ref, wpm_ref)):
        width = wp_ref.shape[0]
        gate = jax.nn.sigmoid(jnp.dot(h, wg_ref[0, :, i * D_MODEL:(i + 1) * D_MODEL].astype(jnp.bfloat16),
                                      preferred_element_type=jnp.float32))
        o_i = o_a if i == 0 else o_ref[col:col + width, :]
        branch = gate * lax.dot_general(o_i, wp_ref[...].astype(jnp.bfloat16), _TN_PLAIN,
                                        preferred_element_type=jnp.float32)
        y = branch if y is None else y + branch
        col += width
    out_ref[...] = x + jnp.dot(y.astype(jnp.bfloat16), wo_ref[...].astype(jnp.bfloat16),
                               preferred_element_type=jnp.float32)


def _mix_call(x2, o, lse, ocls, lsecls, g, w_in, wpa, wpb, wpm, wo, layer):
    t = x2.shape[0]
    tm = TM_MIX
    per_layer = lambda i: (layer, 0, 0)
    return pl.pallas_call(
        functools.partial(_mix_kernel, layer=layer),
        grid=(t // tm,),
        in_specs=[
            pl.BlockSpec((tm, D_MODEL), lambda i: (i, 0)),
            pl.BlockSpec((WIDTH_O, tm), lambda i: (0, i)),
            pl.BlockSpec((LSE_ROWS, tm), lambda i: (0, i)),
            pl.BlockSpec((SEQ, PAIR), lambda i: (i // (SEQ // tm), 0)),
            pl.BlockSpec((SEQ, PAIR), lambda i: (i // (SEQ // tm), 0)),
            pl.BlockSpec(g.shape, lambda i: (0, 0)),
            pl.BlockSpec((pl.Element(1), pl.Element(D_MODEL), pl.Element(GATE_COLS)), lambda i: (layer, 0, COL_GATES),
                         pipeline_mode=pl.Buffered(1)),
            pl.BlockSpec((None,) + wpa.shape[1:], per_layer, pipeline_mode=pl.Buffered(1)),
            pl.BlockSpec((None,) + wpb.shape[1:], per_layer, pipeline_mode=pl.Buffered(1)),
            pl.BlockSpec((None,) + wpm.shape[1:], per_layer, pipeline_mode=pl.Buffered(1)),
            pl.BlockSpec((None,) + wo.shape[1:], per_layer, pipeline_mode=pl.Buffered(1)),
        ],
        out_specs=pl.BlockSpec((tm, D_MODEL), lambda i: (i, 0)),
        out_shape=jax.ShapeDtypeStruct((t, D_MODEL), jnp.float32),
        scratch_shapes=[pltpu.VMEM((tm, PAIR), jnp.float32), pltpu.VMEM((tm, PAIR), jnp.float32)],
        compiler_params=pltpu.CompilerParams(dimension_semantics=("parallel",), vmem_limit_bytes=VMEM_LIMIT),
        name="gated_mix",
    )(x2, o, lse, ocls, lsecls, g, w_in, wpa, wpb, wpm, wo)


def _mlp_kernel(x_ref, g_ref, wup_ref, wdown_ref, gfin_ref, out_ref, *, layer, final_norm):
    x = x_ref[...]
    hm = _rms(x, g_ref[layer:layer + 1, :]).astype(jnp.bfloat16)
    acc = x
    for c in range(D_FF // FF_CHUNK):
        u = jnp.dot(hm, wup_ref[:, c * FF_CHUNK:(c + 1) * FF_CHUNK].astype(jnp.bfloat16),
                    preferred_element_type=jnp.float32)
        u = jnp.square(jnp.maximum(u, 0.0)).astype(jnp.bfloat16)
        acc = acc + jnp.dot(u, wdown_ref[c * FF_CHUNK:(c + 1) * FF_CHUNK, :].astype(jnp.bfloat16),
                            preferred_element_type=jnp.float32)
    out_ref[...] = _rms(acc, gfin_ref[...]) if final_norm else acc


def _mlp_call(x2, g, wup, wdown, gfin, layer, final_norm):
    t = x2.shape[0]
    tm = TM_MLP
    const = lambda i: (0, 0)
    per_layer = lambda i: (layer, 0, 0)
    return pl.pallas_call(
        functools.partial(_mlp_kernel, layer=layer, final_norm=final_norm),
        grid=(t // tm,),
        in_specs=[
            pl.BlockSpec((tm, D_MODEL), lambda i: (i, 0)),
            pl.BlockSpec(g.shape, const),
            pl.BlockSpec((None,) + wup.shape[1:], per_layer, pipeline_mode=pl.Buffered(1)),
            pl.BlockSpec((None,) + wdown.shape[1:], per_layer, pipeline_mode=pl.Buffered(1)),
            pl.BlockSpec((1, D_MODEL), const),
        ],
        out_specs=pl.BlockSpec((tm, D_MODEL), lambda i: (i, 0)),
        out_shape=jax.ShapeDtypeStruct((t, D_MODEL), jnp.float32),
        compiler_params=pltpu.CompilerParams(dimension_semantics=("parallel",), vmem_limit_bytes=VMEM_LIMIT),
        name="mlp",
    )(x2, g, wup, wdown, gfin)


def _rope_tables(seq):
    half = ROT_DIM // 2
    inv_freq = (1.0 / (np.float32(ROPE_THETA) ** (np.arange(0, ROT_DIM, 2, dtype=np.float32) / np.float32(ROT_DIM))))
    ang = np.arange(seq, dtype=np.float32)[:, None] * inv_freq.astype(np.float32)[None, :]
    cos, sin = np.cos(ang).astype(np.float32), np.sin(ang).astype(np.float32)
    ones = np.ones((seq, HEAD_DIM - ROT_DIM), np.float32)
    zeros_half = np.zeros((seq, half), np.float32)
    zeros_rest = np.zeros((seq, HEAD_DIM - ROT_DIM), np.float32)
    c_head = np.concatenate([cos, cos, ones], axis=1)
    lo_head = np.concatenate([-sin, zeros_half, zeros_rest], axis=1)
    hi_head = np.concatenate([zeros_half, sin, zeros_rest], axis=1)
    rep = LANES // HEAD_DIM
    return tuple(jnp.asarray(np.tile(t, (1, rep))) for t in (c_head, lo_head, hi_head))


def kernel(x, mem, norm_mix, w_in, w_proj_a, w_proj_b, w_proj_m, w_out, norm_mem, w_mem_kv, norm_mlp, w_up,
           w_down, norm_final):
    batch, seq, d = x.shape
    assert d == D_MODEL and seq == SEQ and seq % TM_QKV == 0
    depth = w_in.shape[0]
    x2 = x.reshape(batch * seq, d)
    mem2 = mem.reshape(batch * mem.shape[1], d)
    rope_c, rope_s1, rope_s2 = _rope_tables(seq)
    bias = jnp.asarray(_BIAS_NP)
    cbias = jnp.asarray(_CLASS_BIAS_NP)
    for l in range(depth):
        qka, vta, qkb, vtb, qm, cls = _qkv_call(x2, norm_mix, w_in, l, rope_c, rope_s1, rope_s2, batch, seq)
        o, lse, ocls, lsecls = _attn_call(qka, vta, qkb, vtb, qm, cls, mem2, norm_mem, w_mem_kv, l, bias, cbias,
                                          batch, seq)
        x2 = _mix_call(x2, o, lse, ocls, lsecls, norm_mix, w_in, w_proj_a, w_proj_b, w_proj_m, w_out, l)
        x2 = _mlp_call(x2, norm_mlp, w_up, w_down, norm_final.reshape(1, d), l, final_norm=(l == depth - 1))
    return x2.reshape(batch, seq, d)
```

```python
import functools

import jax
import jax.numpy as jnp
import numpy as np
from jax import lax
from jax.experimental import pallas as pl
from jax.experimental.pallas import tpu as pltpu

D_MODEL = 1024
SEQ = 2048
HEAD_DIM = 64
ROT_DIM = HEAD_DIM // 4
ROPE_THETA = 500000.0
DIL_GROUPS = ((128, 1), (512, 4), (2048, 16))
HEADS_PER_DIL_GROUP = 2
N_HEADS_A = len(DIL_GROUPS) * HEADS_PER_DIL_GROUP
N_HEADS_B = 6
N_HEADS_M = 4
MOBA_BLOCK = 256
MOBA_TOPK = 3
N_BLOCKS = SEQ // MOBA_BLOCK
D_FF = 4 * D_MODEL
WIDTH_A = N_HEADS_A * HEAD_DIM
WIDTH_A_OUT = HEADS_PER_DIL_GROUP * HEAD_DIM
WIDTH_B = N_HEADS_B * HEAD_DIM
WIDTH_M = N_HEADS_M * HEAD_DIM
WIDTH_O = WIDTH_A_OUT + WIDTH_B + WIDTH_M
RMS_EPS = 1e-6
NEG_INF = -1e30
Q_SCALE = HEAD_DIM ** -0.5
LOG2_E = 1.4426950408889634

LANES = 128
LSE_ROWS = 8
PAIR = 2 * HEAD_DIM
SUM_ROWS = 16
TQ = MOBA_BLOCK
CLASS_GROUP = len(DIL_GROUPS) - 1
CLASS_STRIDE = DIL_GROUPS[CLASS_GROUP][1]
CLASS_LEN = SEQ // CLASS_STRIDE
CLASSES_PER_TILE = TQ // CLASS_LEN
TM_QKV = 1024
TM_MIX = 1024
TM_MLP = 1024
FF_CHUNK = 1024
VMEM_LIMIT = 60 * 1024 * 1024

_B_ALLNEG = 0
_B_CAUSAL = 1


TILE_GROUPS = DIL_GROUPS[:CLASS_GROUP]
assert DIL_GROUPS[CLASS_GROUP][0] >= SEQ and TQ % CLASS_LEN == 0


def _dil_tile_offsets():
    return tuple(tuple(range(min((w + TQ - 1) // TQ, N_BLOCKS - 1) + 1)) for w, _ in TILE_GROUPS)


def _build_bias_tiles():
    c = np.arange(TQ)[:, None]
    r = np.arange(TQ)[None, :]
    tiles = [np.zeros((TQ, TQ), bool), (r - c) >= 0]
    ids = []
    key_rows = []
    for (w, d), offs in zip(TILE_GROUPS, _dil_tile_offsets()):
        per_off = []
        for o in offs:
            diff = o * TQ + r - c
            per_off.append((diff >= 0) & (diff <= w) & (diff % d == 0))
        uniq, gid = [], []
        for t in per_off:
            for k, u in enumerate(uniq):
                if np.array_equal(t, u):
                    gid.append(k)
                    break
            else:
                uniq.append(t)
                gid.append(len(uniq) - 1)
        ids.append(tuple(len(tiles) + k for k in gid))
        tiles.extend(uniq)
        used = [np.flatnonzero(t.any(axis=1)) for t in per_off]
        key_rows.append(tuple((int(u.min()) // LANES * LANES, -(-(int(u.max()) + 1) // LANES) * LANES) for u in used))
    bias = np.where(np.stack(tiles), 0.0, NEG_INF).astype(np.float32)
    return bias, tuple(ids), tuple(key_rows)


_BIAS_NP, _DIL_BIAS_IDS, _DIL_KEY_ROWS = _build_bias_tiles()
_pos = np.arange(CLASS_LEN)
_CLASS_BIAS_NP = np.where(_pos[None, :] >= _pos[:, None], 0.0, NEG_INF).astype(np.float32)

_NT = (((1,), (1,)), ((), ()))
_TN = (((0,), (1,)), ((), ()))
_TN_PLAIN = (((0,), (0,)), ((), ()))
COL_B = 3 * WIDTH_A
COL_M = COL_B + 3 * WIDTH_B
COL_GATES = COL_M + WIDTH_M
GATE_COLS = 3 * D_MODEL


def _rms(x, g):
    return x * lax.rsqrt(jnp.mean(x * x, axis=-1, keepdims=True) + RMS_EPS) * g


def _qkv_kernel(x_ref, g_ref, win_ref, c_ref, s1_ref, s2_ref, qka_ref, vta_ref, qkb_ref, vtb_ref, qm_ref, cls_ref,
                kmean_ref, *, layer):
    h = _rms(x_ref[...], g_ref[layer:layer + 1, :]).astype(jnp.bfloat16)
    cos = c_ref[...]
    sin_lo = s1_ref[...]
    sin_hi = s2_ref[...]
    for col0, qk_ref, vt_ref, width in ((0, qka_ref, vta_ref, WIDTH_A), (COL_B, qkb_ref, vtb_ref, WIDTH_B)):
        z = jnp.dot(h, win_ref[:, col0:col0 + 2 * width].astype(jnp.bfloat16),
                    preferred_element_type=jnp.float32)
        for blk in range(2 * width // LANES):
            zb = z[:, blk * LANES:(blk + 1) * LANES]
            rb = zb * cos + pltpu.roll(zb, LANES - ROT_DIM // 2, 1) * sin_lo + pltpu.roll(zb, ROT_DIM // 2, 1) * sin_hi
            if blk < width // LANES:
                rb = rb * (Q_SCALE * LOG2_E)
            qk_ref[:, blk * LANES:(blk + 1) * LANES] = rb.astype(jnp.bfloat16)
            if col0 == 0 and blk % (width // LANES) == CLASS_GROUP:
                part = blk // (width // LANES)
                cls_ref[part] = rb
            if col0 == COL_B and blk >= width // LANES:
                kcol = (blk - width // LANES) * LANES
                kmean_ref[0, :, kcol:kcol + LANES] = jnp.mean(rb.reshape(-1, MOBA_BLOCK, LANES), axis=1)
        vt = lax.dot_general(win_ref[:, col0 + 2 * width:col0 + 3 * width].astype(jnp.bfloat16), h, _TN,
                             preferred_element_type=jnp.float32)
        vt_ref[...] = vt.astype(jnp.bfloat16)
        if col0 == 0:
            cls_ref[2] = vt[CLASS_GROUP * PAIR:(CLASS_GROUP + 1) * PAIR, :].T
    qm = jnp.dot(h, win_ref[:, COL_M:COL_GATES].astype(jnp.bfloat16), preferred_element_type=jnp.float32) * (Q_SCALE * LOG2_E)
    qm_ref[...] = qm.astype(jnp.bfloat16)


def _qkv_call(x2, g, w_in, layer, rope_c, rope_s1, rope_s2, batch, seq):
    t = x2.shape[0]
    tm = TM_QKV
    nts = seq // tm
    per_layer = lambda i: (layer, 0, 0)
    return pl.pallas_call(
        functools.partial(_qkv_kernel, layer=layer),
        grid=(t // tm,),
        in_specs=[
            pl.BlockSpec((tm, D_MODEL), lambda i: (i, 0)),
            pl.BlockSpec(g.shape, lambda i: (0, 0)),
            pl.BlockSpec((None, D_MODEL, COL_GATES), per_layer),
            pl.BlockSpec((tm, LANES), lambda i: (i % nts, 0)),
            pl.BlockSpec((tm, LANES), lambda i: (i % nts, 0)),
            pl.BlockSpec((tm, LANES), lambda i: (i % nts, 0)),
        ],
        out_specs=[
            pl.BlockSpec((tm, 2 * WIDTH_A), lambda i: (i, 0)),
            pl.BlockSpec((WIDTH_A, tm), lambda i: (i // nts, i % nts)),
            pl.BlockSpec((tm, 2 * WIDTH_B), lambda i: (i, 0)),
            pl.BlockSpec((WIDTH_B, tm), lambda i: (i // nts, i % nts)),
            pl.BlockSpec((tm, WIDTH_M), lambda i: (i, 0)),
            pl.BlockSpec((3, tm, PAIR), lambda i: (0, i, 0)),
            pl.BlockSpec((1, tm // MOBA_BLOCK, WIDTH_B), lambda i: (i, 0, 0)),
        ],
        out_shape=[
            jax.ShapeDtypeStruct((t, 2 * WIDTH_A), jnp.bfloat16),
            jax.ShapeDtypeStruct((batch * WIDTH_A, seq), jnp.bfloat16),
            jax.ShapeDtypeStruct((t, 2 * WIDTH_B), jnp.bfloat16),
            jax.ShapeDtypeStruct((batch * WIDTH_B, seq), jnp.bfloat16),
            jax.ShapeDtypeStruct((t, WIDTH_M), jnp.bfloat16),
            jax.ShapeDtypeStruct((3, t, PAIR), jnp.float32),
            jax.ShapeDtypeStruct((t // tm, tm // MOBA_BLOCK, WIDTH_B), jnp.float32),
        ],
        compiler_params=pltpu.CompilerParams(dimension_semantics=("parallel",), vmem_limit_bytes=VMEM_LIMIT),
        name="qkv_proj",
    )(x2, g, w_in, rope_c, rope_s1, rope_s2)


def _pv(v_heads, p_bf):
    ones = jnp.ones((SUM_ROWS, p_bf.shape[0]), jnp.bfloat16)
    nq = p_bf.shape[1] // len(v_heads)
    return jnp.concatenate([jnp.dot(jnp.concatenate([v, ones], axis=0), p_bf[:, h * nq:(h + 1) * nq],
                                    preferred_element_type=jnp.float32)
                            for h, v in enumerate(v_heads)], axis=1)


def _after(x, *matmul_results):
    for r in matmul_results:
        bits = lax.bitcast_convert_type(r[-1:, :x.shape[1]], jnp.uint32)
        bits = lax.shift_right_logical(lax.shift_right_logical(bits, jnp.uint32(16)), jnp.uint32(16))
        x = x + lax.bitcast_convert_type(bits, jnp.float32)
    return x


def _scores_t(k_tile, q_heads):
    return lax.dot_general(k_tile, q_heads, _NT, preferred_element_type=jnp.float32)


def _stack_heads(q, n_heads):
    lane = lax.broadcasted_iota(jnp.int32, q.shape, 1)
    zero = jnp.zeros((), q.dtype)
    return jnp.concatenate([jnp.where((lane >= h * HEAD_DIM) & (lane < (h + 1) * HEAD_DIM), q, zero)
                            for h in range(n_heads)], axis=0)


def _head_rows(first_head, n_heads):
    return [slice((first_head + h) * HEAD_DIM, (first_head + h + 1) * HEAD_DIM) for h in range(n_heads)]


def _lanes_to_rows(row, n_heads):
    nq = row.shape[1] // n_heads
    return jnp.concatenate([jnp.broadcast_to(row[:, h * nq:(h + 1) * nq], (HEAD_DIM, nq)) for h in range(n_heads)],
                           axis=0)


def _heads_to_rows(o_t, n_heads):
    nq = o_t.shape[1] // n_heads
    return jnp.concatenate([o_t[:, h * nq:(h + 1) * nq] for h in range(n_heads)], axis=0)


def _attn_kernel(qa_ref, ka_ref, vta_ref, qb_ref, kb_ref, vtb_ref, qm_ref, cls_ref, kmean_ref, mem_ref, gmem_ref,
                 wmkv_ref, bias_ref, cbias_ref, o_ref, lse_ref, ocls_ref, lsecls_ref,
                 km_ref, vmt_ref, selb_ref, m_ref, acc_ref, raw_ref, *, layer):
    qi = pl.program_id(1)

    @pl.when(qi == 0)
    def _per_batch():
        mem_n = _rms(mem_ref[...], gmem_ref[layer:layer + 1, :]).astype(jnp.bfloat16)
        w_mkv = wmkv_ref[...].astype(jnp.bfloat16)
        km_ref[...] = jnp.dot(mem_n, w_mkv[:, :WIDTH_M], preferred_element_type=jnp.float32).astype(jnp.bfloat16)
        vmt_ref[...] = lax.dot_general(w_mkv[:, WIDTH_M:], mem_n, _TN,
                                       preferred_element_type=jnp.float32).astype(jnp.bfloat16)

    def keys_of(j):
        return pl.ds(pl.multiple_of(j * TQ, TQ), TQ)

    def both_heads(bias):
        return jnp.concatenate([bias, bias], axis=1)

    def single_tile(s_t, v_heads):
        m = jnp.max(s_t, axis=0, keepdims=True)
        return m, _pv(v_heads, jnp.exp2(s_t - m).astype(jnp.bfloat16))

    for c in range(CLASSES_PER_TILE):
        rows = pl.ds(qi * CLASSES_PER_TILE + c, CLASS_LEN, stride=CLASS_STRIDE)
        q_heads = _stack_heads(cls_ref[0, rows, :].astype(jnp.bfloat16), HEADS_PER_DIL_GROUP)
        k_cls = cls_ref[1, rows, :].astype(jnp.bfloat16)
        v_t = cls_ref[2, rows, :].T.astype(jnp.bfloat16)
        m, acc = single_tile(_scores_t(k_cls, q_heads) + both_heads(cbias_ref[...]),
                             [v_t[r] for r in _head_rows(0, HEADS_PER_DIL_GROUP)])
        l = acc[HEAD_DIM:HEAD_DIM + 1]
        out = slice(c * CLASS_LEN, (c + 1) * CLASS_LEN)
        ocls_ref[out, :] = _heads_to_rows(acc[:HEAD_DIM] / l, HEADS_PER_DIL_GROUP).T
        lsecls_ref[out, :] = _lanes_to_rows(m + jnp.log2(l), HEADS_PER_DIL_GROUP).T

    a_static = []
    for g, offs in enumerate(_dil_tile_offsets()):
        cols = slice(g * PAIR, (g + 1) * PAIR)
        rows = _head_rows(g * HEADS_PER_DIL_GROUP, HEADS_PER_DIL_GROUP)
        q_heads = _stack_heads(qa_ref[:, cols], HEADS_PER_DIL_GROUP)
        ids = _DIL_BIAS_IDS[g]
        for o in offs:
            lo, hi = _DIL_KEY_ROWS[g][o]
            ks = pl.ds(pl.multiple_of(jnp.maximum(qi - o, 0) * TQ + lo, LANES), hi - lo)
            bid = ids[o] if o == 0 else jnp.where(qi >= o, ids[o], _B_ALLNEG)
            a_static.append((ks, (bid, slice(lo, hi)), cols, rows, q_heads))

    blk = lax.broadcasted_iota(jnp.int32, (N_BLOCKS, 2 * TQ), 0)
    own = keys_of(qi)
    b_heads = []
    for p in range(N_HEADS_B // 2):
        cols = slice(p * PAIR, (p + 1) * PAIR)
        q_heads = _stack_heads(qb_ref[:, cols], 2)
        kmean = jnp.concatenate([kmean_ref[t, :, cols] for t in range(kmean_ref.shape[0])], axis=0)
        kmean_hi = kmean.astype(jnp.bfloat16)
        kmean_lo = (kmean - kmean_hi.astype(jnp.float32)).astype(jnp.bfloat16)
        both = _scores_t(jnp.concatenate([kmean_hi, kmean_lo], axis=0), q_heads)
        gate = both[:N_BLOCKS] + both[N_BLOCKS:]
        gate = jnp.where(blk < qi, gate, NEG_INF)
        beaten = jnp.zeros(gate.shape, jnp.float32)
        for j in range(N_BLOCKS):
            gj = gate[j:j + 1, :]
            wins_tie = jnp.where(gj >= gate, 1.0, 0.0)
            wins_strict = jnp.where(gj > gate, 1.0, 0.0)
            beaten = beaten + jnp.where(blk > j, wins_tie, wins_strict)
        selb_ref[p] = jnp.where((beaten < MOBA_TOPK) & (blk < qi), 0.0, NEG_INF)
        b_heads.append((cols, _head_rows(2 * p, 2), q_heads))
    n_b = len(b_heads)

    def a_scores(part):
        ks, _, cols, _, q_heads = part
        return _scores_t(ka_ref[ks, cols], q_heads)

    def b_scores(p, ks):
        cols, _, q_heads = b_heads[p]
        return _scores_t(kb_ref[ks, cols], q_heads)

    n_a = len(a_static)
    raw = a_scores(a_static[0])
    a_biased = []
    m_a = None
    for i in range(n_a):
        nxt = a_scores(a_static[i + 1]) if i + 1 < n_a else b_scores(0, own)
        s_t = raw + both_heads(bias_ref[a_static[i][1]])
        part_max = _after(jnp.max(s_t, axis=0, keepdims=True), nxt)
        m_a = part_max if m_a is None else jnp.maximum(m_a, part_max)
        a_biased.append(s_t)
        raw = nxt
    m_heads = _stack_heads(qm_ref[...], N_HEADS_M)
    ahead = [lambda: b_scores(1, own), lambda: b_scores(2, own), lambda: _scores_t(km_ref[...], m_heads),
             lambda: b_scores(0, keys_of(0)), lambda: b_scores(1, keys_of(0))]
    later = [raw]
    assert n_a >= len(ahead)
    neg_m = -m_a
    acc_a = None
    for i in range(n_a):
        ks, _, _, rows, _ = a_static[i]
        p_t = jnp.exp2(a_biased[i] + neg_m).astype(jnp.bfloat16)
        if i < len(ahead):
            later.append(ahead[i]())
            neg_m = _after(neg_m, later[-1])
        part_acc = _pv([vta_ref[r, ks] for r in rows], p_t)
        acc_a = part_acc if acc_a is None else acc_a + part_acc
    l_a = acc_a[HEAD_DIM:HEAD_DIM + 1]
    o_tiles = [acc_a[:HEAD_DIM] / l_a]
    lse_a = m_a + jnp.log2(l_a)
    lse_ref[...] = jnp.concatenate([jnp.broadcast_to(lse_a[:, h * TQ:(h + 1) * TQ], (LSE_ROWS // HEADS_PER_DIL_GROUP, TQ))
                                    for h in range(HEADS_PER_DIL_GROUP)], axis=0)

    for p, (_, rows, _) in enumerate(b_heads):
        m_ref[p], acc_ref[p] = single_tile(later[p] + both_heads(bias_ref[_B_CAUSAL]),
                                           [vtb_ref[r, own] for r in rows])
    _, acc_m = single_tile(later[3], [vmt_ref[r, :] for r in _head_rows(0, N_HEADS_M)])
    raw_ref[0] = later[4]
    raw_ref[1] = later[5]

    def past_tile(j, _):
        ks = keys_of(j)
        ks_next = keys_of(jnp.minimum(j + 1, qi - 1))
        scores = [raw_ref[0], raw_ref[1]]
        for c in range(n_b):
            issued = b_scores(c + 2, ks) if c + 2 < n_b else b_scores(c + 2 - n_b, ks_next)
            scores.append(issued)
            sel = selb_ref[c, pl.ds(j, 1), :]
            m_old = m_ref[c]
            m_new = jnp.maximum(m_old, jnp.max(scores[c], axis=0, keepdims=True) + sel)
            alpha = _after(jnp.exp2(m_old - m_new), issued)
            p_t = jnp.exp2(scores[c] + (sel - m_new)).astype(jnp.bfloat16)
            acc_ref[c] = alpha * acc_ref[c] + _pv([vtb_ref[r, ks] for r in b_heads[c][1]], p_t)
            m_ref[c] = m_new
        raw_ref[0] = scores[n_b]
        raw_ref[1] = scores[n_b + 1]
        return 0

    lax.fori_loop(0, qi, past_tile, 0)

    for acc in (*[acc_ref[c] for c in range(n_b)], acc_m):
        o_tiles.append(acc[:HEAD_DIM] / acc[HEAD_DIM:HEAD_DIM + 1])
    row = 0
    for o_t in o_tiles:
        for h in range(o_t.shape[1] // TQ):
            o_ref[row:row + HEAD_DIM, :] = o_t[:, h * TQ:(h + 1) * TQ].astype(jnp.bfloat16)
            row += HEAD_DIM


def _attn_call(qka, vta, qkb, vtb, qm, cls, kmean, mem2, gmem, wmkv, layer, bias, cbias, batch, seq):
    nq = seq // TQ
    n_mem = mem2.shape[0] // batch
    per_layer = lambda b, q: (layer, 0, 0)
    per_tile = lambda b, q: (b * nq + q, 0)
    return pl.pallas_call(
        functools.partial(_attn_kernel, layer=layer),
        grid=(batch, nq),
        in_specs=[
            pl.BlockSpec((TQ, WIDTH_A), lambda b, q: (b * nq + q, 0)),
            pl.BlockSpec((seq, WIDTH_A), lambda b, q: (b, 1)),
            pl.BlockSpec((WIDTH_A, seq), lambda b, q: (b, 0)),
            pl.BlockSpec((TQ, WIDTH_B), lambda b, q: (b * nq + q, 0)),
            pl.BlockSpec((seq, WIDTH_B), lambda b, q: (b, 1)),
            pl.BlockSpec((WIDTH_B, seq), lambda b, q: (b, 0)),
            pl.BlockSpec((TQ, WIDTH_M), lambda b, q: (b * nq + q, 0)),
            pl.BlockSpec((3, seq, PAIR), lambda b, q: (0, b, 0)),
            pl.BlockSpec((kmean.shape[0] // batch,) + kmean.shape[1:], lambda b, q: (b, 0, 0)),
            pl.BlockSpec((n_mem, D_MODEL), lambda b, q: (b, 0)),
            pl.BlockSpec(gmem.shape, lambda b, q: (0, 0)),
            pl.BlockSpec((None, D_MODEL, 2 * WIDTH_M), per_layer),
            pl.BlockSpec(bias.shape, lambda b, q: (0, 0, 0)),
            pl.BlockSpec(cbias.shape, lambda b, q: (0, 0)),
        ],
        out_specs=[pl.BlockSpec((WIDTH_O, TQ), lambda b, q: (0, b * nq + q)),
                   pl.BlockSpec((LSE_ROWS, TQ), lambda b, q: (0, b * nq + q)),
                   pl.BlockSpec((TQ, PAIR), per_tile), pl.BlockSpec((TQ, PAIR), per_tile)],
        out_shape=[jax.ShapeDtypeStruct((WIDTH_O, batch * seq), jnp.bfloat16),
                   jax.ShapeDtypeStruct((LSE_ROWS, batch * seq), jnp.float32),
                   jax.ShapeDtypeStruct((batch * seq, PAIR), jnp.float32),
                   jax.ShapeDtypeStruct((batch * seq, PAIR), jnp.float32)],
        scratch_shapes=[
            pltpu.VMEM((n_mem, WIDTH_M), jnp.bfloat16),
            pltpu.VMEM((WIDTH_M, n_mem), jnp.bfloat16),
            pltpu.VMEM((N_HEADS_B // 2, N_BLOCKS, 2 * TQ), jnp.float32),
            pltpu.VMEM((N_HEADS_B // 2, 1, 2 * TQ), jnp.float32),
            pltpu.VMEM((N_HEADS_B // 2, HEAD_DIM + SUM_ROWS, 2 * TQ), jnp.float32),
            pltpu.VMEM((2, TQ, 2 * TQ), jnp.float32),
        ],
        compiler_params=pltpu.CompilerParams(dimension_semantics=("parallel", "arbitrary"),
                                             vmem_limit_bytes=VMEM_LIMIT),
        name="attn",
    )(qka, qka, vta, qkb, qkb, vtb, qm, cls, kmean, mem2, gmem, wmkv, bias, cbias)


def _mix_kernel(x_ref, o_ref, lse_ref, ocls_ref, lsecls_ref, g_ref, wg_ref, wpa_ref, wpb_ref, wpm_ref, wo_ref,
                out_ref, ocn_ref, lcn_ref, *, layer):
    tm = x_ref.shape[0]
    per_class = tm // CLASS_STRIDE
    first = (pl.program_id(0) % (SEQ // tm)) * per_class
    for r in range(CLASS_STRIDE):
        src = pl.ds(pl.multiple_of(r * CLASS_LEN + first, per_class), per_class)
        dst = pl.ds(r, per_class, stride=CLASS_STRIDE)
        ocn_ref[dst, :] = ocls_ref[src, :]
        lcn_ref[dst, :] = lsecls_ref[src, :]
    lse_t = lse_ref[...]
    per_head = LSE_ROWS // HEADS_PER_DIL_GROUP
    lse_tiled = jnp.concatenate([jnp.broadcast_to(lse_t[h * per_head:h * per_head + 1], (HEAD_DIM, tm))
                                 for h in range(HEADS_PER_DIL_GROUP)], axis=0)
    w_cls = 1.0 / (1.0 + jnp.exp2(lse_tiled - lcn_ref[...].T))
    o_tiled = o_ref[:WIDTH_A_OUT, :].astype(jnp.float32)
    o_a = (o_tiled + w_cls * (ocn_ref[...].T - o_tiled)).astype(jnp.bfloat16)
    half = tm // 2
    halves = [slice(0, half), slice(half, tm)]
    xs = [x_ref[rows, :] for rows in halves]
    hs = [_rms(x, g_ref[layer:layer + 1, :]).astype(jnp.bfloat16) for x in xs]

    def gate_scores(s, i):
        return jnp.dot(hs[s], wg_ref[0, :, i * D_MODEL:(i + 1) * D_MODEL].astype(jnp.bfloat16),
                       preferred_element_type=jnp.float32)

    def wait_for(y, matmul_result):
        tail = y.shape[0] - LSE_ROWS
        return jnp.concatenate([y[:tail], y[tail:] + (_after(jnp.zeros((1, y.shape[1]), jnp.float32), matmul_result))],
                               axis=0)

    def gated(s, first_gate):
        y = None
        col = 0
        for i, wp_ref in enumerate((wpa_ref, wpb_ref, wpm_ref)):
            width = wp_ref.shape[0]
            gate = jax.nn.sigmoid(first_gate if i == 0 else gate_scores(s, i))
            o_i = o_a[:, halves[s]] if i == 0 else o_ref[col:col + width, halves[s]]
            branch = gate * lax.dot_general(o_i, wp_ref[...].astype(jnp.bfloat16), _TN_PLAIN,
                                            preferred_element_type=jnp.float32)
            y = branch if y is None else y + branch
            col += width
        return y

    def project(y):
        return jnp.dot(y.astype(jnp.bfloat16), wo_ref[...].astype(jnp.bfloat16), preferred_element_type=jnp.float32)

    g0_a = gate_scores(0, 0)
    g0_b = gate_scores(1, 0)
    y_a = wait_for(gated(0, g0_a), g0_b)
    out_a = project(y_a)
    y_b = wait_for(gated(1, g0_b), out_a)
    out_ref[halves[0], :] = xs[0] + out_a
    out_ref[halves[1], :] = xs[1] + project(y_b)


def _mix_call(x2, o, lse, ocls, lsecls, g, w_in, wpa, wpb, wpm, wo, layer):
    t = x2.shape[0]
    tm = TM_MIX
    per_layer = lambda i: (layer, 0, 0)
    return pl.pallas_call(
        functools.partial(_mix_kernel, layer=layer),
        grid=(t // tm,),
        in_specs=[
            pl.BlockSpec((tm, D_MODEL), lambda i: (i, 0)),
            pl.BlockSpec((WIDTH_O, tm), lambda i: (0, i)),
            pl.BlockSpec((LSE_ROWS, tm), lambda i: (0, i)),
            pl.BlockSpec((SEQ, PAIR), lambda i: (i // (SEQ // tm), 0)),
            pl.BlockSpec((SEQ, PAIR), lambda i: (i // (SEQ // tm), 0)),
            pl.BlockSpec(g.shape, lambda i: (0, 0)),
            pl.BlockSpec((pl.Element(1), pl.Element(D_MODEL), pl.Element(GATE_COLS)), lambda i: (layer, 0, COL_GATES),
                         pipeline_mode=pl.Buffered(1)),
            pl.BlockSpec((None,) + wpa.shape[1:], per_layer, pipeline_mode=pl.Buffered(1)),
            pl.BlockSpec((None,) + wpb.shape[1:], per_layer, pipeline_mode=pl.Buffered(1)),
            pl.BlockSpec((None,) + wpm.shape[1:], per_layer, pipeline_mode=pl.Buffered(1)),
            pl.BlockSpec((None,) + wo.shape[1:], per_layer, pipeline_mode=pl.Buffered(1)),
        ],
        out_specs=pl.BlockSpec((tm, D_MODEL), lambda i: (i, 0)),
        out_shape=jax.ShapeDtypeStruct((t, D_MODEL), jnp.float32),
        scratch_shapes=[pltpu.VMEM((tm, PAIR), jnp.float32), pltpu.VMEM((tm, PAIR), jnp.float32)],
        compiler_params=pltpu.CompilerParams(dimension_semantics=("parallel",), vmem_limit_bytes=VMEM_LIMIT),
        name="gated_mix",
    )(x2, o, lse, ocls, lsecls, g, w_in, wpa, wpb, wpm, wo)


def _mlp_kernel(x_ref, g_ref, wup_ref, wdown_ref, gfin_ref, out_ref, *, layer, final_norm):
    x = x_ref[...]
    hm = _rms(x, g_ref[layer:layer + 1, :]).astype(jnp.bfloat16)
    acc = x
    for c in range(D_FF // FF_CHUNK):
        u = jnp.dot(hm, wup_ref[:, c * FF_CHUNK:(c + 1) * FF_CHUNK].astype(jnp.bfloat16),
                    preferred_element_type=jnp.float32)
        u = jnp.square(jnp.maximum(u, 0.0)).astype(jnp.bfloat16)
        acc = acc + jnp.dot(u, wdown_ref[c * FF_CHUNK:(c + 1) * FF_CHUNK, :].astype(jnp.bfloat16),
                            preferred_element_type=jnp.float32)
    out_ref[...] = _rms(acc, gfin_ref[...]) if final_norm else acc


def _mlp_call(x2, g, wup, wdown, gfin, layer, final_norm):
    t = x2.shape[0]
    tm = TM_MLP
    const = lambda i: (0, 0)
    per_layer = lambda i: (layer, 0, 0)
    return pl.pallas_call(
        functools.partial(_mlp_kernel, layer=layer, final_norm=final_norm),
        grid=(t // tm,),
        in_specs=[
            pl.BlockSpec((tm, D_MODEL), lambda i: (i, 0)),
            pl.BlockSpec(g.shape, const),
            pl.BlockSpec((None,) + wup.shape[1:], per_layer, pipeline_mode=pl.Buffered(1)),
            pl.BlockSpec((None,) + wdown.shape[1:], per_layer, pipeline_mode=pl.Buffered(1)),
            pl.BlockSpec((1, D_MODEL), const),
        ],
        out_specs=pl.BlockSpec((tm, D_MODEL), lambda i: (i, 0)),
        out_shape=jax.ShapeDtypeStruct((t, D_MODEL), jnp.float32),
        compiler_params=pltpu.CompilerParams(dimension_semantics=("parallel",), vmem_limit_bytes=VMEM_LIMIT),
        name="mlp",
    )(x2, g, wup, wdown, gfin)


def _rope_tables(seq):
    half = ROT_DIM // 2
    inv_freq = (1.0 / (np.float32(ROPE_THETA) ** (np.arange(0, ROT_DIM, 2, dtype=np.float32) / np.float32(ROT_DIM))))
    ang = np.arange(seq, dtype=np.float32)[:, None] * inv_freq.astype(np.float32)[None, :]
    cos, sin = np.cos(ang).astype(np.float32), np.sin(ang).astype(np.float32)
    ones = np.ones((seq, HEAD_DIM - ROT_DIM), np.float32)
    zeros_half = np.zeros((seq, half), np.float32)
    zeros_rest = np.zeros((seq, HEAD_DIM - ROT_DIM), np.float32)
    c_head = np.concatenate([cos, cos, ones], axis=1)
    lo_head = np.concatenate([-sin, zeros_half, zeros_rest], axis=1)
    hi_head = np.concatenate([zeros_half, sin, zeros_rest], axis=1)
    rep = LANES // HEAD_DIM
    return tuple(jnp.asarray(np.tile(t, (1, rep))) for t in (c_head, lo_head, hi_head))


def kernel(x, mem, norm_mix, w_in, w_proj_a, w_proj_b, w_proj_m, w_out, norm_mem, w_mem_kv, norm_mlp, w_up,
           w_down, norm_final):
    batch, seq, d = x.shape
    assert d == D_MODEL and seq == SEQ and seq % TM_QKV == 0
    depth = w_in.shape[0]
    x2 = x.reshape(batch * seq, d)
    mem2 = mem.reshape(batch * mem.shape[1], d)
    rope_c, rope_s1, rope_s2 = _rope_tables(seq)
    bias = jnp.asarray(_BIAS_NP)
    cbias = jnp.asarray(_CLASS_BIAS_NP)
    for l in range(depth):
        qka, vta, qkb, vtb, qm, cls, kmean = _qkv_call(x2, norm_mix, w_in, l, rope_c, rope_s1, rope_s2, batch, seq)
        o, lse, ocls, lsecls = _attn_call(qka, vta, qkb, vtb, qm, cls, kmean, mem2, norm_mem,
                                          w_mem_kv, l, bias, cbias, batch, seq)
        x2 = _mix_call(x2, o, lse, ocls, lsecls, norm_mix, w_in, w_proj_a, w_proj_b, w_proj_m, w_out, l)
        x2 = _mlp_call(x2, norm_mlp, w_up, w_down, norm_final.reshape(1, d), l, final_norm=(l == depth - 1))
    return x2.reshape(batch, seq, d)
```

```python
import functools

import jax
import jax.numpy as jnp
import numpy as np
from jax import lax
from jax.experimental import pallas as pl
from jax.experimental.pallas import tpu as pltpu

D_MODEL = 1024
SEQ = 2048
HEAD_DIM = 64
ROT_DIM = HEAD_DIM // 4
ROPE_THETA = 500000.0
DIL_GROUPS = ((128, 1), (512, 4), (2048, 16))
HEADS_PER_DIL_GROUP = 2
N_HEADS_A = len(DIL_GROUPS) * HEADS_PER_DIL_GROUP
N_HEADS_B = 6
N_HEADS_M = 4
MOBA_BLOCK = 256
MOBA_TOPK = 3
N_BLOCKS = SEQ // MOBA_BLOCK
D_FF = 4 * D_MODEL
WIDTH_A = N_HEADS_A * HEAD_DIM
WIDTH_A_OUT = HEADS_PER_DIL_GROUP * HEAD_DIM
WIDTH_B = N_HEADS_B * HEAD_DIM
WIDTH_M = N_HEADS_M * HEAD_DIM
WIDTH_O = WIDTH_A_OUT + WIDTH_B + WIDTH_M
RMS_EPS = 1e-6
NEG_INF = -1e30
Q_SCALE = HEAD_DIM ** -0.5
LOG2_E = 1.4426950408889634

LANES = 128
LSE_ROWS = 8
PAIR = 2 * HEAD_DIM
SUM_ROWS = 16
TQ = MOBA_BLOCK
CLASS_GROUP = len(DIL_GROUPS) - 1
CLASS_STRIDE = DIL_GROUPS[CLASS_GROUP][1]
CLASS_LEN = SEQ // CLASS_STRIDE
CLASSES_PER_TILE = TQ // CLASS_LEN
TM_QKV = 1024
TM_MIX = 1024
TM_MLP = 1024
FF_CHUNK = 1024
VMEM_LIMIT = 60 * 1024 * 1024

_B_ALLNEG = 0
_B_CAUSAL = 1


TILE_GROUPS = DIL_GROUPS[:CLASS_GROUP]
assert DIL_GROUPS[CLASS_GROUP][0] >= SEQ and TQ % CLASS_LEN == 0


def _dil_tile_offsets():
    return tuple(tuple(range(min((w + TQ - 1) // TQ, N_BLOCKS - 1) + 1)) for w, _ in TILE_GROUPS)


def _build_bias_tiles():
    c = np.arange(TQ)[:, None]
    r = np.arange(TQ)[None, :]
    tiles = [np.zeros((TQ, TQ), bool), (r - c) >= 0]
    ids = []
    key_rows = []
    for (w, d), offs in zip(TILE_GROUPS, _dil_tile_offsets()):
        per_off = []
        for o in offs:
            diff = o * TQ + r - c
            per_off.append((diff >= 0) & (diff <= w) & (diff % d == 0))
        uniq, gid = [], []
        for t in per_off:
            for k, u in enumerate(uniq):
                if np.array_equal(t, u):
                    gid.append(k)
                    break
            else:
                uniq.append(t)
                gid.append(len(uniq) - 1)
        ids.append(tuple(len(tiles) + k for k in gid))
        tiles.extend(uniq)
        used = [np.flatnonzero(t.any(axis=1)) for t in per_off]
        key_rows.append(tuple((int(u.min()) // LANES * LANES, -(-(int(u.max()) + 1) // LANES) * LANES) for u in used))
    bias = np.where(np.stack(tiles), 0.0, NEG_INF).astype(np.float32)
    return bias, tuple(ids), tuple(key_rows)


_BIAS_NP, _DIL_BIAS_IDS, _DIL_KEY_ROWS = _build_bias_tiles()
_pos = np.arange(CLASS_LEN)
_CLASS_BIAS_NP = np.where(_pos[None, :] >= _pos[:, None], 0.0, NEG_INF).astype(np.float32)

_NT = (((1,), (1,)), ((), ()))
_TN = (((0,), (1,)), ((), ()))
_TN_PLAIN = (((0,), (0,)), ((), ()))
COL_B = 3 * WIDTH_A
COL_M = COL_B + 3 * WIDTH_B
COL_GATES = COL_M + WIDTH_M
GATE_COLS = 3 * D_MODEL


def _rms(x, g):
    return x * lax.rsqrt(jnp.mean(x * x, axis=-1, keepdims=True) + RMS_EPS) * g


def _qkv_kernel(x_ref, g_ref, win_ref, c_ref, s1_ref, s2_ref, qka_ref, vta_ref, qkb_ref, vtb_ref, qm_ref, cls_ref,
                kmean_ref, *, layer):
    tm = x_ref.shape[0]
    halves = (slice(0, tm // 2), slice(tm // 2, tm))
    normed = [_rms(x_ref[rows, :], g_ref[layer:layer + 1, :]) for rows in halves]
    for s, rows in enumerate(halves):
        h = normed[s].astype(jnp.bfloat16)
        cos = c_ref[rows, :]
        sin_lo = s1_ref[rows, :]
        sin_hi = s2_ref[rows, :]
        for col0, qk_ref, vt_ref, width in ((0, qka_ref, vta_ref, WIDTH_A), (COL_B, qkb_ref, vtb_ref, WIDTH_B)):
            z = jnp.dot(h, win_ref[:, col0:col0 + 2 * width].astype(jnp.bfloat16),
                        preferred_element_type=jnp.float32)
            for blk in range(2 * width // LANES):
                zb = z[:, blk * LANES:(blk + 1) * LANES]
                rb = (zb * cos + pltpu.roll(zb, LANES - ROT_DIM // 2, 1) * sin_lo
                      + pltpu.roll(zb, ROT_DIM // 2, 1) * sin_hi)
                if blk < width // LANES:
                    rb = rb * (Q_SCALE * LOG2_E)
                qk_ref[rows, blk * LANES:(blk + 1) * LANES] = rb.astype(jnp.bfloat16)
                if col0 == 0 and blk % (width // LANES) == CLASS_GROUP:
                    part = blk // (width // LANES)
                    cls_ref[part, rows, :] = rb
                if col0 == COL_B and blk >= width // LANES:
                    kcol = (blk - width // LANES) * LANES
                    blocks = slice(rows.start // MOBA_BLOCK, rows.stop // MOBA_BLOCK)
                    kmean_ref[0, blocks, kcol:kcol + LANES] = jnp.mean(rb.reshape(-1, MOBA_BLOCK, LANES), axis=1)
            vt = lax.dot_general(win_ref[:, col0 + 2 * width:col0 + 3 * width].astype(jnp.bfloat16), h, _TN,
                                 preferred_element_type=jnp.float32)
            vt_ref[:, rows] = vt.astype(jnp.bfloat16)
            if col0 == 0:
                cls_ref[2, rows, :] = vt[CLASS_GROUP * PAIR:(CLASS_GROUP + 1) * PAIR, :].T
        qm = jnp.dot(h, win_ref[:, COL_M:COL_GATES].astype(jnp.bfloat16),
                     preferred_element_type=jnp.float32) * (Q_SCALE * LOG2_E)
        if s + 1 < len(halves):
            tail = qm.shape[0] - LSE_ROWS
            qm = jnp.concatenate([qm[:tail], qm[tail:] + _after(jnp.zeros((1, qm.shape[1]), jnp.float32),
                                                               normed[s + 1])], axis=0)
        qm_ref[rows, :] = qm.astype(jnp.bfloat16)


def _qkv_call(x2, g, w_in, layer, rope_c, rope_s1, rope_s2, batch, seq):
    t = x2.shape[0]
    tm = TM_QKV
    nts = seq // tm
    per_layer = lambda i: (layer, 0, 0)
    return pl.pallas_call(
        functools.partial(_qkv_kernel, layer=layer),
        grid=(t // tm,),
        in_specs=[
            pl.BlockSpec((tm, D_MODEL), lambda i: (i, 0)),
            pl.BlockSpec(g.shape, lambda i: (0, 0)),
            pl.BlockSpec((None, D_MODEL, COL_GATES), per_layer),
            pl.BlockSpec((tm, LANES), lambda i: (i % nts, 0)),
            pl.BlockSpec((tm, LANES), lambda i: (i % nts, 0)),
            pl.BlockSpec((tm, LANES), lambda i: (i % nts, 0)),
        ],
        out_specs=[
            pl.BlockSpec((tm, 2 * WIDTH_A), lambda i: (i, 0)),
            pl.BlockSpec((WIDTH_A, tm), lambda i: (i // nts, i % nts)),
            pl.BlockSpec((tm, 2 * WIDTH_B), lambda i: (i, 0)),
            pl.BlockSpec((WIDTH_B, tm), lambda i: (i // nts, i % nts)),
            pl.BlockSpec((tm, WIDTH_M), lambda i: (i, 0)),
            pl.BlockSpec((3, tm, PAIR), lambda i: (0, i, 0)),
            pl.BlockSpec((1, tm // MOBA_BLOCK, WIDTH_B), lambda i: (i, 0, 0)),
        ],
        out_shape=[
            jax.ShapeDtypeStruct((t, 2 * WIDTH_A), jnp.bfloat16),
            jax.ShapeDtypeStruct((batch * WIDTH_A, seq), jnp.bfloat16),
            jax.ShapeDtypeStruct((t, 2 * WIDTH_B), jnp.bfloat16),
            jax.ShapeDtypeStruct((batch * WIDTH_B, seq), jnp.bfloat16),
            jax.ShapeDtypeStruct((t, WIDTH_M), jnp.bfloat16),
            jax.ShapeDtypeStruct((3, t, PAIR), jnp.float32),
            jax.ShapeDtypeStruct((t // tm, tm // MOBA_BLOCK, WIDTH_B), jnp.float32),
        ],
        compiler_params=pltpu.CompilerParams(dimension_semantics=("parallel",), vmem_limit_bytes=VMEM_LIMIT),
        name="qkv_proj",
    )(x2, g, w_in, rope_c, rope_s1, rope_s2)


def _pv(v_heads, p_bf):
    ones = jnp.ones((SUM_ROWS, p_bf.shape[0]), jnp.bfloat16)
    nq = p_bf.shape[1] // len(v_heads)
    return jnp.concatenate([jnp.dot(jnp.concatenate([v, ones], axis=0), p_bf[:, h * nq:(h + 1) * nq],
                                    preferred_element_type=jnp.float32)
                            for h, v in enumerate(v_heads)], axis=1)


def _after(x, *matmul_results):
    for r in matmul_results:
        bits = lax.bitcast_convert_type(r[-1:, :x.shape[1]], jnp.uint32)
        bits = lax.shift_right_logical(lax.shift_right_logical(bits, jnp.uint32(16)), jnp.uint32(16))
        x = x + lax.bitcast_convert_type(bits, jnp.float32)
    return x


def _scores_t(k_tile, q_heads):
    return lax.dot_general(k_tile, q_heads, _NT, preferred_element_type=jnp.float32)


def _stack_heads(q, n_heads):
    lane = lax.broadcasted_iota(jnp.int32, q.shape, 1)
    zero = jnp.zeros((), q.dtype)
    return jnp.concatenate([jnp.where((lane >= h * HEAD_DIM) & (lane < (h + 1) * HEAD_DIM), q, zero)
                            for h in range(n_heads)], axis=0)


def _head_rows(first_head, n_heads):
    return [slice((first_head + h) * HEAD_DIM, (first_head + h + 1) * HEAD_DIM) for h in range(n_heads)]


def _lanes_to_rows(row, n_heads):
    nq = row.shape[1] // n_heads
    return jnp.concatenate([jnp.broadcast_to(row[:, h * nq:(h + 1) * nq], (HEAD_DIM, nq)) for h in range(n_heads)],
                           axis=0)


def _heads_to_rows(o_t, n_heads):
    nq = o_t.shape[1] // n_heads
    return jnp.concatenate([o_t[:, h * nq:(h + 1) * nq] for h in range(n_heads)], axis=0)


def _attn_kernel(qa_ref, ka_ref, vta_ref, qb_ref, kb_ref, vtb_ref, qm_ref, cls_ref, kmean_ref, mem_ref, gmem_ref,
                 wmkv_ref, bias_ref, cbias_ref, o_ref, lse_ref, ocls_ref, lsecls_ref,
                 km_ref, vmt_ref, selb_ref, m_ref, acc_ref, raw_ref, *, layer):
    qi = pl.program_id(1)

    @pl.when(qi == 0)
    def _per_batch():
        mem_n = _rms(mem_ref[...], gmem_ref[layer:layer + 1, :]).astype(jnp.bfloat16)
        w_mkv = wmkv_ref[...].astype(jnp.bfloat16)
        km_ref[...] = jnp.dot(mem_n, w_mkv[:, :WIDTH_M], preferred_element_type=jnp.float32).astype(jnp.bfloat16)
        vmt_ref[...] = lax.dot_general(w_mkv[:, WIDTH_M:], mem_n, _TN,
                                       preferred_element_type=jnp.float32).astype(jnp.bfloat16)

    def keys_of(j):
        return pl.ds(pl.multiple_of(j * TQ, TQ), TQ)

    def both_heads(bias):
        return jnp.concatenate([bias, bias], axis=1)

    def single_tile(s_t, v_heads):
        m = jnp.max(s_t, axis=0, keepdims=True)
        return m, _pv(v_heads, jnp.exp2(s_t - m).astype(jnp.bfloat16))

    for c in range(CLASSES_PER_TILE):
        rows = pl.ds(qi * CLASSES_PER_TILE + c, CLASS_LEN, stride=CLASS_STRIDE)
        q_heads = _stack_heads(cls_ref[0, rows, :].astype(jnp.bfloat16), HEADS_PER_DIL_GROUP)
        k_cls = cls_ref[1, rows, :].astype(jnp.bfloat16)
        v_t = cls_ref[2, rows, :].T.astype(jnp.bfloat16)
        m, acc = single_tile(_scores_t(k_cls, q_heads) + both_heads(cbias_ref[...]),
                             [v_t[r] for r in _head_rows(0, HEADS_PER_DIL_GROUP)])
        l = acc[HEAD_DIM:HEAD_DIM + 1]
        out = slice(c * CLASS_LEN, (c + 1) * CLASS_LEN)
        ocls_ref[out, :] = _heads_to_rows(acc[:HEAD_DIM] / l, HEADS_PER_DIL_GROUP).T
        lsecls_ref[out, :] = _lanes_to_rows(m + jnp.log2(l), HEADS_PER_DIL_GROUP).T

    a_static = []
    for g, offs in enumerate(_dil_tile_offsets()):
        cols = slice(g * PAIR, (g + 1) * PAIR)
        rows = _head_rows(g * HEADS_PER_DIL_GROUP, HEADS_PER_DIL_GROUP)
        q_heads = _stack_heads(qa_ref[:, cols], HEADS_PER_DIL_GROUP)
        ids = _DIL_BIAS_IDS[g]
        for o in offs:
            lo, hi = _DIL_KEY_ROWS[g][o]
            ks = pl.ds(pl.multiple_of(jnp.maximum(qi - o, 0) * TQ + lo, LANES), hi - lo)
            bid = ids[o] if o == 0 else jnp.where(qi >= o, ids[o], _B_ALLNEG)
            a_static.append((ks, (bid, slice(lo, hi)), cols, rows, q_heads))

    blk = lax.broadcasted_iota(jnp.int32, (N_BLOCKS, 2 * TQ), 0)
    own = keys_of(qi)
    b_heads = []
    for p in range(N_HEADS_B // 2):
        cols = slice(p * PAIR, (p + 1) * PAIR)
        q_heads = _stack_heads(qb_ref[:, cols], 2)
        kmean = jnp.concatenate([kmean_ref[t, :, cols] for t in range(kmean_ref.shape[0])], axis=0)
        kmean_hi = kmean.astype(jnp.bfloat16)
        kmean_lo = (kmean - kmean_hi.astype(jnp.float32)).astype(jnp.bfloat16)
        both = _scores_t(jnp.concatenate([kmean_hi, kmean_lo], axis=0), q_heads)
        gate = both[:N_BLOCKS] + both[N_BLOCKS:]
        gate = jnp.where(blk < qi, gate, NEG_INF)
        beaten = jnp.zeros(gate.shape, jnp.float32)
        for j in range(N_BLOCKS):
            gj = gate[j:j + 1, :]
            wins_tie = jnp.where(gj >= gate, 1.0, 0.0)
            wins_strict = jnp.where(gj > gate, 1.0, 0.0)
            beaten = beaten + jnp.where(blk > j, wins_tie, wins_strict)
        selb_ref[p] = jnp.where((beaten < MOBA_TOPK) & (blk < qi), 0.0, NEG_INF)
        b_heads.append((cols, _head_rows(2 * p, 2), q_heads))
    n_b = len(b_heads)

    def a_scores(part):
        ks, _, cols, _, q_heads = part
        return _scores_t(ka_ref[ks, cols], q_heads)

    def b_scores(p, ks):
        cols, _, q_heads = b_heads[p]
        return _scores_t(kb_ref[ks, cols], q_heads)

    n_a = len(a_static)
    raw = a_scores(a_static[0])
    a_biased = []
    m_a = None
    for i in range(n_a):
        nxt = a_scores(a_static[i + 1]) if i + 1 < n_a else b_scores(0, own)
        s_t = raw + both_heads(bias_ref[a_static[i][1]])
        part_max = _after(jnp.max(s_t, axis=0, keepdims=True), nxt)
        m_a = part_max if m_a is None else jnp.maximum(m_a, part_max)
        a_biased.append(s_t)
        raw = nxt
    m_heads = _stack_heads(qm_ref[...], N_HEADS_M)
    ahead = [lambda: b_scores(1, own), lambda: b_scores(2, own), lambda: _scores_t(km_ref[...], m_heads),
             lambda: b_scores(0, keys_of(0)), lambda: b_scores(1, keys_of(0))]
    last_first_past = b_scores(n_b - 1, keys_of(0))
    later = [raw]
    assert n_a >= len(ahead)
    neg_m = -m_a
    acc_a = None
    for i in range(n_a):
        ks, _, _, rows, _ = a_static[i]
        p_t = jnp.exp2(a_biased[i] + neg_m).astype(jnp.bfloat16)
        if i < len(ahead):
            later.append(ahead[i]())
            neg_m = _after(neg_m, later[-1])
        part_acc = _pv([vta_ref[r, ks] for r in rows], p_t)
        acc_a = part_acc if acc_a is None else acc_a + part_acc
    l_a = acc_a[HEAD_DIM:HEAD_DIM + 1]
    o_tiles = [acc_a[:HEAD_DIM] / l_a]
    lse_a = m_a + jnp.log2(l_a)
    lse_ref[...] = jnp.concatenate([jnp.broadcast_to(lse_a[:, h * TQ:(h + 1) * TQ], (LSE_ROWS // HEADS_PER_DIL_GROUP, TQ))
                                    for h in range(HEADS_PER_DIL_GROUP)], axis=0)

    for p, (_, rows, _) in enumerate(b_heads):
        m_ref[p], acc_ref[p] = single_tile(later[p] + both_heads(bias_ref[_B_CAUSAL]),
                                           [vtb_ref[r, own] for r in rows])
    _, acc_m = single_tile(later[3], [vmt_ref[r, :] for r in _head_rows(0, N_HEADS_M)])
    raw_ref[0] = later[4]
    raw_ref[1] = later[5]
    raw_ref[n_b - 1] = last_first_past

    def past_tile(j, _):
        ks = keys_of(j)
        ks_next = keys_of(jnp.minimum(j + 1, qi - 1))
        scores = [raw_ref[c] for c in range(n_b)]
        for c in range(n_b):
            issued = b_scores(c, ks_next)
            scores.append(issued)
            sel = selb_ref[c, pl.ds(j, 1), :]
            m_old = m_ref[c]
            m_new = jnp.maximum(m_old, jnp.max(scores[c], axis=0, keepdims=True) + sel)
            alpha = _after(jnp.exp2(m_old - m_new), issued)
            p_t = jnp.exp2(scores[c] + (sel - m_new)).astype(jnp.bfloat16)
            acc_ref[c] = alpha * acc_ref[c] + _pv([vtb_ref[r, ks] for r in b_heads[c][1]], p_t)
            m_ref[c] = m_new
        for c in range(n_b):
            raw_ref[c] = scores[n_b + c]
        return 0

    lax.fori_loop(0, qi, past_tile, 0)

    for acc in (*[acc_ref[c] for c in range(n_b)], acc_m):
        o_tiles.append(acc[:HEAD_DIM] / acc[HEAD_DIM:HEAD_DIM + 1])
    row = 0
    for o_t in o_tiles:
        for h in range(o_t.shape[1] // TQ):
            o_ref[row:row + HEAD_DIM, :] = o_t[:, h * TQ:(h + 1) * TQ].astype(jnp.bfloat16)
            row += HEAD_DIM


def _attn_call(qka, vta, qkb, vtb, qm, cls, kmean, mem2, gmem, wmkv, layer, bias, cbias, batch, seq):
    nq = seq // TQ
    n_mem = mem2.shape[0] // batch
    per_layer = lambda b, q: (layer, 0, 0)
    per_tile = lambda b, q: (b * nq + q, 0)
    return pl.pallas_call(
        functools.partial(_attn_kernel, layer=layer),
        grid=(batch, nq),
        in_specs=[
            pl.BlockSpec((TQ, WIDTH_A), lambda b, q: (b * nq + q, 0)),
            pl.BlockSpec((seq, WIDTH_A), lambda b, q: (b, 1)),
            pl.BlockSpec((WIDTH_A, seq), lambda b, q: (b, 0)),
            pl.BlockSpec((TQ, WIDTH_B), lambda b, q: (b * nq + q, 0)),
            pl.BlockSpec((seq, WIDTH_B), lambda b, q: (b, 1)),
            pl.BlockSpec((WIDTH_B, seq), lambda b, q: (b, 0)),
            pl.BlockSpec((TQ, WIDTH_M), lambda b, q: (b * nq + q, 0)),
            pl.BlockSpec((3, seq, PAIR), lambda b, q: (0, b, 0)),
            pl.BlockSpec((kmean.shape[0] // batch,) + kmean.shape[1:], lambda b, q: (b, 0, 0)),
            pl.BlockSpec((n_mem, D_MODEL), lambda b, q: (b, 0)),
            pl.BlockSpec(gmem.shape, lambda b, q: (0, 0)),
            pl.BlockSpec((None, D_MODEL, 2 * WIDTH_M), per_layer),
            pl.BlockSpec(bias.shape, lambda b, q: (0, 0, 0)),
            pl.BlockSpec(cbias.shape, lambda b, q: (0, 0)),
        ],
        out_specs=[pl.BlockSpec((WIDTH_O, TQ), lambda b, q: (0, b * nq + q)),
                   pl.BlockSpec((LSE_ROWS, TQ), lambda b, q: (0, b * nq + q)),
                   pl.BlockSpec((TQ, PAIR), per_tile), pl.BlockSpec((TQ, PAIR), per_tile)],
        out_shape=[jax.ShapeDtypeStruct((WIDTH_O, batch * seq), jnp.bfloat16),
                   jax.ShapeDtypeStruct((LSE_ROWS, batch * seq), jnp.float32),
                   jax.ShapeDtypeStruct((batch * seq, PAIR), jnp.float32),
                   jax.ShapeDtypeStruct((batch * seq, PAIR), jnp.float32)],
        scratch_shapes=[
            pltpu.VMEM((n_mem, WIDTH_M), jnp.bfloat16),
            pltpu.VMEM((WIDTH_M, n_mem), jnp.bfloat16),
            pltpu.VMEM((N_HEADS_B // 2, N_BLOCKS, 2 * TQ), jnp.float32),
            pltpu.VMEM((N_HEADS_B // 2, 1, 2 * TQ), jnp.float32),
            pltpu.VMEM((N_HEADS_B // 2, HEAD_DIM + SUM_ROWS, 2 * TQ), jnp.float32),
            pltpu.VMEM((N_HEADS_B // 2, TQ, 2 * TQ), jnp.float32),
        ],
        compiler_params=pltpu.CompilerParams(dimension_semantics=("parallel", "arbitrary"),
                                             vmem_limit_bytes=VMEM_LIMIT),
        name="attn",
    )(qka, qka, vta, qkb, qkb, vtb, qm, cls, kmean, mem2, gmem, wmkv, bias, cbias)


def _mix_kernel(x_ref, o_ref, lse_ref, ocls_ref, lsecls_ref, g_ref, wg_ref, wpa_ref, wpb_ref, wpm_ref, wo_ref,
                out_ref, ocn_ref, lcn_ref, *, layer):
    tm = x_ref.shape[0]
    per_class = tm // CLASS_STRIDE
    first = (pl.program_id(0) % (SEQ // tm)) * per_class
    for r in range(CLASS_STRIDE):
        src = pl.ds(pl.multiple_of(r * CLASS_LEN + first, per_class), per_class)
        dst = pl.ds(r, per_class, stride=CLASS_STRIDE)
        ocn_ref[dst, :] = ocls_ref[src, :]
        lcn_ref[dst, :] = lsecls_ref[src, :]
    lse_t = lse_ref[...]
    per_head = LSE_ROWS // HEADS_PER_DIL_GROUP
    lse_tiled = jnp.concatenate([jnp.broadcast_to(lse_t[h * per_head:h * per_head + 1], (HEAD_DIM, tm))
                                 for h in range(HEADS_PER_DIL_GROUP)], axis=0)
    w_cls = 1.0 / (1.0 + jnp.exp2(lse_tiled - lcn_ref[...].T))
    o_tiled = o_ref[:WIDTH_A_OUT, :].astype(jnp.float32)
    o_a = (o_tiled + w_cls * (ocn_ref[...].T - o_tiled)).astype(jnp.bfloat16)
    half = tm // 2
    halves = [slice(0, half), slice(half, tm)]
    xs = [x_ref[rows, :] for rows in halves]
    hs = [_rms(x, g_ref[layer:layer + 1, :]).astype(jnp.bfloat16) for x in xs]

    def gate_scores(s, i):
        return jnp.dot(hs[s], wg_ref[0, :, i * D_MODEL:(i + 1) * D_MODEL].astype(jnp.bfloat16),
                       preferred_element_type=jnp.float32)

    def wait_for(y, matmul_result):
        tail = y.shape[0] - LSE_ROWS
        return jnp.concatenate([y[:tail], y[tail:] + (_after(jnp.zeros((1, y.shape[1]), jnp.float32), matmul_result))],
                               axis=0)

    def gated(s, first_gate):
        y = None
        col = 0
        for i, wp_ref in enumerate((wpa_ref, wpb_ref, wpm_ref)):
            width = wp_ref.shape[0]
            gate = jax.nn.sigmoid(first_gate if i == 0 else gate_scores(s, i))
            o_i = o_a[:, halves[s]] if i == 0 else o_ref[col:col + width, halves[s]]
            branch = gate * lax.dot_general(o_i, wp_ref[...].astype(jnp.bfloat16), _TN_PLAIN,
                                            preferred_element_type=jnp.float32)
            y = branch if y is None else y + branch
            col += width
        return y

    def project(y):
        return jnp.dot(y.astype(jnp.bfloat16), wo_ref[...].astype(jnp.bfloat16), preferred_element_type=jnp.float32)

    g0_a = gate_scores(0, 0)
    g0_b = gate_scores(1, 0)
    y_a = wait_for(gated(0, g0_a), g0_b)
    out_a = project(y_a)
    y_b = wait_for(gated(1, g0_b), out_a)
    out_ref[halves[0], :] = xs[0] + out_a
    out_ref[halves[1], :] = xs[1] + project(y_b)


def _mix_call(x2, o, lse, ocls, lsecls, g, w_in, wpa, wpb, wpm, wo, layer):
    t = x2.shape[0]
    tm = TM_MIX
    per_layer = lambda i: (layer, 0, 0)
    return pl.pallas_call(
        functools.partial(_mix_kernel, layer=layer),
        grid=(t // tm,),
        in_specs=[
            pl.BlockSpec((tm, D_MODEL), lambda i: (i, 0)),
            pl.BlockSpec((WIDTH_O, tm), lambda i: (0, i)),
            pl.BlockSpec((LSE_ROWS, tm), lambda i: (0, i)),
            pl.BlockSpec((SEQ, PAIR), lambda i: (i // (SEQ // tm), 0)),
            pl.BlockSpec((SEQ, PAIR), lambda i: (i // (SEQ // tm), 0)),
            pl.BlockSpec(g.shape, lambda i: (0, 0)),
            pl.BlockSpec((pl.Element(1), pl.Element(D_MODEL), pl.Element(GATE_COLS)), lambda i: (layer, 0, COL_GATES),
                         pipeline_mode=pl.Buffered(1)),
            pl.BlockSpec((None,) + wpa.shape[1:], per_layer, pipeline_mode=pl.Buffered(1)),
            pl.BlockSpec((None,) + wpb.shape[1:], per_layer, pipeline_mode=pl.Buffered(1)),
            pl.BlockSpec((None,) + wpm.shape[1:], per_layer, pipeline_mode=pl.Buffered(1)),
            pl.BlockSpec((None,) + wo.shape[1:], per_layer, pipeline_mode=pl.Buffered(1)),
        ],
        out_specs=pl.BlockSpec((tm, D_MODEL), lambda i: (i, 0)),
        out_shape=jax.ShapeDtypeStruct((t, D_MODEL), jnp.float32),
        scratch_shapes=[pltpu.VMEM((tm, PAIR), jnp.float32), pltpu.VMEM((tm, PAIR), jnp.float32)],
        compiler_params=pltpu.CompilerParams(dimension_semantics=("parallel",), vmem_limit_bytes=VMEM_LIMIT),
        name="gated_mix",
    )(x2, o, lse, ocls, lsecls, g, w_in, wpa, wpb, wpm, wo)


def _mlp_kernel(x_ref, g_ref, wup_ref, wdown_ref, gfin_ref, out_ref, *, layer, final_norm):
    x = x_ref[...]
    hm = _rms(x, g_ref[layer:layer + 1, :]).astype(jnp.bfloat16)
    acc = x
    for c in range(D_FF // FF_CHUNK):
        u = jnp.dot(hm, wup_ref[:, c * FF_CHUNK:(c + 1) * FF_CHUNK].astype(jnp.bfloat16),
                    preferred_element_type=jnp.float32)
        u = jnp.square(jnp.maximum(u, 0.0)).astype(jnp.bfloat16)
        acc = acc + jnp.dot(u, wdown_ref[c * FF_CHUNK:(c + 1) * FF_CHUNK, :].astype(jnp.bfloat16),
                            preferred_element_type=jnp.float32)
    out_ref[...] = _rms(acc, gfin_ref[...]) if final_norm else acc


def _mlp_call(x2, g, wup, wdown, gfin, layer, final_norm):
    t = x2.shape[0]
    tm = TM_MLP
    const = lambda i: (0, 0)
    per_layer = lambda i: (layer, 0, 0)
    return pl.pallas_call(
        functools.partial(_mlp_kernel, layer=layer, final_norm=final_norm),
        grid=(t // tm,),
        in_specs=[
            pl.BlockSpec((tm, D_MODEL), lambda i: (i, 0)),
            pl.BlockSpec(g.shape, const),
            pl.BlockSpec((None,) + wup.shape[1:], per_layer, pipeline_mode=pl.Buffered(1)),
            pl.BlockSpec((None,) + wdown.shape[1:], per_layer, pipeline_mode=pl.Buffered(1)),
            pl.BlockSpec((1, D_MODEL), const),
        ],
        out_specs=pl.BlockSpec((tm, D_MODEL), lambda i: (i, 0)),
        out_shape=jax.ShapeDtypeStruct((t, D_MODEL), jnp.float32),
        compiler_params=pltpu.CompilerParams(dimension_semantics=("parallel",), vmem_limit_bytes=VMEM_LIMIT),
        name="mlp",
    )(x2, g, wup, wdown, gfin)


def _rope_tables(seq):
    half = ROT_DIM // 2
    inv_freq = (1.0 / (np.float32(ROPE_THETA) ** (np.arange(0, ROT_DIM, 2, dtype=np.float32) / np.float32(ROT_DIM))))
    ang = np.arange(seq, dtype=np.float32)[:, None] * inv_freq.astype(np.float32)[None, :]
    cos, sin = np.cos(ang).astype(np.float32), np.sin(ang).astype(np.float32)
    ones = np.ones((seq, HEAD_DIM - ROT_DIM), np.float32)
    zeros_half = np.zeros((seq, half), np.float32)
    zeros_rest = np.zeros((seq, HEAD_DIM - ROT_DIM), np.float32)
    c_head = np.concatenate([cos, cos, ones], axis=1)
    lo_head = np.concatenate([-sin, zeros_half, zeros_rest], axis=1)
    hi_head = np.concatenate([zeros_half, sin, zeros_rest], axis=1)
    rep = LANES // HEAD_DIM
    return tuple(jnp.asarray(np.tile(t, (1, rep))) for t in (c_head, lo_head, hi_head))


def kernel(x, mem, norm_mix, w_in, w_proj_a, w_proj_b, w_proj_m, w_out, norm_mem, w_mem_kv, norm_mlp, w_up,
           w_down, norm_final):
    batch, seq, d = x.shape
    assert d == D_MODEL and seq == SEQ and seq % TM_QKV == 0
    depth = w_in.shape[0]
    x2 = x.reshape(batch * seq, d)
    mem2 = mem.reshape(batch * mem.shape[1], d)
    rope_c, rope_s1, rope_s2 = _rope_tables(seq)
    bias = jnp.asarray(_BIAS_NP)
    cbias = jnp.asarray(_CLASS_BIAS_NP)
    for l in range(depth):
        qka, vta, qkb, vtb, qm, cls, kmean = _qkv_call(x2, norm_mix, w_in, l, rope_c, rope_s1, rope_s2, batch, seq)
        o, lse, ocls, lsecls = _attn_call(qka, vta, qkb, vtb, qm, cls, kmean, mem2, norm_mem,
                                          w_mem_kv, l, bias, cbias, batch, seq)
        x2 = _mix_call(x2, o, lse, ocls, lsecls, norm_mix, w_in, w_proj_a, w_proj_b, w_proj_m, w_out, l)
        x2 = _mlp_call(x2, norm_mlp, w_up, w_down, norm_final.reshape(1, d), l, final_norm=(l == depth - 1))
    return x2.reshape(batch, seq, d)
```

```python
import functools

import jax
import jax.numpy as jnp
import numpy as np
from jax import lax
from jax.experimental import pallas as pl
from jax.experimental.pallas import tpu as pltpu

D_MODEL = 1024
SEQ = 2048
HEAD_DIM = 64
ROT_DIM = HEAD_DIM // 4
ROPE_THETA = 500000.0
DIL_GROUPS = ((128, 1), (512, 4), (2048, 16))
HEADS_PER_DIL_GROUP = 2
N_HEADS_A = len(DIL_GROUPS) * HEADS_PER_DIL_GROUP
N_HEADS_B = 6
N_HEADS_M = 4
MOBA_BLOCK = 256
MOBA_TOPK = 3
N_BLOCKS = SEQ // MOBA_BLOCK
D_FF = 4 * D_MODEL
WIDTH_A = N_HEADS_A * HEAD_DIM
WIDTH_A_OUT = HEADS_PER_DIL_GROUP * HEAD_DIM
WIDTH_B = N_HEADS_B * HEAD_DIM
WIDTH_M = N_HEADS_M * HEAD_DIM
WIDTH_O = WIDTH_A_OUT + WIDTH_B + WIDTH_M
RMS_EPS = 1e-6
NEG_INF = -1e30
Q_SCALE = HEAD_DIM ** -0.5
LOG2_E = 1.4426950408889634

LANES = 128
LSE_ROWS = 8
PAIR = 2 * HEAD_DIM
SUM_ROWS = 16
TQ = MOBA_BLOCK
CLASS_GROUP = len(DIL_GROUPS) - 1
CLASS_STRIDE = DIL_GROUPS[CLASS_GROUP][1]
CLASS_LEN = SEQ // CLASS_STRIDE
CLASSES_PER_TILE = TQ // CLASS_LEN
TM_QKV = 1024
TM_MIX = 1024
TM_MLP = 1024
FF_CHUNK = 1024
VMEM_LIMIT = 60 * 1024 * 1024

_B_ALLNEG = 0
_B_CAUSAL = 1


TILE_GROUPS = DIL_GROUPS[:CLASS_GROUP]
assert DIL_GROUPS[CLASS_GROUP][0] >= SEQ and TQ % CLASS_LEN == 0


def _dil_tile_offsets():
    return tuple(tuple(range(min((w + TQ - 1) // TQ, N_BLOCKS - 1) + 1)) for w, _ in TILE_GROUPS)


def _build_bias_tiles():
    c = np.arange(TQ)[:, None]
    r = np.arange(TQ)[None, :]
    tiles = [np.zeros((TQ, TQ), bool), (r - c) >= 0]
    ids = []
    key_rows = []
    for (w, d), offs in zip(TILE_GROUPS, _dil_tile_offsets()):
        per_off = []
        for o in offs:
            diff = o * TQ + r - c
            per_off.append((diff >= 0) & (diff <= w) & (diff % d == 0))
        uniq, gid = [], []
        for t in per_off:
            for k, u in enumerate(uniq):
                if np.array_equal(t, u):
                    gid.append(k)
                    break
            else:
                uniq.append(t)
                gid.append(len(uniq) - 1)
        ids.append(tuple(len(tiles) + k for k in gid))
        tiles.extend(uniq)
        used = [np.flatnonzero(t.any(axis=1)) for t in per_off]
        key_rows.append(tuple((int(u.min()) // LANES * LANES, -(-(int(u.max()) + 1) // LANES) * LANES) for u in used))
    bias = np.where(np.stack(tiles), 0.0, NEG_INF).astype(np.float32)
    return bias, tuple(ids), tuple(key_rows)


_BIAS_NP, _DIL_BIAS_IDS, _DIL_KEY_ROWS = _build_bias_tiles()
_pos = np.arange(CLASS_LEN)
_CLASS_BIAS_NP = np.where(_pos[None, :] >= _pos[:, None], 0.0, NEG_INF).astype(np.float32)

_NT = (((1,), (1,)), ((), ()))
_TN = (((0,), (1,)), ((), ()))
_TN_PLAIN = (((0,), (0,)), ((), ()))
COL_B = 3 * WIDTH_A
COL_M = COL_B + 3 * WIDTH_B
COL_GATES = COL_M + WIDTH_M
GATE_COLS = 3 * D_MODEL


def _rms(x, g):
    return x * lax.rsqrt(jnp.mean(x * x, axis=-1, keepdims=True) + RMS_EPS) * g


def _qkv_kernel(x_ref, g_ref, win_ref, c_ref, s1_ref, s2_ref, qka_ref, vta_ref, qkb_ref, vtb_ref, qm_ref, cls_ref,
                kmean_ref, *, layer):
    tm = x_ref.shape[0]
    halves = (slice(0, tm // 2), slice(tm // 2, tm))
    normed = [_rms(x_ref[rows, :], g_ref[layer:layer + 1, :]) for rows in halves]
    for s, rows in enumerate(halves):
        h = normed[s].astype(jnp.bfloat16)
        cos = c_ref[rows, :]
        sin_lo = s1_ref[rows, :]
        sin_hi = s2_ref[rows, :]
        for col0, qk_ref, vt_ref, width in ((0, qka_ref, vta_ref, WIDTH_A), (COL_B, qkb_ref, vtb_ref, WIDTH_B)):
            z = jnp.dot(h, win_ref[:, col0:col0 + 2 * width].astype(jnp.bfloat16),
                        preferred_element_type=jnp.float32)
            for blk in range(2 * width // LANES):
                zb = z[:, blk * LANES:(blk + 1) * LANES]
                rb = (zb * cos + pltpu.roll(zb, LANES - ROT_DIM // 2, 1) * sin_lo
                      + pltpu.roll(zb, ROT_DIM // 2, 1) * sin_hi)
                if blk < width // LANES:
                    rb = rb * (Q_SCALE * LOG2_E)
                qk_ref[rows, blk * LANES:(blk + 1) * LANES] = rb.astype(jnp.bfloat16)
                if col0 == 0 and blk % (width // LANES) == CLASS_GROUP:
                    part = blk // (width // LANES)
                    cls_ref[part, rows, :] = rb
                if col0 == COL_B and blk >= width // LANES:
                    kcol = (blk - width // LANES) * LANES
                    blocks = slice(rows.start // MOBA_BLOCK, rows.stop // MOBA_BLOCK)
                    kmean_ref[0, blocks, kcol:kcol + LANES] = jnp.mean(rb.reshape(-1, MOBA_BLOCK, LANES), axis=1)
            vt = lax.dot_general(win_ref[:, col0 + 2 * width:col0 + 3 * width].astype(jnp.bfloat16), h, _TN,
                                 preferred_element_type=jnp.float32)
            vt_ref[:, rows] = vt.astype(jnp.bfloat16)
            if col0 == 0:
                cls_ref[2, rows, :] = vt[CLASS_GROUP * PAIR:(CLASS_GROUP + 1) * PAIR, :].T
        qm = jnp.dot(h, win_ref[:, COL_M:COL_GATES].astype(jnp.bfloat16),
                     preferred_element_type=jnp.float32) * (Q_SCALE * LOG2_E)
        if s + 1 < len(halves):
            tail = qm.shape[0] - LSE_ROWS
            qm = jnp.concatenate([qm[:tail], qm[tail:] + _after(jnp.zeros((1, qm.shape[1]), jnp.float32),
                                                               normed[s + 1])], axis=0)
        qm_ref[rows, :] = qm.astype(jnp.bfloat16)


def _qkv_call(x2, g, w_in, layer, rope_c, rope_s1, rope_s2, batch, seq):
    t = x2.shape[0]
    tm = TM_QKV
    nts = seq // tm
    per_layer = lambda i: (layer, 0, 0)
    return pl.pallas_call(
        functools.partial(_qkv_kernel, layer=layer),
        grid=(t // tm,),
        in_specs=[
            pl.BlockSpec((tm, D_MODEL), lambda i: (i, 0)),
            pl.BlockSpec(g.shape, lambda i: (0, 0)),
            pl.BlockSpec((None, D_MODEL, COL_GATES), per_layer),
            pl.BlockSpec((tm, LANES), lambda i: (i % nts, 0)),
            pl.BlockSpec((tm, LANES), lambda i: (i % nts, 0)),
            pl.BlockSpec((tm, LANES), lambda i: (i % nts, 0)),
        ],
        out_specs=[
            pl.BlockSpec((tm, 2 * WIDTH_A), lambda i: (i, 0)),
            pl.BlockSpec((WIDTH_A, tm), lambda i: (i // nts, i % nts)),
            pl.BlockSpec((tm, 2 * WIDTH_B), lambda i: (i, 0)),
            pl.BlockSpec((WIDTH_B, tm), lambda i: (i // nts, i % nts)),
            pl.BlockSpec((tm, WIDTH_M), lambda i: (i, 0)),
            pl.BlockSpec((3, tm, PAIR), lambda i: (0, i, 0)),
            pl.BlockSpec((1, tm // MOBA_BLOCK, WIDTH_B), lambda i: (i, 0, 0)),
        ],
        out_shape=[
            jax.ShapeDtypeStruct((t, 2 * WIDTH_A), jnp.bfloat16),
            jax.ShapeDtypeStruct((batch * WIDTH_A, seq), jnp.bfloat16),
            jax.ShapeDtypeStruct((t, 2 * WIDTH_B), jnp.bfloat16),
            jax.ShapeDtypeStruct((batch * WIDTH_B, seq), jnp.bfloat16),
            jax.ShapeDtypeStruct((t, WIDTH_M), jnp.bfloat16),
            jax.ShapeDtypeStruct((3, t, PAIR), jnp.float32),
            jax.ShapeDtypeStruct((t // tm, tm // MOBA_BLOCK, WIDTH_B), jnp.float32),
        ],
        compiler_params=pltpu.CompilerParams(dimension_semantics=("parallel",), vmem_limit_bytes=VMEM_LIMIT),
        name="qkv_proj",
    )(x2, g, w_in, rope_c, rope_s1, rope_s2)


def _pv(v_heads, p_bf):
    ones = jnp.ones((SUM_ROWS, p_bf.shape[0]), jnp.bfloat16)
    nq = p_bf.shape[1] // len(v_heads)
    return jnp.concatenate([jnp.dot(jnp.concatenate([v, ones], axis=0), p_bf[:, h * nq:(h + 1) * nq],
                                    preferred_element_type=jnp.float32)
                            for h, v in enumerate(v_heads)], axis=1)


def _after(x, *matmul_results):
    for r in matmul_results:
        bits = lax.bitcast_convert_type(r[-1:, :x.shape[1]], jnp.uint32)
        bits = lax.shift_right_logical(lax.shift_right_logical(bits, jnp.uint32(16)), jnp.uint32(16))
        x = x + lax.bitcast_convert_type(bits, jnp.float32)
    return x


def _scores_t(k_tile, q_heads):
    return lax.dot_general(k_tile, q_heads, _NT, preferred_element_type=jnp.float32)


def _stack_heads(q, n_heads):
    lane = lax.broadcasted_iota(jnp.int32, q.shape, 1)
    zero = jnp.zeros((), q.dtype)
    return jnp.concatenate([jnp.where((lane >= h * HEAD_DIM) & (lane < (h + 1) * HEAD_DIM), q, zero)
                            for h in range(n_heads)], axis=0)


def _head_rows(first_head, n_heads):
    return [slice((first_head + h) * HEAD_DIM, (first_head + h + 1) * HEAD_DIM) for h in range(n_heads)]


def _lanes_to_rows(row, n_heads):
    nq = row.shape[1] // n_heads
    return jnp.concatenate([jnp.broadcast_to(row[:, h * nq:(h + 1) * nq], (HEAD_DIM, nq)) for h in range(n_heads)],
                           axis=0)


def _heads_to_rows(o_t, n_heads):
    nq = o_t.shape[1] // n_heads
    return jnp.concatenate([o_t[:, h * nq:(h + 1) * nq] for h in range(n_heads)], axis=0)


def _attn_kernel(qa_ref, ka_ref, vta_ref, qb_ref, kb_ref, vtb_ref, qm_ref, cls_ref, kmean_ref, mem_ref, gmem_ref,
                 wmkv_ref, bias_ref, cbias_ref, o_ref, lse_ref, ocls_ref, lsecls_ref,
                 km_ref, vmt_ref, selb_ref, m_ref, acc_ref, raw_ref, rawmax_ref, *, layer):
    qi = pl.program_id(1)

    @pl.when(qi == 0)
    def _per_batch():
        mem_n = _rms(mem_ref[...], gmem_ref[layer:layer + 1, :]).astype(jnp.bfloat16)
        w_mkv = wmkv_ref[...].astype(jnp.bfloat16)
        km_ref[...] = jnp.dot(mem_n, w_mkv[:, :WIDTH_M], preferred_element_type=jnp.float32).astype(jnp.bfloat16)
        vmt_ref[...] = lax.dot_general(w_mkv[:, WIDTH_M:], mem_n, _TN,
                                       preferred_element_type=jnp.float32).astype(jnp.bfloat16)

    def keys_of(j):
        return pl.ds(pl.multiple_of(j * TQ, TQ), TQ)

    def both_heads(bias):
        return jnp.concatenate([bias, bias], axis=1)

    def single_tile(s_t, v_heads):
        m = jnp.max(s_t, axis=0, keepdims=True)
        return m, _pv(v_heads, jnp.exp2(s_t - m).astype(jnp.bfloat16))

    for c in range(CLASSES_PER_TILE):
        rows = pl.ds(qi * CLASSES_PER_TILE + c, CLASS_LEN, stride=CLASS_STRIDE)
        q_heads = _stack_heads(cls_ref[0, rows, :].astype(jnp.bfloat16), HEADS_PER_DIL_GROUP)
        k_cls = cls_ref[1, rows, :].astype(jnp.bfloat16)
        v_t = cls_ref[2, rows, :].T.astype(jnp.bfloat16)
        m, acc = single_tile(_scores_t(k_cls, q_heads) + both_heads(cbias_ref[...]),
                             [v_t[r] for r in _head_rows(0, HEADS_PER_DIL_GROUP)])
        l = acc[HEAD_DIM:HEAD_DIM + 1]
        out = slice(c * CLASS_LEN, (c + 1) * CLASS_LEN)
        ocls_ref[out, :] = _heads_to_rows(acc[:HEAD_DIM] / l, HEADS_PER_DIL_GROUP).T
        lsecls_ref[out, :] = _lanes_to_rows(m + jnp.log2(l), HEADS_PER_DIL_GROUP).T

    a_static = []
    for g, offs in enumerate(_dil_tile_offsets()):
        cols = slice(g * PAIR, (g + 1) * PAIR)
        rows = _head_rows(g * HEADS_PER_DIL_GROUP, HEADS_PER_DIL_GROUP)
        q_heads = _stack_heads(qa_ref[:, cols], HEADS_PER_DIL_GROUP)
        ids = _DIL_BIAS_IDS[g]
        for o in offs:
            lo, hi = _DIL_KEY_ROWS[g][o]
            ks = pl.ds(pl.multiple_of(jnp.maximum(qi - o, 0) * TQ + lo, LANES), hi - lo)
            bid = ids[o] if o == 0 else jnp.where(qi >= o, ids[o], _B_ALLNEG)
            a_static.append((ks, (bid, slice(lo, hi)), cols, rows, q_heads))

    blk = lax.broadcasted_iota(jnp.int32, (N_BLOCKS, 2 * TQ), 0)
    own = keys_of(qi)
    b_heads = []
    for p in range(N_HEADS_B // 2):
        cols = slice(p * PAIR, (p + 1) * PAIR)
        q_heads = _stack_heads(qb_ref[:, cols], 2)
        kmean = jnp.concatenate([kmean_ref[t, :, cols] for t in range(kmean_ref.shape[0])], axis=0)
        kmean_hi = kmean.astype(jnp.bfloat16)
        kmean_lo = (kmean - kmean_hi.astype(jnp.float32)).astype(jnp.bfloat16)
        both = _scores_t(jnp.concatenate([kmean_hi, kmean_lo], axis=0), q_heads)
        gate = both[:N_BLOCKS] + both[N_BLOCKS:]
        gate = jnp.where(blk < qi, gate, NEG_INF)
        beaten = jnp.zeros(gate.shape, jnp.float32)
        for j in range(N_BLOCKS):
            gj = gate[j:j + 1, :]
            wins_tie = jnp.where(gj >= gate, 1.0, 0.0)
            wins_strict = jnp.where(gj > gate, 1.0, 0.0)
            beaten = beaten + jnp.where(blk > j, wins_tie, wins_strict)
        selb_ref[p] = jnp.where((beaten < MOBA_TOPK) & (blk < qi), 0.0, NEG_INF)
        b_heads.append((cols, _head_rows(2 * p, 2), q_heads))
    n_b = len(b_heads)

    def a_scores(part):
        ks, _, cols, _, q_heads = part
        return _scores_t(ka_ref[ks, cols], q_heads)

    def b_scores(p, ks):
        cols, _, q_heads = b_heads[p]
        return _scores_t(kb_ref[ks, cols], q_heads)

    n_a = len(a_static)
    raw = a_scores(a_static[0])
    a_biased = []
    m_a = None
    for i in range(n_a):
        nxt = a_scores(a_static[i + 1]) if i + 1 < n_a else b_scores(0, own)
        s_t = raw + both_heads(bias_ref[a_static[i][1]])
        part_max = _after(jnp.max(s_t, axis=0, keepdims=True), nxt)
        m_a = part_max if m_a is None else jnp.maximum(m_a, part_max)
        a_biased.append(s_t)
        raw = nxt
    m_heads = _stack_heads(qm_ref[...], N_HEADS_M)
    ahead = [lambda: b_scores(1, own), lambda: b_scores(2, own), lambda: _scores_t(km_ref[...], m_heads),
             lambda: b_scores(0, keys_of(0)), lambda: b_scores(1, keys_of(0))]
    last_first_past = b_scores(n_b - 1, keys_of(0))
    later = [raw]
    assert n_a >= len(ahead)
    neg_m = -m_a
    acc_a = None
    for i in range(n_a):
        ks, _, _, rows, _ = a_static[i]
        p_t = jnp.exp2(a_biased[i] + neg_m).astype(jnp.bfloat16)
        if i < len(ahead):
            later.append(ahead[i]())
            neg_m = _after(neg_m, later[-1])
        part_acc = _pv([vta_ref[r, ks] for r in rows], p_t)
        acc_a = part_acc if acc_a is None else acc_a + part_acc
    l_a = acc_a[HEAD_DIM:HEAD_DIM + 1]
    o_tiles = [acc_a[:HEAD_DIM] / l_a]
    lse_a = m_a + jnp.log2(l_a)
    lse_ref[...] = jnp.concatenate([jnp.broadcast_to(lse_a[:, h * TQ:(h + 1) * TQ], (LSE_ROWS // HEADS_PER_DIL_GROUP, TQ))
                                    for h in range(HEADS_PER_DIL_GROUP)], axis=0)

    for p, (_, rows, _) in enumerate(b_heads):
        m_ref[p], acc_ref[p] = single_tile(later[p] + both_heads(bias_ref[_B_CAUSAL]),
                                           [vtb_ref[r, own] for r in rows])
    _, acc_m = single_tile(later[3], [vmt_ref[r, :] for r in _head_rows(0, N_HEADS_M)])
    for c, first in enumerate((later[4], later[5], last_first_past)):
        raw_ref[c] = first
        rawmax_ref[c] = jnp.max(first, axis=0, keepdims=True)

    def past_tile(j, _):
        ks = keys_of(j)
        ks_next = keys_of(jnp.minimum(j + 1, qi - 1))
        scores = [raw_ref[c] for c in range(n_b)]
        for c in range(n_b):
            issued = b_scores(c, ks_next)
            scores.append(issued)
            sel = selb_ref[c, pl.ds(j, 1), :]
            m_old = m_ref[c]
            m_new = jnp.maximum(m_old, rawmax_ref[c] + sel)
            alpha = _after(jnp.exp2(m_old - m_new), issued)
            p_t = jnp.exp2(scores[c] + (sel - m_new)).astype(jnp.bfloat16)
            acc_ref[c] = alpha * acc_ref[c] + _pv([vtb_ref[r, ks] for r in b_heads[c][1]], p_t)
            m_ref[c] = m_new
        for c in range(n_b):
            raw_ref[c] = scores[n_b + c]
            rawmax_ref[c] = jnp.max(scores[n_b + c], axis=0, keepdims=True)
        return 0

    lax.fori_loop(0, qi, past_tile, 0)

    for acc in (*[acc_ref[c] for c in range(n_b)], acc_m):
        o_tiles.append(acc[:HEAD_DIM] / acc[HEAD_DIM:HEAD_DIM + 1])
    row = 0
    for o_t in o_tiles:
        for h in range(o_t.shape[1] // TQ):
            o_ref[row:row + HEAD_DIM, :] = o_t[:, h * TQ:(h + 1) * TQ].astype(jnp.bfloat16)
            row += HEAD_DIM


def _attn_call(qka, vta, qkb, vtb, qm, cls, kmean, mem2, gmem, wmkv, layer, bias, cbias, batch, seq):
    nq = seq // TQ
    n_mem = mem2.shape[0] // batch
    per_layer = lambda b, q: (layer, 0, 0)
    per_tile = lambda b, q: (b * nq + q, 0)
    return pl.pallas_call(
        functools.partial(_attn_kernel, layer=layer),
        grid=(batch, nq),
        in_specs=[
            pl.BlockSpec((TQ, WIDTH_A), lambda b, q: (b * nq + q, 0)),
            pl.BlockSpec((seq, WIDTH_A), lambda b, q: (b, 1)),
            pl.BlockSpec((WIDTH_A, seq), lambda b, q: (b, 0)),
            pl.BlockSpec((TQ, WIDTH_B), lambda b, q: (b * nq + q, 0)),
            pl.BlockSpec((seq, WIDTH_B), lambda b, q: (b, 1)),
            pl.BlockSpec((WIDTH_B, seq), lambda b, q: (b, 0)),
            pl.BlockSpec((TQ, WIDTH_M), lambda b, q: (b * nq + q, 0)),
            pl.BlockSpec((3, seq, PAIR), lambda b, q: (0, b, 0)),
            pl.BlockSpec((kmean.shape[0] // batch,) + kmean.shape[1:], lambda b, q: (b, 0, 0)),
            pl.BlockSpec((n_mem, D_MODEL), lambda b, q: (b, 0)),
            pl.BlockSpec(gmem.shape, lambda b, q: (0, 0)),
            pl.BlockSpec((None, D_MODEL, 2 * WIDTH_M), per_layer),
            pl.BlockSpec(bias.shape, lambda b, q: (0, 0, 0)),
            pl.BlockSpec(cbias.shape, lambda b, q: (0, 0)),
        ],
        out_specs=[pl.BlockSpec((WIDTH_O, TQ), lambda b, q: (0, b * nq + q)),
                   pl.BlockSpec((LSE_ROWS, TQ), lambda b, q: (0, b * nq + q)),
                   pl.BlockSpec((TQ, PAIR), per_tile), pl.BlockSpec((TQ, PAIR), per_tile)],
        out_shape=[jax.ShapeDtypeStruct((WIDTH_O, batch * seq), jnp.bfloat16),
                   jax.ShapeDtypeStruct((LSE_ROWS, batch * seq), jnp.float32),
                   jax.ShapeDtypeStruct((batch * seq, PAIR), jnp.float32),
                   jax.ShapeDtypeStruct((batch * seq, PAIR), jnp.float32)],
        scratch_shapes=[
            pltpu.VMEM((n_mem, WIDTH_M), jnp.bfloat16),
            pltpu.VMEM((WIDTH_M, n_mem), jnp.bfloat16),
            pltpu.VMEM((N_HEADS_B // 2, N_BLOCKS, 2 * TQ), jnp.float32),
            pltpu.VMEM((N_HEADS_B // 2, 1, 2 * TQ), jnp.float32),
            pltpu.VMEM((N_HEADS_B // 2, HEAD_DIM + SUM_ROWS, 2 * TQ), jnp.float32),
            pltpu.VMEM((N_HEADS_B // 2, TQ, 2 * TQ), jnp.float32),
            pltpu.VMEM((N_HEADS_B // 2, 1, 2 * TQ), jnp.float32),
        ],
        compiler_params=pltpu.CompilerParams(dimension_semantics=("parallel", "arbitrary"),
                                             vmem_limit_bytes=VMEM_LIMIT),
        name="attn",
    )(qka, qka, vta, qkb, qkb, vtb, qm, cls, kmean, mem2, gmem, wmkv, bias, cbias)


def _mix_kernel(x_ref, o_ref, lse_ref, ocls_ref, lsecls_ref, g_ref, wg_ref, wpa_ref, wpb_ref, wpm_ref, wo_ref,
                out_ref, ocn_ref, lcn_ref, *, layer):
    tm = x_ref.shape[0]
    per_class = tm // CLASS_STRIDE
    first = (pl.program_id(0) % (SEQ // tm)) * per_class
    for r in range(CLASS_STRIDE):
        src = pl.ds(pl.multiple_of(r * CLASS_LEN + first, per_class), per_class)
        dst = pl.ds(r, per_class, stride=CLASS_STRIDE)
        ocn_ref[dst, :] = ocls_ref[src, :]
        lcn_ref[dst, :] = lsecls_ref[src, :]
    lse_t = lse_ref[...]
    per_head = LSE_ROWS // HEADS_PER_DIL_GROUP
    lse_tiled = jnp.concatenate([jnp.broadcast_to(lse_t[h * per_head:h * per_head + 1], (HEAD_DIM, tm))
                                 for h in range(HEADS_PER_DIL_GROUP)], axis=0)
    w_cls = 1.0 / (1.0 + jnp.exp2(lse_tiled - lcn_ref[...].T))
    o_tiled = o_ref[:WIDTH_A_OUT, :].astype(jnp.float32)
    o_a = (o_tiled + w_cls * (ocn_ref[...].T - o_tiled)).astype(jnp.bfloat16)
    half = tm // 2
    halves = [slice(0, half), slice(half, tm)]
    xs = [x_ref[rows, :] for rows in halves]
    hs = [_rms(x, g_ref[layer:layer + 1, :]).astype(jnp.bfloat16) for x in xs]

    def gate_scores(s, i):
        return jnp.dot(hs[s], wg_ref[0, :, i * D_MODEL:(i + 1) * D_MODEL].astype(jnp.bfloat16),
                       preferred_element_type=jnp.float32)

    def wait_for(y, matmul_result):
        tail = y.shape[0] - LSE_ROWS
        return jnp.concatenate([y[:tail], y[tail:] + (_after(jnp.zeros((1, y.shape[1]), jnp.float32), matmul_result))],
                               axis=0)

    def gated(s, first_gate):
        y = None
        col = 0
        for i, wp_ref in enumerate((wpa_ref, wpb_ref, wpm_ref)):
            width = wp_ref.shape[0]
            gate = jax.nn.sigmoid(first_gate if i == 0 else gate_scores(s, i))
            o_i = o_a[:, halves[s]] if i == 0 else o_ref[col:col + width, halves[s]]
            branch = gate * lax.dot_general(o_i, wp_ref[...].astype(jnp.bfloat16), _TN_PLAIN,
                                            preferred_element_type=jnp.float32)
            y = branch if y is None else y + branch
            col += width
        return y

    def project(y):
        return jnp.dot(y.astype(jnp.bfloat16), wo_ref[...].astype(jnp.bfloat16), preferred_element_type=jnp.float32)

    g0_a = gate_scores(0, 0)
    g0_b = gate_scores(1, 0)
    y_a = wait_for(gated(0, g0_a), g0_b)
    out_a = project(y_a)
    y_b = wait_for(gated(1, g0_b), out_a)
    out_ref[halves[0], :] = xs[0] + out_a
    out_ref[halves[1], :] = xs[1] + project(y_b)


def _mix_call(x2, o, lse, ocls, lsecls, g, w_in, wpa, wpb, wpm, wo, layer):
    t = x2.shape[0]
    tm = TM_MIX
    per_layer = lambda i: (layer, 0, 0)
    return pl.pallas_call(
        functools.partial(_mix_kernel, layer=layer),
        grid=(t // tm,),
        in_specs=[
            pl.BlockSpec((tm, D_MODEL), lambda i: (i, 0)),
            pl.BlockSpec((WIDTH_O, tm), lambda i: (0, i)),
            pl.BlockSpec((LSE_ROWS, tm), lambda i: (0, i)),
            pl.BlockSpec((SEQ, PAIR), lambda i: (i // (SEQ // tm), 0)),
            pl.BlockSpec((SEQ, PAIR), lambda i: (i // (SEQ // tm), 0)),
            pl.BlockSpec(g.shape, lambda i: (0, 0)),
            pl.BlockSpec((pl.Element(1), pl.Element(D_MODEL), pl.Element(GATE_COLS)), lambda i: (layer, 0, COL_GATES),
                         pipeline_mode=pl.Buffered(1)),
            pl.BlockSpec((None,) + wpa.shape[1:], per_layer, pipeline_mode=pl.Buffered(1)),
            pl.BlockSpec((None,) + wpb.shape[1:], per_layer, pipeline_mode=pl.Buffered(1)),
            pl.BlockSpec((None,) + wpm.shape[1:], per_layer, pipeline_mode=pl.Buffered(1)),
            pl.BlockSpec((None,) + wo.shape[1:], per_layer, pipeline_mode=pl.Buffered(1)),
        ],
        out_specs=pl.BlockSpec((tm, D_MODEL), lambda i: (i, 0)),
        out_shape=jax.ShapeDtypeStruct((t, D_MODEL), jnp.float32),
        scratch_shapes=[pltpu.VMEM((tm, PAIR), jnp.float32), pltpu.VMEM((tm, PAIR), jnp.float32)],
        compiler_params=pltpu.CompilerParams(dimension_semantics=("parallel",), vmem_limit_bytes=VMEM_LIMIT),
        name="gated_mix",
    )(x2, o, lse, ocls, lsecls, g, w_in, wpa, wpb, wpm, wo)


def _mlp_kernel(x_ref, g_ref, wup_ref, wdown_ref, gfin_ref, out_ref, *, layer, final_norm):
    x = x_ref[...]
    hm = _rms(x, g_ref[layer:layer + 1, :]).astype(jnp.bfloat16)
    acc = x
    for c in range(D_FF // FF_CHUNK):
        u = jnp.dot(hm, wup_ref[:, c * FF_CHUNK:(c + 1) * FF_CHUNK].astype(jnp.bfloat16),
                    preferred_element_type=jnp.float32)
        u = jnp.square(jnp.maximum(u, 0.0)).astype(jnp.bfloat16)
        acc = acc + jnp.dot(u, wdown_ref[c * FF_CHUNK:(c + 1) * FF_CHUNK, :].astype(jnp.bfloat16),
                            preferred_element_type=jnp.float32)
    out_ref[...] = _rms(acc, gfin_ref[...]) if final_norm else acc


def _mlp_call(x2, g, wup, wdown, gfin, layer, final_norm):
    t = x2.shape[0]
    tm = TM_MLP
    const = lambda i: (0, 0)
    per_layer = lambda i: (layer, 0, 0)
    return pl.pallas_call(
        functools.partial(_mlp_kernel, layer=layer, final_norm=final_norm),
        grid=(t // tm,),
        in_specs=[
            pl.BlockSpec((tm, D_MODEL), lambda i: (i, 0)),
            pl.BlockSpec(g.shape, const),
            pl.BlockSpec((None,) + wup.shape[1:], per_layer, pipeline_mode=pl.Buffered(1)),
            pl.BlockSpec((None,) + wdown.shape[1:], per_layer, pipeline_mode=pl.Buffered(1)),
            pl.BlockSpec((1, D_MODEL), const),
        ],
        out_specs=pl.BlockSpec((tm, D_MODEL), lambda i: (i, 0)),
        out_shape=jax.ShapeDtypeStruct((t, D_MODEL), jnp.float32),
        compiler_params=pltpu.CompilerParams(dimension_semantics=("parallel",), vmem_limit_bytes=VMEM_LIMIT),
        name="mlp",
    )(x2, g, wup, wdown, gfin)


def _rope_tables(seq):
    half = ROT_DIM // 2
    inv_freq = (1.0 / (np.float32(ROPE_THETA) ** (np.arange(0, ROT_DIM, 2, dtype=np.float32) / np.float32(ROT_DIM))))
    ang = np.arange(seq, dtype=np.float32)[:, None] * inv_freq.astype(np.float32)[None, :]
    cos, sin = np.cos(ang).astype(np.float32), np.sin(ang).astype(np.float32)
    ones = np.ones((seq, HEAD_DIM - ROT_DIM), np.float32)
    zeros_half = np.zeros((seq, half), np.float32)
    zeros_rest = np.zeros((seq, HEAD_DIM - ROT_DIM), np.float32)
    c_head = np.concatenate([cos, cos, ones], axis=1)
    lo_head = np.concatenate([-sin, zeros_half, zeros_rest], axis=1)
    hi_head = np.concatenate([zeros_half, sin, zeros_rest], axis=1)
    rep = LANES // HEAD_DIM
    return tuple(jnp.asarray(np.tile(t, (1, rep))) for t in (c_head, lo_head, hi_head))


def kernel(x, mem, norm_mix, w_in, w_proj_a, w_proj_b, w_proj_m, w_out, norm_mem, w_mem_kv, norm_mlp, w_up,
           w_down, norm_final):
    batch, seq, d = x.shape
    assert d == D_MODEL and seq == SEQ and seq % TM_QKV == 0
    depth = w_in.shape[0]
    x2 = x.reshape(batch * seq, d)
    mem2 = mem.reshape(batch * mem.shape[1], d)
    rope_c, rope_s1, rope_s2 = _rope_tables(seq)
    bias = jnp.asarray(_BIAS_NP)
    cbias = jnp.asarray(_CLASS_BIAS_NP)
    for l in range(depth):
        qka, vta, qkb, vtb, qm, cls, kmean = _qkv_call(x2, norm_mix, w_in, l, rope_c, rope_s1, rope_s2, batch, seq)
        o, lse, ocls, lsecls = _attn_call(qka, vta, qkb, vtb, qm, cls, kmean, mem2, norm_mem,
                                          w_mem_kv, l, bias, cbias, batch, seq)
        x2 = _mix_call(x2, o, lse, ocls, lsecls, norm_mix, w_in, w_proj_a, w_proj_b, w_proj_m, w_out, l)
        x2 = _mlp_call(x2, norm_mlp, w_up, w_down, norm_final.reshape(1, d), l, final_norm=(l == depth - 1))
    return x2.reshape(batch, seq, d)
```

```python
import functools

import jax
import jax.numpy as jnp
import numpy as np
from jax import lax
from jax.experimental import pallas as pl
from jax.experimental.pallas import tpu as pltpu

D_MODEL = 1024
SEQ = 2048
HEAD_DIM = 64
ROT_DIM = HEAD_DIM // 4
ROPE_THETA = 500000.0
DIL_GROUPS = ((128, 1), (512, 4), (2048, 16))
HEADS_PER_DIL_GROUP = 2
N_HEADS_A = len(DIL_GROUPS) * HEADS_PER_DIL_GROUP
N_HEADS_B = 6
N_HEADS_M = 4
MOBA_BLOCK = 256
MOBA_TOPK = 3
N_BLOCKS = SEQ // MOBA_BLOCK
D_FF = 4 * D_MODEL
WIDTH_A = N_HEADS_A * HEAD_DIM
WIDTH_A_OUT = HEADS_PER_DIL_GROUP * HEAD_DIM
WIDTH_B = N_HEADS_B * HEAD_DIM
WIDTH_M = N_HEADS_M * HEAD_DIM
WIDTH_O = WIDTH_A_OUT + WIDTH_B + WIDTH_M
RMS_EPS = 1e-6
NEG_INF = -1e30
Q_SCALE = HEAD_DIM ** -0.5
LOG2_E = 1.4426950408889634

LANES = 128
LSE_ROWS = 8
PAIR = 2 * HEAD_DIM
SUM_ROWS = 16
TQ = MOBA_BLOCK
CLASS_GROUP = len(DIL_GROUPS) - 1
CLASS_STRIDE = DIL_GROUPS[CLASS_GROUP][1]
CLASS_LEN = SEQ // CLASS_STRIDE
CLASSES_PER_TILE = TQ // CLASS_LEN
TM_QKV = 1024
TM_MIX = 1024
TM_MLP = 512
FF_CHUNK = 1024
VMEM_LIMIT = 60 * 1024 * 1024

_B_ALLNEG = 0
_B_CAUSAL = 1


TILE_GROUPS = DIL_GROUPS[:CLASS_GROUP]
assert DIL_GROUPS[CLASS_GROUP][0] >= SEQ and TQ % CLASS_LEN == 0


def _dil_tile_offsets():
    return tuple(tuple(range(min((w + TQ - 1) // TQ, N_BLOCKS - 1) + 1)) for w, _ in TILE_GROUPS)


def _build_bias_tiles():
    c = np.arange(TQ)[:, None]
    r = np.arange(TQ)[None, :]
    tiles = [np.zeros((TQ, TQ), bool), (r - c) >= 0]
    ids = []
    key_rows = []
    for (w, d), offs in zip(TILE_GROUPS, _dil_tile_offsets()):
        per_off = []
        for o in offs:
            diff = o * TQ + r - c
            per_off.append((diff >= 0) & (diff <= w) & (diff % d == 0))
        uniq, gid = [], []
        for t in per_off:
            for k, u in enumerate(uniq):
                if np.array_equal(t, u):
                    gid.append(k)
                    break
            else:
                uniq.append(t)
                gid.append(len(uniq) - 1)
        ids.append(tuple(len(tiles) + k for k in gid))
        tiles.extend(uniq)
        used = [np.flatnonzero(t.any(axis=1)) for t in per_off]
        key_rows.append(tuple((int(u.min()) // LANES * LANES, -(-(int(u.max()) + 1) // LANES) * LANES) for u in used))
    bias = np.where(np.stack(tiles), 0.0, NEG_INF).astype(np.float32)
    return bias, tuple(ids), tuple(key_rows)


_BIAS_NP, _DIL_BIAS_IDS, _DIL_KEY_ROWS = _build_bias_tiles()
_pos = np.arange(CLASS_LEN)
_CLASS_BIAS_NP = np.where(_pos[None, :] >= _pos[:, None], 0.0, NEG_INF).astype(np.float32)

_NT = (((1,), (1,)), ((), ()))
_TN = (((0,), (1,)), ((), ()))
_TN_PLAIN = (((0,), (0,)), ((), ()))
COL_B = 3 * WIDTH_A
COL_M = COL_B + 3 * WIDTH_B
COL_GATES = COL_M + WIDTH_M
GATE_COLS = 3 * D_MODEL


def _rms(x, g):
    return x * lax.rsqrt(jnp.mean(x * x, axis=-1, keepdims=True) + RMS_EPS) * g


def _qkv_kernel(x_ref, g_ref, win_ref, c_ref, s1_ref, s2_ref, qka_ref, vta_ref, qkb_ref, vtb_ref, qm_ref, cls_ref,
                kmean_ref, *, layer):
    tm = x_ref.shape[0]
    halves = (slice(0, tm // 2), slice(tm // 2, tm))
    normed = [_rms(x_ref[rows, :], g_ref[layer:layer + 1, :]) for rows in halves]
    for s, rows in enumerate(halves):
        h = normed[s].astype(jnp.bfloat16)
        cos = c_ref[rows, :]
        sin_lo = s1_ref[rows, :]
        sin_hi = s2_ref[rows, :]
        for col0, qk_ref, vt_ref, width in ((0, qka_ref, vta_ref, WIDTH_A), (COL_B, qkb_ref, vtb_ref, WIDTH_B)):
            z = jnp.dot(h, win_ref[:, col0:col0 + 2 * width].astype(jnp.bfloat16),
                        preferred_element_type=jnp.float32)
            for blk in range(2 * width // LANES):
                zb = z[:, blk * LANES:(blk + 1) * LANES]
                rb = (zb * cos + pltpu.roll(zb, LANES - ROT_DIM // 2, 1) * sin_lo
                      + pltpu.roll(zb, ROT_DIM // 2, 1) * sin_hi)
                if blk < width // LANES:
                    rb = rb * (Q_SCALE * LOG2_E)
                qk_ref[rows, blk * LANES:(blk + 1) * LANES] = rb.astype(jnp.bfloat16)
                if col0 == 0 and blk % (width // LANES) == CLASS_GROUP:
                    part = blk // (width // LANES)
                    cls_ref[part, rows, :] = rb
                if col0 == COL_B and blk >= width // LANES:
                    kcol = (blk - width // LANES) * LANES
                    blocks = slice(rows.start // MOBA_BLOCK, rows.stop // MOBA_BLOCK)
                    kmean_ref[0, blocks, kcol:kcol + LANES] = jnp.mean(rb.reshape(-1, MOBA_BLOCK, LANES), axis=1)
            vt = lax.dot_general(win_ref[:, col0 + 2 * width:col0 + 3 * width].astype(jnp.bfloat16), h, _TN,
                                 preferred_element_type=jnp.float32)
            vt_ref[:, rows] = vt.astype(jnp.bfloat16)
            if col0 == 0:
                cls_ref[2, rows, :] = vt[CLASS_GROUP * PAIR:(CLASS_GROUP + 1) * PAIR, :].T
        qm = jnp.dot(h, win_ref[:, COL_M:COL_GATES].astype(jnp.bfloat16),
                     preferred_element_type=jnp.float32) * (Q_SCALE * LOG2_E)
        if s + 1 < len(halves):
            tail = qm.shape[0] - LSE_ROWS
            qm = jnp.concatenate([qm[:tail], qm[tail:] + _after(jnp.zeros((1, qm.shape[1]), jnp.float32),
                                                               normed[s + 1])], axis=0)
        qm_ref[rows, :] = qm.astype(jnp.bfloat16)


def _qkv_call(x2, g, w_in, layer, rope_c, rope_s1, rope_s2, batch, seq):
    t = x2.shape[0]
    tm = TM_QKV
    nts = seq // tm
    per_layer = lambda i: (layer, 0, 0)
    return pl.pallas_call(
        functools.partial(_qkv_kernel, layer=layer),
        grid=(t // tm,),
        in_specs=[
            pl.BlockSpec((tm, D_MODEL), lambda i: (i, 0)),
            pl.BlockSpec(g.shape, lambda i: (0, 0)),
            pl.BlockSpec((None, D_MODEL, COL_GATES), per_layer),
            pl.BlockSpec((tm, LANES), lambda i: (i % nts, 0)),
            pl.BlockSpec((tm, LANES), lambda i: (i % nts, 0)),
            pl.BlockSpec((tm, LANES), lambda i: (i % nts, 0)),
        ],
        out_specs=[
            pl.BlockSpec((tm, 2 * WIDTH_A), lambda i: (i, 0)),
            pl.BlockSpec((WIDTH_A, tm), lambda i: (i // nts, i % nts)),
            pl.BlockSpec((tm, 2 * WIDTH_B), lambda i: (i, 0)),
            pl.BlockSpec((WIDTH_B, tm), lambda i: (i // nts, i % nts)),
            pl.BlockSpec((tm, WIDTH_M), lambda i: (i, 0)),
            pl.BlockSpec((3, tm, PAIR), lambda i: (0, i, 0)),
            pl.BlockSpec((1, tm // MOBA_BLOCK, WIDTH_B), lambda i: (i, 0, 0)),
        ],
        out_shape=[
            jax.ShapeDtypeStruct((t, 2 * WIDTH_A), jnp.bfloat16),
            jax.ShapeDtypeStruct((batch * WIDTH_A, seq), jnp.bfloat16),
            jax.ShapeDtypeStruct((t, 2 * WIDTH_B), jnp.bfloat16),
            jax.ShapeDtypeStruct((batch * WIDTH_B, seq), jnp.bfloat16),
            jax.ShapeDtypeStruct((t, WIDTH_M), jnp.bfloat16),
            jax.ShapeDtypeStruct((3, t, PAIR), jnp.float32),
            jax.ShapeDtypeStruct((t // tm, tm // MOBA_BLOCK, WIDTH_B), jnp.float32),
        ],
        compiler_params=pltpu.CompilerParams(dimension_semantics=("parallel",), vmem_limit_bytes=VMEM_LIMIT),
        name="qkv_proj",
    )(x2, g, w_in, rope_c, rope_s1, rope_s2)


def _pv(v_heads, p_bf):
    ones = jnp.ones((SUM_ROWS, p_bf.shape[0]), jnp.bfloat16)
    nq = p_bf.shape[1] // len(v_heads)
    return jnp.concatenate([jnp.dot(jnp.concatenate([v, ones], axis=0), p_bf[:, h * nq:(h + 1) * nq],
                                    preferred_element_type=jnp.float32)
                            for h, v in enumerate(v_heads)], axis=1)


def _after(x, *matmul_results):
    for r in matmul_results:
        bits = lax.bitcast_convert_type(r[-1:, :x.shape[1]], jnp.uint32)
        bits = lax.shift_right_logical(lax.shift_right_logical(bits, jnp.uint32(16)), jnp.uint32(16))
        x = x + lax.bitcast_convert_type(bits, jnp.float32)
    return x


def _scores_t(k_tile, q_heads):
    return lax.dot_general(k_tile, q_heads, _NT, preferred_element_type=jnp.float32)


def _stack_heads(q, n_heads):
    lane = lax.broadcasted_iota(jnp.int32, q.shape, 1)
    zero = jnp.zeros((), q.dtype)
    return jnp.concatenate([jnp.where((lane >= h * HEAD_DIM) & (lane < (h + 1) * HEAD_DIM), q, zero)
                            for h in range(n_heads)], axis=0)


def _head_rows(first_head, n_heads):
    return [slice((first_head + h) * HEAD_DIM, (first_head + h + 1) * HEAD_DIM) for h in range(n_heads)]


def _lanes_to_rows(row, n_heads):
    nq = row.shape[1] // n_heads
    return jnp.concatenate([jnp.broadcast_to(row[:, h * nq:(h + 1) * nq], (HEAD_DIM, nq)) for h in range(n_heads)],
                           axis=0)


def _heads_to_rows(o_t, n_heads):
    nq = o_t.shape[1] // n_heads
    return jnp.concatenate([o_t[:, h * nq:(h + 1) * nq] for h in range(n_heads)], axis=0)


def _attn_kernel(qa_ref, ka_ref, vta_ref, qb_ref, kb_ref, vtb_ref, qm_ref, cls_ref, kmean_ref, mem_ref, gmem_ref,
                 wmkv_ref, bias_ref, cbias_ref, o_ref, lse_ref, ocls_ref, lsecls_ref,
                 km_ref, vmt_ref, selb_ref, m_ref, acc_ref, raw_ref, rawmax_ref, *, layer):
    qi = pl.program_id(1)

    @pl.when(qi == 0)
    def _per_batch():
        mem_n = _rms(mem_ref[...], gmem_ref[layer:layer + 1, :]).astype(jnp.bfloat16)
        w_mkv = wmkv_ref[...].astype(jnp.bfloat16)
        km_ref[...] = jnp.dot(mem_n, w_mkv[:, :WIDTH_M], preferred_element_type=jnp.float32).astype(jnp.bfloat16)
        vmt_ref[...] = lax.dot_general(w_mkv[:, WIDTH_M:], mem_n, _TN,
                                       preferred_element_type=jnp.float32).astype(jnp.bfloat16)

    def keys_of(j):
        return pl.ds(pl.multiple_of(j * TQ, TQ), TQ)

    def both_heads(bias):
        return jnp.concatenate([bias, bias], axis=1)

    def single_tile(s_t, v_heads):
        m = jnp.max(s_t, axis=0, keepdims=True)
        return m, _pv(v_heads, jnp.exp2(s_t - m).astype(jnp.bfloat16))

    for c in range(CLASSES_PER_TILE):
        rows = pl.ds(qi * CLASSES_PER_TILE + c, CLASS_LEN, stride=CLASS_STRIDE)
        q_heads = _stack_heads(cls_ref[0, rows, :].astype(jnp.bfloat16), HEADS_PER_DIL_GROUP)
        k_cls = cls_ref[1, rows, :].astype(jnp.bfloat16)
        v_t = cls_ref[2, rows, :].T.astype(jnp.bfloat16)
        m, acc = single_tile(_scores_t(k_cls, q_heads) + both_heads(cbias_ref[...]),
                             [v_t[r] for r in _head_rows(0, HEADS_PER_DIL_GROUP)])
        l = acc[HEAD_DIM:HEAD_DIM + 1]
        out = slice(c * CLASS_LEN, (c + 1) * CLASS_LEN)
        ocls_ref[out, :] = _heads_to_rows(acc[:HEAD_DIM] / l, HEADS_PER_DIL_GROUP).T
        lsecls_ref[out, :] = _lanes_to_rows(m + jnp.log2(l), HEADS_PER_DIL_GROUP).T

    a_static = []
    for g, offs in enumerate(_dil_tile_offsets()):
        cols = slice(g * PAIR, (g + 1) * PAIR)
        rows = _head_rows(g * HEADS_PER_DIL_GROUP, HEADS_PER_DIL_GROUP)
        q_heads = _stack_heads(qa_ref[:, cols], HEADS_PER_DIL_GROUP)
        ids = _DIL_BIAS_IDS[g]
        for o in offs:
            lo, hi = _DIL_KEY_ROWS[g][o]
            ks = pl.ds(pl.multiple_of(jnp.maximum(qi - o, 0) * TQ + lo, LANES), hi - lo)
            bid = ids[o] if o == 0 else jnp.where(qi >= o, ids[o], _B_ALLNEG)
            a_static.append((ks, (bid, slice(lo, hi)), cols, rows, q_heads))

    blk = lax.broadcasted_iota(jnp.int32, (N_BLOCKS, 2 * TQ), 0)
    own = keys_of(qi)
    b_heads = []
    for p in range(N_HEADS_B // 2):
        cols = slice(p * PAIR, (p + 1) * PAIR)
        q_heads = _stack_heads(qb_ref[:, cols], 2)
        kmean = jnp.concatenate([kmean_ref[t, :, cols] for t in range(kmean_ref.shape[0])], axis=0)
        kmean_hi = kmean.astype(jnp.bfloat16)
        kmean_lo = (kmean - kmean_hi.astype(jnp.float32)).astype(jnp.bfloat16)
        both = _scores_t(jnp.concatenate([kmean_hi, kmean_lo], axis=0), q_heads)
        gate = both[:N_BLOCKS] + both[N_BLOCKS:]
        gate = jnp.where(blk < qi, gate, NEG_INF)
        beaten = jnp.zeros(gate.shape, jnp.float32)
        for j in range(N_BLOCKS):
            gj = gate[j:j + 1, :]
            wins_tie = jnp.where(gj >= gate, 1.0, 0.0)
            wins_strict = jnp.where(gj > gate, 1.0, 0.0)
            beaten = beaten + jnp.where(blk > j, wins_tie, wins_strict)
        selb_ref[p] = jnp.where((beaten < MOBA_TOPK) & (blk < qi), 0.0, NEG_INF)
        b_heads.append((cols, _head_rows(2 * p, 2), q_heads))
    n_b = len(b_heads)

    def a_scores(part):
        ks, _, cols, _, q_heads = part
        return _scores_t(ka_ref[ks, cols], q_heads)

    def b_scores(p, ks):
        cols, _, q_heads = b_heads[p]
        return _scores_t(kb_ref[ks, cols], q_heads)

    n_a = len(a_static)
    raw = a_scores(a_static[0])
    a_biased = []
    m_a = None
    for i in range(n_a):
        nxt = a_scores(a_static[i + 1]) if i + 1 < n_a else b_scores(0, own)
        s_t = raw + both_heads(bias_ref[a_static[i][1]])
        part_max = _after(jnp.max(s_t, axis=0, keepdims=True), nxt)
        m_a = part_max if m_a is None else jnp.maximum(m_a, part_max)
        a_biased.append(s_t)
        raw = nxt
    m_heads = _stack_heads(qm_ref[...], N_HEADS_M)
    ahead = [lambda: b_scores(1, own), lambda: b_scores(2, own), lambda: _scores_t(km_ref[...], m_heads),
             lambda: b_scores(0, keys_of(0)), lambda: b_scores(1, keys_of(0))]
    last_first_past = b_scores(n_b - 1, keys_of(0))
    later = [raw]
    assert n_a >= len(ahead)
    neg_m = -m_a
    acc_a = None
    for i in range(n_a):
        ks, _, _, rows, _ = a_static[i]
        p_t = jnp.exp2(a_biased[i] + neg_m).astype(jnp.bfloat16)
        if i < len(ahead):
            later.append(ahead[i]())
            neg_m = _after(neg_m, later[-1])
        part_acc = _pv([vta_ref[r, ks] for r in rows], p_t)
        acc_a = part_acc if acc_a is None else acc_a + part_acc
    l_a = acc_a[HEAD_DIM:HEAD_DIM + 1]
    o_tiles = [acc_a[:HEAD_DIM] / l_a]
    lse_a = m_a + jnp.log2(l_a)
    lse_ref[...] = jnp.concatenate([jnp.broadcast_to(lse_a[:, h * TQ:(h + 1) * TQ], (LSE_ROWS // HEADS_PER_DIL_GROUP, TQ))
                                    for h in range(HEADS_PER_DIL_GROUP)], axis=0)

    for p, (_, rows, _) in enumerate(b_heads):
        m_ref[p], acc_ref[p] = single_tile(later[p] + both_heads(bias_ref[_B_CAUSAL]),
                                           [vtb_ref[r, own] for r in rows])
    _, acc_m = single_tile(later[3], [vmt_ref[r, :] for r in _head_rows(0, N_HEADS_M)])
    for c, first in enumerate((later[4], later[5], last_first_past)):
        raw_ref[c] = first
        rawmax_ref[c] = jnp.max(first, axis=0, keepdims=True)

    def past_tile(j, _):
        ks = keys_of(j)
        ks_next = keys_of(jnp.minimum(j + 1, qi - 1))
        scores = [raw_ref[c] for c in range(n_b)]
        for c in range(n_b):
            issued = b_scores(c, ks_next)
            scores.append(issued)
            sel = selb_ref[c, pl.ds(j, 1), :]
            m_old = m_ref[c]
            m_new = jnp.maximum(m_old, rawmax_ref[c] + sel)
            alpha = _after(jnp.exp2(m_old - m_new), issued)
            p_t = jnp.exp2(scores[c] + (sel - m_new)).astype(jnp.bfloat16)
            acc_ref[c] = alpha * acc_ref[c] + _pv([vtb_ref[r, ks] for r in b_heads[c][1]], p_t)
            m_ref[c] = m_new
        for c in range(n_b):
            raw_ref[c] = scores[n_b + c]
            rawmax_ref[c] = jnp.max(scores[n_b + c], axis=0, keepdims=True)
        return 0

    lax.fori_loop(0, qi, past_tile, 0)

    for acc in (*[acc_ref[c] for c in range(n_b)], acc_m):
        o_tiles.append(acc[:HEAD_DIM] / acc[HEAD_DIM:HEAD_DIM + 1])
    row = 0
    for o_t in o_tiles:
        for h in range(o_t.shape[1] // TQ):
            o_ref[row:row + HEAD_DIM, :] = o_t[:, h * TQ:(h + 1) * TQ].astype(jnp.bfloat16)
            row += HEAD_DIM


def _attn_call(qka, vta, qkb, vtb, qm, cls, kmean, mem2, gmem, wmkv, layer, bias, cbias, batch, seq):
    nq = seq // TQ
    n_mem = mem2.shape[0] // batch
    per_layer = lambda b, q: (layer, 0, 0)
    per_tile = lambda b, q: (b * nq + q, 0)
    return pl.pallas_call(
        functools.partial(_attn_kernel, layer=layer),
        grid=(batch, nq),
        in_specs=[
            pl.BlockSpec((TQ, WIDTH_A), lambda b, q: (b * nq + q, 0)),
            pl.BlockSpec((seq, WIDTH_A), lambda b, q: (b, 1)),
            pl.BlockSpec((WIDTH_A, seq), lambda b, q: (b, 0)),
            pl.BlockSpec((TQ, WIDTH_B), lambda b, q: (b * nq + q, 0)),
            pl.BlockSpec((seq, WIDTH_B), lambda b, q: (b, 1)),
            pl.BlockSpec((WIDTH_B, seq), lambda b, q: (b, 0)),
            pl.BlockSpec((TQ, WIDTH_M), lambda b, q: (b * nq + q, 0)),
            pl.BlockSpec((3, seq, PAIR), lambda b, q: (0, b, 0)),
            pl.BlockSpec((kmean.shape[0] // batch,) + kmean.shape[1:], lambda b, q: (b, 0, 0)),
            pl.BlockSpec((n_mem, D_MODEL), lambda b, q: (b, 0)),
            pl.BlockSpec(gmem.shape, lambda b, q: (0, 0)),
            pl.BlockSpec((None, D_MODEL, 2 * WIDTH_M), per_layer),
            pl.BlockSpec(bias.shape, lambda b, q: (0, 0, 0)),
            pl.BlockSpec(cbias.shape, lambda b, q: (0, 0)),
        ],
        out_specs=[pl.BlockSpec((WIDTH_O, TQ), lambda b, q: (0, b * nq + q)),
                   pl.BlockSpec((LSE_ROWS, TQ), lambda b, q: (0, b * nq + q)),
                   pl.BlockSpec((TQ, PAIR), per_tile), pl.BlockSpec((TQ, PAIR), per_tile)],
        out_shape=[jax.ShapeDtypeStruct((WIDTH_O, batch * seq), jnp.bfloat16),
                   jax.ShapeDtypeStruct((LSE_ROWS, batch * seq), jnp.float32),
                   jax.ShapeDtypeStruct((batch * seq, PAIR), jnp.float32),
                   jax.ShapeDtypeStruct((batch * seq, PAIR), jnp.float32)],
        scratch_shapes=[
            pltpu.VMEM((n_mem, WIDTH_M), jnp.bfloat16),
            pltpu.VMEM((WIDTH_M, n_mem), jnp.bfloat16),
            pltpu.VMEM((N_HEADS_B // 2, N_BLOCKS, 2 * TQ), jnp.float32),
            pltpu.VMEM((N_HEADS_B // 2, 1, 2 * TQ), jnp.float32),
            pltpu.VMEM((N_HEADS_B // 2, HEAD_DIM + SUM_ROWS, 2 * TQ), jnp.float32),
            pltpu.VMEM((N_HEADS_B // 2, TQ, 2 * TQ), jnp.float32),
            pltpu.VMEM((N_HEADS_B // 2, 1, 2 * TQ), jnp.float32),
        ],
        compiler_params=pltpu.CompilerParams(dimension_semantics=("parallel", "arbitrary"),
                                             vmem_limit_bytes=VMEM_LIMIT),
        name="attn",
    )(qka, qka, vta, qkb, qkb, vtb, qm, cls, kmean, mem2, gmem, wmkv, bias, cbias)


def _mix_kernel(x_ref, o_ref, lse_ref, ocls_ref, lsecls_ref, g_ref, wg_ref, wpa_ref, wpb_ref, wpm_ref, wo_ref,
                out_ref, ocn_ref, lcn_ref, *, layer):
    tm = x_ref.shape[0]
    per_class = tm // CLASS_STRIDE
    first = (pl.program_id(0) % (SEQ // tm)) * per_class
    for r in range(CLASS_STRIDE):
        src = pl.ds(pl.multiple_of(r * CLASS_LEN + first, per_class), per_class)
        dst = pl.ds(r, per_class, stride=CLASS_STRIDE)
        ocn_ref[dst, :] = ocls_ref[src, :]
        lcn_ref[dst, :] = lsecls_ref[src, :]
    lse_t = lse_ref[...]
    per_head = LSE_ROWS // HEADS_PER_DIL_GROUP
    lse_tiled = jnp.concatenate([jnp.broadcast_to(lse_t[h * per_head:h * per_head + 1], (HEAD_DIM, tm))
                                 for h in range(HEADS_PER_DIL_GROUP)], axis=0)
    w_cls = 1.0 / (1.0 + jnp.exp2(lse_tiled - lcn_ref[...].T))
    o_tiled = o_ref[:WIDTH_A_OUT, :].astype(jnp.float32)
    o_a = (o_tiled + w_cls * (ocn_ref[...].T - o_tiled)).astype(jnp.bfloat16)
    half = tm // 2
    halves = [slice(0, half), slice(half, tm)]
    xs = [x_ref[rows, :] for rows in halves]
    hs = [_rms(x, g_ref[layer:layer + 1, :]).astype(jnp.bfloat16) for x in xs]

    def gate_scores(s, i):
        return jnp.dot(hs[s], wg_ref[0, :, i * D_MODEL:(i + 1) * D_MODEL].astype(jnp.bfloat16),
                       preferred_element_type=jnp.float32)

    def wait_for(y, matmul_result):
        tail = y.shape[0] - LSE_ROWS
        return jnp.concatenate([y[:tail], y[tail:] + (_after(jnp.zeros((1, y.shape[1]), jnp.float32), matmul_result))],
                               axis=0)

    def gated(s, first_gate):
        y = None
        col = 0
        for i, wp_ref in enumerate((wpa_ref, wpb_ref, wpm_ref)):
            width = wp_ref.shape[0]
            gate = jax.nn.sigmoid(first_gate if i == 0 else gate_scores(s, i))
            o_i = o_a[:, halves[s]] if i == 0 else o_ref[col:col + width, halves[s]]
            branch = gate * lax.dot_general(o_i, wp_ref[...].astype(jnp.bfloat16), _TN_PLAIN,
                                            preferred_element_type=jnp.float32)
            y = branch if y is None else y + branch
            col += width
        return y

    def project(y):
        return jnp.dot(y.astype(jnp.bfloat16), wo_ref[...].astype(jnp.bfloat16), preferred_element_type=jnp.float32)

    g0_a = gate_scores(0, 0)
    g0_b = gate_scores(1, 0)
    y_a = wait_for(gated(0, g0_a), g0_b)
    out_a = project(y_a)
    y_b = wait_for(gated(1, g0_b), out_a)
    out_ref[halves[0], :] = xs[0] + out_a
    out_ref[halves[1], :] = xs[1] + project(y_b)


def _mix_call(x2, o, lse, ocls, lsecls, g, w_in, wpa, wpb, wpm, wo, layer):
    t = x2.shape[0]
    tm = TM_MIX
    per_layer = lambda i: (layer, 0, 0)
    return pl.pallas_call(
        functools.partial(_mix_kernel, layer=layer),
        grid=(t // tm,),
        in_specs=[
            pl.BlockSpec((tm, D_MODEL), lambda i: (i, 0)),
            pl.BlockSpec((WIDTH_O, tm), lambda i: (0, i)),
            pl.BlockSpec((LSE_ROWS, tm), lambda i: (0, i)),
            pl.BlockSpec((SEQ, PAIR), lambda i: (i // (SEQ // tm), 0)),
            pl.BlockSpec((SEQ, PAIR), lambda i: (i // (SEQ // tm), 0)),
            pl.BlockSpec(g.shape, lambda i: (0, 0)),
            pl.BlockSpec((pl.Element(1), pl.Element(D_MODEL), pl.Element(GATE_COLS)), lambda i: (layer, 0, COL_GATES),
                         pipeline_mode=pl.Buffered(1)),
            pl.BlockSpec((None,) + wpa.shape[1:], per_layer, pipeline_mode=pl.Buffered(1)),
            pl.BlockSpec((None,) + wpb.shape[1:], per_layer, pipeline_mode=pl.Buffered(1)),
            pl.BlockSpec((None,) + wpm.shape[1:], per_layer, pipeline_mode=pl.Buffered(1)),
            pl.BlockSpec((None,) + wo.shape[1:], per_layer, pipeline_mode=pl.Buffered(1)),
        ],
        out_specs=pl.BlockSpec((tm, D_MODEL), lambda i: (i, 0)),
        out_shape=jax.ShapeDtypeStruct((t, D_MODEL), jnp.float32),
        scratch_shapes=[pltpu.VMEM((tm, PAIR), jnp.float32), pltpu.VMEM((tm, PAIR), jnp.float32)],
        compiler_params=pltpu.CompilerParams(dimension_semantics=("parallel",), vmem_limit_bytes=VMEM_LIMIT),
        name="gated_mix",
    )(x2, o, lse, ocls, lsecls, g, w_in, wpa, wpb, wpm, wo)


def _mlp_kernel(x_ref, g_ref, wup_hbm, wdown_hbm, gfin_ref, out_ref, wup_ref, wdown_ref, sems, *, layer, final_norm):
    chunks = [pl.ds(c * FF_CHUNK, FF_CHUNK) for c in range(D_FF // FF_CHUNK)]

    def up_copy(c):
        return pltpu.make_async_copy(wup_hbm.at[layer, :, chunks[c]], wup_ref.at[:, chunks[c]], sems.at[0, c])

    def down_copy(c):
        return pltpu.make_async_copy(wdown_hbm.at[layer, chunks[c], :], wdown_ref.at[chunks[c], :], sems.at[1, c])

    first_step = pl.program_id(0) == 0

    @pl.when(first_step)
    def _start_weight_stream():
        for c in range(len(chunks)):
            up_copy(c).start()
            down_copy(c).start()

    x = x_ref[...]
    hm = _rms(x, g_ref[layer:layer + 1, :]).astype(jnp.bfloat16)
    acc = x
    for c in range(len(chunks)):
        pl.when(first_step)(lambda c=c: up_copy(c).wait())
        u = jnp.dot(hm, wup_ref[:, c * FF_CHUNK:(c + 1) * FF_CHUNK].astype(jnp.bfloat16),
                    preferred_element_type=jnp.float32)
        u = jnp.square(jnp.maximum(u, 0.0)).astype(jnp.bfloat16)
        pl.when(first_step)(lambda c=c: down_copy(c).wait())
        acc = acc + jnp.dot(u, wdown_ref[c * FF_CHUNK:(c + 1) * FF_CHUNK, :].astype(jnp.bfloat16),
                            preferred_element_type=jnp.float32)
    out_ref[...] = _rms(acc, gfin_ref[...]) if final_norm else acc


def _mlp_call(x2, g, wup, wdown, gfin, layer, final_norm):
    t = x2.shape[0]
    tm = TM_MLP
    const = lambda i: (0, 0)
    per_layer = lambda i: (layer, 0, 0)
    return pl.pallas_call(
        functools.partial(_mlp_kernel, layer=layer, final_norm=final_norm),
        grid=(t // tm,),
        in_specs=[
            pl.BlockSpec((tm, D_MODEL), lambda i: (i, 0)),
            pl.BlockSpec(g.shape, const),
            pl.BlockSpec(memory_space=pl.ANY),
            pl.BlockSpec(memory_space=pl.ANY),
            pl.BlockSpec((1, D_MODEL), const),
        ],
        out_specs=pl.BlockSpec((tm, D_MODEL), lambda i: (i, 0)),
        out_shape=jax.ShapeDtypeStruct((t, D_MODEL), jnp.float32),
        scratch_shapes=[pltpu.VMEM(wup.shape[1:], jnp.float32), pltpu.VMEM(wdown.shape[1:], jnp.float32),
                        pltpu.SemaphoreType.DMA((2, D_FF // FF_CHUNK))],
        compiler_params=pltpu.CompilerParams(dimension_semantics=("arbitrary",), vmem_limit_bytes=VMEM_LIMIT),
        name="mlp",
    )(x2, g, wup, wdown, gfin)


def _rope_tables(seq):
    half = ROT_DIM // 2
    inv_freq = (1.0 / (np.float32(ROPE_THETA) ** (np.arange(0, ROT_DIM, 2, dtype=np.float32) / np.float32(ROT_DIM))))
    ang = np.arange(seq, dtype=np.float32)[:, None] * inv_freq.astype(np.float32)[None, :]
    cos, sin = np.cos(ang).astype(np.float32), np.sin(ang).astype(np.float32)
    ones = np.ones((seq, HEAD_DIM - ROT_DIM), np.float32)
    zeros_half = np.zeros((seq, half), np.float32)
    zeros_rest = np.zeros((seq, HEAD_DIM - ROT_DIM), np.float32)
    c_head = np.concatenate([cos, cos, ones], axis=1)
    lo_head = np.concatenate([-sin, zeros_half, zeros_rest], axis=1)
    hi_head = np.concatenate([zeros_half, sin, zeros_rest], axis=1)
    rep = LANES // HEAD_DIM
    return tuple(jnp.asarray(np.tile(t, (1, rep))) for t in (c_head, lo_head, hi_head))


def kernel(x, mem, norm_mix, w_in, w_proj_a, w_proj_b, w_proj_m, w_out, norm_mem, w_mem_kv, norm_mlp, w_up,
           w_down, norm_final):
    batch, seq, d = x.shape
    assert d == D_MODEL and seq == SEQ and seq % TM_QKV == 0
    depth = w_in.shape[0]
    x2 = x.reshape(batch * seq, d)
    mem2 = mem.reshape(batch * mem.shape[1], d)
    rope_c, rope_s1, rope_s2 = _rope_tables(seq)
    bias = jnp.asarray(_BIAS_NP)
    cbias = jnp.asarray(_CLASS_BIAS_NP)
    for l in range(depth):
        qka, vta, qkb, vtb, qm, cls, kmean = _qkv_call(x2, norm_mix, w_in, l, rope_c, rope_s1, rope_s2, batch, seq)
        o, lse, ocls, lsecls = _attn_call(qka, vta, qkb, vtb, qm, cls, kmean, mem2, norm_mem,
                                          w_mem_kv, l, bias, cbias, batch, seq)
        x2 = _mix_call(x2, o, lse, ocls, lsecls, norm_mix, w_in, w_proj_a, w_proj_b, w_proj_m, w_out, l)
        x2 = _mlp_call(x2, norm_mlp, w_up, w_down, norm_final.reshape(1, d), l, final_norm=(l == depth - 1))
    return x2.reshape(batch, seq, d)
```

```python
import functools

import jax
import jax.numpy as jnp
import numpy as np
from jax import lax
from jax.experimental import pallas as pl
from jax.experimental.pallas import tpu as pltpu

D_MODEL = 1024
SEQ = 2048
HEAD_DIM = 64
ROT_DIM = HEAD_DIM // 4
ROPE_THETA = 500000.0
DIL_GROUPS = ((128, 1), (512, 4), (2048, 16))
HEADS_PER_DIL_GROUP = 2
N_HEADS_A = len(DIL_GROUPS) * HEADS_PER_DIL_GROUP
N_HEADS_B = 6
N_HEADS_M = 4
MOBA_BLOCK = 256
MOBA_TOPK = 3
N_BLOCKS = SEQ // MOBA_BLOCK
D_FF = 4 * D_MODEL
WIDTH_A = N_HEADS_A * HEAD_DIM
WIDTH_A_OUT = HEADS_PER_DIL_GROUP * HEAD_DIM
WIDTH_B = N_HEADS_B * HEAD_DIM
WIDTH_M = N_HEADS_M * HEAD_DIM
WIDTH_O = WIDTH_A_OUT + WIDTH_B + WIDTH_M
RMS_EPS = 1e-6
NEG_INF = -1e30
Q_SCALE = HEAD_DIM ** -0.5
LOG2_E = 1.4426950408889634

LANES = 128
LSE_ROWS = 8
PAIR = 2 * HEAD_DIM
SUM_ROWS = 16
TQ = MOBA_BLOCK
CLASS_GROUP = len(DIL_GROUPS) - 1
CLASS_STRIDE = DIL_GROUPS[CLASS_GROUP][1]
CLASS_LEN = SEQ // CLASS_STRIDE
CLASSES_PER_TILE = TQ // CLASS_LEN
TM_QKV = 1024
TM_MIX = 1024
TM_MLP = 512
FF_CHUNK = 1024
STAGE_SLOTS = 2
VMEM_LIMIT = 60 * 1024 * 1024

_B_ALLNEG = 0
_B_CAUSAL = 1


TILE_GROUPS = DIL_GROUPS[:CLASS_GROUP]
assert DIL_GROUPS[CLASS_GROUP][0] >= SEQ and TQ % CLASS_LEN == 0


def _dil_tile_offsets():
    return tuple(tuple(range(min((w + TQ - 1) // TQ, N_BLOCKS - 1) + 1)) for w, _ in TILE_GROUPS)


def _build_bias_tiles():
    c = np.arange(TQ)[:, None]
    r = np.arange(TQ)[None, :]
    tiles = [np.zeros((TQ, TQ), bool), (r - c) >= 0]
    ids = []
    key_rows = []
    for (w, d), offs in zip(TILE_GROUPS, _dil_tile_offsets()):
        per_off = []
        for o in offs:
            diff = o * TQ + r - c
            per_off.append((diff >= 0) & (diff <= w) & (diff % d == 0))
        uniq, gid = [], []
        for t in per_off:
            for k, u in enumerate(uniq):
                if np.array_equal(t, u):
                    gid.append(k)
                    break
            else:
                uniq.append(t)
                gid.append(len(uniq) - 1)
        ids.append(tuple(len(tiles) + k for k in gid))
        tiles.extend(uniq)
        used = [np.flatnonzero(t.any(axis=1)) for t in per_off]
        key_rows.append(tuple((int(u.min()) // LANES * LANES, -(-(int(u.max()) + 1) // LANES) * LANES) for u in used))
    bias = np.where(np.stack(tiles), 0.0, NEG_INF).astype(np.float32)
    return bias, tuple(ids), tuple(key_rows)


_BIAS_NP, _DIL_BIAS_IDS, _DIL_KEY_ROWS = _build_bias_tiles()
_pos = np.arange(CLASS_LEN)
_CLASS_BIAS_NP = np.where(_pos[None, :] >= _pos[:, None], 0.0, NEG_INF).astype(np.float32)

_NT = (((1,), (1,)), ((), ()))
_TN = (((0,), (1,)), ((), ()))
_TN_PLAIN = (((0,), (0,)), ((), ()))
COL_B = 3 * WIDTH_A
COL_M = COL_B + 3 * WIDTH_B
COL_GATES = COL_M + WIDTH_M
GATE_COLS = 3 * D_MODEL


def _rms(x, g):
    return x * lax.rsqrt(jnp.mean(x * x, axis=-1, keepdims=True) + RMS_EPS) * g


def _qkv_kernel(x_ref, g_ref, win_ref, c_ref, s1_ref, s2_ref, qka_ref, vta_ref, qkb_ref, vtb_ref, qm_ref, cls_ref,
                kmean_ref, *, layer):
    tm = x_ref.shape[0]
    halves = (slice(0, tm // 2), slice(tm // 2, tm))
    normed = [_rms(x_ref[rows, :], g_ref[layer:layer + 1, :]) for rows in halves]
    for s, rows in enumerate(halves):
        h = normed[s].astype(jnp.bfloat16)
        cos = c_ref[rows, :]
        sin_lo = s1_ref[rows, :]
        sin_hi = s2_ref[rows, :]
        for col0, qk_ref, vt_ref, width in ((0, qka_ref, vta_ref, WIDTH_A), (COL_B, qkb_ref, vtb_ref, WIDTH_B)):
            z = jnp.dot(h, win_ref[:, col0:col0 + 2 * width].astype(jnp.bfloat16),
                        preferred_element_type=jnp.float32)
            for blk in range(2 * width // LANES):
                zb = z[:, blk * LANES:(blk + 1) * LANES]
                rb = (zb * cos + pltpu.roll(zb, LANES - ROT_DIM // 2, 1) * sin_lo
                      + pltpu.roll(zb, ROT_DIM // 2, 1) * sin_hi)
                if blk < width // LANES:
                    rb = rb * (Q_SCALE * LOG2_E)
                qk_ref[rows, blk * LANES:(blk + 1) * LANES] = rb.astype(jnp.bfloat16)
                if col0 == 0 and blk % (width // LANES) == CLASS_GROUP:
                    part = blk // (width // LANES)
                    cls_ref[part, rows, :] = rb
                if col0 == COL_B and blk >= width // LANES:
                    kcol = (blk - width // LANES) * LANES
                    blocks = slice(rows.start // MOBA_BLOCK, rows.stop // MOBA_BLOCK)
                    kmean_ref[0, blocks, kcol:kcol + LANES] = jnp.mean(rb.reshape(-1, MOBA_BLOCK, LANES), axis=1)
            vt = lax.dot_general(win_ref[:, col0 + 2 * width:col0 + 3 * width].astype(jnp.bfloat16), h, _TN,
                                 preferred_element_type=jnp.float32)
            vt_ref[:, rows] = vt.astype(jnp.bfloat16)
            if col0 == 0:
                cls_ref[2, rows, :] = vt[CLASS_GROUP * PAIR:(CLASS_GROUP + 1) * PAIR, :].T
        qm = jnp.dot(h, win_ref[:, COL_M:COL_GATES].astype(jnp.bfloat16),
                     preferred_element_type=jnp.float32) * (Q_SCALE * LOG2_E)
        if s + 1 < len(halves):
            tail = qm.shape[0] - LSE_ROWS
            qm = jnp.concatenate([qm[:tail], qm[tail:] + _after(jnp.zeros((1, qm.shape[1]), jnp.float32),
                                                               normed[s + 1])], axis=0)
        qm_ref[rows, :] = qm.astype(jnp.bfloat16)


def _qkv_call(x2, g, w_in, layer, rope_c, rope_s1, rope_s2, batch, seq):
    t = x2.shape[0]
    tm = TM_QKV
    nts = seq // tm
    per_layer = lambda i: (layer, 0, 0)
    return pl.pallas_call(
        functools.partial(_qkv_kernel, layer=layer),
        grid=(t // tm,),
        in_specs=[
            pl.BlockSpec((tm, D_MODEL), lambda i: (i, 0)),
            pl.BlockSpec(g.shape, lambda i: (0, 0)),
            pl.BlockSpec((None, D_MODEL, COL_GATES), per_layer),
            pl.BlockSpec((tm, LANES), lambda i: (i % nts, 0)),
            pl.BlockSpec((tm, LANES), lambda i: (i % nts, 0)),
            pl.BlockSpec((tm, LANES), lambda i: (i % nts, 0)),
        ],
        out_specs=[
            pl.BlockSpec((tm, 2 * WIDTH_A), lambda i: (i, 0)),
            pl.BlockSpec((WIDTH_A, tm), lambda i: (i // nts, i % nts)),
            pl.BlockSpec((tm, 2 * WIDTH_B), lambda i: (i, 0)),
            pl.BlockSpec((WIDTH_B, tm), lambda i: (i // nts, i % nts)),
            pl.BlockSpec((tm, WIDTH_M), lambda i: (i, 0)),
            pl.BlockSpec((3, tm, PAIR), lambda i: (0, i, 0)),
            pl.BlockSpec((1, tm // MOBA_BLOCK, WIDTH_B), lambda i: (i, 0, 0)),
        ],
        out_shape=[
            jax.ShapeDtypeStruct((t, 2 * WIDTH_A), jnp.bfloat16),
            jax.ShapeDtypeStruct((batch * WIDTH_A, seq), jnp.bfloat16),
            jax.ShapeDtypeStruct((t, 2 * WIDTH_B), jnp.bfloat16),
            jax.ShapeDtypeStruct((batch * WIDTH_B, seq), jnp.bfloat16),
            jax.ShapeDtypeStruct((t, WIDTH_M), jnp.bfloat16),
            jax.ShapeDtypeStruct((3, t, PAIR), jnp.float32),
            jax.ShapeDtypeStruct((t // tm, tm // MOBA_BLOCK, WIDTH_B), jnp.float32),
        ],
        compiler_params=pltpu.CompilerParams(dimension_semantics=("parallel",), vmem_limit_bytes=VMEM_LIMIT),
        name="qkv_proj",
    )(x2, g, w_in, rope_c, rope_s1, rope_s2)


def _pv(v_heads, p_bf):
    ones = jnp.ones((SUM_ROWS, p_bf.shape[0]), jnp.bfloat16)
    nq = p_bf.shape[1] // len(v_heads)
    return jnp.concatenate([jnp.dot(jnp.concatenate([v, ones], axis=0), p_bf[:, h * nq:(h + 1) * nq],
                                    preferred_element_type=jnp.float32)
                            for h, v in enumerate(v_heads)], axis=1)


def _after(x, *matmul_results):
    for r in matmul_results:
        bits = lax.bitcast_convert_type(r[-1:, :x.shape[1]], jnp.uint32)
        bits = lax.shift_right_logical(lax.shift_right_logical(bits, jnp.uint32(16)), jnp.uint32(16))
        x = x + lax.bitcast_convert_type(bits, jnp.float32)
    return x


def _scores_t(k_tile, q_heads):
    return lax.dot_general(k_tile, q_heads, _NT, preferred_element_type=jnp.float32)


def _stack_heads(q, n_heads):
    lane = lax.broadcasted_iota(jnp.int32, q.shape, 1)
    zero = jnp.zeros((), q.dtype)
    return jnp.concatenate([jnp.where((lane >= h * HEAD_DIM) & (lane < (h + 1) * HEAD_DIM), q, zero)
                            for h in range(n_heads)], axis=0)


def _head_rows(first_head, n_heads):
    return [slice((first_head + h) * HEAD_DIM, (first_head + h + 1) * HEAD_DIM) for h in range(n_heads)]


def _lanes_to_rows(row, n_heads):
    nq = row.shape[1] // n_heads
    return jnp.concatenate([jnp.broadcast_to(row[:, h * nq:(h + 1) * nq], (HEAD_DIM, nq)) for h in range(n_heads)],
                           axis=0)


def _heads_to_rows(o_t, n_heads):
    nq = o_t.shape[1] // n_heads
    return jnp.concatenate([o_t[:, h * nq:(h + 1) * nq] for h in range(n_heads)], axis=0)


def _attn_kernel(qa_ref, ka_ref, vta_ref, qb_ref, kb_ref, vtb_ref, qm_ref, cls_ref, kmean_ref, mem_ref, gmem_ref,
                 wmkv_ref, bias_ref, cbias_ref, o_ref, lse_ref, ocls_ref, lsecls_ref,
                 km_ref, vmt_ref, selb_ref, m_ref, acc_ref, raw_ref, rawmax_ref, *, layer):
    qi = pl.program_id(1)

    @pl.when(qi == 0)
    def _per_batch():
        mem_n = _rms(mem_ref[...], gmem_ref[layer:layer + 1, :]).astype(jnp.bfloat16)
        w_mkv = wmkv_ref[...].astype(jnp.bfloat16)
        km_ref[...] = jnp.dot(mem_n, w_mkv[:, :WIDTH_M], preferred_element_type=jnp.float32).astype(jnp.bfloat16)
        vmt_ref[...] = lax.dot_general(w_mkv[:, WIDTH_M:], mem_n, _TN,
                                       preferred_element_type=jnp.float32).astype(jnp.bfloat16)

    def keys_of(j):
        return pl.ds(pl.multiple_of(j * TQ, TQ), TQ)

    def both_heads(bias):
        return jnp.concatenate([bias, bias], axis=1)

    def single_tile(s_t, v_heads):
        m = jnp.max(s_t, axis=0, keepdims=True)
        return m, _pv(v_heads, jnp.exp2(s_t - m).astype(jnp.bfloat16))

    for c in range(CLASSES_PER_TILE):
        rows = pl.ds(qi * CLASSES_PER_TILE + c, CLASS_LEN, stride=CLASS_STRIDE)
        q_heads = _stack_heads(cls_ref[0, rows, :].astype(jnp.bfloat16), HEADS_PER_DIL_GROUP)
        k_cls = cls_ref[1, rows, :].astype(jnp.bfloat16)
        v_t = cls_ref[2, rows, :].T.astype(jnp.bfloat16)
        m, acc = single_tile(_scores_t(k_cls, q_heads) + both_heads(cbias_ref[...]),
                             [v_t[r] for r in _head_rows(0, HEADS_PER_DIL_GROUP)])
        l = acc[HEAD_DIM:HEAD_DIM + 1]
        out = slice(c * CLASS_LEN, (c + 1) * CLASS_LEN)
        ocls_ref[out, :] = _heads_to_rows(acc[:HEAD_DIM] / l, HEADS_PER_DIL_GROUP).T
        lsecls_ref[out, :] = _lanes_to_rows(m + jnp.log2(l), HEADS_PER_DIL_GROUP).T

    a_static = []
    for g, offs in enumerate(_dil_tile_offsets()):
        cols = slice(g * PAIR, (g + 1) * PAIR)
        rows = _head_rows(g * HEADS_PER_DIL_GROUP, HEADS_PER_DIL_GROUP)
        q_heads = _stack_heads(qa_ref[:, cols], HEADS_PER_DIL_GROUP)
        ids = _DIL_BIAS_IDS[g]
        for o in offs:
            lo, hi = _DIL_KEY_ROWS[g][o]
            ks = pl.ds(pl.multiple_of(jnp.maximum(qi - o, 0) * TQ + lo, LANES), hi - lo)
            bid = ids[o] if o == 0 else jnp.where(qi >= o, ids[o], _B_ALLNEG)
            a_static.append((ks, (bid, slice(lo, hi)), cols, rows, q_heads))

    blk = lax.broadcasted_iota(jnp.int32, (N_BLOCKS, 2 * TQ), 0)
    own = keys_of(qi)
    b_heads = []
    for p in range(N_HEADS_B // 2):
        cols = slice(p * PAIR, (p + 1) * PAIR)
        q_heads = _stack_heads(qb_ref[:, cols], 2)
        kmean = jnp.concatenate([kmean_ref[t, :, cols] for t in range(kmean_ref.shape[0])], axis=0)
        kmean_hi = kmean.astype(jnp.bfloat16)
        kmean_lo = (kmean - kmean_hi.astype(jnp.float32)).astype(jnp.bfloat16)
        both = _scores_t(jnp.concatenate([kmean_hi, kmean_lo], axis=0), q_heads)
        gate = both[:N_BLOCKS] + both[N_BLOCKS:]
        gate = jnp.where(blk < qi, gate, NEG_INF)
        beaten = jnp.zeros(gate.shape, jnp.float32)
        for j in range(N_BLOCKS):
            gj = gate[j:j + 1, :]
            wins_tie = jnp.where(gj >= gate, 1.0, 0.0)
            wins_strict = jnp.where(gj > gate, 1.0, 0.0)
            beaten = beaten + jnp.where(blk > j, wins_tie, wins_strict)
        selb_ref[p] = jnp.where((beaten < MOBA_TOPK) & (blk < qi), 0.0, NEG_INF)
        b_heads.append((cols, _head_rows(2 * p, 2), q_heads))
    n_b = len(b_heads)

    def a_scores(part):
        ks, _, cols, _, q_heads = part
        return _scores_t(ka_ref[ks, cols], q_heads)

    def b_scores(p, ks):
        cols, _, q_heads = b_heads[p]
        return _scores_t(kb_ref[ks, cols], q_heads)

    n_a = len(a_static)
    raw = a_scores(a_static[0])
    a_biased = []
    m_a = None
    for i in range(n_a):
        nxt = a_scores(a_static[i + 1]) if i + 1 < n_a else b_scores(0, own)
        s_t = raw + both_heads(bias_ref[a_static[i][1]])
        part_max = _after(jnp.max(s_t, axis=0, keepdims=True), nxt)
        m_a = part_max if m_a is None else jnp.maximum(m_a, part_max)
        a_biased.append(s_t)
        raw = nxt
    m_heads = _stack_heads(qm_ref[...], N_HEADS_M)
    ahead = [lambda: b_scores(1, own), lambda: b_scores(2, own), lambda: _scores_t(km_ref[...], m_heads),
             lambda: b_scores(0, keys_of(0)), lambda: b_scores(1, keys_of(0))]
    last_first_past = b_scores(n_b - 1, keys_of(0))
    later = [raw]
    assert n_a >= len(ahead)
    neg_m = -m_a
    acc_a = None
    for i in range(n_a):
        ks, _, _, rows, _ = a_static[i]
        p_t = jnp.exp2(a_biased[i] + neg_m).astype(jnp.bfloat16)
        if i < len(ahead):
            later.append(ahead[i]())
            neg_m = _after(neg_m, later[-1])
        part_acc = _pv([vta_ref[r, ks] for r in rows], p_t)
        acc_a = part_acc if acc_a is None else acc_a + part_acc
    l_a = acc_a[HEAD_DIM:HEAD_DIM + 1]
    o_tiles = [acc_a[:HEAD_DIM] / l_a]
    lse_a = m_a + jnp.log2(l_a)
    lse_ref[...] = jnp.concatenate([jnp.broadcast_to(lse_a[:, h * TQ:(h + 1) * TQ], (LSE_ROWS // HEADS_PER_DIL_GROUP, TQ))
                                    for h in range(HEADS_PER_DIL_GROUP)], axis=0)

    for p, (_, rows, _) in enumerate(b_heads):
        m_ref[p], acc_ref[p] = single_tile(later[p] + both_heads(bias_ref[_B_CAUSAL]),
                                           [vtb_ref[r, own] for r in rows])
    _, acc_m = single_tile(later[3], [vmt_ref[r, :] for r in _head_rows(0, N_HEADS_M)])
    for c, first in enumerate((later[4], later[5], last_first_past)):
        raw_ref[c] = first
        rawmax_ref[c] = jnp.max(first, axis=0, keepdims=True)

    def past_tile(j, _):
        ks = keys_of(j)
        ks_next = keys_of(jnp.minimum(j + 1, qi - 1))
        scores = [raw_ref[c] for c in range(n_b)]
        for c in range(n_b):
            issued = b_scores(c, ks_next)
            scores.append(issued)
            sel = selb_ref[c, pl.ds(j, 1), :]
            m_old = m_ref[c]
            m_new = jnp.maximum(m_old, rawmax_ref[c] + sel)
            alpha = _after(jnp.exp2(m_old - m_new), issued)
            p_t = jnp.exp2(scores[c] + (sel - m_new)).astype(jnp.bfloat16)
            acc_ref[c] = alpha * acc_ref[c] + _pv([vtb_ref[r, ks] for r in b_heads[c][1]], p_t)
            m_ref[c] = m_new
        for c in range(n_b):
            raw_ref[c] = scores[n_b + c]
            rawmax_ref[c] = jnp.max(scores[n_b + c], axis=0, keepdims=True)
        return 0

    lax.fori_loop(0, qi, past_tile, 0)

    for acc in (*[acc_ref[c] for c in range(n_b)], acc_m):
        o_tiles.append(acc[:HEAD_DIM] / acc[HEAD_DIM:HEAD_DIM + 1])
    row = 0
    for o_t in o_tiles:
        for h in range(o_t.shape[1] // TQ):
            o_ref[row:row + HEAD_DIM, :] = o_t[:, h * TQ:(h + 1) * TQ].astype(jnp.bfloat16)
            row += HEAD_DIM


def _attn_call(qka, vta, qkb, vtb, qm, cls, kmean, mem2, gmem, wmkv, layer, bias, cbias, batch, seq):
    nq = seq // TQ
    n_mem = mem2.shape[0] // batch
    per_layer = lambda b, q: (layer, 0, 0)
    per_tile = lambda b, q: (b * nq + q, 0)
    return pl.pallas_call(
        functools.partial(_attn_kernel, layer=layer),
        grid=(batch, nq),
        in_specs=[
            pl.BlockSpec((TQ, WIDTH_A), lambda b, q: (b * nq + q, 0)),
            pl.BlockSpec((seq, WIDTH_A), lambda b, q: (b, 1)),
            pl.BlockSpec((WIDTH_A, seq), lambda b, q: (b, 0)),
            pl.BlockSpec((TQ, WIDTH_B), lambda b, q: (b * nq + q, 0)),
            pl.BlockSpec((seq, WIDTH_B), lambda b, q: (b, 1)),
            pl.BlockSpec((WIDTH_B, seq), lambda b, q: (b, 0)),
            pl.BlockSpec((TQ, WIDTH_M), lambda b, q: (b * nq + q, 0)),
            pl.BlockSpec((3, seq, PAIR), lambda b, q: (0, b, 0)),
            pl.BlockSpec((kmean.shape[0] // batch,) + kmean.shape[1:], lambda b, q: (b, 0, 0)),
            pl.BlockSpec((n_mem, D_MODEL), lambda b, q: (b, 0)),
            pl.BlockSpec(gmem.shape, lambda b, q: (0, 0)),
            pl.BlockSpec((None, D_MODEL, 2 * WIDTH_M), per_layer),
            pl.BlockSpec(bias.shape, lambda b, q: (0, 0, 0)),
            pl.BlockSpec(cbias.shape, lambda b, q: (0, 0)),
        ],
        out_specs=[pl.BlockSpec((WIDTH_O, TQ), lambda b, q: (0, b * nq + q)),
                   pl.BlockSpec((LSE_ROWS, TQ), lambda b, q: (0, b * nq + q)),
                   pl.BlockSpec((TQ, PAIR), per_tile), pl.BlockSpec((TQ, PAIR), per_tile)],
        out_shape=[jax.ShapeDtypeStruct((WIDTH_O, batch * seq), jnp.bfloat16),
                   jax.ShapeDtypeStruct((LSE_ROWS, batch * seq), jnp.float32),
                   jax.ShapeDtypeStruct((batch * seq, PAIR), jnp.float32),
                   jax.ShapeDtypeStruct((batch * seq, PAIR), jnp.float32)],
        scratch_shapes=[
            pltpu.VMEM((n_mem, WIDTH_M), jnp.bfloat16),
            pltpu.VMEM((WIDTH_M, n_mem), jnp.bfloat16),
            pltpu.VMEM((N_HEADS_B // 2, N_BLOCKS, 2 * TQ), jnp.float32),
            pltpu.VMEM((N_HEADS_B // 2, 1, 2 * TQ), jnp.float32),
            pltpu.VMEM((N_HEADS_B // 2, HEAD_DIM + SUM_ROWS, 2 * TQ), jnp.float32),
            pltpu.VMEM((N_HEADS_B // 2, TQ, 2 * TQ), jnp.float32),
            pltpu.VMEM((N_HEADS_B // 2, 1, 2 * TQ), jnp.float32),
        ],
        compiler_params=pltpu.CompilerParams(dimension_semantics=("parallel", "arbitrary"),
                                             vmem_limit_bytes=VMEM_LIMIT),
        name="attn",
    )(qka, qka, vta, qkb, qkb, vtb, qm, cls, kmean, mem2, gmem, wmkv, bias, cbias)


def _mix_kernel(x_ref, o_ref, lse_ref, ocls_ref, lsecls_ref, g_ref, wg_ref, wpa_ref, wpb_ref, wpm_ref, wo_ref,
                out_ref, ocn_ref, lcn_ref, *, layer):
    tm = x_ref.shape[0]
    per_class = tm // CLASS_STRIDE
    first = (pl.program_id(0) % (SEQ // tm)) * per_class
    for r in range(CLASS_STRIDE):
        src = pl.ds(pl.multiple_of(r * CLASS_LEN + first, per_class), per_class)
        dst = pl.ds(r, per_class, stride=CLASS_STRIDE)
        ocn_ref[dst, :] = ocls_ref[src, :]
        lcn_ref[dst, :] = lsecls_ref[src, :]
    lse_t = lse_ref[...]
    per_head = LSE_ROWS // HEADS_PER_DIL_GROUP
    lse_tiled = jnp.concatenate([jnp.broadcast_to(lse_t[h * per_head:h * per_head + 1], (HEAD_DIM, tm))
                                 for h in range(HEADS_PER_DIL_GROUP)], axis=0)
    w_cls = 1.0 / (1.0 + jnp.exp2(lse_tiled - lcn_ref[...].T))
    o_tiled = o_ref[:WIDTH_A_OUT, :].astype(jnp.float32)
    o_a = (o_tiled + w_cls * (ocn_ref[...].T - o_tiled)).astype(jnp.bfloat16)
    half = tm // 2
    halves = [slice(0, half), slice(half, tm)]
    xs = [x_ref[rows, :] for rows in halves]
    hs = [_rms(x, g_ref[layer:layer + 1, :]).astype(jnp.bfloat16) for x in xs]

    def gate_scores(s, i):
        return jnp.dot(hs[s], wg_ref[0, :, i * D_MODEL:(i + 1) * D_MODEL].astype(jnp.bfloat16),
                       preferred_element_type=jnp.float32)

    def wait_for(y, matmul_result):
        tail = y.shape[0] - LSE_ROWS
        return jnp.concatenate([y[:tail], y[tail:] + (_after(jnp.zeros((1, y.shape[1]), jnp.float32), matmul_result))],
                               axis=0)

    def gated(s, first_gate):
        y = None
        col = 0
        for i, wp_ref in enumerate((wpa_ref, wpb_ref, wpm_ref)):
            width = wp_ref.shape[0]
            gate = jax.nn.sigmoid(first_gate if i == 0 else gate_scores(s, i))
            o_i = o_a[:, halves[s]] if i == 0 else o_ref[col:col + width, halves[s]]
            branch = gate * lax.dot_general(o_i, wp_ref[...].astype(jnp.bfloat16), _TN_PLAIN,
                                            preferred_element_type=jnp.float32)
            y = branch if y is None else y + branch
            col += width
        return y

    def project(y):
        return jnp.dot(y.astype(jnp.bfloat16), wo_ref[...].astype(jnp.bfloat16), preferred_element_type=jnp.float32)

    g0_a = gate_scores(0, 0)
    g0_b = gate_scores(1, 0)
    y_a = wait_for(gated(0, g0_a), g0_b)
    out_a = project(y_a)
    y_b = wait_for(gated(1, g0_b), out_a)
    out_ref[halves[0], :] = xs[0] + out_a
    out_ref[halves[1], :] = xs[1] + project(y_b)


def _mix_call(x2, o, lse, ocls, lsecls, g, w_in, wpa, wpb, wpm, wo, layer):
    t = x2.shape[0]
    tm = TM_MIX
    per_layer = lambda i: (layer, 0, 0)
    return pl.pallas_call(
        functools.partial(_mix_kernel, layer=layer),
        grid=(t // tm,),
        in_specs=[
            pl.BlockSpec((tm, D_MODEL), lambda i: (i, 0)),
            pl.BlockSpec((WIDTH_O, tm), lambda i: (0, i)),
            pl.BlockSpec((LSE_ROWS, tm), lambda i: (0, i)),
            pl.BlockSpec((SEQ, PAIR), lambda i: (i // (SEQ // tm), 0)),
            pl.BlockSpec((SEQ, PAIR), lambda i: (i // (SEQ // tm), 0)),
            pl.BlockSpec(g.shape, lambda i: (0, 0)),
            pl.BlockSpec((pl.Element(1), pl.Element(D_MODEL), pl.Element(GATE_COLS)), lambda i: (layer, 0, COL_GATES),
                         pipeline_mode=pl.Buffered(1)),
            pl.BlockSpec((None,) + wpa.shape[1:], per_layer, pipeline_mode=pl.Buffered(1)),
            pl.BlockSpec((None,) + wpb.shape[1:], per_layer, pipeline_mode=pl.Buffered(1)),
            pl.BlockSpec((None,) + wpm.shape[1:], per_layer, pipeline_mode=pl.Buffered(1)),
            pl.BlockSpec((None,) + wo.shape[1:], per_layer, pipeline_mode=pl.Buffered(1)),
        ],
        out_specs=pl.BlockSpec((tm, D_MODEL), lambda i: (i, 0)),
        out_shape=jax.ShapeDtypeStruct((t, D_MODEL), jnp.float32),
        scratch_shapes=[pltpu.VMEM((tm, PAIR), jnp.float32), pltpu.VMEM((tm, PAIR), jnp.float32)],
        compiler_params=pltpu.CompilerParams(dimension_semantics=("parallel",), vmem_limit_bytes=VMEM_LIMIT),
        name="gated_mix",
    )(x2, o, lse, ocls, lsecls, g, w_in, wpa, wpb, wpm, wo)


def _mlp_kernel(x_ref, g_ref, wup_hbm, wdown_hbm, gfin_ref, out_ref, wup_ref, wdown_ref, up_stage, down_stage, sems,
                *, layer, final_norm):
    n_chunks = D_FF // FF_CHUNK
    chunks = [pl.ds(c * FF_CHUNK, FF_CHUNK) for c in range(n_chunks)]

    def up_copy(c):
        return pltpu.make_async_copy(wup_hbm.at[layer, :, chunks[c]], up_stage.at[c % STAGE_SLOTS],
                                     sems.at[0, c % STAGE_SLOTS])

    def down_copy(c):
        return pltpu.make_async_copy(wdown_hbm.at[layer, chunks[c], :], down_stage.at[c % STAGE_SLOTS],
                                     sems.at[1, c % STAGE_SLOTS])

    def run(stream):
        if stream:
            for c in range(min(STAGE_SLOTS, n_chunks)):
                up_copy(c).start()
                down_copy(c).start()
        x = x_ref[...]
        hm = _rms(x, g_ref[layer:layer + 1, :]).astype(jnp.bfloat16)
        acc = x
        for c in range(n_chunks):
            if stream:
                up_copy(c).wait()
                wup_ref[:, chunks[c]] = up_stage[c % STAGE_SLOTS].astype(jnp.bfloat16)
                down_copy(c).wait()
                wdown_ref[chunks[c], :] = down_stage[c % STAGE_SLOTS].astype(jnp.bfloat16)
                if c + STAGE_SLOTS < n_chunks:
                    up_copy(c + STAGE_SLOTS).start()
                    down_copy(c + STAGE_SLOTS).start()
            u = jnp.dot(hm, wup_ref[:, chunks[c]], preferred_element_type=jnp.float32)
            u = jnp.square(jnp.maximum(u, 0.0)).astype(jnp.bfloat16)
            acc = acc + jnp.dot(u, wdown_ref[chunks[c], :], preferred_element_type=jnp.float32)
        out_ref[...] = _rms(acc, gfin_ref[...]) if final_norm else acc

    first_step = pl.program_id(0) == 0
    pl.when(first_step)(lambda: run(True))
    pl.when(jnp.logical_not(first_step))(lambda: run(False))


def _mlp_call(x2, g, wup, wdown, gfin, layer, final_norm):
    t = x2.shape[0]
    tm = TM_MLP
    const = lambda i: (0, 0)
    per_layer = lambda i: (layer, 0, 0)
    return pl.pallas_call(
        functools.partial(_mlp_kernel, layer=layer, final_norm=final_norm),
        grid=(t // tm,),
        in_specs=[
            pl.BlockSpec((tm, D_MODEL), lambda i: (i, 0)),
            pl.BlockSpec(g.shape, const),
            pl.BlockSpec(memory_space=pl.ANY),
            pl.BlockSpec(memory_space=pl.ANY),
            pl.BlockSpec((1, D_MODEL), const),
        ],
        out_specs=pl.BlockSpec((tm, D_MODEL), lambda i: (i, 0)),
        out_shape=jax.ShapeDtypeStruct((t, D_MODEL), jnp.float32),
        scratch_shapes=[pltpu.VMEM(wup.shape[1:], jnp.bfloat16), pltpu.VMEM(wdown.shape[1:], jnp.bfloat16),
                        pltpu.VMEM((STAGE_SLOTS, wup.shape[1], FF_CHUNK), jnp.float32),
                        pltpu.VMEM((STAGE_SLOTS, FF_CHUNK, wdown.shape[2]), jnp.float32),
                        pltpu.SemaphoreType.DMA((2, STAGE_SLOTS))],
        compiler_params=pltpu.CompilerParams(dimension_semantics=("arbitrary",), vmem_limit_bytes=VMEM_LIMIT),
        name="mlp",
    )(x2, g, wup, wdown, gfin)


def _rope_tables(seq):
    half = ROT_DIM // 2
    inv_freq = (1.0 / (np.float32(ROPE_THETA) ** (np.arange(0, ROT_DIM, 2, dtype=np.float32) / np.float32(ROT_DIM))))
    ang = np.arange(seq, dtype=np.float32)[:, None] * inv_freq.astype(np.float32)[None, :]
    cos, sin = np.cos(ang).astype(np.float32), np.sin(ang).astype(np.float32)
    ones = np.ones((seq, HEAD_DIM - ROT_DIM), np.float32)
    zeros_half = np.zeros((seq, half), np.float32)
    zeros_rest = np.zeros((seq, HEAD_DIM - ROT_DIM), np.float32)
    c_head = np.concatenate([cos, cos, ones], axis=1)
    lo_head = np.concatenate([-sin, zeros_half, zeros_rest], axis=1)
    hi_head = np.concatenate([zeros_half, sin, zeros_rest], axis=1)
    rep = LANES // HEAD_DIM
    return tuple(jnp.asarray(np.tile(t, (1, rep))) for t in (c_head, lo_head, hi_head))


def kernel(x, mem, norm_mix, w_in, w_proj_a, w_proj_b, w_proj_m, w_out, norm_mem, w_mem_kv, norm_mlp, w_up,
           w_down, norm_final):
    batch, seq, d = x.shape
    assert d == D_MODEL and seq == SEQ and seq % TM_QKV == 0
    depth = w_in.shape[0]
    x2 = x.reshape(batch * seq, d)
    mem2 = mem.reshape(batch * mem.shape[1], d)
    rope_c, rope_s1, rope_s2 = _rope_tables(seq)
    bias = jnp.asarray(_BIAS_NP)
    cbias = jnp.asarray(_CLASS_BIAS_NP)
    for l in range(depth):
        qka, vta, qkb, vtb, qm, cls, kmean = _qkv_call(x2, norm_mix, w_in, l, rope_c, rope_s1, rope_s2, batch, seq)
        o, lse, ocls, lsecls = _attn_call(qka, vta, qkb, vtb, qm, cls, kmean, mem2, norm_mem,
                                          w_mem_kv, l, bias, cbias, batch, seq)
        x2 = _mix_call(x2, o, lse, ocls, lsecls, norm_mix, w_in, w_proj_a, w_proj_b, w_proj_m, w_out, l)
        x2 = _mlp_call(x2, norm_mlp, w_up, w_down, norm_final.reshape(1, d), l, final_norm=(l == depth - 1))
    return x2.reshape(batch, seq, d)
```

```python
import functools

import jax
import jax.numpy as jnp
import numpy as np
from jax import lax
from jax.experimental import pallas as pl
from jax.experimental.pallas import tpu as pltpu

D_MODEL = 1024
SEQ = 2048
HEAD_DIM = 64
ROT_DIM = HEAD_DIM // 4
ROPE_THETA = 500000.0
DIL_GROUPS = ((128, 1), (512, 4), (2048, 16))
HEADS_PER_DIL_GROUP = 2
N_HEADS_A = len(DIL_GROUPS) * HEADS_PER_DIL_GROUP
N_HEADS_B = 6
N_HEADS_M = 4
MOBA_BLOCK = 256
MOBA_TOPK = 3
N_BLOCKS = SEQ // MOBA_BLOCK
D_FF = 4 * D_MODEL
WIDTH_A = N_HEADS_A * HEAD_DIM
WIDTH_A_OUT = HEADS_PER_DIL_GROUP * HEAD_DIM
WIDTH_B = N_HEADS_B * HEAD_DIM
WIDTH_M = N_HEADS_M * HEAD_DIM
WIDTH_O = WIDTH_A_OUT + WIDTH_B + WIDTH_M
RMS_EPS = 1e-6
NEG_INF = -1e30
Q_SCALE = HEAD_DIM ** -0.5
LOG2_E = 1.4426950408889634

LANES = 128
LSE_ROWS = 8
PAIR = 2 * HEAD_DIM
SUM_ROWS = 16
TQ = MOBA_BLOCK
CLASS_GROUP = len(DIL_GROUPS) - 1
CLASS_STRIDE = DIL_GROUPS[CLASS_GROUP][1]
CLASS_LEN = SEQ // CLASS_STRIDE
CLASSES_PER_TILE = TQ // CLASS_LEN
TM_QKV = 1024
TM_MIX = 1024
TM_MLP = 1024
FF_CHUNK = 1024
STAGE_CHUNK = 512
STAGE_SLOTS = 2
VMEM_LIMIT = 60 * 1024 * 1024

_B_ALLNEG = 0
_B_CAUSAL = 1


TILE_GROUPS = DIL_GROUPS[:CLASS_GROUP]
assert DIL_GROUPS[CLASS_GROUP][0] >= SEQ and TQ % CLASS_LEN == 0


def _dil_tile_offsets():
    return tuple(tuple(range(min((w + TQ - 1) // TQ, N_BLOCKS - 1) + 1)) for w, _ in TILE_GROUPS)


def _build_bias_tiles():
    c = np.arange(TQ)[:, None]
    r = np.arange(TQ)[None, :]
    tiles = [np.zeros((TQ, TQ), bool), (r - c) >= 0]
    ids = []
    key_rows = []
    for (w, d), offs in zip(TILE_GROUPS, _dil_tile_offsets()):
        per_off = []
        for o in offs:
            diff = o * TQ + r - c
            per_off.append((diff >= 0) & (diff <= w) & (diff % d == 0))
        uniq, gid = [], []
        for t in per_off:
            for k, u in enumerate(uniq):
                if np.array_equal(t, u):
                    gid.append(k)
                    break
            else:
                uniq.append(t)
                gid.append(len(uniq) - 1)
        ids.append(tuple(len(tiles) + k for k in gid))
        tiles.extend(uniq)
        used = [np.flatnonzero(t.any(axis=1)) for t in per_off]
        key_rows.append(tuple((int(u.min()) // LANES * LANES, -(-(int(u.max()) + 1) // LANES) * LANES) for u in used))
    bias = np.where(np.stack(tiles), 0.0, NEG_INF).astype(np.float32)
    return bias, tuple(ids), tuple(key_rows)


_BIAS_NP, _DIL_BIAS_IDS, _DIL_KEY_ROWS = _build_bias_tiles()
_pos = np.arange(CLASS_LEN)
_CLASS_BIAS_NP = np.where(_pos[None, :] >= _pos[:, None], 0.0, NEG_INF).astype(np.float32)

_NT = (((1,), (1,)), ((), ()))
_TN = (((0,), (1,)), ((), ()))
_TN_PLAIN = (((0,), (0,)), ((), ()))
COL_B = 3 * WIDTH_A
COL_M = COL_B + 3 * WIDTH_B
COL_GATES = COL_M + WIDTH_M
GATE_COLS = 3 * D_MODEL


def _rms(x, g):
    return x * lax.rsqrt(jnp.mean(x * x, axis=-1, keepdims=True) + RMS_EPS) * g


def _qkv_kernel(x_ref, g_ref, win_ref, c_ref, s1_ref, s2_ref, qka_ref, vta_ref, qkb_ref, vtb_ref, qm_ref, cls_ref,
                kmean_ref, *, layer):
    tm = x_ref.shape[0]
    halves = (slice(0, tm // 2), slice(tm // 2, tm))
    normed = [_rms(x_ref[rows, :], g_ref[layer:layer + 1, :]) for rows in halves]
    for s, rows in enumerate(halves):
        h = normed[s].astype(jnp.bfloat16)
        cos = c_ref[rows, :]
        sin_lo = s1_ref[rows, :]
        sin_hi = s2_ref[rows, :]
        for col0, qk_ref, vt_ref, width in ((0, qka_ref, vta_ref, WIDTH_A), (COL_B, qkb_ref, vtb_ref, WIDTH_B)):
            z = jnp.dot(h, win_ref[:, col0:col0 + 2 * width].astype(jnp.bfloat16),
                        preferred_element_type=jnp.float32)
            for blk in range(2 * width // LANES):
                zb = z[:, blk * LANES:(blk + 1) * LANES]
                rb = (zb * cos + pltpu.roll(zb, LANES - ROT_DIM // 2, 1) * sin_lo
                      + pltpu.roll(zb, ROT_DIM // 2, 1) * sin_hi)
                if blk < width // LANES:
                    rb = rb * (Q_SCALE * LOG2_E)
                qk_ref[rows, blk * LANES:(blk + 1) * LANES] = rb.astype(jnp.bfloat16)
                if col0 == 0 and blk % (width // LANES) == CLASS_GROUP:
                    part = blk // (width // LANES)
                    cls_ref[part, rows, :] = rb
                if col0 == COL_B and blk >= width // LANES:
                    kcol = (blk - width // LANES) * LANES
                    blocks = slice(rows.start // MOBA_BLOCK, rows.stop // MOBA_BLOCK)
                    kmean_ref[0, blocks, kcol:kcol + LANES] = jnp.mean(rb.reshape(-1, MOBA_BLOCK, LANES), axis=1)
            vt = lax.dot_general(win_ref[:, col0 + 2 * width:col0 + 3 * width].astype(jnp.bfloat16), h, _TN,
                                 preferred_element_type=jnp.float32)
            vt_ref[:, rows] = vt.astype(jnp.bfloat16)
            if col0 == 0:
                cls_ref[2, rows, :] = vt[CLASS_GROUP * PAIR:(CLASS_GROUP + 1) * PAIR, :].T
        qm = jnp.dot(h, win_ref[:, COL_M:COL_GATES].astype(jnp.bfloat16),
                     preferred_element_type=jnp.float32) * (Q_SCALE * LOG2_E)
        if s + 1 < len(halves):
            tail = qm.shape[0] - LSE_ROWS
            qm = jnp.concatenate([qm[:tail], qm[tail:] + _after(jnp.zeros((1, qm.shape[1]), jnp.float32),
                                                               normed[s + 1])], axis=0)
        qm_ref[rows, :] = qm.astype(jnp.bfloat16)


def _qkv_call(x2, g, w_in, layer, rope_c, rope_s1, rope_s2, batch, seq):
    t = x2.shape[0]
    tm = TM_QKV
    nts = seq // tm
    per_layer = lambda i: (layer, 0, 0)
    return pl.pallas_call(
        functools.partial(_qkv_kernel, layer=layer),
        grid=(t // tm,),
        in_specs=[
            pl.BlockSpec((tm, D_MODEL), lambda i: (i, 0)),
            pl.BlockSpec(g.shape, lambda i: (0, 0)),
            pl.BlockSpec((None, D_MODEL, COL_GATES), per_layer),
            pl.BlockSpec((tm, LANES), lambda i: (i % nts, 0)),
            pl.BlockSpec((tm, LANES), lambda i: (i % nts, 0)),
            pl.BlockSpec((tm, LANES), lambda i: (i % nts, 0)),
        ],
        out_specs=[
            pl.BlockSpec((tm, 2 * WIDTH_A), lambda i: (i, 0)),
            pl.BlockSpec((WIDTH_A, tm), lambda i: (i // nts, i % nts)),
            pl.BlockSpec((tm, 2 * WIDTH_B), lambda i: (i, 0)),
            pl.BlockSpec((WIDTH_B, tm), lambda i: (i // nts, i % nts)),
            pl.BlockSpec((tm, WIDTH_M), lambda i: (i, 0)),
            pl.BlockSpec((3, tm, PAIR), lambda i: (0, i, 0)),
            pl.BlockSpec((1, tm // MOBA_BLOCK, WIDTH_B), lambda i: (i, 0, 0)),
        ],
        out_shape=[
            jax.ShapeDtypeStruct((t, 2 * WIDTH_A), jnp.bfloat16),
            jax.ShapeDtypeStruct((batch * WIDTH_A, seq), jnp.bfloat16),
            jax.ShapeDtypeStruct((t, 2 * WIDTH_B), jnp.bfloat16),
            jax.ShapeDtypeStruct((batch * WIDTH_B, seq), jnp.bfloat16),
            jax.ShapeDtypeStruct((t, WIDTH_M), jnp.bfloat16),
            jax.ShapeDtypeStruct((3, t, PAIR), jnp.float32),
            jax.ShapeDtypeStruct((t // tm, tm // MOBA_BLOCK, WIDTH_B), jnp.float32),
        ],
        compiler_params=pltpu.CompilerParams(dimension_semantics=("parallel",), vmem_limit_bytes=VMEM_LIMIT),
        name="qkv_proj",
    )(x2, g, w_in, rope_c, rope_s1, rope_s2)


def _pv(v_heads, p_bf):
    ones = jnp.ones((SUM_ROWS, p_bf.shape[0]), jnp.bfloat16)
    nq = p_bf.shape[1] // len(v_heads)
    return jnp.concatenate([jnp.dot(jnp.concatenate([v, ones], axis=0), p_bf[:, h * nq:(h + 1) * nq],
                                    preferred_element_type=jnp.float32)
                            for h, v in enumerate(v_heads)], axis=1)


def _after(x, *matmul_results):
    for r in matmul_results:
        bits = lax.bitcast_convert_type(r[-1:, :x.shape[1]], jnp.uint32)
        bits = lax.shift_right_logical(lax.shift_right_logical(bits, jnp.uint32(16)), jnp.uint32(16))
        x = x + lax.bitcast_convert_type(bits, jnp.float32)
    return x


def _scores_t(k_tile, q_heads):
    return lax.dot_general(k_tile, q_heads, _NT, preferred_element_type=jnp.float32)


def _stack_heads(q, n_heads):
    lane = lax.broadcasted_iota(jnp.int32, q.shape, 1)
    zero = jnp.zeros((), q.dtype)
    return jnp.concatenate([jnp.where((lane >= h * HEAD_DIM) & (lane < (h + 1) * HEAD_DIM), q, zero)
                            for h in range(n_heads)], axis=0)


def _head_rows(first_head, n_heads):
    return [slice((first_head + h) * HEAD_DIM, (first_head + h + 1) * HEAD_DIM) for h in range(n_heads)]


def _lanes_to_rows(row, n_heads):
    nq = row.shape[1] // n_heads
    return jnp.concatenate([jnp.broadcast_to(row[:, h * nq:(h + 1) * nq], (HEAD_DIM, nq)) for h in range(n_heads)],
                           axis=0)


def _heads_to_rows(o_t, n_heads):
    nq = o_t.shape[1] // n_heads
    return jnp.concatenate([o_t[:, h * nq:(h + 1) * nq] for h in range(n_heads)], axis=0)


def _attn_kernel(qa_ref, ka_ref, vta_ref, qb_ref, kb_ref, vtb_ref, qm_ref, cls_ref, kmean_ref, mem_ref, gmem_ref,
                 wmkv_ref, bias_ref, cbias_ref, o_ref, lse_ref, ocls_ref, lsecls_ref,
                 km_ref, vmt_ref, selb_ref, m_ref, acc_ref, raw_ref, rawmax_ref, *, layer):
    qi = pl.program_id(1)

    @pl.when(qi == 0)
    def _per_batch():
        mem_n = _rms(mem_ref[...], gmem_ref[layer:layer + 1, :]).astype(jnp.bfloat16)
        w_mkv = wmkv_ref[...].astype(jnp.bfloat16)
        km_ref[...] = jnp.dot(mem_n, w_mkv[:, :WIDTH_M], preferred_element_type=jnp.float32).astype(jnp.bfloat16)
        vmt_ref[...] = lax.dot_general(w_mkv[:, WIDTH_M:], mem_n, _TN,
                                       preferred_element_type=jnp.float32).astype(jnp.bfloat16)

    def keys_of(j):
        return pl.ds(pl.multiple_of(j * TQ, TQ), TQ)

    def both_heads(bias):
        return jnp.concatenate([bias, bias], axis=1)

    def single_tile(s_t, v_heads):
        m = jnp.max(s_t, axis=0, keepdims=True)
        return m, _pv(v_heads, jnp.exp2(s_t - m).astype(jnp.bfloat16))

    for c in range(CLASSES_PER_TILE):
        rows = pl.ds(qi * CLASSES_PER_TILE + c, CLASS_LEN, stride=CLASS_STRIDE)
        q_heads = _stack_heads(cls_ref[0, rows, :].astype(jnp.bfloat16), HEADS_PER_DIL_GROUP)
        k_cls = cls_ref[1, rows, :].astype(jnp.bfloat16)
        v_t = cls_ref[2, rows, :].T.astype(jnp.bfloat16)
        m, acc = single_tile(_scores_t(k_cls, q_heads) + both_heads(cbias_ref[...]),
                             [v_t[r] for r in _head_rows(0, HEADS_PER_DIL_GROUP)])
        l = acc[HEAD_DIM:HEAD_DIM + 1]
        out = slice(c * CLASS_LEN, (c + 1) * CLASS_LEN)
        ocls_ref[out, :] = _heads_to_rows(acc[:HEAD_DIM] / l, HEADS_PER_DIL_GROUP).T
        lsecls_ref[out, :] = _lanes_to_rows(m + jnp.log2(l), HEADS_PER_DIL_GROUP).T

    a_static = []
    for g, offs in enumerate(_dil_tile_offsets()):
        cols = slice(g * PAIR, (g + 1) * PAIR)
        rows = _head_rows(g * HEADS_PER_DIL_GROUP, HEADS_PER_DIL_GROUP)
        q_heads = _stack_heads(qa_ref[:, cols], HEADS_PER_DIL_GROUP)
        ids = _DIL_BIAS_IDS[g]
        for o in offs:
            lo, hi = _DIL_KEY_ROWS[g][o]
            ks = pl.ds(pl.multiple_of(jnp.maximum(qi - o, 0) * TQ + lo, LANES), hi - lo)
            bid = ids[o] if o == 0 else jnp.where(qi >= o, ids[o], _B_ALLNEG)
            a_static.append((ks, (bid, slice(lo, hi)), cols, rows, q_heads))

    blk = lax.broadcasted_iota(jnp.int32, (N_BLOCKS, 2 * TQ), 0)
    own = keys_of(qi)
    b_heads = []
    for p in range(N_HEADS_B // 2):
        cols = slice(p * PAIR, (p + 1) * PAIR)
        q_heads = _stack_heads(qb_ref[:, cols], 2)
        kmean = jnp.concatenate([kmean_ref[t, :, cols] for t in range(kmean_ref.shape[0])], axis=0)
        kmean_hi = kmean.astype(jnp.bfloat16)
        kmean_lo = (kmean - kmean_hi.astype(jnp.float32)).astype(jnp.bfloat16)
        both = _scores_t(jnp.concatenate([kmean_hi, kmean_lo], axis=0), q_heads)
        gate = both[:N_BLOCKS] + both[N_BLOCKS:]
        gate = jnp.where(blk < qi, gate, NEG_INF)
        beaten = jnp.zeros(gate.shape, jnp.float32)
        for j in range(N_BLOCKS):
            gj = gate[j:j + 1, :]
            wins_tie = jnp.where(gj >= gate, 1.0, 0.0)
            wins_strict = jnp.where(gj > gate, 1.0, 0.0)
            beaten = beaten + jnp.where(blk > j, wins_tie, wins_strict)
        selb_ref[p] = jnp.where((beaten < MOBA_TOPK) & (blk < qi), 0.0, NEG_INF)
        b_heads.append((cols, _head_rows(2 * p, 2), q_heads))
    n_b = len(b_heads)

    def a_scores(part):
        ks, _, cols, _, q_heads = part
        return _scores_t(ka_ref[ks, cols], q_heads)

    def b_scores(p, ks):
        cols, _, q_heads = b_heads[p]
        return _scores_t(kb_ref[ks, cols], q_heads)

    n_a = len(a_static)
    raw = a_scores(a_static[0])
    a_biased = []
    m_a = None
    for i in range(n_a):
        nxt = a_scores(a_static[i + 1]) if i + 1 < n_a else b_scores(0, own)
        s_t = raw + both_heads(bias_ref[a_static[i][1]])
        part_max = _after(jnp.max(s_t, axis=0, keepdims=True), nxt)
        m_a = part_max if m_a is None else jnp.maximum(m_a, part_max)
        a_biased.append(s_t)
        raw = nxt
    m_heads = _stack_heads(qm_ref[...], N_HEADS_M)
    ahead = [lambda: b_scores(1, own), lambda: b_scores(2, own), lambda: _scores_t(km_ref[...], m_heads),
             lambda: b_scores(0, keys_of(0)), lambda: b_scores(1, keys_of(0))]
    last_first_past = b_scores(n_b - 1, keys_of(0))
    later = [raw]
    assert n_a >= len(ahead)
    neg_m = -m_a
    acc_a = None
    for i in range(n_a):
        ks, _, _, rows, _ = a_static[i]
        p_t = jnp.exp2(a_biased[i] + neg_m).astype(jnp.bfloat16)
        if i < len(ahead):
            later.append(ahead[i]())
            neg_m = _after(neg_m, later[-1])
        part_acc = _pv([vta_ref[r, ks] for r in rows], p_t)
        acc_a = part_acc if acc_a is None else acc_a + part_acc
    l_a = acc_a[HEAD_DIM:HEAD_DIM + 1]
    o_tiles = [acc_a[:HEAD_DIM] / l_a]
    lse_a = m_a + jnp.log2(l_a)
    lse_ref[...] = jnp.concatenate([jnp.broadcast_to(lse_a[:, h * TQ:(h + 1) * TQ], (LSE_ROWS // HEADS_PER_DIL_GROUP, TQ))
                                    for h in range(HEADS_PER_DIL_GROUP)], axis=0)

    for p, (_, rows, _) in enumerate(b_heads):
        m_ref[p], acc_ref[p] = single_tile(later[p] + both_heads(bias_ref[_B_CAUSAL]),
                                           [vtb_ref[r, own] for r in rows])
    _, acc_m = single_tile(later[3], [vmt_ref[r, :] for r in _head_rows(0, N_HEADS_M)])
    for c, first in enumerate((later[4], later[5], last_first_past)):
        raw_ref[c] = first
        rawmax_ref[c] = jnp.max(first, axis=0, keepdims=True)

    def past_tile(j, _):
        ks = keys_of(j)
        ks_next = keys_of(jnp.minimum(j + 1, qi - 1))
        scores = [raw_ref[c] for c in range(n_b)]
        for c in range(n_b):
            issued = b_scores(c, ks_next)
            scores.append(issued)
            sel = selb_ref[c, pl.ds(j, 1), :]
            m_old = m_ref[c]
            m_new = jnp.maximum(m_old, rawmax_ref[c] + sel)
            alpha = _after(jnp.exp2(m_old - m_new), issued)
            p_t = jnp.exp2(scores[c] + (sel - m_new)).astype(jnp.bfloat16)
            acc_ref[c] = alpha * acc_ref[c] + _pv([vtb_ref[r, ks] for r in b_heads[c][1]], p_t)
            m_ref[c] = m_new
        for c in range(n_b):
            raw_ref[c] = scores[n_b + c]
            rawmax_ref[c] = jnp.max(scores[n_b + c], axis=0, keepdims=True)
        return 0

    lax.fori_loop(0, qi, past_tile, 0)

    for acc in (*[acc_ref[c] for c in range(n_b)], acc_m):
        o_tiles.append(acc[:HEAD_DIM] / acc[HEAD_DIM:HEAD_DIM + 1])
    row = 0
    for o_t in o_tiles:
        for h in range(o_t.shape[1] // TQ):
            o_ref[row:row + HEAD_DIM, :] = o_t[:, h * TQ:(h + 1) * TQ].astype(jnp.bfloat16)
            row += HEAD_DIM


def _attn_call(qka, vta, qkb, vtb, qm, cls, kmean, mem2, gmem, wmkv, layer, bias, cbias, batch, seq):
    nq = seq // TQ
    n_mem = mem2.shape[0] // batch
    per_layer = lambda b, q: (layer, 0, 0)
    per_tile = lambda b, q: (b * nq + q, 0)
    return pl.pallas_call(
        functools.partial(_attn_kernel, layer=layer),
        grid=(batch, nq),
        in_specs=[
            pl.BlockSpec((TQ, WIDTH_A), lambda b, q: (b * nq + q, 0)),
            pl.BlockSpec((seq, WIDTH_A), lambda b, q: (b, 1)),
            pl.BlockSpec((WIDTH_A, seq), lambda b, q: (b, 0)),
            pl.BlockSpec((TQ, WIDTH_B), lambda b, q: (b * nq + q, 0)),
            pl.BlockSpec((seq, WIDTH_B), lambda b, q: (b, 1)),
            pl.BlockSpec((WIDTH_B, seq), lambda b, q: (b, 0)),
            pl.BlockSpec((TQ, WIDTH_M), lambda b, q: (b * nq + q, 0)),
            pl.BlockSpec((3, seq, PAIR), lambda b, q: (0, b, 0)),
            pl.BlockSpec((kmean.shape[0] // batch,) + kmean.shape[1:], lambda b, q: (b, 0, 0)),
            pl.BlockSpec((n_mem, D_MODEL), lambda b, q: (b, 0)),
            pl.BlockSpec(gmem.shape, lambda b, q: (0, 0)),
            pl.BlockSpec((None, D_MODEL, 2 * WIDTH_M), per_layer),
            pl.BlockSpec(bias.shape, lambda b, q: (0, 0, 0)),
            pl.BlockSpec(cbias.shape, lambda b, q: (0, 0)),
        ],
        out_specs=[pl.BlockSpec((WIDTH_O, TQ), lambda b, q: (0, b * nq + q)),
                   pl.BlockSpec((LSE_ROWS, TQ), lambda b, q: (0, b * nq + q)),
                   pl.BlockSpec((TQ, PAIR), per_tile), pl.BlockSpec((TQ, PAIR), per_tile)],
        out_shape=[jax.ShapeDtypeStruct((WIDTH_O, batch * seq), jnp.bfloat16),
                   jax.ShapeDtypeStruct((LSE_ROWS, batch * seq), jnp.float32),
                   jax.ShapeDtypeStruct((batch * seq, PAIR), jnp.float32),
                   jax.ShapeDtypeStruct((batch * seq, PAIR), jnp.float32)],
        scratch_shapes=[
            pltpu.VMEM((n_mem, WIDTH_M), jnp.bfloat16),
            pltpu.VMEM((WIDTH_M, n_mem), jnp.bfloat16),
            pltpu.VMEM((N_HEADS_B // 2, N_BLOCKS, 2 * TQ), jnp.float32),
            pltpu.VMEM((N_HEADS_B // 2, 1, 2 * TQ), jnp.float32),
            pltpu.VMEM((N_HEADS_B // 2, HEAD_DIM + SUM_ROWS, 2 * TQ), jnp.float32),
            pltpu.VMEM((N_HEADS_B // 2, TQ, 2 * TQ), jnp.float32),
            pltpu.VMEM((N_HEADS_B // 2, 1, 2 * TQ), jnp.float32),
        ],
        compiler_params=pltpu.CompilerParams(dimension_semantics=("parallel", "arbitrary"),
                                             vmem_limit_bytes=VMEM_LIMIT),
        name="attn",
    )(qka, qka, vta, qkb, qkb, vtb, qm, cls, kmean, mem2, gmem, wmkv, bias, cbias)


def _mix_kernel(x_ref, o_ref, lse_ref, ocls_ref, lsecls_ref, g_ref, wg_ref, wpa_ref, wpb_ref, wpm_ref, wo_ref,
                out_ref, ocn_ref, lcn_ref, *, layer):
    tm = x_ref.shape[0]
    per_class = tm // CLASS_STRIDE
    first = (pl.program_id(0) % (SEQ // tm)) * per_class
    for r in range(CLASS_STRIDE):
        src = pl.ds(pl.multiple_of(r * CLASS_LEN + first, per_class), per_class)
        dst = pl.ds(r, per_class, stride=CLASS_STRIDE)
        ocn_ref[dst, :] = ocls_ref[src, :]
        lcn_ref[dst, :] = lsecls_ref[src, :]
    lse_t = lse_ref[...]
    per_head = LSE_ROWS // HEADS_PER_DIL_GROUP
    lse_tiled = jnp.concatenate([jnp.broadcast_to(lse_t[h * per_head:h * per_head + 1], (HEAD_DIM, tm))
                                 for h in range(HEADS_PER_DIL_GROUP)], axis=0)
    w_cls = 1.0 / (1.0 + jnp.exp2(lse_tiled - lcn_ref[...].T))
    o_tiled = o_ref[:WIDTH_A_OUT, :].astype(jnp.float32)
    o_a = (o_tiled + w_cls * (ocn_ref[...].T - o_tiled)).astype(jnp.bfloat16)
    half = tm // 2
    halves = [slice(0, half), slice(half, tm)]
    xs = [x_ref[rows, :] for rows in halves]
    hs = [_rms(x, g_ref[layer:layer + 1, :]).astype(jnp.bfloat16) for x in xs]

    def gate_scores(s, i):
        return jnp.dot(hs[s], wg_ref[0, :, i * D_MODEL:(i + 1) * D_MODEL].astype(jnp.bfloat16),
                       preferred_element_type=jnp.float32)

    def wait_for(y, matmul_result):
        tail = y.shape[0] - LSE_ROWS
        return jnp.concatenate([y[:tail], y[tail:] + (_after(jnp.zeros((1, y.shape[1]), jnp.float32), matmul_result))],
                               axis=0)

    def gated(s, first_gate):
        y = None
        col = 0
        for i, wp_ref in enumerate((wpa_ref, wpb_ref, wpm_ref)):
            width = wp_ref.shape[0]
            gate = jax.nn.sigmoid(first_gate if i == 0 else gate_scores(s, i))
            o_i = o_a[:, halves[s]] if i == 0 else o_ref[col:col + width, halves[s]]
            branch = gate * lax.dot_general(o_i, wp_ref[...].astype(jnp.bfloat16), _TN_PLAIN,
                                            preferred_element_type=jnp.float32)
            y = branch if y is None else y + branch
            col += width
        return y

    def project(y):
        return jnp.dot(y.astype(jnp.bfloat16), wo_ref[...].astype(jnp.bfloat16), preferred_element_type=jnp.float32)

    g0_a = gate_scores(0, 0)
    g0_b = gate_scores(1, 0)
    y_a = wait_for(gated(0, g0_a), g0_b)
    out_a = project(y_a)
    y_b = wait_for(gated(1, g0_b), out_a)
    out_ref[halves[0], :] = xs[0] + out_a
    out_ref[halves[1], :] = xs[1] + project(y_b)


def _mix_call(x2, o, lse, ocls, lsecls, g, w_in, wpa, wpb, wpm, wo, layer):
    t = x2.shape[0]
    tm = TM_MIX
    per_layer = lambda i: (layer, 0, 0)
    return pl.pallas_call(
        functools.partial(_mix_kernel, layer=layer),
        grid=(t // tm,),
        in_specs=[
            pl.BlockSpec((tm, D_MODEL), lambda i: (i, 0)),
            pl.BlockSpec((WIDTH_O, tm), lambda i: (0, i)),
            pl.BlockSpec((LSE_ROWS, tm), lambda i: (0, i)),
            pl.BlockSpec((SEQ, PAIR), lambda i: (i // (SEQ // tm), 0)),
            pl.BlockSpec((SEQ, PAIR), lambda i: (i // (SEQ // tm), 0)),
            pl.BlockSpec(g.shape, lambda i: (0, 0)),
            pl.BlockSpec((pl.Element(1), pl.Element(D_MODEL), pl.Element(GATE_COLS)), lambda i: (layer, 0, COL_GATES),
                         pipeline_mode=pl.Buffered(1)),
            pl.BlockSpec((None,) + wpa.shape[1:], per_layer, pipeline_mode=pl.Buffered(1)),
            pl.BlockSpec((None,) + wpb.shape[1:], per_layer, pipeline_mode=pl.Buffered(1)),
            pl.BlockSpec((None,) + wpm.shape[1:], per_layer, pipeline_mode=pl.Buffered(1)),
            pl.BlockSpec((None,) + wo.shape[1:], per_layer, pipeline_mode=pl.Buffered(1)),
        ],
        out_specs=pl.BlockSpec((tm, D_MODEL), lambda i: (i, 0)),
        out_shape=jax.ShapeDtypeStruct((t, D_MODEL), jnp.float32),
        scratch_shapes=[pltpu.VMEM((tm, PAIR), jnp.float32), pltpu.VMEM((tm, PAIR), jnp.float32)],
        compiler_params=pltpu.CompilerParams(dimension_semantics=("parallel",), vmem_limit_bytes=VMEM_LIMIT),
        name="gated_mix",
    )(x2, o, lse, ocls, lsecls, g, w_in, wpa, wpb, wpm, wo)


def _mlp_kernel(x_ref, g_ref, wup_hbm, wdown_hbm, gfin_ref, out_ref, wup_ref, wdown_ref, up_stage, down_stage, sems,
                *, layer, final_norm):
    n_chunks = D_FF // FF_CHUNK
    chunks = [pl.ds(c * FF_CHUNK, FF_CHUNK) for c in range(n_chunks)]
    n_pieces = D_FF // STAGE_CHUNK
    pieces = [pl.ds(s * STAGE_CHUNK, STAGE_CHUNK) for s in range(n_pieces)]
    per_chunk = FF_CHUNK // STAGE_CHUNK

    def up_copy(s):
        return pltpu.make_async_copy(wup_hbm.at[layer, :, pieces[s]], up_stage.at[s % STAGE_SLOTS],
                                     sems.at[0, s % STAGE_SLOTS])

    def down_copy(s):
        return pltpu.make_async_copy(wdown_hbm.at[layer, pieces[s], :], down_stage.at[s % STAGE_SLOTS],
                                     sems.at[1, s % STAGE_SLOTS])

    def run(stream):
        if stream:
            for s in range(min(STAGE_SLOTS, n_pieces)):
                up_copy(s).start()
                down_copy(s).start()
        x = x_ref[...]
        hm = _rms(x, g_ref[layer:layer + 1, :]).astype(jnp.bfloat16)
        acc = x
        for c in range(n_chunks):
            for s in range(c * per_chunk, (c + 1) * per_chunk) if stream else ():
                up_copy(s).wait()
                wup_ref[:, pieces[s]] = up_stage[s % STAGE_SLOTS].astype(jnp.bfloat16)
                down_copy(s).wait()
                wdown_ref[pieces[s], :] = down_stage[s % STAGE_SLOTS].astype(jnp.bfloat16)
                if s + STAGE_SLOTS < n_pieces:
                    up_copy(s + STAGE_SLOTS).start()
                    down_copy(s + STAGE_SLOTS).start()
            u = jnp.dot(hm, wup_ref[:, chunks[c]], preferred_element_type=jnp.float32)
            u = jnp.square(jnp.maximum(u, 0.0)).astype(jnp.bfloat16)
            acc = acc + jnp.dot(u, wdown_ref[chunks[c], :], preferred_element_type=jnp.float32)
        out_ref[...] = _rms(acc, gfin_ref[...]) if final_norm else acc

    first_step = pl.program_id(0) == 0
    pl.when(first_step)(lambda: run(True))
    pl.when(jnp.logical_not(first_step))(lambda: run(False))


def _mlp_call(x2, g, wup, wdown, gfin, layer, final_norm):
    t = x2.shape[0]
    tm = TM_MLP
    const = lambda i: (0, 0)
    per_layer = lambda i: (layer, 0, 0)
    return pl.pallas_call(
        functools.partial(_mlp_kernel, layer=layer, final_norm=final_norm),
        grid=(t // tm,),
        in_specs=[
            pl.BlockSpec((tm, D_MODEL), lambda i: (i, 0)),
            pl.BlockSpec(g.shape, const),
            pl.BlockSpec(memory_space=pl.ANY),
            pl.BlockSpec(memory_space=pl.ANY),
            pl.BlockSpec((1, D_MODEL), const),
        ],
        out_specs=pl.BlockSpec((tm, D_MODEL), lambda i: (i, 0)),
        out_shape=jax.ShapeDtypeStruct((t, D_MODEL), jnp.float32),
        scratch_shapes=[pltpu.VMEM(wup.shape[1:], jnp.bfloat16), pltpu.VMEM(wdown.shape[1:], jnp.bfloat16),
                        pltpu.VMEM((STAGE_SLOTS, wup.shape[1], STAGE_CHUNK), jnp.float32),
                        pltpu.VMEM((STAGE_SLOTS, STAGE_CHUNK, wdown.shape[2]), jnp.float32),
                        pltpu.SemaphoreType.DMA((2, STAGE_SLOTS))],
        compiler_params=pltpu.CompilerParams(dimension_semantics=("arbitrary",), vmem_limit_bytes=VMEM_LIMIT),
        name="mlp",
    )(x2, g, wup, wdown, gfin)


def _rope_tables(seq):
    half = ROT_DIM // 2
    inv_freq = (1.0 / (np.float32(ROPE_THETA) ** (np.arange(0, ROT_DIM, 2, dtype=np.float32) / np.float32(ROT_DIM))))
    ang = np.arange(seq, dtype=np.float32)[:, None] * inv_freq.astype(np.float32)[None, :]
    cos, sin = np.cos(ang).astype(np.float32), np.sin(ang).astype(np.float32)
    ones = np.ones((seq, HEAD_DIM - ROT_DIM), np.float32)
    zeros_half = np.zeros((seq, half), np.float32)
    zeros_rest = np.zeros((seq, HEAD_DIM - ROT_DIM), np.float32)
    c_head = np.concatenate([cos, cos, ones], axis=1)
    lo_head = np.concatenate([-sin, zeros_half, zeros_rest], axis=1)
    hi_head = np.concatenate([zeros_half, sin, zeros_rest], axis=1)
    rep = LANES // HEAD_DIM
    return tuple(jnp.asarray(np.tile(t, (1, rep))) for t in (c_head, lo_head, hi_head))


def kernel(x, mem, norm_mix, w_in, w_proj_a, w_proj_b, w_proj_m, w_out, norm_mem, w_mem_kv, norm_mlp, w_up,
           w_down, norm_final):
    batch, seq, d = x.shape
    assert d == D_MODEL and seq == SEQ and seq % TM_QKV == 0
    depth = w_in.shape[0]
    x2 = x.reshape(batch * seq, d)
    mem2 = mem.reshape(batch * mem.shape[1], d)
    rope_c, rope_s1, rope_s2 = _rope_tables(seq)
    bias = jnp.asarray(_BIAS_NP)
    cbias = jnp.asarray(_CLASS_BIAS_NP)
    for l in range(depth):
        qka, vta, qkb, vtb, qm, cls, kmean = _qkv_call(x2, norm_mix, w_in, l, rope_c, rope_s1, rope_s2, batch, seq)
        o, lse, ocls, lsecls = _attn_call(qka, vta, qkb, vtb, qm, cls, kmean, mem2, norm_mem,
                                          w_mem_kv, l, bias, cbias, batch, seq)
        x2 = _mix_call(x2, o, lse, ocls, lsecls, norm_mix, w_in, w_proj_a, w_proj_b, w_proj_m, w_out, l)
        x2 = _mlp_call(x2, norm_mlp, w_up, w_down, norm_final.reshape(1, d), l, final_norm=(l == depth - 1))
    return x2.reshape(batch, seq, d)
```

```python
import functools

import jax
import jax.numpy as jnp
import numpy as np
from jax import lax
from jax.experimental import pallas as pl
from jax.experimental.pallas import tpu as pltpu

D_MODEL = 1024
SEQ = 2048
HEAD_DIM = 64
ROT_DIM = HEAD_DIM // 4
ROPE_THETA = 500000.0
DIL_GROUPS = ((128, 1), (512, 4), (2048, 16))
HEADS_PER_DIL_GROUP = 2
N_HEADS_A = len(DIL_GROUPS) * HEADS_PER_DIL_GROUP
N_HEADS_B = 6
N_HEADS_M = 4
MOBA_BLOCK = 256
MOBA_TOPK = 3
N_BLOCKS = SEQ // MOBA_BLOCK
D_FF = 4 * D_MODEL
WIDTH_A = N_HEADS_A * HEAD_DIM
WIDTH_A_OUT = HEADS_PER_DIL_GROUP * HEAD_DIM
WIDTH_B = N_HEADS_B * HEAD_DIM
WIDTH_M = N_HEADS_M * HEAD_DIM
WIDTH_O = WIDTH_A_OUT + WIDTH_B + WIDTH_M
RMS_EPS = 1e-6
NEG_INF = -1e30
Q_SCALE = HEAD_DIM ** -0.5
LOG2_E = 1.4426950408889634

LANES = 128
LSE_ROWS = 8
PAIR = 2 * HEAD_DIM
SUM_ROWS = 16
TQ = MOBA_BLOCK
CLASS_GROUP = len(DIL_GROUPS) - 1
CLASS_STRIDE = DIL_GROUPS[CLASS_GROUP][1]
CLASS_LEN = SEQ // CLASS_STRIDE
CLASSES_PER_TILE = TQ // CLASS_LEN
TM_QKV = 1024
TM_MIX = 1024
TM_MLP = 512
FF_CHUNK = 1024
STAGE_SLOTS = 2
VMEM_LIMIT = 60 * 1024 * 1024

_B_ALLNEG = 0
_B_CAUSAL = 1


TILE_GROUPS = DIL_GROUPS[:CLASS_GROUP]
assert DIL_GROUPS[CLASS_GROUP][0] >= SEQ and TQ % CLASS_LEN == 0


def _dil_tile_offsets():
    return tuple(tuple(range(min((w + TQ - 1) // TQ, N_BLOCKS - 1) + 1)) for w, _ in TILE_GROUPS)


def _build_bias_tiles():
    c = np.arange(TQ)[:, None]
    r = np.arange(TQ)[None, :]
    tiles = [np.zeros((TQ, TQ), bool), (r - c) >= 0]
    ids = []
    key_rows = []
    for (w, d), offs in zip(TILE_GROUPS, _dil_tile_offsets()):
        per_off = []
        for o in offs:
            diff = o * TQ + r - c
            per_off.append((diff >= 0) & (diff <= w) & (diff % d == 0))
        uniq, gid = [], []
        for t in per_off:
            for k, u in enumerate(uniq):
                if np.array_equal(t, u):
                    gid.append(k)
                    break
            else:
                uniq.append(t)
                gid.append(len(uniq) - 1)
        ids.append(tuple(len(tiles) + k for k in gid))
        tiles.extend(uniq)
        used = [np.flatnonzero(t.any(axis=1)) for t in per_off]
        key_rows.append(tuple((int(u.min()) // LANES * LANES, -(-(int(u.max()) + 1) // LANES) * LANES) for u in used))
    bias = np.where(np.stack(tiles), 0.0, NEG_INF).astype(np.float32)
    return bias, tuple(ids), tuple(key_rows)


_BIAS_NP, _DIL_BIAS_IDS, _DIL_KEY_ROWS = _build_bias_tiles()
_pos = np.arange(CLASS_LEN)
_CLASS_BIAS_NP = np.where(_pos[None, :] >= _pos[:, None], 0.0, NEG_INF).astype(np.float32)

_NT = (((1,), (1,)), ((), ()))
_TN = (((0,), (1,)), ((), ()))
_TN_PLAIN = (((0,), (0,)), ((), ()))
COL_B = 3 * WIDTH_A
COL_M = COL_B + 3 * WIDTH_B
COL_GATES = COL_M + WIDTH_M
GATE_COLS = 3 * D_MODEL


def _rms(x, g):
    return x * lax.rsqrt(jnp.mean(x * x, axis=-1, keepdims=True) + RMS_EPS) * g


def _qkv_kernel(x_ref, g_ref, win_ref, c_ref, s1_ref, s2_ref, qka_ref, vta_ref, qkb_ref, vtb_ref, qm_ref, cls_ref,
                kmean_ref, *, layer):
    tm = x_ref.shape[0]
    halves = (slice(0, tm // 2), slice(tm // 2, tm))
    normed = [_rms(x_ref[rows, :], g_ref[layer:layer + 1, :]) for rows in halves]
    for s, rows in enumerate(halves):
        h = normed[s].astype(jnp.bfloat16)
        cos = c_ref[rows, :]
        sin_lo = s1_ref[rows, :]
        sin_hi = s2_ref[rows, :]
        for col0, qk_ref, vt_ref, width in ((0, qka_ref, vta_ref, WIDTH_A), (COL_B, qkb_ref, vtb_ref, WIDTH_B)):
            z = jnp.dot(h, win_ref[:, col0:col0 + 2 * width].astype(jnp.bfloat16),
                        preferred_element_type=jnp.float32)
            for blk in range(2 * width // LANES):
                zb = z[:, blk * LANES:(blk + 1) * LANES]
                rb = (zb * cos + pltpu.roll(zb, LANES - ROT_DIM // 2, 1) * sin_lo
                      + pltpu.roll(zb, ROT_DIM // 2, 1) * sin_hi)
                if blk < width // LANES:
                    rb = rb * (Q_SCALE * LOG2_E)
                qk_ref[rows, blk * LANES:(blk + 1) * LANES] = rb.astype(jnp.bfloat16)
                if col0 == 0 and blk % (width // LANES) == CLASS_GROUP:
                    part = blk // (width // LANES)
                    cls_ref[part, rows, :] = rb
                if col0 == COL_B and blk >= width // LANES:
                    kcol = (blk - width // LANES) * LANES
                    blocks = slice(rows.start // MOBA_BLOCK, rows.stop // MOBA_BLOCK)
                    kmean_ref[0, blocks, kcol:kcol + LANES] = jnp.mean(rb.reshape(-1, MOBA_BLOCK, LANES), axis=1)
            vt = lax.dot_general(win_ref[:, col0 + 2 * width:col0 + 3 * width].astype(jnp.bfloat16), h, _TN,
                                 preferred_element_type=jnp.float32)
            vt_ref[:, rows] = vt.astype(jnp.bfloat16)
            if col0 == 0:
                cls_ref[2, rows, :] = vt[CLASS_GROUP * PAIR:(CLASS_GROUP + 1) * PAIR, :].T
        qm = jnp.dot(h, win_ref[:, COL_M:COL_GATES].astype(jnp.bfloat16),
                     preferred_element_type=jnp.float32) * (Q_SCALE * LOG2_E)
        if s + 1 < len(halves):
            tail = qm.shape[0] - LSE_ROWS
            qm = jnp.concatenate([qm[:tail], qm[tail:] + _after(jnp.zeros((1, qm.shape[1]), jnp.float32),
                                                               normed[s + 1])], axis=0)
        qm_ref[rows, :] = qm.astype(jnp.bfloat16)


def _qkv_call(x2, g, w_in, layer, rope_c, rope_s1, rope_s2, batch, seq):
    t = x2.shape[0]
    tm = TM_QKV
    nts = seq // tm
    per_layer = lambda i: (layer, 0, 0)
    return pl.pallas_call(
        functools.partial(_qkv_kernel, layer=layer),
        grid=(t // tm,),
        in_specs=[
            pl.BlockSpec((tm, D_MODEL), lambda i: (i, 0)),
            pl.BlockSpec(g.shape, lambda i: (0, 0)),
            pl.BlockSpec((None, D_MODEL, COL_GATES), per_layer),
            pl.BlockSpec((tm, LANES), lambda i: (i % nts, 0)),
            pl.BlockSpec((tm, LANES), lambda i: (i % nts, 0)),
            pl.BlockSpec((tm, LANES), lambda i: (i % nts, 0)),
        ],
        out_specs=[
            pl.BlockSpec((tm, 2 * WIDTH_A), lambda i: (i, 0)),
            pl.BlockSpec((WIDTH_A, tm), lambda i: (i // nts, i % nts)),
            pl.BlockSpec((tm, 2 * WIDTH_B), lambda i: (i, 0)),
            pl.BlockSpec((WIDTH_B, tm), lambda i: (i // nts, i % nts)),
            pl.BlockSpec((tm, WIDTH_M), lambda i: (i, 0)),
            pl.BlockSpec((3, tm, PAIR), lambda i: (0, i, 0)),
            pl.BlockSpec((1, tm // MOBA_BLOCK, WIDTH_B), lambda i: (i, 0, 0)),
        ],
        out_shape=[
            jax.ShapeDtypeStruct((t, 2 * WIDTH_A), jnp.bfloat16),
            jax.ShapeDtypeStruct((batch * WIDTH_A, seq), jnp.bfloat16),
            jax.ShapeDtypeStruct((t, 2 * WIDTH_B), jnp.bfloat16),
            jax.ShapeDtypeStruct((batch * WIDTH_B, seq), jnp.bfloat16),
            jax.ShapeDtypeStruct((t, WIDTH_M), jnp.bfloat16),
            jax.ShapeDtypeStruct((3, t, PAIR), jnp.float32),
            jax.ShapeDtypeStruct((t // tm, tm // MOBA_BLOCK, WIDTH_B), jnp.float32),
        ],
        compiler_params=pltpu.CompilerParams(dimension_semantics=("parallel",), vmem_limit_bytes=VMEM_LIMIT),
        name="qkv_proj",
    )(x2, g, w_in, rope_c, rope_s1, rope_s2)


def _pv(v_heads, p_bf):
    ones = jnp.ones((SUM_ROWS, p_bf.shape[0]), jnp.bfloat16)
    nq = p_bf.shape[1] // len(v_heads)
    return jnp.concatenate([jnp.dot(jnp.concatenate([v, ones], axis=0), p_bf[:, h * nq:(h + 1) * nq],
                                    preferred_element_type=jnp.float32)
                            for h, v in enumerate(v_heads)], axis=1)


def _after(x, *matmul_results):
    for r in matmul_results:
        bits = lax.bitcast_convert_type(r[-1:, :x.shape[1]], jnp.uint32)
        bits = lax.shift_right_logical(lax.shift_right_logical(bits, jnp.uint32(16)), jnp.uint32(16))
        x = x + lax.bitcast_convert_type(bits, jnp.float32)
    return x


def _scores_t(k_tile, q_heads):
    return lax.dot_general(k_tile, q_heads, _NT, preferred_element_type=jnp.float32)


def _stack_heads(q, n_heads):
    lane = lax.broadcasted_iota(jnp.int32, q.shape, 1)
    zero = jnp.zeros((), q.dtype)
    return jnp.concatenate([jnp.where((lane >= h * HEAD_DIM) & (lane < (h + 1) * HEAD_DIM), q, zero)
                            for h in range(n_heads)], axis=0)


def _head_rows(first_head, n_heads):
    return [slice((first_head + h) * HEAD_DIM, (first_head + h + 1) * HEAD_DIM) for h in range(n_heads)]


def _lanes_to_rows(row, n_heads):
    nq = row.shape[1] // n_heads
    return jnp.concatenate([jnp.broadcast_to(row[:, h * nq:(h + 1) * nq], (HEAD_DIM, nq)) for h in range(n_heads)],
                           axis=0)


def _heads_to_rows(o_t, n_heads):
    nq = o_t.shape[1] // n_heads
    return jnp.concatenate([o_t[:, h * nq:(h + 1) * nq] for h in range(n_heads)], axis=0)


def _attn_kernel(qa_ref, ka_ref, vta_ref, qb_ref, kb_ref, vtb_ref, qm_ref, cls_ref, kmean_ref, mem_ref, gmem_ref,
                 wmkv_ref, bias_ref, cbias_ref, o_ref, lse_ref, ocls_ref, lsecls_ref,
                 km_ref, vmt_ref, selb_ref, m_ref, acc_ref, raw_ref, rawmax_ref, *, layer):
    qi = pl.program_id(1)

    @pl.when(qi == 0)
    def _per_batch():
        mem_n = _rms(mem_ref[...], gmem_ref[layer:layer + 1, :]).astype(jnp.bfloat16)
        w_mkv = wmkv_ref[...].astype(jnp.bfloat16)
        km_ref[...] = jnp.dot(mem_n, w_mkv[:, :WIDTH_M], preferred_element_type=jnp.float32).astype(jnp.bfloat16)
        vmt_ref[...] = lax.dot_general(w_mkv[:, WIDTH_M:], mem_n, _TN,
                                       preferred_element_type=jnp.float32).astype(jnp.bfloat16)

    def keys_of(j):
        return pl.ds(pl.multiple_of(j * TQ, TQ), TQ)

    def both_heads(bias):
        return jnp.concatenate([bias, bias], axis=1)

    def single_tile(s_t, v_heads):
        m = jnp.max(s_t, axis=0, keepdims=True)
        return m, _pv(v_heads, jnp.exp2(s_t - m).astype(jnp.bfloat16))

    for c in range(CLASSES_PER_TILE):
        rows = pl.ds(qi * CLASSES_PER_TILE + c, CLASS_LEN, stride=CLASS_STRIDE)
        q_heads = _stack_heads(cls_ref[0, rows, :].astype(jnp.bfloat16), HEADS_PER_DIL_GROUP)
        k_cls = cls_ref[1, rows, :].astype(jnp.bfloat16)
        v_t = cls_ref[2, rows, :].T.astype(jnp.bfloat16)
        m, acc = single_tile(_scores_t(k_cls, q_heads) + both_heads(cbias_ref[...]),
                             [v_t[r] for r in _head_rows(0, HEADS_PER_DIL_GROUP)])
        l = acc[HEAD_DIM:HEAD_DIM + 1]
        out = slice(c * CLASS_LEN, (c + 1) * CLASS_LEN)
        ocls_ref[out, :] = _heads_to_rows(acc[:HEAD_DIM] / l, HEADS_PER_DIL_GROUP).T
        lsecls_ref[out, :] = _lanes_to_rows(m + jnp.log2(l), HEADS_PER_DIL_GROUP).T

    a_static = []
    for g, offs in enumerate(_dil_tile_offsets()):
        cols = slice(g * PAIR, (g + 1) * PAIR)
        rows = _head_rows(g * HEADS_PER_DIL_GROUP, HEADS_PER_DIL_GROUP)
        q_heads = _stack_heads(qa_ref[:, cols], HEADS_PER_DIL_GROUP)
        ids = _DIL_BIAS_IDS[g]
        for o in offs:
            lo, hi = _DIL_KEY_ROWS[g][o]
            ks = pl.ds(pl.multiple_of(jnp.maximum(qi - o, 0) * TQ + lo, LANES), hi - lo)
            bid = ids[o] if o == 0 else jnp.where(qi >= o, ids[o], _B_ALLNEG)
            a_static.append((ks, (bid, slice(lo, hi)), cols, rows, q_heads))

    blk = lax.broadcasted_iota(jnp.int32, (N_BLOCKS, 2 * TQ), 0)
    own = keys_of(qi)
    b_heads = []
    for p in range(N_HEADS_B // 2):
        cols = slice(p * PAIR, (p + 1) * PAIR)
        q_heads = _stack_heads(qb_ref[:, cols], 2)
        kmean = jnp.concatenate([kmean_ref[t, :, cols] for t in range(kmean_ref.shape[0])], axis=0)
        kmean_hi = kmean.astype(jnp.bfloat16)
        kmean_lo = (kmean - kmean_hi.astype(jnp.float32)).astype(jnp.bfloat16)
        both = _scores_t(jnp.concatenate([kmean_hi, kmean_lo], axis=0), q_heads)
        gate = both[:N_BLOCKS] + both[N_BLOCKS:]
        gate = jnp.where(blk < qi, gate, NEG_INF)
        beaten = jnp.zeros(gate.shape, jnp.float32)
        for j in range(N_BLOCKS):
            gj = gate[j:j + 1, :]
            wins_tie = jnp.where(gj >= gate, 1.0, 0.0)
            wins_strict = jnp.where(gj > gate, 1.0, 0.0)
            beaten = beaten + jnp.where(blk > j, wins_tie, wins_strict)
        selb_ref[p] = jnp.where((beaten < MOBA_TOPK) & (blk < qi), 0.0, NEG_INF)
        b_heads.append((cols, _head_rows(2 * p, 2), q_heads))
    n_b = len(b_heads)

    def a_scores(part):
        ks, _, cols, _, q_heads = part
        return _scores_t(ka_ref[ks, cols], q_heads)

    def b_scores(p, ks):
        cols, _, q_heads = b_heads[p]
        return _scores_t(kb_ref[ks, cols], q_heads)

    n_a = len(a_static)
    raw = a_scores(a_static[0])
    a_biased = []
    m_a = None
    for i in range(n_a):
        nxt = a_scores(a_static[i + 1]) if i + 1 < n_a else b_scores(0, own)
        s_t = raw + both_heads(bias_ref[a_static[i][1]])
        part_max = _after(jnp.max(s_t, axis=0, keepdims=True), nxt)
        m_a = part_max if m_a is None else jnp.maximum(m_a, part_max)
        a_biased.append(s_t)
        raw = nxt
    m_heads = _stack_heads(qm_ref[...], N_HEADS_M)
    ahead = [lambda: b_scores(1, own), lambda: b_scores(2, own), lambda: _scores_t(km_ref[...], m_heads),
             lambda: b_scores(0, keys_of(0)), lambda: b_scores(1, keys_of(0))]
    last_first_past = b_scores(n_b - 1, keys_of(0))
    later = [raw]
    assert n_a >= len(ahead)
    neg_m = -m_a
    acc_a = None
    for i in range(n_a):
        ks, _, _, rows, _ = a_static[i]
        p_t = jnp.exp2(a_biased[i] + neg_m).astype(jnp.bfloat16)
        if i < len(ahead):
            later.append(ahead[i]())
            neg_m = _after(neg_m, later[-1])
        part_acc = _pv([vta_ref[r, ks] for r in rows], p_t)
        acc_a = part_acc if acc_a is None else acc_a + part_acc
    l_a = acc_a[HEAD_DIM:HEAD_DIM + 1]
    o_tiles = [acc_a[:HEAD_DIM] / l_a]
    lse_a = m_a + jnp.log2(l_a)
    lse_ref[...] = jnp.concatenate([jnp.broadcast_to(lse_a[:, h * TQ:(h + 1) * TQ], (LSE_ROWS // HEADS_PER_DIL_GROUP, TQ))
                                    for h in range(HEADS_PER_DIL_GROUP)], axis=0)

    for p, (_, rows, _) in enumerate(b_heads):
        m_ref[p], acc_ref[p] = single_tile(later[p] + both_heads(bias_ref[_B_CAUSAL]),
                                           [vtb_ref[r, own] for r in rows])
    _, acc_m = single_tile(later[3], [vmt_ref[r, :] for r in _head_rows(0, N_HEADS_M)])
    for c, first in enumerate((later[4], later[5], last_first_past)):
        raw_ref[c] = first
        rawmax_ref[c] = jnp.max(first, axis=0, keepdims=True)

    def past_tile(j, _):
        ks = keys_of(j)
        ks_next = keys_of(jnp.minimum(j + 1, qi - 1))
        scores = [raw_ref[c] for c in range(n_b)]
        for c in range(n_b):
            issued = b_scores(c, ks_next)
            scores.append(issued)
            sel = selb_ref[c, pl.ds(j, 1), :]
            m_old = m_ref[c]
            m_new = jnp.maximum(m_old, rawmax_ref[c] + sel)
            alpha = _after(jnp.exp2(m_old - m_new), issued)
            p_t = jnp.exp2(scores[c] + (sel - m_new)).astype(jnp.bfloat16)
            acc_ref[c] = alpha * acc_ref[c] + _pv([vtb_ref[r, ks] for r in b_heads[c][1]], p_t)
            m_ref[c] = m_new
        for c in range(n_b):
            raw_ref[c] = scores[n_b + c]
            rawmax_ref[c] = jnp.max(scores[n_b + c], axis=0, keepdims=True)
        return 0

    lax.fori_loop(0, qi, past_tile, 0)

    for acc in (*[acc_ref[c] for c in range(n_b)], acc_m):
        o_tiles.append(acc[:HEAD_DIM] / acc[HEAD_DIM:HEAD_DIM + 1])
    row = 0
    for o_t in o_tiles:
        for h in range(o_t.shape[1] // TQ):
            o_ref[row:row + HEAD_DIM, :] = o_t[:, h * TQ:(h + 1) * TQ].astype(jnp.bfloat16)
            row += HEAD_DIM


def _attn_call(qka, vta, qkb, vtb, qm, cls, kmean, mem2, gmem, wmkv, layer, bias, cbias, batch, seq):
    nq = seq // TQ
    n_mem = mem2.shape[0] // batch
    per_layer = lambda b, q: (layer, 0, 0)
    per_tile = lambda b, q: (b * nq + q, 0)
    return pl.pallas_call(
        functools.partial(_attn_kernel, layer=layer),
        grid=(batch, nq),
        in_specs=[
            pl.BlockSpec((TQ, WIDTH_A), lambda b, q: (b * nq + q, 0)),
            pl.BlockSpec((seq, WIDTH_A), lambda b, q: (b, 1)),
            pl.BlockSpec((WIDTH_A, seq), lambda b, q: (b, 0)),
            pl.BlockSpec((TQ, WIDTH_B), lambda b, q: (b * nq + q, 0)),
            pl.BlockSpec((seq, WIDTH_B), lambda b, q: (b, 1)),
            pl.BlockSpec((WIDTH_B, seq), lambda b, q: (b, 0)),
            pl.BlockSpec((TQ, WIDTH_M), lambda b, q: (b * nq + q, 0)),
            pl.BlockSpec((3, seq, PAIR), lambda b, q: (0, b, 0)),
            pl.BlockSpec((kmean.shape[0] // batch,) + kmean.shape[1:], lambda b, q: (b, 0, 0)),
            pl.BlockSpec((n_mem, D_MODEL), lambda b, q: (b, 0)),
            pl.BlockSpec(gmem.shape, lambda b, q: (0, 0)),
            pl.BlockSpec((None, D_MODEL, 2 * WIDTH_M), per_layer),
            pl.BlockSpec(bias.shape, lambda b, q: (0, 0, 0)),
            pl.BlockSpec(cbias.shape, lambda b, q: (0, 0)),
        ],
        out_specs=[pl.BlockSpec((WIDTH_O, TQ), lambda b, q: (0, b * nq + q)),
                   pl.BlockSpec((LSE_ROWS, TQ), lambda b, q: (0, b * nq + q)),
                   pl.BlockSpec((TQ, PAIR), per_tile), pl.BlockSpec((TQ, PAIR), per_tile)],
        out_shape=[jax.ShapeDtypeStruct((WIDTH_O, batch * seq), jnp.bfloat16),
                   jax.ShapeDtypeStruct((LSE_ROWS, batch * seq), jnp.float32),
                   jax.ShapeDtypeStruct((batch * seq, PAIR), jnp.float32),
                   jax.ShapeDtypeStruct((batch * seq, PAIR), jnp.float32)],
        scratch_shapes=[
            pltpu.VMEM((n_mem, WIDTH_M), jnp.bfloat16),
            pltpu.VMEM((WIDTH_M, n_mem), jnp.bfloat16),
            pltpu.VMEM((N_HEADS_B // 2, N_BLOCKS, 2 * TQ), jnp.float32),
            pltpu.VMEM((N_HEADS_B // 2, 1, 2 * TQ), jnp.float32),
            pltpu.VMEM((N_HEADS_B // 2, HEAD_DIM + SUM_ROWS, 2 * TQ), jnp.float32),
            pltpu.VMEM((N_HEADS_B // 2, TQ, 2 * TQ), jnp.float32),
            pltpu.VMEM((N_HEADS_B // 2, 1, 2 * TQ), jnp.float32),
        ],
        compiler_params=pltpu.CompilerParams(dimension_semantics=("parallel", "arbitrary"),
                                             vmem_limit_bytes=VMEM_LIMIT),
        name="attn",
    )(qka, qka, vta, qkb, qkb, vtb, qm, cls, kmean, mem2, gmem, wmkv, bias, cbias)


def _mix_kernel(x_ref, o_ref, lse_ref, ocls_ref, lsecls_ref, g_ref, wg_ref, wpa_ref, wpb_ref, wpm_ref, wo_ref,
                out_ref, ocn_ref, lcn_ref, *, layer):
    tm = x_ref.shape[0]
    per_class = tm // CLASS_STRIDE
    first = (pl.program_id(0) % (SEQ // tm)) * per_class
    for r in range(CLASS_STRIDE):
        src = pl.ds(pl.multiple_of(r * CLASS_LEN + first, per_class), per_class)
        dst = pl.ds(r, per_class, stride=CLASS_STRIDE)
        ocn_ref[dst, :] = ocls_ref[src, :]
        lcn_ref[dst, :] = lsecls_ref[src, :]
    lse_t = lse_ref[...]
    per_head = LSE_ROWS // HEADS_PER_DIL_GROUP
    lse_tiled = jnp.concatenate([jnp.broadcast_to(lse_t[h * per_head:h * per_head + 1], (HEAD_DIM, tm))
                                 for h in range(HEADS_PER_DIL_GROUP)], axis=0)
    w_cls = 1.0 / (1.0 + jnp.exp2(lse_tiled - lcn_ref[...].T))
    o_tiled = o_ref[:WIDTH_A_OUT, :].astype(jnp.float32)
    o_a = (o_tiled + w_cls * (ocn_ref[...].T - o_tiled)).astype(jnp.bfloat16)
    half = tm // 2
    halves = [slice(0, half), slice(half, tm)]
    xs = [x_ref[rows, :] for rows in halves]
    hs = [_rms(x, g_ref[layer:layer + 1, :]).astype(jnp.bfloat16) for x in xs]

    def gate_scores(s, i):
        return jnp.dot(hs[s], wg_ref[0, :, i * D_MODEL:(i + 1) * D_MODEL].astype(jnp.bfloat16),
                       preferred_element_type=jnp.float32)

    def wait_for(y, matmul_result):
        tail = y.shape[0] - LSE_ROWS
        return jnp.concatenate([y[:tail], y[tail:] + (_after(jnp.zeros((1, y.shape[1]), jnp.float32), matmul_result))],
                               axis=0)

    def gated(s, first_gate):
        y = None
        col = 0
        for i, wp_ref in enumerate((wpa_ref, wpb_ref, wpm_ref)):
            width = wp_ref.shape[0]
            gate = jax.nn.sigmoid(first_gate if i == 0 else gate_scores(s, i))
            o_i = o_a[:, halves[s]] if i == 0 else o_ref[col:col + width, halves[s]]
            branch = gate * lax.dot_general(o_i, wp_ref[...].astype(jnp.bfloat16), _TN_PLAIN,
                                            preferred_element_type=jnp.float32)
            y = branch if y is None else y + branch
            col += width
        return y

    def project(y):
        return jnp.dot(y.astype(jnp.bfloat16), wo_ref[...].astype(jnp.bfloat16), preferred_element_type=jnp.float32)

    g0_a = gate_scores(0, 0)
    g0_b = gate_scores(1, 0)
    y_a = wait_for(gated(0, g0_a), g0_b)
    out_a = project(y_a)
    y_b = wait_for(gated(1, g0_b), out_a)
    out_ref[halves[0], :] = xs[0] + out_a
    out_ref[halves[1], :] = xs[1] + project(y_b)


def _mix_call(x2, o, lse, ocls, lsecls, g, w_in, wpa, wpb, wpm, wo, layer):
    t = x2.shape[0]
    tm = TM_MIX
    per_layer = lambda i: (layer, 0, 0)
    return pl.pallas_call(
        functools.partial(_mix_kernel, layer=layer),
        grid=(t // tm,),
        in_specs=[
            pl.BlockSpec((tm, D_MODEL), lambda i: (i, 0)),
            pl.BlockSpec((WIDTH_O, tm), lambda i: (0, i)),
            pl.BlockSpec((LSE_ROWS, tm), lambda i: (0, i)),
            pl.BlockSpec((SEQ, PAIR), lambda i: (i // (SEQ // tm), 0)),
            pl.BlockSpec((SEQ, PAIR), lambda i: (i // (SEQ // tm), 0)),
            pl.BlockSpec(g.shape, lambda i: (0, 0)),
            pl.BlockSpec((pl.Element(1), pl.Element(D_MODEL), pl.Element(GATE_COLS)), lambda i: (layer, 0, COL_GATES),
                         pipeline_mode=pl.Buffered(1)),
            pl.BlockSpec((None,) + wpa.shape[1:], per_layer, pipeline_mode=pl.Buffered(1)),
            pl.BlockSpec((None,) + wpb.shape[1:], per_layer, pipeline_mode=pl.Buffered(1)),
            pl.BlockSpec((None,) + wpm.shape[1:], per_layer, pipeline_mode=pl.Buffered(1)),
            pl.BlockSpec((None,) + wo.shape[1:], per_layer, pipeline_mode=pl.Buffered(1)),
        ],
        out_specs=pl.BlockSpec((tm, D_MODEL), lambda i: (i, 0)),
        out_shape=jax.ShapeDtypeStruct((t, D_MODEL), jnp.float32),
        scratch_shapes=[pltpu.VMEM((tm, PAIR), jnp.float32), pltpu.VMEM((tm, PAIR), jnp.float32)],
        compiler_params=pltpu.CompilerParams(dimension_semantics=("parallel",), vmem_limit_bytes=VMEM_LIMIT),
        name="gated_mix",
    )(x2, o, lse, ocls, lsecls, g, w_in, wpa, wpb, wpm, wo)


def _mlp_kernel(x_ref, g_ref, wup_hbm, wdown_hbm, gfin_ref, out_ref, wup_ref, wdown_ref, up_stage, down_stage, sems,
                *, layer, final_norm):
    n_chunks = D_FF // FF_CHUNK
    chunks = [pl.ds(c * FF_CHUNK, FF_CHUNK) for c in range(n_chunks)]

    def up_copy(c):
        return pltpu.make_async_copy(wup_hbm.at[layer, :, chunks[c]], up_stage.at[c % STAGE_SLOTS],
                                     sems.at[0, c % STAGE_SLOTS])

    def down_copy(c):
        return pltpu.make_async_copy(wdown_hbm.at[layer, chunks[c], :], down_stage.at[c % STAGE_SLOTS],
                                     sems.at[1, c % STAGE_SLOTS])

    def run(stream):
        if stream:
            for c in range(min(STAGE_SLOTS, n_chunks)):
                up_copy(c).start()
                down_copy(c).start(priority=1)
        x = x_ref[...]
        hm = _rms(x, g_ref[layer:layer + 1, :]).astype(jnp.bfloat16)
        acc = x
        for c in range(n_chunks):
            if stream:
                up_copy(c).wait()
                wup_ref[:, chunks[c]] = up_stage[c % STAGE_SLOTS].astype(jnp.bfloat16)
                down_copy(c).wait()
                wdown_ref[chunks[c], :] = down_stage[c % STAGE_SLOTS].astype(jnp.bfloat16)
                if c + STAGE_SLOTS < n_chunks:
                    up_copy(c + STAGE_SLOTS).start()
                    down_copy(c + STAGE_SLOTS).start(priority=1)
            u = jnp.dot(hm, wup_ref[:, chunks[c]], preferred_element_type=jnp.float32)
            u = jnp.square(jnp.maximum(u, 0.0)).astype(jnp.bfloat16)
            acc = acc + jnp.dot(u, wdown_ref[chunks[c], :], preferred_element_type=jnp.float32)
        out_ref[...] = _rms(acc, gfin_ref[...]) if final_norm else acc

    first_step = pl.program_id(0) == 0
    pl.when(first_step)(lambda: run(True))
    pl.when(jnp.logical_not(first_step))(lambda: run(False))


def _mlp_call(x2, g, wup, wdown, gfin, layer, final_norm):
    t = x2.shape[0]
    tm = TM_MLP
    const = lambda i: (0, 0)
    per_layer = lambda i: (layer, 0, 0)
    return pl.pallas_call(
        functools.partial(_mlp_kernel, layer=layer, final_norm=final_norm),
        grid=(t // tm,),
        in_specs=[
            pl.BlockSpec((tm, D_MODEL), lambda i: (i, 0)),
            pl.BlockSpec(g.shape, const),
            pl.BlockSpec(memory_space=pl.ANY),
            pl.BlockSpec(memory_space=pl.ANY),
            pl.BlockSpec((1, D_MODEL), const),
        ],
        out_specs=pl.BlockSpec((tm, D_MODEL), lambda i: (i, 0)),
        out_shape=jax.ShapeDtypeStruct((t, D_MODEL), jnp.float32),
        scratch_shapes=[pltpu.VMEM(wup.shape[1:], jnp.bfloat16), pltpu.VMEM(wdown.shape[1:], jnp.bfloat16),
                        pltpu.VMEM((STAGE_SLOTS, wup.shape[1], FF_CHUNK), jnp.float32),
                        pltpu.VMEM((STAGE_SLOTS, FF_CHUNK, wdown.shape[2]), jnp.float32),
                        pltpu.SemaphoreType.DMA((2, STAGE_SLOTS))],
        compiler_params=pltpu.CompilerParams(dimension_semantics=("arbitrary",), vmem_limit_bytes=VMEM_LIMIT),
        name="mlp",
    )(x2, g, wup, wdown, gfin)


def _rope_tables(seq):
    half = ROT_DIM // 2
    inv_freq = (1.0 / (np.float32(ROPE_THETA) ** (np.arange(0, ROT_DIM, 2, dtype=np.float32) / np.float32(ROT_DIM))))
    ang = np.arange(seq, dtype=np.float32)[:, None] * inv_freq.astype(np.float32)[None, :]
    cos, sin = np.cos(ang).astype(np.float32), np.sin(ang).astype(np.float32)
    ones = np.ones((seq, HEAD_DIM - ROT_DIM), np.float32)
    zeros_half = np.zeros((seq, half), np.float32)
    zeros_rest = np.zeros((seq, HEAD_DIM - ROT_DIM), np.float32)
    c_head = np.concatenate([cos, cos, ones], axis=1)
    lo_head = np.concatenate([-sin, zeros_half, zeros_rest], axis=1)
    hi_head = np.concatenate([zeros_half, sin, zeros_rest], axis=1)
    rep = LANES // HEAD_DIM
    return tuple(jnp.asarray(np.tile(t, (1, rep))) for t in (c_head, lo_head, hi_head))


def kernel(x, mem, norm_mix, w_in, w_proj_a, w_proj_b, w_proj_m, w_out, norm_mem, w_mem_kv, norm_mlp, w_up,
           w_down, norm_final):
    batch, seq, d = x.shape
    assert d == D_MODEL and seq == SEQ and seq % TM_QKV == 0
    depth = w_in.shape[0]
    x2 = x.reshape(batch * seq, d)
    mem2 = mem.reshape(batch * mem.shape[1], d)
    rope_c, rope_s1, rope_s2 = _rope_tables(seq)
    bias = jnp.asarray(_BIAS_NP)
    cbias = jnp.asarray(_CLASS_BIAS_NP)
    for l in range(depth):
        qka, vta, qkb, vtb, qm, cls, kmean = _qkv_call(x2, norm_mix, w_in, l, rope_c, rope_s1, rope_s2, batch, seq)
        o, lse, ocls, lsecls = _attn_call(qka, vta, qkb, vtb, qm, cls, kmean, mem2, norm_mem,
                                          w_mem_kv, l, bias, cbias, batch, seq)
        x2 = _mix_call(x2, o, lse, ocls, lsecls, norm_mix, w_in, w_proj_a, w_proj_b, w_proj_m, w_out, l)
        x2 = _mlp_call(x2, norm_mlp, w_up, w_down, norm_final.reshape(1, d), l, final_norm=(l == depth - 1))
    return x2.reshape(batch, seq, d)
```

```python
import functools

import jax
import jax.numpy as jnp
import numpy as np
from jax import lax
from jax.experimental import pallas as pl
from jax.experimental.pallas import tpu as pltpu

D_MODEL = 1024
SEQ = 2048
HEAD_DIM = 64
ROT_DIM = HEAD_DIM // 4
ROPE_THETA = 500000.0
DIL_GROUPS = ((128, 1), (512, 4), (2048, 16))
HEADS_PER_DIL_GROUP = 2
N_HEADS_A = len(DIL_GROUPS) * HEADS_PER_DIL_GROUP
N_HEADS_B = 6
N_HEADS_M = 4
MOBA_BLOCK = 256
MOBA_TOPK = 3
N_BLOCKS = SEQ // MOBA_BLOCK
D_FF = 4 * D_MODEL
WIDTH_A = N_HEADS_A * HEAD_DIM
WIDTH_A_OUT = HEADS_PER_DIL_GROUP * HEAD_DIM
WIDTH_B = N_HEADS_B * HEAD_DIM
WIDTH_M = N_HEADS_M * HEAD_DIM
WIDTH_O = WIDTH_A_OUT + WIDTH_B + WIDTH_M
RMS_EPS = 1e-6
NEG_INF = -1e30
Q_SCALE = HEAD_DIM ** -0.5
LOG2_E = 1.4426950408889634

LANES = 128
LSE_ROWS = 8
PAIR = 2 * HEAD_DIM
SUM_ROWS = 16
TQ = MOBA_BLOCK
CLASS_GROUP = len(DIL_GROUPS) - 1
CLASS_STRIDE = DIL_GROUPS[CLASS_GROUP][1]
CLASS_LEN = SEQ // CLASS_STRIDE
CLASSES_PER_TILE = TQ // CLASS_LEN
TM_QKV = 1024
TM_MIX = 1024
TM_MLP = 512
FF_CHUNK = 1024
STAGE_SLOTS = 2
VMEM_LIMIT = 60 * 1024 * 1024

_B_ALLNEG = 0
_B_CAUSAL = 1


TILE_GROUPS = DIL_GROUPS[:CLASS_GROUP]
assert DIL_GROUPS[CLASS_GROUP][0] >= SEQ and TQ % CLASS_LEN == 0


def _dil_tile_offsets():
    return tuple(tuple(range(min((w + TQ - 1) // TQ, N_BLOCKS - 1) + 1)) for w, _ in TILE_GROUPS)


def _build_bias_tiles():
    c = np.arange(TQ)[:, None]
    r = np.arange(TQ)[None, :]
    tiles = [np.zeros((TQ, TQ), bool), (r - c) >= 0]
    ids = []
    key_rows = []
    for (w, d), offs in zip(TILE_GROUPS, _dil_tile_offsets()):
        per_off = []
        for o in offs:
            diff = o * TQ + r - c
            per_off.append((diff >= 0) & (diff <= w) & (diff % d == 0))
        uniq, gid = [], []
        for t in per_off:
            for k, u in enumerate(uniq):
                if np.array_equal(t, u):
                    gid.append(k)
                    break
            else:
                uniq.append(t)
                gid.append(len(uniq) - 1)
        ids.append(tuple(len(tiles) + k for k in gid))
        tiles.extend(uniq)
        used = [np.flatnonzero(t.any(axis=1)) for t in per_off]
        key_rows.append(tuple((int(u.min()) // LANES * LANES, -(-(int(u.max()) + 1) // LANES) * LANES) for u in used))
    bias = np.where(np.stack(tiles), 0.0, NEG_INF).astype(np.float32)
    return bias, tuple(ids), tuple(key_rows)


_BIAS_NP, _DIL_BIAS_IDS, _DIL_KEY_ROWS = _build_bias_tiles()
_pos = np.arange(CLASS_LEN)
_CLASS_BIAS_NP = np.where(_pos[None, :] >= _pos[:, None], 0.0, NEG_INF).astype(np.float32)

_NT = (((1,), (1,)), ((), ()))
_TN = (((0,), (1,)), ((), ()))
_TN_PLAIN = (((0,), (0,)), ((), ()))
COL_B = 3 * WIDTH_A
COL_M = COL_B + 3 * WIDTH_B
COL_GATES = COL_M + WIDTH_M
GATE_COLS = 3 * D_MODEL


def _rms(x, g):
    return x * lax.rsqrt(jnp.mean(x * x, axis=-1, keepdims=True) + RMS_EPS) * g


def _qkv_kernel(x_ref, g_ref, win_ref, c_ref, s1_ref, s2_ref, qka_ref, vta_ref, qkb_ref, vtb_ref, qm_ref, cls_ref,
                kmean_ref, *, layer):
    tm = x_ref.shape[0]
    halves = (slice(0, tm // 2), slice(tm // 2, tm))
    normed = [_rms(x_ref[rows, :], g_ref[layer:layer + 1, :]) for rows in halves]
    for s, rows in enumerate(halves):
        h = normed[s].astype(jnp.bfloat16)
        cos = c_ref[rows, :]
        sin_lo = s1_ref[rows, :]
        sin_hi = s2_ref[rows, :]
        for col0, qk_ref, vt_ref, width in ((0, qka_ref, vta_ref, WIDTH_A), (COL_B, qkb_ref, vtb_ref, WIDTH_B)):
            z = jnp.dot(h, win_ref[:, col0:col0 + 2 * width].astype(jnp.bfloat16),
                        preferred_element_type=jnp.float32)
            for blk in range(2 * width // LANES):
                zb = z[:, blk * LANES:(blk + 1) * LANES]
                rb = (zb * cos + pltpu.roll(zb, LANES - ROT_DIM // 2, 1) * sin_lo
                      + pltpu.roll(zb, ROT_DIM // 2, 1) * sin_hi)
                if blk < width // LANES:
                    rb = rb * (Q_SCALE * LOG2_E)
                qk_ref[rows, blk * LANES:(blk + 1) * LANES] = rb.astype(jnp.bfloat16)
                if col0 == 0 and blk % (width // LANES) == CLASS_GROUP:
                    part = blk // (width // LANES)
                    cls_ref[part, rows, :] = rb
                if col0 == COL_B and blk >= width // LANES:
                    kcol = (blk - width // LANES) * LANES
                    blocks = slice(rows.start // MOBA_BLOCK, rows.stop // MOBA_BLOCK)
                    kmean_ref[0, blocks, kcol:kcol + LANES] = jnp.mean(rb.reshape(-1, MOBA_BLOCK, LANES), axis=1)
            vt = lax.dot_general(win_ref[:, col0 + 2 * width:col0 + 3 * width].astype(jnp.bfloat16), h, _TN,
                                 preferred_element_type=jnp.float32)
            vt_ref[:, rows] = vt.astype(jnp.bfloat16)
            if col0 == 0:
                cls_ref[2, rows, :] = vt[CLASS_GROUP * PAIR:(CLASS_GROUP + 1) * PAIR, :].T
        qm = jnp.dot(h, win_ref[:, COL_M:COL_GATES].astype(jnp.bfloat16),
                     preferred_element_type=jnp.float32) * (Q_SCALE * LOG2_E)
        if s + 1 < len(halves):
            tail = qm.shape[0] - LSE_ROWS
            qm = jnp.concatenate([qm[:tail], qm[tail:] + _after(jnp.zeros((1, qm.shape[1]), jnp.float32),
                                                               normed[s + 1])], axis=0)
        qm_ref[rows, :] = qm.astype(jnp.bfloat16)


def _qkv_call(x2, g, w_in, layer, rope_c, rope_s1, rope_s2, batch, seq):
    t = x2.shape[0]
    tm = TM_QKV
    nts = seq // tm
    per_layer = lambda i: (layer, 0, 0)
    return pl.pallas_call(
        functools.partial(_qkv_kernel, layer=layer),
        grid=(t // tm,),
        in_specs=[
            pl.BlockSpec((tm, D_MODEL), lambda i: (i, 0)),
            pl.BlockSpec(g.shape, lambda i: (0, 0)),
            pl.BlockSpec((None, D_MODEL, COL_GATES), per_layer),
            pl.BlockSpec((tm, LANES), lambda i: (i % nts, 0)),
            pl.BlockSpec((tm, LANES), lambda i: (i % nts, 0)),
            pl.BlockSpec((tm, LANES), lambda i: (i % nts, 0)),
        ],
        out_specs=[
            pl.BlockSpec((tm, 2 * WIDTH_A), lambda i: (i, 0)),
            pl.BlockSpec((WIDTH_A, tm), lambda i: (i // nts, i % nts)),
            pl.BlockSpec((tm, 2 * WIDTH_B), lambda i: (i, 0)),
            pl.BlockSpec((WIDTH_B, tm), lambda i: (i // nts, i % nts)),
            pl.BlockSpec((tm, WIDTH_M), lambda i: (i, 0)),
            pl.BlockSpec((3, tm, PAIR), lambda i: (0, i, 0)),
            pl.BlockSpec((1, tm // MOBA_BLOCK, WIDTH_B), lambda i: (i, 0, 0)),
        ],
        out_shape=[
            jax.ShapeDtypeStruct((t, 2 * WIDTH_A), jnp.bfloat16),
            jax.ShapeDtypeStruct((batch * WIDTH_A, seq), jnp.bfloat16),
            jax.ShapeDtypeStruct((t, 2 * WIDTH_B), jnp.bfloat16),
            jax.ShapeDtypeStruct((batch * WIDTH_B, seq), jnp.bfloat16),
            jax.ShapeDtypeStruct((t, WIDTH_M), jnp.bfloat16),
            jax.ShapeDtypeStruct((3, t, PAIR), jnp.float32),
            jax.ShapeDtypeStruct((t // tm, tm // MOBA_BLOCK, WIDTH_B), jnp.float32),
        ],
        compiler_params=pltpu.CompilerParams(dimension_semantics=("parallel",), vmem_limit_bytes=VMEM_LIMIT),
        name="qkv_proj",
    )(x2, g, w_in, rope_c, rope_s1, rope_s2)


def _pv(v_heads, p_bf):
    ones = jnp.ones((SUM_ROWS, p_bf.shape[0]), jnp.bfloat16)
    nq = p_bf.shape[1] // len(v_heads)
    return jnp.concatenate([jnp.dot(jnp.concatenate([v, ones], axis=0), p_bf[:, h * nq:(h + 1) * nq],
                                    preferred_element_type=jnp.float32)
                            for h, v in enumerate(v_heads)], axis=1)


def _after(x, *matmul_results):
    for r in matmul_results:
        bits = lax.bitcast_convert_type(r[-1:, :x.shape[1]], jnp.uint32)
        bits = lax.shift_right_logical(lax.shift_right_logical(bits, jnp.uint32(16)), jnp.uint32(16))
        x = x + lax.bitcast_convert_type(bits, jnp.float32)
    return x


def _scores_t(k_tile, q_heads):
    return lax.dot_general(k_tile, q_heads, _NT, preferred_element_type=jnp.float32)


def _stack_heads(q, n_heads):
    lane = lax.broadcasted_iota(jnp.int32, q.shape, 1)
    zero = jnp.zeros((), q.dtype)
    return jnp.concatenate([jnp.where((lane >= h * HEAD_DIM) & (lane < (h + 1) * HEAD_DIM), q, zero)
                            for h in range(n_heads)], axis=0)


def _head_rows(first_head, n_heads):
    return [slice((first_head + h) * HEAD_DIM, (first_head + h + 1) * HEAD_DIM) for h in range(n_heads)]


def _lanes_to_rows(row, n_heads):
    nq = row.shape[1] // n_heads
    return jnp.concatenate([jnp.broadcast_to(row[:, h * nq:(h + 1) * nq], (HEAD_DIM, nq)) for h in range(n_heads)],
                           axis=0)


def _heads_to_rows(o_t, n_heads):
    nq = o_t.shape[1] // n_heads
    return jnp.concatenate([o_t[:, h * nq:(h + 1) * nq] for h in range(n_heads)], axis=0)


def _attn_kernel(qa_ref, ka_ref, vta_ref, qb_ref, kb_ref, vtb_ref, qm_ref, cls_ref, kmean_ref, mem_ref, gmem_ref,
                 wmkv_ref, bias_ref, cbias_ref, o_ref, lse_ref, ocls_ref, lsecls_ref,
                 km_ref, vmt_ref, selb_ref, m_ref, acc_ref, raw_ref, rawmax_ref, *, layer):
    qi = pl.program_id(1)

    @pl.when(qi == 0)
    def _per_batch():
        mem_n = _rms(mem_ref[...], gmem_ref[layer:layer + 1, :]).astype(jnp.bfloat16)
        w_mkv = wmkv_ref[...].astype(jnp.bfloat16)
        km_ref[...] = jnp.dot(mem_n, w_mkv[:, :WIDTH_M], preferred_element_type=jnp.float32).astype(jnp.bfloat16)
        vmt_ref[...] = lax.dot_general(w_mkv[:, WIDTH_M:], mem_n, _TN,
                                       preferred_element_type=jnp.float32).astype(jnp.bfloat16)

    def keys_of(j):
        return pl.ds(pl.multiple_of(j * TQ, TQ), TQ)

    def both_heads(bias):
        return jnp.concatenate([bias, bias], axis=1)

    def single_tile(s_t, v_heads):
        m = jnp.max(s_t, axis=0, keepdims=True)
        return m, _pv(v_heads, jnp.exp2(s_t - m).astype(jnp.bfloat16))

    for c in range(CLASSES_PER_TILE):
        rows = pl.ds(qi * CLASSES_PER_TILE + c, CLASS_LEN, stride=CLASS_STRIDE)
        q_heads = _stack_heads(cls_ref[0, rows, :].astype(jnp.bfloat16), HEADS_PER_DIL_GROUP)
        k_cls = cls_ref[1, rows, :].astype(jnp.bfloat16)
        v_t = cls_ref[2, rows, :].T.astype(jnp.bfloat16)
        m, acc = single_tile(_scores_t(k_cls, q_heads) + both_heads(cbias_ref[...]),
                             [v_t[r] for r in _head_rows(0, HEADS_PER_DIL_GROUP)])
        l = acc[HEAD_DIM:HEAD_DIM + 1]
        out = slice(c * CLASS_LEN, (c + 1) * CLASS_LEN)
        ocls_ref[out, :] = _heads_to_rows(acc[:HEAD_DIM] / l, HEADS_PER_DIL_GROUP).T
        lsecls_ref[out, :] = _lanes_to_rows(m + jnp.log2(l), HEADS_PER_DIL_GROUP).T

    a_static = []
    for g, offs in enumerate(_dil_tile_offsets()):
        cols = slice(g * PAIR, (g + 1) * PAIR)
        rows = _head_rows(g * HEADS_PER_DIL_GROUP, HEADS_PER_DIL_GROUP)
        q_heads = _stack_heads(qa_ref[:, cols], HEADS_PER_DIL_GROUP)
        ids = _DIL_BIAS_IDS[g]
        for o in offs:
            lo, hi = _DIL_KEY_ROWS[g][o]
            ks = pl.ds(pl.multiple_of(jnp.maximum(qi - o, 0) * TQ + lo, LANES), hi - lo)
            bid = ids[o] if o == 0 else jnp.where(qi >= o, ids[o], _B_ALLNEG)
            a_static.append((ks, (bid, slice(lo, hi)), cols, rows, q_heads))

    blk = lax.broadcasted_iota(jnp.int32, (N_BLOCKS, 2 * TQ), 0)
    own = keys_of(qi)
    b_heads = []
    for p in range(N_HEADS_B // 2):
        cols = slice(p * PAIR, (p + 1) * PAIR)
        q_heads = _stack_heads(qb_ref[:, cols], 2)
        kmean = jnp.concatenate([kmean_ref[t, :, cols] for t in range(kmean_ref.shape[0])], axis=0)
        kmean_hi = kmean.astype(jnp.bfloat16)
        kmean_lo = (kmean - kmean_hi.astype(jnp.float32)).astype(jnp.bfloat16)
        both = _scores_t(jnp.concatenate([kmean_hi, kmean_lo], axis=0), q_heads)
        gate = both[:N_BLOCKS] + both[N_BLOCKS:]
        gate = jnp.where(blk < qi, gate, NEG_INF)
        beaten = jnp.zeros(gate.shape, jnp.float32)
        for j in range(N_BLOCKS):
            gj = gate[j:j + 1, :]
            wins_tie = jnp.where(gj >= gate, 1.0, 0.0)
            wins_strict = jnp.where(gj > gate, 1.0, 0.0)
            beaten = beaten + jnp.where(blk > j, wins_tie, wins_strict)
        selb_ref[p] = jnp.where((beaten < MOBA_TOPK) & (blk < qi), 0.0, NEG_INF)
        b_heads.append((cols, _head_rows(2 * p, 2), q_heads))
    n_b = len(b_heads)

    def a_scores(part):
        ks, _, cols, _, q_heads = part
        return _scores_t(ka_ref[ks, cols], q_heads)

    def b_scores(p, ks):
        cols, _, q_heads = b_heads[p]
        return _scores_t(kb_ref[ks, cols], q_heads)

    n_a = len(a_static)
    raw = a_scores(a_static[0])
    a_biased = []
    m_a = None
    for i in range(n_a):
        nxt = a_scores(a_static[i + 1]) if i + 1 < n_a else b_scores(0, own)
        s_t = raw + both_heads(bias_ref[a_static[i][1]])
        part_max = _after(jnp.max(s_t, axis=0, keepdims=True), nxt)
        m_a = part_max if m_a is None else jnp.maximum(m_a, part_max)
        a_biased.append(s_t)
        raw = nxt
    m_heads = _stack_heads(qm_ref[...], N_HEADS_M)
    ahead = [lambda: b_scores(1, own), lambda: b_scores(2, own), lambda: _scores_t(km_ref[...], m_heads),
             lambda: b_scores(0, keys_of(0)), lambda: b_scores(1, keys_of(0))]
    last_first_past = b_scores(n_b - 1, keys_of(0))
    later = [raw]
    assert n_a >= len(ahead)
    neg_m = -m_a
    acc_a = None
    for i in range(n_a):
        ks, _, _, rows, _ = a_static[i]
        p_t = jnp.exp2(a_biased[i] + neg_m).astype(jnp.bfloat16)
        if i < len(ahead):
            later.append(ahead[i]())
            neg_m = _after(neg_m, later[-1])
        part_acc = _pv([vta_ref[r, ks] for r in rows], p_t)
        acc_a = part_acc if acc_a is None else acc_a + part_acc
    l_a = acc_a[HEAD_DIM:HEAD_DIM + 1]
    o_tiles = [acc_a[:HEAD_DIM] / l_a]
    lse_a = m_a + jnp.log2(l_a)
    lse_ref[...] = jnp.concatenate([jnp.broadcast_to(lse_a[:, h * TQ:(h + 1) * TQ], (LSE_ROWS // HEADS_PER_DIL_GROUP, TQ))
                                    for h in range(HEADS_PER_DIL_GROUP)], axis=0)

    for p, (_, rows, _) in enumerate(b_heads):
        m_ref[p], acc_ref[p] = single_tile(later[p] + both_heads(bias_ref[_B_CAUSAL]),
                                           [vtb_ref[r, own] for r in rows])
    _, acc_m = single_tile(later[3], [vmt_ref[r, :] for r in _head_rows(0, N_HEADS_M)])
    for c, first in enumerate((later[4], later[5], last_first_past)):
        raw_ref[c] = first
        rawmax_ref[c] = jnp.max(first, axis=0, keepdims=True)

    def past_tile(j, _):
        ks = keys_of(j)
        ks_next = keys_of(jnp.minimum(j + 1, qi - 1))
        scores = [raw_ref[c] for c in range(n_b)]
        for c in range(n_b):
            issued = b_scores(c, ks_next)
            scores.append(issued)
            sel = selb_ref[c, pl.ds(j, 1), :]
            m_old = m_ref[c]
            m_new = jnp.maximum(m_old, rawmax_ref[c] + sel)
            alpha = _after(jnp.exp2(m_old - m_new), issued)
            p_t = jnp.exp2(scores[c] + (sel - m_new)).astype(jnp.bfloat16)
            acc_ref[c] = alpha * acc_ref[c] + _pv([vtb_ref[r, ks] for r in b_heads[c][1]], p_t)
            m_ref[c] = m_new
        for c in range(n_b):
            raw_ref[c] = scores[n_b + c]
            rawmax_ref[c] = jnp.max(scores[n_b + c], axis=0, keepdims=True)
        return 0

    lax.fori_loop(0, qi, past_tile, 0)

    for acc in (*[acc_ref[c] for c in range(n_b)], acc_m):
        o_tiles.append(acc[:HEAD_DIM] / acc[HEAD_DIM:HEAD_DIM + 1])
    row = 0
    for o_t in o_tiles:
        for h in range(o_t.shape[1] // TQ):
            o_ref[row:row + HEAD_DIM, :] = o_t[:, h * TQ:(h + 1) * TQ].astype(jnp.bfloat16)
            row += HEAD_DIM


def _attn_call(qka, vta, qkb, vtb, qm, cls, kmean, mem2, gmem, wmkv, layer, bias, cbias, batch, seq):
    nq = seq // TQ
    n_mem = mem2.shape[0] // batch
    per_layer = lambda b, q: (layer, 0, 0)
    per_tile = lambda b, q: (b * nq + q, 0)
    return pl.pallas_call(
        functools.partial(_attn_kernel, layer=layer),
        grid=(batch, nq),
        in_specs=[
            pl.BlockSpec((TQ, WIDTH_A), lambda b, q: (b * nq + q, 0)),
            pl.BlockSpec((seq, WIDTH_A), lambda b, q: (b, 1)),
            pl.BlockSpec((WIDTH_A, seq), lambda b, q: (b, 0)),
            pl.BlockSpec((TQ, WIDTH_B), lambda b, q: (b * nq + q, 0)),
            pl.BlockSpec((seq, WIDTH_B), lambda b, q: (b, 1)),
            pl.BlockSpec((WIDTH_B, seq), lambda b, q: (b, 0)),
            pl.BlockSpec((TQ, WIDTH_M), lambda b, q: (b * nq + q, 0)),
            pl.BlockSpec((3, seq, PAIR), lambda b, q: (0, b, 0)),
            pl.BlockSpec((kmean.shape[0] // batch,) + kmean.shape[1:], lambda b, q: (b, 0, 0)),
            pl.BlockSpec((n_mem, D_MODEL), lambda b, q: (b, 0)),
            pl.BlockSpec(gmem.shape, lambda b, q: (0, 0)),
            pl.BlockSpec((None, D_MODEL, 2 * WIDTH_M), per_layer),
            pl.BlockSpec(bias.shape, lambda b, q: (0, 0, 0)),
            pl.BlockSpec(cbias.shape, lambda b, q: (0, 0)),
        ],
        out_specs=[pl.BlockSpec((WIDTH_O, TQ), lambda b, q: (0, b * nq + q)),
                   pl.BlockSpec((LSE_ROWS, TQ), lambda b, q: (0, b * nq + q)),
                   pl.BlockSpec((TQ, PAIR), per_tile), pl.BlockSpec((TQ, PAIR), per_tile)],
        out_shape=[jax.ShapeDtypeStruct((WIDTH_O, batch * seq), jnp.bfloat16),
                   jax.ShapeDtypeStruct((LSE_ROWS, batch * seq), jnp.float32),
                   jax.ShapeDtypeStruct((batch * seq, PAIR), jnp.float32),
                   jax.ShapeDtypeStruct((batch * seq, PAIR), jnp.float32)],
        scratch_shapes=[
            pltpu.VMEM((n_mem, WIDTH_M), jnp.bfloat16),
            pltpu.VMEM((WIDTH_M, n_mem), jnp.bfloat16),
            pltpu.VMEM((N_HEADS_B // 2, N_BLOCKS, 2 * TQ), jnp.float32),
            pltpu.VMEM((N_HEADS_B // 2, 1, 2 * TQ), jnp.float32),
            pltpu.VMEM((N_HEADS_B // 2, HEAD_DIM + SUM_ROWS, 2 * TQ), jnp.float32),
            pltpu.VMEM((N_HEADS_B // 2, TQ, 2 * TQ), jnp.float32),
            pltpu.VMEM((N_HEADS_B // 2, 1, 2 * TQ), jnp.float32),
        ],
        compiler_params=pltpu.CompilerParams(dimension_semantics=("parallel", "arbitrary"),
                                             vmem_limit_bytes=VMEM_LIMIT),
        name="attn",
    )(qka, qka, vta, qkb, qkb, vtb, qm, cls, kmean, mem2, gmem, wmkv, bias, cbias)


def _mix_kernel(x_ref, o_ref, lse_ref, ocls_ref, lsecls_ref, g_ref, wg_ref, wpa_ref, wpb_ref, wpm_ref, wo_ref,
                out_ref, ocn_ref, lcn_ref, *, layer):
    tm = x_ref.shape[0]
    per_class = tm // CLASS_STRIDE
    first = (pl.program_id(0) % (SEQ // tm)) * per_class
    for r in range(CLASS_STRIDE):
        src = pl.ds(pl.multiple_of(r * CLASS_LEN + first, per_class), per_class)
        dst = pl.ds(r, per_class, stride=CLASS_STRIDE)
        ocn_ref[dst, :] = ocls_ref[src, :]
        lcn_ref[dst, :] = lsecls_ref[src, :]
    lse_t = lse_ref[...]
    per_head = LSE_ROWS // HEADS_PER_DIL_GROUP
    lse_tiled = jnp.concatenate([jnp.broadcast_to(lse_t[h * per_head:h * per_head + 1], (HEAD_DIM, tm))
                                 for h in range(HEADS_PER_DIL_GROUP)], axis=0)
    w_cls = 1.0 / (1.0 + jnp.exp2(lse_tiled - lcn_ref[...].T))
    o_tiled = o_ref[:WIDTH_A_OUT, :].astype(jnp.float32)
    o_a = (o_tiled + w_cls * (ocn_ref[...].T - o_tiled)).astype(jnp.bfloat16)
    half = tm // 2
    halves = [slice(0, half), slice(half, tm)]
    xs = [x_ref[rows, :] for rows in halves]
    hs = [_rms(x, g_ref[layer:layer + 1, :]).astype(jnp.bfloat16) for x in xs]

    def gate_scores(s, i):
        return jnp.dot(hs[s], wg_ref[0, :, i * D_MODEL:(i + 1) * D_MODEL].astype(jnp.bfloat16),
                       preferred_element_type=jnp.float32)

    def wait_for(y, matmul_result):
        tail = y.shape[0] - LSE_ROWS
        return jnp.concatenate([y[:tail], y[tail:] + (_after(jnp.zeros((1, y.shape[1]), jnp.float32), matmul_result))],
                               axis=0)

    def gated(s, first_gate):
        y = None
        col = 0
        for i, wp_ref in enumerate((wpa_ref, wpb_ref, wpm_ref)):
            width = wp_ref.shape[0]
            gate = jax.nn.sigmoid(first_gate if i == 0 else gate_scores(s, i))
            o_i = o_a[:, halves[s]] if i == 0 else o_ref[col:col + width, halves[s]]
            branch = gate * lax.dot_general(o_i, wp_ref[...].astype(jnp.bfloat16), _TN_PLAIN,
                                            preferred_element_type=jnp.float32)
            y = branch if y is None else y + branch
            col += width
        return y

    def project(y):
        return jnp.dot(y.astype(jnp.bfloat16), wo_ref[...].astype(jnp.bfloat16), preferred_element_type=jnp.float32)

    g0_a = gate_scores(0, 0)
    g0_b = gate_scores(1, 0)
    y_a = wait_for(gated(0, g0_a), g0_b)
    out_a = project(y_a)
    y_b = wait_for(gated(1, g0_b), out_a)
    out_ref[halves[0], :] = xs[0] + out_a
    out_ref[halves[1], :] = xs[1] + project(y_b)


def _mix_call(x2, o, lse, ocls, lsecls, g, w_in, wpa, wpb, wpm, wo, layer):
    t = x2.shape[0]
    tm = TM_MIX
    per_layer = lambda i: (layer, 0, 0)
    return pl.pallas_call(
        functools.partial(_mix_kernel, layer=layer),
        grid=(t // tm,),
        in_specs=[
            pl.BlockSpec((tm, D_MODEL), lambda i: (i, 0)),
            pl.BlockSpec((WIDTH_O, tm), lambda i: (0, i)),
            pl.BlockSpec((LSE_ROWS, tm), lambda i: (0, i)),
            pl.BlockSpec((SEQ, PAIR), lambda i: (i // (SEQ // tm), 0)),
            pl.BlockSpec((SEQ, PAIR), lambda i: (i // (SEQ // tm), 0)),
            pl.BlockSpec(g.shape, lambda i: (0, 0)),
            pl.BlockSpec((pl.Element(1), pl.Element(D_MODEL), pl.Element(GATE_COLS)), lambda i: (layer, 0, COL_GATES),
                         pipeline_mode=pl.Buffered(1)),
            pl.BlockSpec((None,) + wpa.shape[1:], per_layer, pipeline_mode=pl.Buffered(1)),
            pl.BlockSpec((None,) + wpb.shape[1:], per_layer, pipeline_mode=pl.Buffered(1)),
            pl.BlockSpec((None,) + wpm.shape[1:], per_layer, pipeline_mode=pl.Buffered(1)),
            pl.BlockSpec((None,) + wo.shape[1:], per_layer, pipeline_mode=pl.Buffered(1)),
        ],
        out_specs=pl.BlockSpec((tm, D_MODEL), lambda i: (i, 0)),
        out_shape=jax.ShapeDtypeStruct((t, D_MODEL), jnp.float32),
        scratch_shapes=[pltpu.VMEM((tm, PAIR), jnp.float32), pltpu.VMEM((tm, PAIR), jnp.float32)],
        compiler_params=pltpu.CompilerParams(dimension_semantics=("parallel",), vmem_limit_bytes=VMEM_LIMIT),
        name="gated_mix",
    )(x2, o, lse, ocls, lsecls, g, w_in, wpa, wpb, wpm, wo)


def _mlp_kernel(x_ref, g_ref, wup_hbm, wdown_hbm, gfin_ref, out_ref, wup_ref, wdown_ref, up_stage, down_stage, sems,
                *, layer, final_norm):
    n_chunks = D_FF // FF_CHUNK
    chunks = [pl.ds(c * FF_CHUNK, FF_CHUNK) for c in range(n_chunks)]

    def up_copy(c):
        return pltpu.make_async_copy(wup_hbm.at[layer, :, chunks[c]], up_stage.at[c % STAGE_SLOTS],
                                     sems.at[0, c % STAGE_SLOTS])

    def down_copy(c):
        return pltpu.make_async_copy(wdown_hbm.at[layer, chunks[c], :], down_stage.at[c % STAGE_SLOTS],
                                     sems.at[1, c % STAGE_SLOTS])

    def run(stream):
        if stream:
            for c in range(min(STAGE_SLOTS, n_chunks)):
                up_copy(c).start()
                down_copy(c).start()
        x = x_ref[...]
        hm = _rms(x, g_ref[layer:layer + 1, :]).astype(jnp.bfloat16)
        acc = x
        for c in range(n_chunks):
            if stream:
                up_copy(c).wait()
                wup_ref[:, chunks[c]] = up_stage[c % STAGE_SLOTS].astype(jnp.bfloat16)
                if c + STAGE_SLOTS < n_chunks:
                    up_copy(c + STAGE_SLOTS).start()
                down_copy(c).wait()
                wdown_ref[chunks[c], :] = down_stage[c % STAGE_SLOTS].astype(jnp.bfloat16)
                if c + STAGE_SLOTS < n_chunks:
                    down_copy(c + STAGE_SLOTS).start()
            u = jnp.dot(hm, wup_ref[:, chunks[c]], preferred_element_type=jnp.float32)
            u = jnp.square(jnp.maximum(u, 0.0)).astype(jnp.bfloat16)
            acc = acc + jnp.dot(u, wdown_ref[chunks[c], :], preferred_element_type=jnp.float32)
        out_ref[...] = _rms(acc, gfin_ref[...]) if final_norm else acc

    first_step = pl.program_id(0) == 0
    pl.when(first_step)(lambda: run(True))
    pl.when(jnp.logical_not(first_step))(lambda: run(False))


def _mlp_call(x2, g, wup, wdown, gfin, layer, final_norm):
    t = x2.shape[0]
    tm = TM_MLP
    const = lambda i: (0, 0)
    per_layer = lambda i: (layer, 0, 0)
    return pl.pallas_call(
        functools.partial(_mlp_kernel, layer=layer, final_norm=final_norm),
        grid=(t // tm,),
        in_specs=[
            pl.BlockSpec((tm, D_MODEL), lambda i: (i, 0)),
            pl.BlockSpec(g.shape, const),
            pl.BlockSpec(memory_space=pl.ANY),
            pl.BlockSpec(memory_space=pl.ANY),
            pl.BlockSpec((1, D_MODEL), const),
        ],
        out_specs=pl.BlockSpec((tm, D_MODEL), lambda i: (i, 0)),
        out_shape=jax.ShapeDtypeStruct((t, D_MODEL), jnp.float32),
        scratch_shapes=[pltpu.VMEM(wup.shape[1:], jnp.bfloat16), pltpu.VMEM(wdown.shape[1:], jnp.bfloat16),
                        pltpu.VMEM((STAGE_SLOTS, wup.shape[1], FF_CHUNK), jnp.float32),
                        pltpu.VMEM((STAGE_SLOTS, FF_CHUNK, wdown.shape[2]), jnp.float32),
                        pltpu.SemaphoreType.DMA((2, STAGE_SLOTS))],
        compiler_params=pltpu.CompilerParams(dimension_semantics=("arbitrary",), vmem_limit_bytes=VMEM_LIMIT),
        name="mlp",
    )(x2, g, wup, wdown, gfin)


def _rope_tables(seq):
    half = ROT_DIM // 2
    inv_freq = (1.0 / (np.float32(ROPE_THETA) ** (np.arange(0, ROT_DIM, 2, dtype=np.float32) / np.float32(ROT_DIM))))
    ang = np.arange(seq, dtype=np.float32)[:, None] * inv_freq.astype(np.float32)[None, :]
    cos, sin = np.cos(ang).astype(np.float32), np.sin(ang).astype(np.float32)
    ones = np.ones((seq, HEAD_DIM - ROT_DIM), np.float32)
    zeros_half = np.zeros((seq, half), np.float32)
    zeros_rest = np.zeros((seq, HEAD_DIM - ROT_DIM), np.float32)
    c_head = np.concatenate([cos, cos, ones], axis=1)
    lo_head = np.concatenate([-sin, zeros_half, zeros_rest], axis=1)
    hi_head = np.concatenate([zeros_half, sin, zeros_rest], axis=1)
    rep = LANES // HEAD_DIM
    return tuple(jnp.asarray(np.tile(t, (1, rep))) for t in (c_head, lo_head, hi_head))


def kernel(x, mem, norm_mix, w_in, w_proj_a, w_proj_b, w_proj_m, w_out, norm_mem, w_mem_kv, norm_mlp, w_up,
           w_down, norm_final):
    batch, seq, d = x.shape
    assert d == D_MODEL and seq == SEQ and seq % TM_QKV == 0
    depth = w_in.shape[0]
    x2 = x.reshape(batch * seq, d)
    mem2 = mem.reshape(batch * mem.shape[1], d)
    rope_c, rope_s1, rope_s2 = _rope_tables(seq)
    bias = jnp.asarray(_BIAS_NP)
    cbias = jnp.asarray(_CLASS_BIAS_NP)
    for l in range(depth):
        qka, vta, qkb, vtb, qm, cls, kmean = _qkv_call(x2, norm_mix, w_in, l, rope_c, rope_s1, rope_s2, batch, seq)
        o, lse, ocls, lsecls = _attn_call(qka, vta, qkb, vtb, qm, cls, kmean, mem2, norm_mem,
                                          w_mem_kv, l, bias, cbias, batch, seq)
        x2 = _mix_call(x2, o, lse, ocls, lsecls, norm_mix, w_in, w_proj_a, w_proj_b, w_proj_m, w_out, l)
        x2 = _mlp_call(x2, norm_mlp, w_up, w_down, norm_final.reshape(1, d), l, final_norm=(l == depth - 1))
    return x2.reshape(batch, seq, d)
```
